```python
import jax
import jax.numpy as jnp
from jax import lax
import numpy as np


D_MODEL = 1024
BATCH = 16
SEQ = 2048
DEPTH = 4

GRID_W = 64
CTX_LEN = 256
N_MIXERS = 3
ALPHA = (2.0 * DEPTH) ** 0.25
BETA = (8.0 * DEPTH) ** -0.25
LN_EPS = 1e-5
RMS_EPS = 1e-6
N_MOD = 6

CONV_W = 31
D_GLU = 2 * D_MODEL
SHORT_W = 3
MLA_HEADS = 8
QK_NOPE = 128
QK_ROPE = 64
V_HEAD = 128
Q_RANK = 384
KV_RANK = 256
ROPE_THETA = 10000.0
ATTN_SCALE = (QK_NOPE + QK_ROPE) ** -0.5
Q_BLOCK = 128
N_EXPERTS = 64
TOP_K = 8
N_GROUPS = 8
TOPK_GROUPS = 4
D_EXPERT = D_MODEL // 4
D_SHARED = D_MODEL // 4
ROUTED_SCALE = 2.5
MOE_BLOCK = 128

N_A = (DEPTH + 2) // 3
N_B = (DEPTH + 1) // 3
N_C = DEPTH // 3

kernel_name = 'hybrid_dit_conformer_shortconv_mla_moe'


def layer_norm(x, g, b):
    xf = x.astype(jnp.float32)
    mu = jnp.mean(xf, -1, keepdims=True)
    var = jnp.mean(jnp.square(xf - mu), -1, keepdims=True)
    return ((xf - mu) * lax.rsqrt(var + LN_EPS)).astype(x.dtype) * g + b


def rms_norm(x, g):
    xf = x.astype(jnp.float32)
    return (xf * lax.rsqrt(jnp.mean(xf * xf, -1, keepdims=True) + RMS_EPS)).astype(x.dtype) * g


def depthwise_conv(u, w):
    k = w.shape[0]
    p = (k - 1) // 2
    return lax.conv_general_dilated(
        u, w[:, None, :].astype(u.dtype), window_strides=(1,), padding=[(p, p)],
        dimension_numbers=('NWC', 'WIO', 'NWC'), feature_group_count=u.shape[-1])


def swiglu(t, w1, w3, w2):
    return (jax.nn.silu(t @ w1) * (t @ w3)) @ w2


def axial_rope_tables(rows):
    n_freq = QK_ROPE // 4
    inv_freq = ROPE_THETA ** (-jnp.arange(n_freq, dtype=jnp.float32) / n_freq)
    r = jnp.broadcast_to(jnp.arange(rows, dtype=jnp.float32)[:, None], (rows, GRID_W)).reshape(-1)
    col = jnp.broadcast_to(jnp.arange(GRID_W, dtype=jnp.float32)[None, :], (rows, GRID_W)).reshape(-1)
    ang = jnp.concatenate([r[:, None] * inv_freq, col[:, None] * inv_freq], -1)
    return jnp.cos(ang), jnp.sin(ang)


def apply_rope(t, cos, sin):
    t1, t2 = jnp.split(t, 2, axis=-1)
    return jnp.concatenate([t1 * cos - t2 * sin, t1 * sin + t2 * cos], -1).astype(t.dtype)


def conformer_conv(h, w1, b1, dw, dwb, ng, nb, w2, b2):
    u = h @ w1 + b1
    u = u[..., :D_MODEL] * jax.nn.sigmoid(u[..., D_MODEL:])
    u = depthwise_conv(u, dw) + dwb
    u = jax.nn.silu(layer_norm(u, ng, nb))
    return u @ w2 + b2


def short_conv(h, w_in, dw, w_out):
    gb, gc, v = jnp.split(h @ w_in, 3, axis=-1)
    u = depthwise_conv(gc * v, dw)
    return (gb * u) @ w_out


def mla_down(t, w_dqkv, q_g, kv_g):
    d = t @ w_dqkv
    cq = rms_norm(d[..., :Q_RANK], q_g)
    ckv = rms_norm(d[..., Q_RANK:Q_RANK + KV_RANK], kv_g)
    k_pe = d[..., Q_RANK + KV_RANK:]
    return cq, ckv, k_pe


def mla_queries(cq, w_uq):
    b, l, _ = cq.shape
    q = (cq @ w_uq).reshape(b, l, MLA_HEADS, QK_NOPE + QK_ROPE)
    return q[..., :QK_NOPE], q[..., QK_NOPE:]


def mla_keys_values(ckv, w_uk, w_uv):
    b, l, _ = ckv.shape
    k_nope = (ckv @ w_uk).reshape(b, l, MLA_HEADS, QK_NOPE)
    v = (ckv @ w_uv).reshape(b, l, MLA_HEADS, V_HEAD)
    return k_nope, v


def mla_attend(qn, qp, kn, kp, v):
    s = jnp.einsum('bqhn,bkhn->bhqk', qn, kn) + jnp.einsum('bqhr,bkr->bhqk', qp, kp)
    p = jax.nn.softmax(s.astype(jnp.float32) * ATTN_SCALE, axis=-1).astype(v.dtype)
    return jnp.einsum('bhqk,bkhv->bqhv', p, v)


def mla_mixer(h, hc, cos, sin, w_dqkv, q_g, kv_g, w_uq, w_uk, w_uv, w_o, need_ctx):
    b, s, _ = h.shape
    cq, ckv, kp = mla_down(h, w_dqkv, q_g, kv_g)
    qn, qp = mla_queries(cq, w_uq)
    kn, v = mla_keys_values(ckv, w_uk, w_uv)
    qp = apply_rope(qp, cos[None, :, None, :], sin[None, :, None, :])
    kp = apply_rope(kp, cos[None], sin[None])
    cq_c, ckv_c, kp_c = mla_down(hc, w_dqkv, q_g, kv_g)
    kn_c, v_c = mla_keys_values(ckv_c, w_uk, w_uv)
    kn_all = jnp.concatenate([kn, kn_c], axis=1)
    kp_all = jnp.concatenate([kp, kp_c], axis=1)
    v_all = jnp.concatenate([v, v_c], axis=1)
    nb = s // Q_BLOCK

    def blocks(t):
        return jnp.moveaxis(t.reshape(b, nb, Q_BLOCK, *t.shape[2:]), 1, 0)

    o = lax.map(lambda qs: mla_attend(qs[0], qs[1], kn_all, kp_all, v_all), (blocks(qn), blocks(qp)))
    o = jnp.moveaxis(o, 0, 1).reshape(b, s, MLA_HEADS * V_HEAD)
    y = o @ w_o
    if need_ctx:
        qn_c, qp_c = mla_queries(cq_c, w_uq)
        oc = mla_attend(qn_c, qp_c, kn_c, kp_c, v_c)
        yc = oc.reshape(b, hc.shape[1], MLA_HEADS * V_HEAD) @ w_o
    else:
        yc = None
    return y, yc


def expert_dispatch(h, topi, wts, w1, w3, w2):
    t, d = h.shape
    a = t * TOP_K
    e_flat = topi.reshape(a)
    order = jnp.argsort(e_flat)
    e_sorted = e_flat[order]
    tok_sorted = (order // TOP_K).astype(jnp.int32)
    gate_sorted = wts.reshape(a)[order]
    counts = jnp.bincount(e_flat, length=N_EXPERTS)
    padded = (counts + MOE_BLOCK - 1) // MOE_BLOCK * MOE_BLOCK
    pad_end = jnp.cumsum(padded)
    pad_start = pad_end - padded
    start = jnp.cumsum(counts) - counts
    dest = pad_start[e_sorted] + (jnp.arange(a, dtype=jnp.int32) - start[e_sorted])
    n_blocks = -(-a // MOE_BLOCK) + N_EXPERTS
    p = n_blocks * MOE_BLOCK
    tok_buf = jnp.full((p,), t, jnp.int32).at[dest].set(tok_sorted)
    gate_buf = jnp.zeros((p,), h.dtype).at[dest].set(gate_sorted)
    block_start = jnp.arange(n_blocks, dtype=pad_end.dtype) * MOE_BLOCK
    block_expert = jnp.minimum(jnp.searchsorted(pad_end, block_start, side='right'), N_EXPERTS - 1)
    h_pad = jnp.concatenate([h, jnp.zeros((1, d), h.dtype)], axis=0)

    def run_block(args):
        toks, e = args
        return swiglu(h_pad[toks], w1[e], w3[e], w2[e])

    yb = lax.map(run_block, (tok_buf.reshape(n_blocks, MOE_BLOCK), block_expert))
    out = jnp.zeros((t + 1, d), h.dtype).at[tok_buf].add(yb.reshape(p, d) * gate_buf[:, None])
    return out[:t]


def moe(h, w_router, r_bias, w1, w3, w2, ws1, ws3, ws2):
    t = h.shape[0]
    s = jax.nn.sigmoid((h @ w_router).astype(jnp.float32))
    sel = s + r_bias.astype(jnp.float32)
    grp = sel.reshape(t, N_GROUPS, N_EXPERTS // N_GROUPS)
    gscore = jnp.sum(lax.top_k(grp, 2)[0], axis=-1)
    _, gidx = lax.top_k(gscore, TOPK_GROUPS)
    gmask = jnp.any(gidx[:, :, None] == jnp.arange(N_GROUPS)[None, None, :], axis=1)
    emask = jnp.repeat(gmask, N_EXPERTS // N_GROUPS, axis=1)
    _, topi = lax.top_k(jnp.where(emask, sel, -jnp.inf), TOP_K)
    w = jnp.take_along_axis(s, topi, axis=-1)
    w = (w / jnp.sum(w, -1, keepdims=True) * ROUTED_SCALE).astype(h.dtype)
    return expert_dispatch(h, topi, w, w1, w3, w2) + swiglu(h, ws1, ws3, ws2)


def setup_inputs(seed: int = 0) -> dict:
    key = jax.random.key(seed)
    ks = iter(jax.random.split(key, 40))
    D = D_MODEL

    def nrm(shape, scale):
        return jax.random.normal(next(ks), shape, jnp.float32) * scale

    def gain(shape):
        return 1.0 + nrm(shape, 0.02)

    return {
        'x': nrm((BATCH, SEQ, D), 1.0),
        'c': nrm((BATCH, D), 1.0),
        'ctx': nrm((BATCH, CTX_LEN, D), 1.0),
        'c_ctx': nrm((D,), 1.0),
        'ada_w': nrm((DEPTH, D, N_MOD * D), 0.5 * D ** -0.5),
        'ada_b': nrm((DEPTH, N_MOD * D), 0.02),
        'ln_g': gain((DEPTH, 2, D)),
        'ln_b': nrm((DEPTH, 2, D), 0.02),
        'conf_w1': nrm((N_A, D, D_GLU), D ** -0.5),
        'conf_b1': nrm((N_A, D_GLU), 0.02),
        'conf_dw': nrm((N_A, CONV_W, D), CONV_W ** -0.5),
        'conf_dwb': nrm((N_A, D), 0.02),
        'conf_ng': gain((N_A, D)),
        'conf_nb': nrm((N_A, D), 0.02),
        'conf_w2': nrm((N_A, D, D), BETA * D ** -0.5),
        'conf_b2': nrm((N_A, D), 0.02),
        'sc_w_in': nrm((N_B, D, 3 * D), D ** -0.5),
        'sc_dw': nrm((N_B, SHORT_W, D), SHORT_W ** -0.5),
        'sc_w_out': nrm((N_B, D, D), BETA * D ** -0.5),
        'mla_w_dqkv': nrm((N_C, D, Q_RANK + KV_RANK + QK_ROPE), D ** -0.5),
        'mla_q_g': gain((N_C, Q_RANK)),
        'mla_kv_g': gain((N_C, KV_RANK)),
        'mla_w_uq': nrm((N_C, Q_RANK, MLA_HEADS * (QK_NOPE + QK_ROPE)), Q_RANK ** -0.5),
        'mla_w_uk': nrm((N_C, KV_RANK, MLA_HEADS * QK_NOPE), KV_RANK ** -0.5),
        'mla_w_uv': nrm((N_C, KV_RANK, MLA_HEADS * V_HEAD), BETA * KV_RANK ** -0.5),
        'mla_w_o': nrm((N_C, MLA_HEADS * V_HEAD, D), BETA * (MLA_HEADS * V_HEAD) ** -0.5),
        'moe_router': nrm((DEPTH, D, N_EXPERTS), D ** -0.5),
        'moe_bias': nrm((DEPTH, N_EXPERTS), 0.01),
        'moe_w1': nrm((DEPTH, N_EXPERTS, D, D_EXPERT), D ** -0.5),
        'moe_w3': nrm((DEPTH, N_EXPERTS, D, D_EXPERT), D ** -0.5),
        'moe_w2': nrm((DEPTH, N_EXPERTS, D_EXPERT, D), BETA * D_EXPERT ** -0.5),
        'sh_w1': nrm((DEPTH, D, D_SHARED), D ** -0.5),
        'sh_w3': nrm((DEPTH, D, D_SHARED), D ** -0.5),
        'sh_w2': nrm((DEPTH, D_SHARED, D), BETA * D_SHARED ** -0.5),
    }


def reference(x, c, ctx, c_ctx, ada_w, ada_b, ln_g, ln_b,
              conf_w1, conf_b1, conf_dw, conf_dwb, conf_ng, conf_nb, conf_w2, conf_b2,
              sc_w_in, sc_dw, sc_w_out,
              mla_w_dqkv, mla_q_g, mla_kv_g, mla_w_uq, mla_w_uk, mla_w_uv, mla_w_o,
              moe_router, moe_bias, moe_w1, moe_w3, moe_w2, sh_w1, sh_w3, sh_w2):
    b, s, d = x.shape
    l_ctx = ctx.shape[1]
    rows = s // GRID_W
    cos, sin = axial_rope_tables(rows)
    attn_layers = [i for i in range(DEPTH) if i % N_MIXERS == 2]
    last_ctx_reader = attn_layers[-1] if attn_layers else -1

    for i in range(DEPTH):
        need_ctx = i < last_ctx_reader
        kind, j = i % N_MIXERS, i // N_MIXERS
        mod = jax.nn.silu(c) @ ada_w[i] + ada_b[i]
        mod_c = jax.nn.silu(c_ctx) @ ada_w[i] + ada_b[i]
        sh1, sc1, g1, sh2, sc2, g2 = jnp.split(mod[:, None, :], N_MOD, axis=-1)
        csh1, csc1, cg1, csh2, csc2, cg2 = jnp.split(mod_c, N_MOD, axis=-1)

        h = x * (1.0 + sc1) + sh1
        hc = ctx * (1.0 + csc1) + csh1
        if kind == 0:
            prm = (conf_w1[j], conf_b1[j], conf_dw[j], conf_dwb[j], conf_ng[j], conf_nb[j], conf_w2[j], conf_b2[j])
            y = conformer_conv(h, *prm)
            yc = conformer_conv(hc, *prm) if need_ctx else None
        elif kind == 1:
            prm = (sc_w_in[j], sc_dw[j], sc_w_out[j])
            y = short_conv(h, *prm)
            yc = short_conv(hc, *prm) if need_ctx else None
        else:
            y, yc = mla_mixer(h, hc, cos, sin, mla_w_dqkv[j], mla_q_g[j], mla_kv_g[j],
                              mla_w_uq[j], mla_w_uk[j], mla_w_uv[j], mla_w_o[j], need_ctx)
        x = layer_norm(ALPHA * x + g1 * y, ln_g[i, 0], ln_b[i, 0])
        if need_ctx:
            ctx = layer_norm(ALPHA * ctx + cg1 * yc, ln_g[i, 0], ln_b[i, 0])

        prm = (moe_router[i], moe_bias[i], moe_w1[i], moe_w3[i], moe_w2[i], sh_w1[i], sh_w3[i], sh_w2[i])
        h = (x * (1.0 + sc2) + sh2).reshape(b * s, d)
        if need_ctx:
            hc = (ctx * (1.0 + csc2) + csh2).reshape(b * l_ctx, d)
            out = moe(jnp.concatenate([h, hc], axis=0), *prm)
            y = out[:b * s].reshape(b, s, d)
            yc = out[b * s:].reshape(b, l_ctx, d)
            ctx = layer_norm(ALPHA * ctx + cg2 * yc, ln_g[i, 1], ln_b[i, 1])
        else:
            y = moe(h, *prm).reshape(b, s, d)
        x = layer_norm(ALPHA * x + g2 * y, ln_g[i, 1], ln_b[i, 1])
    return x
```

```python
import functools

import jax
import jax.numpy as jnp
from jax import lax
from jax.experimental import pallas as pl
from jax.experimental.pallas import tpu as pltpu

F32 = jnp.float32
BF16 = jnp.bfloat16

GRID_W = 64
N_MIXERS = 3
LN_EPS = 1e-5
RMS_EPS = 1e-6
N_MOD = 6
MLA_HEADS = 8
QK_NOPE = 128
QK_ROPE = 64
V_HEAD = 128
ROPE_THETA = 10000.0
ATTN_SCALE = (QK_NOPE + QK_ROPE) ** -0.5
TOP_K = 8
N_GROUPS = 8
TOPK_GROUPS = 4
ROUTED_SCALE = 2.5

SEQ_TILE = 256
CONV_HALO = 16
SHORT_HALO = 8
CONV_ROW_CHUNK = 64
LANE = 128
Q_TILE = 512
MOE_BLOCK_ROWS = 256
MOD_COL_TILE = 1536
VMEM_LIMIT = 48 * 1024 * 1024


def _params(n_axes):
    return pltpu.CompilerParams(dimension_semantics=("arbitrary",) * n_axes,
                                vmem_limit_bytes=VMEM_LIMIT)


def _split_bf16(a):
    hi = a.astype(BF16)
    lo = (a - hi.astype(F32)).astype(BF16)
    return hi, lo


def _dot(a, b):
    return jnp.dot(a, b, preferred_element_type=F32)


def _dot_nt(a, b):
    return lax.dot_general(a, b, (((1,), (1,)), ((), ())), preferred_element_type=F32)


def _layer_norm(v, g, b):
    mu = jnp.mean(v, axis=-1, keepdims=True)
    c = v - mu
    var = jnp.mean(c * c, axis=-1, keepdims=True)
    return c * lax.rsqrt(var + LN_EPS) * g + b


def _silu(v):
    return v * jax.nn.sigmoid(v)


def _mod_kernel(c_ref, w_ref, b_ref, o_ref):
    a = _silu(c_ref[...])
    a_hi, a_lo = _split_bf16(a)
    w_hi, w_lo = _split_bf16(w_ref[0])
    o_ref[0] = _dot(a_hi, w_hi) + _dot(a_hi, w_lo) + _dot(a_lo, w_hi) + b_ref[0]


def _modulation(c_all, ada_w, ada_b):
    depth, d, n = ada_w.shape
    rows = c_all.shape[0]
    tn = MOD_COL_TILE
    return pl.pallas_call(
        _mod_kernel,
        out_shape=jax.ShapeDtypeStruct((depth, rows, n), F32),
        grid=(depth, n // tn),
        in_specs=[pl.BlockSpec((rows, d), lambda i, j: (0, 0)),
                  pl.BlockSpec((1, d, tn), lambda i, j: (i, 0, j)),
                  pl.BlockSpec((1, 1, tn), lambda i, j: (i, 0, j))],
        out_specs=pl.BlockSpec((1, rows, tn), lambda i, j: (i, 0, j)),
        compiler_params=_params(2),
        name="adaln_mod",
    )(c_all, ada_w, ada_b.reshape(depth, 1, n))


def _tok_spec(d, tl=SEQ_TILE):
    return pl.BlockSpec((1, tl, d), lambda b, l: (b, l, 0))


def _mod_spec(d, n_batch, ntx):
    return pl.BlockSpec((1, N_MOD, d), lambda b, l: (jnp.where(l < ntx, b, n_batch), 0, 0))


def _full_spec(shape):
    zeros = (0,) * len(shape)
    return pl.BlockSpec(shape, lambda b, l: zeros)


def _halo_specs(d, halo, seq_len, tl=SEQ_TILE):
    per_tile = tl // halo
    last = seq_len // halo - 1
    prev = pl.BlockSpec((1, halo, d), lambda b, l: (b, jnp.maximum(l * per_tile - 1, 0), 0))
    nxt = pl.BlockSpec((1, halo, d), lambda b, l: (b, jnp.minimum((l + 1) * per_tile, last), 0))
    return prev, nxt


def _segment_edges(l, ntx, nt):
    first = jnp.logical_or(l == 0, l == ntx)
    last = jnp.logical_or(l == ntx - 1, l == nt - 1)
    return first, last


def _conf_in_kernel(x_ref, mod_ref, w1_ref, b1_ref, u_ref):
    d = x_ref.shape[-1]
    h = x_ref[0] * (1.0 + mod_ref[0, 1:2, :]) + mod_ref[0, 0:1, :]
    z = _dot(h.astype(BF16), w1_ref[...]) + b1_ref[...]
    u_ref[0] = z[:, :d] * jax.nn.sigmoid(z[:, d:])


def _conf_in(x_all, mod, w1, b1, nt, ntx):
    nb, seq_len, d = x_all.shape
    return pl.pallas_call(
        _conf_in_kernel,
        out_shape=jax.ShapeDtypeStruct((nb, nt * SEQ_TILE, d), F32),
        grid=(nb, nt),
        in_specs=[_tok_spec(d), _mod_spec(d, nb, ntx), _full_spec(w1.shape), _full_spec(b1.shape)],
        out_specs=_tok_spec(d),
        compiler_params=_params(2),
        name="conf_in",
    )(x_all, mod, w1, b1)


def _conf_conv_kernel(ntx, nt, up_ref, uc_ref, un_ref, dw_ref, dwb_ref, ng_ref, nb_ref,
                      a_ref, ext_ref, conv_ref):
    tl, d = uc_ref.shape[1], uc_ref.shape[2]
    taps = dw_ref.shape[0]
    lead = CONV_HALO - (taps - 1) // 2
    first, last = _segment_edges(pl.program_id(1), ntx, nt)
    ext_ref[0:CONV_HALO, :] = jnp.where(first, 0.0, up_ref[0])
    ext_ref[CONV_HALO:CONV_HALO + tl, :] = uc_ref[0]
    ext_ref[CONV_HALO + tl:, :] = jnp.where(last, 0.0, un_ref[0])
    for r0 in range(0, tl, CONV_ROW_CHUNK):
        for c0 in range(0, d, LANE):
            acc = jnp.zeros((CONV_ROW_CHUNK, LANE), F32)
            for k in range(taps):
                acc = acc + dw_ref[k:k + 1, c0:c0 + LANE] * ext_ref[r0 + lead + k:r0 + lead + k + CONV_ROW_CHUNK, c0:c0 + LANE]
            conv_ref[r0:r0 + CONV_ROW_CHUNK, c0:c0 + LANE] = acc
    v = _layer_norm(conv_ref[...] + dwb_ref[...], ng_ref[...], nb_ref[...])
    a_ref[0] = _silu(v).astype(BF16)


def _conf_conv(u, dw, dwb, ng, nb_, nt, ntx):
    nb, seq_len, d = u.shape
    prev, nxt = _halo_specs(d, CONV_HALO, seq_len)
    return pl.pallas_call(
        functools.partial(_conf_conv_kernel, ntx, nt),
        out_shape=jax.ShapeDtypeStruct((nb, seq_len, d), BF16),
        grid=(nb, nt),
        in_specs=[prev, _tok_spec(d), nxt, _full_spec(dw.shape), _full_spec(dwb.shape),
                  _full_spec(ng.shape), _full_spec(nb_.shape)],
        out_specs=_tok_spec(d),
        scratch_shapes=[pltpu.VMEM((SEQ_TILE + 2 * CONV_HALO, d), F32),
                        pltpu.VMEM((SEQ_TILE, d), F32)],
        compiler_params=_params(2),
        name="conf_conv",
    )(u, u, u, dw, dwb, ng, nb_)


def _sc_in_kernel(x_ref, mod_ref, w_ref, gb_ref, p_ref):
    d = x_ref.shape[-1]
    h = x_ref[0] * (1.0 + mod_ref[0, 1:2, :]) + mod_ref[0, 0:1, :]
    z = _dot(h.astype(BF16), w_ref[...])
    gb_ref[0] = z[:, :d]
    p_ref[0] = z[:, d:2 * d] * z[:, 2 * d:]


def _sc_in(x_all, mod, w_in, nt, ntx):
    nb, seq_len, d = x_all.shape
    shp = jax.ShapeDtypeStruct((nb, nt * SEQ_TILE, d), F32)
    return pl.pallas_call(
        _sc_in_kernel,
        out_shape=(shp, shp),
        grid=(nb, nt),
        in_specs=[_tok_spec(d), _mod_spec(d, nb, ntx), _full_spec(w_in.shape)],
        out_specs=(_tok_spec(d), _tok_spec(d)),
        compiler_params=_params(2),
        name="sc_in",
    )(x_all, mod, w_in)


def _sc_conv_kernel(ntx, nt, pp_ref, pc_ref, pn_ref, gb_ref, dw_ref, a_ref, ext_ref):
    tl = pc_ref.shape[1]
    taps = dw_ref.shape[0]
    lead = SHORT_HALO - (taps - 1) // 2
    first, last = _segment_edges(pl.program_id(1), ntx, nt)
    ext_ref[0:SHORT_HALO, :] = jnp.where(first, 0.0, pp_ref[0])
    ext_ref[SHORT_HALO:SHORT_HALO + tl, :] = pc_ref[0]
    ext_ref[SHORT_HALO + tl:, :] = jnp.where(last, 0.0, pn_ref[0])
    acc = dw_ref[0:1, :] * ext_ref[lead:lead + tl, :]
    for k in range(1, taps):
        acc = acc + dw_ref[k:k + 1, :] * ext_ref[lead + k:lead + k + tl, :]
    a_ref[0] = (gb_ref[0] * acc).astype(BF16)


def _sc_conv(p, gb, dw, nt, ntx):
    nb, seq_len, d = p.shape
    prev, nxt = _halo_specs(d, SHORT_HALO, seq_len)
    return pl.pallas_call(
        functools.partial(_sc_conv_kernel, ntx, nt),
        out_shape=jax.ShapeDtypeStruct((nb, seq_len, d), BF16),
        grid=(nb, nt),
        in_specs=[prev, _tok_spec(d), nxt, _tok_spec(d), _full_spec(dw.shape)],
        out_specs=_tok_spec(d),
        scratch_shapes=[pltpu.VMEM((SEQ_TILE + 2 * SHORT_HALO, d), F32)],
        compiler_params=_params(2),
        name="sc_conv",
    )(p, p, p, gb, dw)


def _rms(v, g):
    return v * lax.rsqrt(jnp.mean(v * v, axis=-1, keepdims=True) + RMS_EPS) * g


def _mla_proj_kernel(ntx, q_rank,
                     x_ref, mod_ref, wd_ref, wkp_ref, qg_ref, kvg_ref, wqn_ref, wqr_ref, wuk_ref, wuv_ref,
                     cos_ref, sin_ref, q_ref, k_ref, v_ref):
    is_latent = pl.program_id(1) < ntx
    h = (x_ref[0] * (1.0 + mod_ref[0, 1:2, :]) + mod_ref[0, 0:1, :]).astype(BF16)
    dn = _dot(h, wd_ref[...])
    cq = _rms(dn[:, :q_rank], qg_ref[...]).astype(BF16)
    ckv = _rms(dn[:, q_rank:], kvg_ref[...]).astype(BF16)
    cos = jnp.where(is_latent, cos_ref[...], 1.0)
    sin = jnp.where(is_latent, sin_ref[...], 0.0)
    kp2 = _dot(h, wkp_ref[...])
    kp = kp2[:, :LANE] * cos + kp2[:, LANE:] * sin
    kn = _dot(ckv, wuk_ref[...])
    vv = _dot(ckv, wuv_ref[...])
    qn = _dot(cq, wqn_ref[...]) * ATTN_SCALE
    qr2 = _dot(cq, wqr_ref[...])
    hw = MLA_HEADS * LANE
    for hd in range(MLA_HEADS):
        sl = slice(hd * LANE, (hd + 1) * LANE)
        qr = (qr2[:, sl] * cos + qr2[:, hw + hd * LANE:hw + (hd + 1) * LANE] * sin) * ATTN_SCALE
        q_ref[0, hd, :, 0:LANE] = qn[:, sl].astype(BF16)
        q_ref[0, hd, :, LANE:] = qr.astype(BF16)
        k_ref[0, hd, :, 0:LANE] = kn[:, sl].astype(BF16)
        k_ref[0, hd, :, LANE:] = kp.astype(BF16)
        v_ref[0, hd] = vv[:, sl].astype(BF16)


def _mla_proj(x_all, mod, wd, wkp, qg, kvg, wqn, wqr, wuk, wuv, cos_t, sin_t, nt, ntx):
    nb, seq_len, d = x_all.shape
    q_rank = qg.shape[-1]
    tl = SEQ_TILE
    rope_spec = pl.BlockSpec((tl, LANE), lambda b, l: (jnp.minimum(l, ntx - 1), 0))
    qk_shape = jax.ShapeDtypeStruct((nb, MLA_HEADS, nt * tl, 2 * LANE), BF16)
    v_shape = jax.ShapeDtypeStruct((nb, MLA_HEADS, nt * tl, LANE), BF16)
    qk_spec = pl.BlockSpec((1, MLA_HEADS, tl, 2 * LANE), lambda b, l: (b, 0, l, 0))
    v_spec = pl.BlockSpec((1, MLA_HEADS, tl, LANE), lambda b, l: (b, 0, l, 0))
    return pl.pallas_call(
        functools.partial(_mla_proj_kernel, ntx, q_rank),
        out_shape=(qk_shape, qk_shape, v_shape),
        grid=(nb, nt),
        in_specs=[_tok_spec(d), _mod_spec(d, nb, ntx), _full_spec(wd.shape), _full_spec(wkp.shape),
                  _full_spec(qg.shape), _full_spec(kvg.shape), _full_spec(wqn.shape), _full_spec(wqr.shape),
                  _full_spec(wuk.shape), _full_spec(wuv.shape), rope_spec, rope_spec],
        out_specs=(qk_spec, qk_spec, v_spec),
        compiler_params=_params(2),
        name="mla_proj",
    )(x_all, mod, wd, wkp, qg, kvg, wqn, wqr, wuk, wuv, cos_t, sin_t)


def _attn_kernel(q_ref, k_ref, v_ref, o_ref):
    s = _dot_nt(q_ref[0, 0], k_ref[0, 0])
    m = jnp.max(s, axis=-1, keepdims=True)
    p = jnp.exp(s - m)
    denom = jnp.sum(p, axis=-1, keepdims=True)
    o = _dot(p.astype(BF16), v_ref[0, 0])
    o_ref[0] = (o / denom).astype(BF16)


def _attention(q, k, v, n_q):
    nb, nh, n_k, dk = k.shape
    tq = Q_TILE
    return pl.pallas_call(
        _attn_kernel,
        out_shape=jax.ShapeDtypeStruct((nb, n_q, nh * V_HEAD), BF16),
        grid=(nb, nh, n_q // tq),
        in_specs=[pl.BlockSpec((1, 1, tq, dk), lambda b, h, i: (b, h, i, 0)),
                  pl.BlockSpec((1, 1, n_k, dk), lambda b, h, i: (b, h, 0, 0)),
                  pl.BlockSpec((1, 1, n_k, V_HEAD), lambda b, h, i: (b, h, 0, 0))],
        out_specs=pl.BlockSpec((1, tq, V_HEAD), lambda b, h, i: (b, i, h)),
        compiler_params=_params(3),
        name="mla_attn",
    )(q, k, v)


def _route(sel, s):
    n_e, n_t = sel.shape
    per = n_e // N_GROUPS
    sel3 = sel.reshape(N_GROUPS, per, n_t)
    s3 = s.reshape(N_GROUPS, per, n_t)
    iota_p = lax.broadcasted_iota(jnp.int32, (N_GROUPS, per, n_t), 1).astype(F32)
    iota_g = lax.broadcasted_iota(jnp.int32, (N_GROUPS, 1, n_t), 0).astype(F32)
    neg = -jnp.inf
    m1 = jnp.max(sel3, axis=1, keepdims=True)
    first = jnp.min(jnp.where(sel3 == m1, iota_p, float(per)), axis=1, keepdims=True)
    m2 = jnp.max(jnp.where(iota_p == first, neg, sel3), axis=1, keepdims=True)
    gs = m1 + m2
    gsel = jnp.zeros((N_GROUPS, 1, n_t), F32)
    for _ in range(TOPK_GROUPS):
        gm = jnp.max(gs, axis=0, keepdims=True)
        gfirst = jnp.min(jnp.where(gs == gm, iota_g, float(N_GROUPS)), axis=0, keepdims=True)
        pick = iota_g == gfirst
        gsel = jnp.where(pick, 1.0, gsel)
        gs = jnp.where(pick, neg, gs)
    val = jnp.where(gsel > 0.0, sel3, neg)
    iota_e = lax.broadcasted_iota(jnp.int32, (N_GROUPS, per, n_t), 0).astype(F32) * per + iota_p
    chosen = jnp.zeros((N_GROUPS, per, n_t), F32)
    idx, wts = [], []
    for _ in range(TOP_K):
        m = jnp.max(jnp.max(val, axis=1, keepdims=True), axis=0, keepdims=True)
        e = jnp.min(jnp.min(jnp.where(val == m, iota_e, float(n_e)), axis=1, keepdims=True), axis=0, keepdims=True)
        pick = iota_e == e
        wts.append(jnp.sum(jnp.sum(jnp.where(pick, s3, 0.0), axis=1, keepdims=True), axis=0, keepdims=True))
        idx.append(e)
        chosen = jnp.where(pick, 1.0, chosen)
        val = jnp.where(pick, neg, val)
    total = wts[0]
    for w in wts[1:]:
        total = total + w
    wts = [w / total * ROUTED_SCALE for w in wts]
    return idx, wts, chosen.reshape(n_e, n_t)


def _post_kernel(alpha, a_ref, w_ref, b_ref, x_ref, mod_ref, lng_ref, lnb_ref, rw_hi_ref, rw_lo_ref, rb_ref,
                 xn_ref, h2_ref, topi_ref, topw_ref, mask_ref):
    y = _dot(a_ref[0], w_ref[...]) + b_ref[...]
    xn = _layer_norm(alpha * x_ref[0] + mod_ref[0, 2:3, :] * y, lng_ref[...], lnb_ref[...])
    xn_ref[0] = xn
    h2 = xn * (1.0 + mod_ref[0, 4:5, :]) + mod_ref[0, 3:4, :]
    h_hi, h_lo = _split_bf16(h2)
    h2_ref[0] = h_hi
    logits = _dot_nt(rw_hi_ref[...], h_hi) + _dot_nt(rw_hi_ref[...], h_lo) + _dot_nt(rw_lo_ref[...], h_hi)
    s = jax.nn.sigmoid(logits)
    idx, wts, chosen = _route(s + rb_ref[...], s)
    n_t = s.shape[1]
    for k in range(TOP_K):
        topi_ref[k:k + 1, :] = idx[k].reshape(1, n_t).astype(jnp.int32)
        topw_ref[k:k + 1, :] = wts[k].reshape(1, n_t)
    mask_ref[...] = chosen.astype(jnp.int32)


def _post(alpha, a, w, bias, x_all, mod, lng, lnb, rw_hi, rw_lo, rb, nt, ntx):
    nb, _, dk = a.shape
    d = x_all.shape[-1]
    n_e = rw_hi.shape[0]
    tl = SEQ_TILE
    n_tok = nb * nt * tl
    col_spec = lambda rows: pl.BlockSpec((rows, tl), lambda b, l: (0, b * nt + l))
    return pl.pallas_call(
        functools.partial(_post_kernel, alpha),
        out_shape=(jax.ShapeDtypeStruct((nb, nt * tl, d), F32),
                   jax.ShapeDtypeStruct((nb, nt * tl, d), BF16),
                   jax.ShapeDtypeStruct((TOP_K, n_tok), jnp.int32),
                   jax.ShapeDtypeStruct((TOP_K, n_tok), F32),
                   jax.ShapeDtypeStruct((n_e, n_tok), jnp.int32)),
        grid=(nb, nt),
        in_specs=[_tok_spec(dk), _full_spec(w.shape), _full_spec(bias.shape), _tok_spec(d),
                  _mod_spec(d, nb, ntx), _full_spec(lng.shape), _full_spec(lnb.shape),
                  _full_spec(rw_hi.shape), _full_spec(rw_lo.shape), _full_spec(rb.shape)],
        out_specs=(_tok_spec(d), _tok_spec(d), col_spec(TOP_K), col_spec(TOP_K), col_spec(n_e)),
        compiler_params=_params(2),
        name="mixer_post",
    )(a, w, bias, x_all, mod, lng, lnb, rw_hi, rw_lo, rb)


def _gmm_kernel(be_ref, nu_ref, xs_ref, w13_ref, w2_ref, ys_ref):
    de = w2_ref.shape[1]

    @pl.when(pl.program_id(0) < nu_ref[0])
    def _():
        z = _dot(xs_ref[...], w13_ref[0])
        hmid = (_silu(z[:, :de]) * z[:, de:]).astype(BF16)
        ys_ref[...] = _dot(hmid, w2_ref[0]).astype(BF16)


def _grouped_ffn(block_expert, n_used, xs, w13, w2):
    n_rows, d = xs.shape
    bm = MOE_BLOCK_ROWS
    de = w2.shape[1]
    grid_spec = pltpu.PrefetchScalarGridSpec(
        num_scalar_prefetch=2,
        grid=(n_rows // bm,),
        in_specs=[pl.BlockSpec((bm, d), lambda i, be, nu: (i, 0)),
                  pl.BlockSpec((1, d, 2 * de), lambda i, be, nu: (be[i], 0, 0)),
                  pl.BlockSpec((1, de, d), lambda i, be, nu: (be[i], 0, 0))],
        out_specs=pl.BlockSpec((bm, d), lambda i, be, nu: (i, 0)),
    )
    return pl.pallas_call(
        _gmm_kernel,
        out_shape=jax.ShapeDtypeStruct((n_rows, d), BF16),
        grid_spec=grid_spec,
        compiler_params=_params(1),
        name="moe_grouped_ffn",
    )(block_expert, n_used, xs, w13, w2)


def _moe_out_kernel(alpha, x_ref, h2_ref, r_ref, mod_ref, ws13_ref, ws2_ref, lng_ref, lnb_ref, o_ref):
    de = ws2_ref.shape[0]
    z = _dot(h2_ref[0], ws13_ref[...])
    hmid = (_silu(z[:, :de]) * z[:, de:]).astype(BF16)
    y = r_ref[0] + _dot(hmid, ws2_ref[...])
    o_ref[0] = _layer_norm(alpha * x_ref[0] + mod_ref[0, 5:6, :] * y, lng_ref[...], lnb_ref[...])


def _moe_out(alpha, xn, h2, routed, mod, ws13, ws2, lng, lnb, nt, ntx):
    nb, _, d = xn.shape
    return pl.pallas_call(
        functools.partial(_moe_out_kernel, alpha),
        out_shape=jax.ShapeDtypeStruct((nb, nt * SEQ_TILE, d), F32),
        grid=(nb, nt),
        in_specs=[_tok_spec(d), _tok_spec(d), _tok_spec(d), _mod_spec(d, nb, ntx),
                  _full_spec(ws13.shape), _full_spec(ws2.shape), _full_spec(lng.shape), _full_spec(lnb.shape)],
        out_specs=_tok_spec(d),
        compiler_params=_params(2),
        name="moe_out",
    )(xn, h2, routed, mod, ws13, ws2, lng, lnb)


def _dispatch_plan(mask, topi):
    n_e, n_tok = mask.shape
    bm = MOE_BLOCK_ROWS
    csum = jnp.cumsum(mask, axis=1)
    counts = csum[:, -1]
    padded = (counts + bm - 1) // bm * bm
    pad_end = jnp.cumsum(padded)
    pos = (pad_end - padded)[:, None] + csum - mask
    onehot = topi[:, None, :] == jnp.arange(n_e, dtype=jnp.int32)[None, :, None]
    dest = jnp.sum(jnp.where(onehot, pos[None], 0), axis=1)
    n_blocks = n_tok * TOP_K // bm + n_e
    block_start = jnp.arange(n_blocks, dtype=jnp.int32) * bm
    block_expert = jnp.minimum(jnp.searchsorted(pad_end, block_start, side="right"), n_e - 1).astype(jnp.int32)
    n_used = (pad_end[-1] // bm).astype(jnp.int32).reshape(1)
    return dest.astype(jnp.int32), block_expert, n_used, n_blocks * bm


def _rope_tables(seq):
    n_freq = QK_ROPE // 4
    inv_freq = ROPE_THETA ** (-jnp.arange(n_freq, dtype=F32) / n_freq)
    pos = jnp.arange(seq, dtype=jnp.int32)
    r = (pos // GRID_W).astype(F32)
    col = (pos % GRID_W).astype(F32)
    ang = jnp.concatenate([r[:, None] * inv_freq, col[:, None] * inv_freq], -1)
    cos, sin = jnp.cos(ang), jnp.sin(ang)
    zeros = jnp.zeros((seq, LANE - QK_ROPE), F32)
    cos_slot = jnp.concatenate([cos, cos, zeros], -1)
    sin_slot = jnp.concatenate([-sin, sin, zeros], -1)
    return cos_slot, sin_slot


def _rope_slot_weights(w_rope):
    k, n, _ = w_rope.shape
    half = QK_ROPE // 2
    swapped = jnp.concatenate([w_rope[..., half:], w_rope[..., :half]], -1)
    pad = jnp.zeros((k, n, LANE - QK_ROPE), w_rope.dtype)
    plain = jnp.concatenate([w_rope, pad], -1).reshape(k, n * LANE)
    swp = jnp.concatenate([swapped, pad], -1).reshape(k, n * LANE)
    return jnp.concatenate([plain, swp], -1)


def kernel(x, c, ctx, c_ctx, ada_w, ada_b, ln_g, ln_b, conf_w1, conf_b1, conf_dw, conf_dwb, conf_ng, conf_nb, conf_w2, conf_b2, sc_w_in, sc_dw, sc_w_out, mla_w_dqkv, mla_q_g, mla_kv_g, mla_w_uq, mla_w_uk, mla_w_uv, mla_w_o, moe_router, moe_bias, moe_w1, moe_w3, moe_w2, sh_w1, sh_w3, sh_w2):
    nb, seq, d = x.shape
    l_ctx = ctx.shape[1]
    depth = ada_w.shape[0]
    alpha = (2.0 * depth) ** 0.25
    tl = SEQ_TILE
    assert seq % tl == 0 and l_ctx % tl == 0 and seq % Q_TILE == 0 and seq % GRID_W == 0
    ntx = seq // tl
    nt_all = (seq + l_ctx) // tl
    attn_layers = [i for i in range(depth) if i % N_MIXERS == 2]
    last_ctx_reader = attn_layers[-1] if attn_layers else -1

    rows = -(-(nb + 1) // 8) * 8
    c_all = jnp.zeros((rows, d), F32).at[:nb].set(c).at[nb].set(c_ctx)
    mod_all = _modulation(c_all, ada_w, ada_b).reshape(depth, rows, N_MOD, d)

    x_all = jnp.concatenate([x, ctx], axis=1)
    q_rank, kv_rank = mla_q_g.shape[1], mla_kv_g.shape[1]
    cos_t, sin_t = _rope_tables(seq)

    for i in range(depth):
        need_ctx = i < last_ctx_reader
        kind, j = i % N_MIXERS, i // N_MIXERS
        nt = nt_all if need_ctx else ntx
        mod = mod_all[i]
        row = lambda v: v.reshape(1, -1)

        if kind == 0:
            u = _conf_in(x_all, mod, conf_w1[j].astype(BF16), row(conf_b1[j]), nt, ntx)
            a = _conf_conv(u, conf_dw[j], row(conf_dwb[j]), row(conf_ng[j]), row(conf_nb[j]), nt, ntx)
            w_last, b_last = conf_w2[j].astype(BF16), row(conf_b2[j])
        elif kind == 1:
            gb, p = _sc_in(x_all, mod, sc_w_in[j].astype(BF16), nt, ntx)
            a = _sc_conv(p, gb, sc_dw[j], nt, ntx)
            w_last, b_last = sc_w_out[j].astype(BF16), jnp.zeros((1, d), F32)
        else:
            wdq = mla_w_dqkv[j]
            wd = wdq[:, :q_rank + kv_rank].astype(BF16)
            wkp = _rope_slot_weights(wdq[:, None, q_rank + kv_rank:]).astype(BF16)
            wuq = mla_w_uq[j].reshape(q_rank, MLA_HEADS, QK_NOPE + QK_ROPE)
            wqn = wuq[:, :, :QK_NOPE].reshape(q_rank, MLA_HEADS * QK_NOPE).astype(BF16)
            wqr = _rope_slot_weights(wuq[:, :, QK_NOPE:]).astype(BF16)
            nt_kv = nt_all
            q, k, v = _mla_proj(x_all, mod, wd, wkp, row(mla_q_g[j]), row(mla_kv_g[j]), wqn, wqr,
                                mla_w_uk[j].astype(BF16), mla_w_uv[j].astype(BF16), cos_t, sin_t, nt_kv, ntx)
            a = _attention(q, k, v, nt * tl)
            w_last, b_last = mla_w_o[j].astype(BF16), jnp.zeros((1, d), F32)

        rw_hi, rw_lo = _split_bf16(moe_router[i].T)
        xn, h2, topi, topw, mask = _post(alpha, a, w_last, b_last, x_all, mod, row(ln_g[i, 0]), row(ln_b[i, 0]),
                                         rw_hi, rw_lo, moe_bias[i].reshape(-1, 1), nt, ntx)

        dest, block_expert, n_used, n_rows = _dispatch_plan(mask, topi)
        n_tok = nb * nt * tl
        tok_ids = jnp.tile(jnp.arange(n_tok, dtype=jnp.int32), TOP_K)
        tok_buf = jnp.zeros((n_rows,), jnp.int32).at[dest.reshape(-1)].set(tok_ids)
        xs = h2.reshape(n_tok, d)[tok_buf]
        w13 = jnp.concatenate([moe_w1[i], moe_w3[i]], axis=-1).astype(BF16)
        ys = _grouped_ffn(block_expert, n_used, xs, w13, moe_w2[i].astype(BF16))
        routed = jnp.sum(ys[dest].astype(F32) * topw[:, :, None], axis=0).reshape(nb, nt * tl, d)

        ws13 = jnp.concatenate([sh_w1[i], sh_w3[i]], axis=-1).astype(BF16)
        x_all = _moe_out(alpha, xn, h2, routed, mod, ws13, sh_w2[i].astype(BF16),
                         row(ln_g[i, 1]), row(ln_b[i, 1]), nt, ntx)
    return x_all[:, :seq]
```

```python
import functools

import jax
import jax.numpy as jnp
from jax import lax
from jax.experimental import pallas as pl
from jax.experimental.pallas import tpu as pltpu

F32 = jnp.float32
BF16 = jnp.bfloat16

GRID_W = 64
N_MIXERS = 3
LN_EPS = 1e-5
RMS_EPS = 1e-6
N_MOD = 6
MLA_HEADS = 8
QK_NOPE = 128
QK_ROPE = 64
V_HEAD = 128
ROPE_THETA = 10000.0
ATTN_SCALE = (QK_NOPE + QK_ROPE) ** -0.5
TOP_K = 8
N_GROUPS = 8
TOPK_GROUPS = 4
ROUTED_SCALE = 2.5

SEQ_TILE = 256
CONV_HALO = 16
SHORT_HALO = 8
CONV_ROW_CHUNK = 64
LANE = 128
Q_TILE = 512
MOE_BLOCK_ROWS = 256
MOD_COL_TILE = 1536
VMEM_LIMIT = 48 * 1024 * 1024


def _params(n_axes):
    return pltpu.CompilerParams(dimension_semantics=("arbitrary",) * n_axes,
                                vmem_limit_bytes=VMEM_LIMIT)


def _split_bf16(a):
    hi = a.astype(BF16)
    lo = (a - hi.astype(F32)).astype(BF16)
    return hi, lo


def _dot(a, b):
    return jnp.dot(a, b, preferred_element_type=F32)


def _dot_nt(a, b):
    return lax.dot_general(a, b, (((1,), (1,)), ((), ())), preferred_element_type=F32)


def _layer_norm(v, g, b):
    mu = jnp.mean(v, axis=-1, keepdims=True)
    c = v - mu
    var = jnp.mean(c * c, axis=-1, keepdims=True)
    return c * lax.rsqrt(var + LN_EPS) * g + b


def _silu(v):
    return v * jax.nn.sigmoid(v)


def _mod_kernel(c_ref, w_ref, b_ref, o_ref):
    a = _silu(c_ref[...])
    a_hi, a_lo = _split_bf16(a)
    w_hi, w_lo = _split_bf16(w_ref[0])
    o_ref[0] = _dot(a_hi, w_hi) + _dot(a_hi, w_lo) + _dot(a_lo, w_hi) + b_ref[0]


def _modulation(c_all, ada_w, ada_b):
    depth, d, n = ada_w.shape
    rows = c_all.shape[0]
    tn = MOD_COL_TILE
    return pl.pallas_call(
        _mod_kernel,
        out_shape=jax.ShapeDtypeStruct((depth, rows, n), F32),
        grid=(depth, n // tn),
        in_specs=[pl.BlockSpec((rows, d), lambda i, j: (0, 0)),
                  pl.BlockSpec((1, d, tn), lambda i, j: (i, 0, j)),
                  pl.BlockSpec((1, 1, tn), lambda i, j: (i, 0, j))],
        out_specs=pl.BlockSpec((1, rows, tn), lambda i, j: (i, 0, j)),
        compiler_params=_params(2),
        name="adaln_mod",
    )(c_all, ada_w, ada_b.reshape(depth, 1, n))


def _tok_spec(d, tl=SEQ_TILE):
    return pl.BlockSpec((1, tl, d), lambda b, l: (b, l, 0))


def _mod_spec(d, n_batch, ntx):
    return pl.BlockSpec((1, N_MOD, d), lambda b, l: (jnp.where(l < ntx, b, n_batch), 0, 0))


def _full_spec(shape):
    zeros = (0,) * len(shape)
    return pl.BlockSpec(shape, lambda b, l: zeros)


def _halo_specs(d, halo, seq_len, tl=SEQ_TILE):
    per_tile = tl // halo
    last = seq_len // halo - 1
    prev = pl.BlockSpec((1, halo, d), lambda b, l: (b, jnp.maximum(l * per_tile - 1, 0), 0))
    nxt = pl.BlockSpec((1, halo, d), lambda b, l: (b, jnp.minimum((l + 1) * per_tile, last), 0))
    return prev, nxt


def _segment_edges(l, ntx, nt):
    first = jnp.logical_or(l == 0, l == ntx)
    last = jnp.logical_or(l == ntx - 1, l == nt - 1)
    return first, last


def _conf_in_kernel(x_ref, mod_ref, w1_ref, b1_ref, u_ref):
    d = x_ref.shape[-1]
    h = x_ref[0] * (1.0 + mod_ref[0, 1:2, :]) + mod_ref[0, 0:1, :]
    z = _dot(h.astype(BF16), w1_ref[...]) + b1_ref[...]
    u_ref[0] = z[:, :d] * jax.nn.sigmoid(z[:, d:])


def _conf_in(x_all, mod, w1, b1, nt, ntx):
    nb, seq_len, d = x_all.shape
    return pl.pallas_call(
        _conf_in_kernel,
        out_shape=jax.ShapeDtypeStruct((nb, nt * SEQ_TILE, d), F32),
        grid=(nb, nt),
        in_specs=[_tok_spec(d), _mod_spec(d, nb, ntx), _full_spec(w1.shape), _full_spec(b1.shape)],
        out_specs=_tok_spec(d),
        compiler_params=_params(2),
        name="conf_in",
    )(x_all, mod, w1, b1)


def _conf_conv_kernel(ntx, nt, up_ref, uc_ref, un_ref, dw_ref, dwb_ref, ng_ref, nb_ref,
                      a_ref, ext_ref, conv_ref):
    tl, d = uc_ref.shape[1], uc_ref.shape[2]
    taps = dw_ref.shape[0]
    lead = CONV_HALO - (taps - 1) // 2
    first, last = _segment_edges(pl.program_id(1), ntx, nt)
    ext_ref[0:CONV_HALO, :] = jnp.where(first, 0.0, up_ref[0])
    ext_ref[CONV_HALO:CONV_HALO + tl, :] = uc_ref[0]
    ext_ref[CONV_HALO + tl:, :] = jnp.where(last, 0.0, un_ref[0])
    for r0 in range(0, tl, CONV_ROW_CHUNK):
        for c0 in range(0, d, LANE):
            acc = jnp.zeros((CONV_ROW_CHUNK, LANE), F32)
            for k in range(taps):
                acc = acc + dw_ref[k:k + 1, c0:c0 + LANE] * ext_ref[r0 + lead + k:r0 + lead + k + CONV_ROW_CHUNK, c0:c0 + LANE]
            conv_ref[r0:r0 + CONV_ROW_CHUNK, c0:c0 + LANE] = acc
    v = _layer_norm(conv_ref[...] + dwb_ref[...], ng_ref[...], nb_ref[...])
    a_ref[0] = _silu(v).astype(BF16)


def _conf_conv(u, dw, dwb, ng, nb_, nt, ntx):
    nb, seq_len, d = u.shape
    prev, nxt = _halo_specs(d, CONV_HALO, seq_len)
    return pl.pallas_call(
        functools.partial(_conf_conv_kernel, ntx, nt),
        out_shape=jax.ShapeDtypeStruct((nb, seq_len, d), BF16),
        grid=(nb, nt),
        in_specs=[prev, _tok_spec(d), nxt, _full_spec(dw.shape), _full_spec(dwb.shape),
                  _full_spec(ng.shape), _full_spec(nb_.shape)],
        out_specs=_tok_spec(d),
        scratch_shapes=[pltpu.VMEM((SEQ_TILE + 2 * CONV_HALO, d), F32),
                        pltpu.VMEM((SEQ_TILE, d), F32)],
        compiler_params=_params(2),
        name="conf_conv",
    )(u, u, u, dw, dwb, ng, nb_)


def _sc_in_kernel(x_ref, mod_ref, w_ref, gb_ref, p_ref):
    d = x_ref.shape[-1]
    h = x_ref[0] * (1.0 + mod_ref[0, 1:2, :]) + mod_ref[0, 0:1, :]
    z = _dot(h.astype(BF16), w_ref[...])
    gb_ref[0] = z[:, :d]
    p_ref[0] = z[:, d:2 * d] * z[:, 2 * d:]


def _sc_in(x_all, mod, w_in, nt, ntx):
    nb, seq_len, d = x_all.shape
    shp = jax.ShapeDtypeStruct((nb, nt * SEQ_TILE, d), F32)
    return pl.pallas_call(
        _sc_in_kernel,
        out_shape=(shp, shp),
        grid=(nb, nt),
        in_specs=[_tok_spec(d), _mod_spec(d, nb, ntx), _full_spec(w_in.shape)],
        out_specs=(_tok_spec(d), _tok_spec(d)),
        compiler_params=_params(2),
        name="sc_in",
    )(x_all, mod, w_in)


def _sc_conv_kernel(ntx, nt, pp_ref, pc_ref, pn_ref, gb_ref, dw_ref, a_ref, ext_ref):
    tl = pc_ref.shape[1]
    taps = dw_ref.shape[0]
    lead = SHORT_HALO - (taps - 1) // 2
    first, last = _segment_edges(pl.program_id(1), ntx, nt)
    ext_ref[0:SHORT_HALO, :] = jnp.where(first, 0.0, pp_ref[0])
    ext_ref[SHORT_HALO:SHORT_HALO + tl, :] = pc_ref[0]
    ext_ref[SHORT_HALO + tl:, :] = jnp.where(last, 0.0, pn_ref[0])
    acc = dw_ref[0:1, :] * ext_ref[lead:lead + tl, :]
    for k in range(1, taps):
        acc = acc + dw_ref[k:k + 1, :] * ext_ref[lead + k:lead + k + tl, :]
    a_ref[0] = (gb_ref[0] * acc).astype(BF16)


def _sc_conv(p, gb, dw, nt, ntx):
    nb, seq_len, d = p.shape
    prev, nxt = _halo_specs(d, SHORT_HALO, seq_len)
    return pl.pallas_call(
        functools.partial(_sc_conv_kernel, ntx, nt),
        out_shape=jax.ShapeDtypeStruct((nb, seq_len, d), BF16),
        grid=(nb, nt),
        in_specs=[prev, _tok_spec(d), nxt, _tok_spec(d), _full_spec(dw.shape)],
        out_specs=_tok_spec(d),
        scratch_shapes=[pltpu.VMEM((SEQ_TILE + 2 * SHORT_HALO, d), F32)],
        compiler_params=_params(2),
        name="sc_conv",
    )(p, p, p, gb, dw)


def _rms(v, g):
    return v * lax.rsqrt(jnp.mean(v * v, axis=-1, keepdims=True) + RMS_EPS) * g


def _mla_proj_kernel(ntx, q_rank,
                     x_ref, mod_ref, wd_ref, wkp_ref, qg_ref, kvg_ref, wqn_ref, wqr_ref, wuk_ref, wuv_ref,
                     cos_ref, sin_ref, q_ref, k_ref, v_ref):
    is_latent = pl.program_id(1) < ntx
    h = (x_ref[0] * (1.0 + mod_ref[0, 1:2, :]) + mod_ref[0, 0:1, :]).astype(BF16)
    dn = _dot(h, wd_ref[...])
    cq = _rms(dn[:, :q_rank], qg_ref[...]).astype(BF16)
    ckv = _rms(dn[:, q_rank:], kvg_ref[...]).astype(BF16)
    cos = jnp.where(is_latent, cos_ref[...], 1.0)
    sin = jnp.where(is_latent, sin_ref[...], 0.0)
    kp2 = _dot(h, wkp_ref[...])
    kp = kp2[:, :LANE] * cos + kp2[:, LANE:] * sin
    kn = _dot(ckv, wuk_ref[...])
    vv = _dot(ckv, wuv_ref[...])
    qn = _dot(cq, wqn_ref[...]) * ATTN_SCALE
    qr2 = _dot(cq, wqr_ref[...])
    hw = MLA_HEADS * LANE
    for hd in range(MLA_HEADS):
        sl = slice(hd * LANE, (hd + 1) * LANE)
        qr = (qr2[:, sl] * cos + qr2[:, hw + hd * LANE:hw + (hd + 1) * LANE] * sin) * ATTN_SCALE
        q_ref[0, hd, :, 0:LANE] = qn[:, sl].astype(BF16)
        q_ref[0, hd, :, LANE:] = qr.astype(BF16)
        k_ref[0, hd, :, 0:LANE] = kn[:, sl].astype(BF16)
        k_ref[0, hd, :, LANE:] = kp.astype(BF16)
        v_ref[0, hd] = vv[:, sl].astype(BF16)


def _mla_proj(x_all, mod, wd, wkp, qg, kvg, wqn, wqr, wuk, wuv, cos_t, sin_t, nt, ntx):
    nb, seq_len, d = x_all.shape
    q_rank = qg.shape[-1]
    tl = SEQ_TILE
    rope_spec = pl.BlockSpec((tl, LANE), lambda b, l: (jnp.minimum(l, ntx - 1), 0))
    qk_shape = jax.ShapeDtypeStruct((nb, MLA_HEADS, nt * tl, 2 * LANE), BF16)
    v_shape = jax.ShapeDtypeStruct((nb, MLA_HEADS, nt * tl, LANE), BF16)
    qk_spec = pl.BlockSpec((1, MLA_HEADS, tl, 2 * LANE), lambda b, l: (b, 0, l, 0))
    v_spec = pl.BlockSpec((1, MLA_HEADS, tl, LANE), lambda b, l: (b, 0, l, 0))
    return pl.pallas_call(
        functools.partial(_mla_proj_kernel, ntx, q_rank),
        out_shape=(qk_shape, qk_shape, v_shape),
        grid=(nb, nt),
        in_specs=[_tok_spec(d), _mod_spec(d, nb, ntx), _full_spec(wd.shape), _full_spec(wkp.shape),
                  _full_spec(qg.shape), _full_spec(kvg.shape), _full_spec(wqn.shape), _full_spec(wqr.shape),
                  _full_spec(wuk.shape), _full_spec(wuv.shape), rope_spec, rope_spec],
        out_specs=(qk_spec, qk_spec, v_spec),
        compiler_params=_params(2),
        name="mla_proj",
    )(x_all, mod, wd, wkp, qg, kvg, wqn, wqr, wuk, wuv, cos_t, sin_t)


def _attn_kernel(q_ref, k_ref, v_ref, o_ref):
    s = _dot_nt(q_ref[0, 0], k_ref[0, 0])
    m = jnp.max(s, axis=-1, keepdims=True)
    p = jnp.exp(s - m)
    denom = jnp.sum(p, axis=-1, keepdims=True)
    o = _dot(p.astype(BF16), v_ref[0, 0])
    o_ref[0] = (o / denom).astype(BF16)


def _attention(q, k, v, n_q):
    nb, nh, n_k, dk = k.shape
    tq = Q_TILE
    return pl.pallas_call(
        _attn_kernel,
        out_shape=jax.ShapeDtypeStruct((nb, n_q, nh * V_HEAD), BF16),
        grid=(nb, nh, n_q // tq),
        in_specs=[pl.BlockSpec((1, 1, tq, dk), lambda b, h, i: (b, h, i, 0)),
                  pl.BlockSpec((1, 1, n_k, dk), lambda b, h, i: (b, h, 0, 0)),
                  pl.BlockSpec((1, 1, n_k, V_HEAD), lambda b, h, i: (b, h, 0, 0))],
        out_specs=pl.BlockSpec((1, tq, V_HEAD), lambda b, h, i: (b, i, h)),
        compiler_params=_params(3),
        name="mla_attn",
    )(q, k, v)


def _route(sel, s):
    n_e, n_t = sel.shape
    per = n_e // N_GROUPS
    sel3 = sel.reshape(N_GROUPS, per, n_t)
    s3 = s.reshape(N_GROUPS, per, n_t)
    iota_p = lax.broadcasted_iota(jnp.int32, (N_GROUPS, per, n_t), 1).astype(F32)
    iota_g = lax.broadcasted_iota(jnp.int32, (N_GROUPS, 1, n_t), 0).astype(F32)
    neg = -jnp.inf
    m1 = jnp.max(sel3, axis=1, keepdims=True)
    first = jnp.min(jnp.where(sel3 == m1, iota_p, float(per)), axis=1, keepdims=True)
    m2 = jnp.max(jnp.where(iota_p == first, neg, sel3), axis=1, keepdims=True)
    gs = m1 + m2
    gsel = jnp.zeros((N_GROUPS, 1, n_t), F32)
    for _ in range(TOPK_GROUPS):
        gm = jnp.max(gs, axis=0, keepdims=True)
        gfirst = jnp.min(jnp.where(gs == gm, iota_g, float(N_GROUPS)), axis=0, keepdims=True)
        pick = iota_g == gfirst
        gsel = jnp.where(pick, 1.0, gsel)
        gs = jnp.where(pick, neg, gs)
    val = jnp.where(gsel > 0.0, sel3, neg)
    iota_e = lax.broadcasted_iota(jnp.int32, (N_GROUPS, per, n_t), 0).astype(F32) * per + iota_p
    chosen = jnp.zeros((N_GROUPS, per, n_t), F32)
    idx, wts = [], []
    for _ in range(TOP_K):
        m = jnp.max(jnp.max(val, axis=1, keepdims=True), axis=0, keepdims=True)
        e = jnp.min(jnp.min(jnp.where(val == m, iota_e, float(n_e)), axis=1, keepdims=True), axis=0, keepdims=True)
        pick = iota_e == e
        wts.append(jnp.sum(jnp.sum(jnp.where(pick, s3, 0.0), axis=1, keepdims=True), axis=0, keepdims=True))
        idx.append(e)
        chosen = jnp.where(pick, 1.0, chosen)
        val = jnp.where(pick, neg, val)
    total = wts[0]
    for w in wts[1:]:
        total = total + w
    wts = [w / total * ROUTED_SCALE for w in wts]
    return idx, wts, chosen, iota_e


def _post_kernel(alpha, a_ref, w_ref, b_ref, x_ref, mod_ref, lng_ref, lnb_ref, rw_hi_ref, rw_lo_ref, rb_ref, tri_ref,
                 xn_ref, h2_ref, topi_ref, topw_ref, rank_ref, cnt_ref, run_ref):
    first_step = jnp.logical_and(pl.program_id(0) == 0, pl.program_id(1) == 0)

    @pl.when(first_step)
    def _():
        run_ref[...] = jnp.zeros_like(run_ref)

    y = _dot(a_ref[0], w_ref[...]) + b_ref[...]
    xn = _layer_norm(alpha * x_ref[0] + mod_ref[0, 2:3, :] * y, lng_ref[...], lnb_ref[...])
    xn_ref[0] = xn
    h2 = xn * (1.0 + mod_ref[0, 4:5, :]) + mod_ref[0, 3:4, :]
    h_hi, h_lo = _split_bf16(h2)
    h2_ref[0] = h_hi
    logits = _dot_nt(rw_hi_ref[...], h_hi) + _dot_nt(rw_hi_ref[...], h_lo) + _dot_nt(rw_lo_ref[...], h_hi)
    s = jax.nn.sigmoid(logits)
    idx, wts, chosen3, iota_e = _route(s + rb_ref[...], s)
    n_e, n_t = s.shape
    chosen = chosen3.reshape(n_e, n_t)
    rank = run_ref[:, 0:1] + _dot(chosen.astype(BF16), tri_ref[...])
    rank3 = rank.reshape(chosen3.shape)
    for k in range(TOP_K):
        rk = jnp.sum(jnp.sum(jnp.where(iota_e == idx[k], rank3, 0.0), axis=1, keepdims=True), axis=0, keepdims=True)
        topi_ref[k:k + 1, :] = idx[k].reshape(1, n_t).astype(jnp.int32)
        topw_ref[k:k + 1, :] = wts[k].reshape(1, n_t)
        rank_ref[k:k + 1, :] = rk.reshape(1, n_t).astype(jnp.int32)
    run_ref[...] = run_ref[...] + jnp.sum(chosen, axis=1, keepdims=True)
    cnt_ref[...] = run_ref[...].astype(jnp.int32)


def _post(alpha, a, w, bias, x_all, mod, lng, lnb, rw_hi, rw_lo, rb, nt, ntx):
    nb, _, dk = a.shape
    d = x_all.shape[-1]
    n_e = rw_hi.shape[0]
    tl = SEQ_TILE
    n_tok = nb * nt * tl
    col_spec = lambda rows: pl.BlockSpec((rows, tl), lambda b, l: (0, b * nt + l))
    row_i = lax.broadcasted_iota(jnp.int32, (tl, tl), 0)
    col_i = lax.broadcasted_iota(jnp.int32, (tl, tl), 1)
    tri = jnp.where(row_i < col_i, 1.0, 0.0).astype(BF16)
    return pl.pallas_call(
        functools.partial(_post_kernel, alpha),
        out_shape=(jax.ShapeDtypeStruct((nb, nt * tl, d), F32),
                   jax.ShapeDtypeStruct((nb, nt * tl, d), BF16),
                   jax.ShapeDtypeStruct((TOP_K, n_tok), jnp.int32),
                   jax.ShapeDtypeStruct((TOP_K, n_tok), F32),
                   jax.ShapeDtypeStruct((TOP_K, n_tok), jnp.int32),
                   jax.ShapeDtypeStruct((n_e, LANE), jnp.int32)),
        grid=(nb, nt),
        in_specs=[_tok_spec(dk), _full_spec(w.shape), _full_spec(bias.shape), _tok_spec(d),
                  _mod_spec(d, nb, ntx), _full_spec(lng.shape), _full_spec(lnb.shape),
                  _full_spec(rw_hi.shape), _full_spec(rw_lo.shape), _full_spec(rb.shape), _full_spec(tri.shape)],
        out_specs=(_tok_spec(d), _tok_spec(d), col_spec(TOP_K), col_spec(TOP_K), col_spec(TOP_K),
                   _full_spec((n_e, LANE))),
        scratch_shapes=[pltpu.VMEM((n_e, LANE), F32)],
        compiler_params=_params(2),
        name="mixer_post",
    )(a, w, bias, x_all, mod, lng, lnb, rw_hi, rw_lo, rb, tri)


def _gmm_kernel(be_ref, nu_ref, xs_ref, w1_ref, w3_ref, w2_ref, ys_ref, w13_s, w2_s):
    i = pl.program_id(0)
    de = w2_ref.shape[1]
    changed = jnp.logical_or(i == 0, be_ref[i] != be_ref[jnp.maximum(i - 1, 0)])

    @pl.when(changed)
    def _():
        w13_s[:, :de] = w1_ref[0].astype(BF16)
        w13_s[:, de:] = w3_ref[0].astype(BF16)
        w2_s[...] = w2_ref[0].astype(BF16)

    @pl.when(i < nu_ref[0])
    def _():
        z = _dot(xs_ref[...], w13_s[...])
        hmid = (_silu(z[:, :de]) * z[:, de:]).astype(BF16)
        ys_ref[...] = _dot(hmid, w2_s[...]).astype(BF16)


def _grouped_ffn(block_expert, n_used, xs, w1, w3, w2):
    n_rows, d = xs.shape
    bm = MOE_BLOCK_ROWS
    de = w2.shape[1]
    grid_spec = pltpu.PrefetchScalarGridSpec(
        num_scalar_prefetch=2,
        grid=(n_rows // bm,),
        in_specs=[pl.BlockSpec((bm, d), lambda i, be, nu: (i, 0)),
                  pl.BlockSpec((1, d, de), lambda i, be, nu: (be[i], 0, 0)),
                  pl.BlockSpec((1, d, de), lambda i, be, nu: (be[i], 0, 0)),
                  pl.BlockSpec((1, de, d), lambda i, be, nu: (be[i], 0, 0))],
        out_specs=pl.BlockSpec((bm, d), lambda i, be, nu: (i, 0)),
        scratch_shapes=[pltpu.VMEM((d, 2 * de), BF16), pltpu.VMEM((de, d), BF16)],
    )
    return pl.pallas_call(
        _gmm_kernel,
        out_shape=jax.ShapeDtypeStruct((n_rows, d), BF16),
        grid_spec=grid_spec,
        compiler_params=_params(1),
        name="moe_grouped_ffn",
    )(block_expert, n_used, xs, w1, w3, w2)


def _moe_out_kernel(alpha, x_ref, h2_ref, yg_ref, tw_ref, mod_ref, ws13_ref, ws2_ref, lng_ref, lnb_ref, o_ref):
    de = ws2_ref.shape[0]
    z = _dot(h2_ref[0], ws13_ref[...])
    hmid = (_silu(z[:, :de]) * z[:, de:]).astype(BF16)
    y = _dot(hmid, ws2_ref[...])
    for k in range(yg_ref.shape[0]):
        y = y + yg_ref[k].astype(F32) * tw_ref[:, k:k + 1]
    o_ref[0] = _layer_norm(alpha * x_ref[0] + mod_ref[0, 5:6, :] * y, lng_ref[...], lnb_ref[...])


def _moe_out(alpha, xn, h2, y_rows, topw_t, mod, ws13, ws2, lng, lnb, nt, ntx):
    nb, _, d = xn.shape
    n_k = y_rows.shape[0]
    tl = SEQ_TILE
    return pl.pallas_call(
        functools.partial(_moe_out_kernel, alpha),
        out_shape=jax.ShapeDtypeStruct((nb, nt * tl, d), F32),
        grid=(nb, nt),
        in_specs=[_tok_spec(d), _tok_spec(d),
                  pl.BlockSpec((n_k, tl, d), lambda b, l: (0, b * nt + l, 0)),
                  pl.BlockSpec((tl, n_k), lambda b, l: (b * nt + l, 0)),
                  _mod_spec(d, nb, ntx),
                  _full_spec(ws13.shape), _full_spec(ws2.shape), _full_spec(lng.shape), _full_spec(lnb.shape)],
        out_specs=_tok_spec(d),
        compiler_params=_params(2),
        name="moe_out",
    )(xn, h2, y_rows, topw_t, mod, ws13, ws2, lng, lnb)


def _dispatch_plan(counts, topi, rank, n_tok):
    n_e = counts.shape[0]
    bm = MOE_BLOCK_ROWS
    padded = (counts + bm - 1) // bm * bm
    pad_end = jnp.cumsum(padded)
    pad_start = pad_end - padded
    onehot = topi[:, None, :] == jnp.arange(n_e, dtype=jnp.int32)[None, :, None]
    dest = rank + jnp.sum(jnp.where(onehot, pad_start[None, :, None], 0), axis=1)
    n_blocks = n_tok * TOP_K // bm + n_e
    block_start = jnp.arange(n_blocks, dtype=jnp.int32) * bm
    block_expert = jnp.minimum(jnp.searchsorted(pad_end, block_start, side="right"), n_e - 1).astype(jnp.int32)
    n_used = (pad_end[-1] // bm).astype(jnp.int32).reshape(1)
    return dest.astype(jnp.int32), block_expert, n_used, n_blocks * bm


def _rope_tables(seq):
    n_freq = QK_ROPE // 4
    inv_freq = ROPE_THETA ** (-jnp.arange(n_freq, dtype=F32) / n_freq)
    pos = jnp.arange(seq, dtype=jnp.int32)
    r = (pos // GRID_W).astype(F32)
    col = (pos % GRID_W).astype(F32)
    ang = jnp.concatenate([r[:, None] * inv_freq, col[:, None] * inv_freq], -1)
    cos, sin = jnp.cos(ang), jnp.sin(ang)
    zeros = jnp.zeros((seq, LANE - QK_ROPE), F32)
    cos_slot = jnp.concatenate([cos, cos, zeros], -1)
    sin_slot = jnp.concatenate([-sin, sin, zeros], -1)
    return cos_slot, sin_slot


def _rope_slot_weights(w_rope):
    k, n, _ = w_rope.shape
    half = QK_ROPE // 2
    swapped = jnp.concatenate([w_rope[..., half:], w_rope[..., :half]], -1)
    pad = jnp.zeros((k, n, LANE - QK_ROPE), w_rope.dtype)
    plain = jnp.concatenate([w_rope, pad], -1).reshape(k, n * LANE)
    swp = jnp.concatenate([swapped, pad], -1).reshape(k, n * LANE)
    return jnp.concatenate([plain, swp], -1)


def kernel(x, c, ctx, c_ctx, ada_w, ada_b, ln_g, ln_b, conf_w1, conf_b1, conf_dw, conf_dwb, conf_ng, conf_nb, conf_w2, conf_b2, sc_w_in, sc_dw, sc_w_out, mla_w_dqkv, mla_q_g, mla_kv_g, mla_w_uq, mla_w_uk, mla_w_uv, mla_w_o, moe_router, moe_bias, moe_w1, moe_w3, moe_w2, sh_w1, sh_w3, sh_w2):
    nb, seq, d = x.shape
    l_ctx = ctx.shape[1]
    depth = ada_w.shape[0]
    alpha = (2.0 * depth) ** 0.25
    tl = SEQ_TILE
    assert seq % tl == 0 and l_ctx % tl == 0 and seq % Q_TILE == 0 and seq % GRID_W == 0
    ntx = seq // tl
    nt_all = (seq + l_ctx) // tl
    attn_layers = [i for i in range(depth) if i % N_MIXERS == 2]
    last_ctx_reader = attn_layers[-1] if attn_layers else -1

    rows = -(-(nb + 1) // 8) * 8
    c_all = jnp.zeros((rows, d), F32).at[:nb].set(c).at[nb].set(c_ctx)
    mod_all = _modulation(c_all, ada_w, ada_b).reshape(depth, rows, N_MOD, d)

    x_all = jnp.concatenate([x, ctx], axis=1)
    q_rank, kv_rank = mla_q_g.shape[1], mla_kv_g.shape[1]
    cos_t, sin_t = _rope_tables(seq)

    for i in range(depth):
        need_ctx = i < last_ctx_reader
        kind, j = i % N_MIXERS, i // N_MIXERS
        nt = nt_all if need_ctx else ntx
        mod = mod_all[i]
        row = lambda v: v.reshape(1, -1)

        if kind == 0:
            u = _conf_in(x_all, mod, conf_w1[j].astype(BF16), row(conf_b1[j]), nt, ntx)
            a = _conf_conv(u, conf_dw[j], row(conf_dwb[j]), row(conf_ng[j]), row(conf_nb[j]), nt, ntx)
            w_last, b_last = conf_w2[j].astype(BF16), row(conf_b2[j])
        elif kind == 1:
            gb, p = _sc_in(x_all, mod, sc_w_in[j].astype(BF16), nt, ntx)
            a = _sc_conv(p, gb, sc_dw[j], nt, ntx)
            w_last, b_last = sc_w_out[j].astype(BF16), jnp.zeros((1, d), F32)
        else:
            wdq = mla_w_dqkv[j]
            wd = wdq[:, :q_rank + kv_rank].astype(BF16)
            wkp = _rope_slot_weights(wdq[:, None, q_rank + kv_rank:]).astype(BF16)
            wuq = mla_w_uq[j].reshape(q_rank, MLA_HEADS, QK_NOPE + QK_ROPE)
            wqn = wuq[:, :, :QK_NOPE].reshape(q_rank, MLA_HEADS * QK_NOPE).astype(BF16)
            wqr = _rope_slot_weights(wuq[:, :, QK_NOPE:]).astype(BF16)
            nt_kv = nt_all
            q, k, v = _mla_proj(x_all, mod, wd, wkp, row(mla_q_g[j]), row(mla_kv_g[j]), wqn, wqr,
                                mla_w_uk[j].astype(BF16), mla_w_uv[j].astype(BF16), cos_t, sin_t, nt_kv, ntx)
            a = _attention(q, k, v, nt * tl)
            w_last, b_last = mla_w_o[j].astype(BF16), jnp.zeros((1, d), F32)

        rw_hi, rw_lo = _split_bf16(moe_router[i].T)
        xn, h2, topi, topw, rank, counts = _post(alpha, a, w_last, b_last, x_all, mod, row(ln_g[i, 0]),
                                                 row(ln_b[i, 0]), rw_hi, rw_lo, moe_bias[i].reshape(-1, 1), nt, ntx)

        n_tok = nb * nt * tl
        dest, block_expert, n_used, n_rows = _dispatch_plan(counts[:, 0], topi, rank, n_tok)
        tok_ids = jnp.tile(jnp.arange(n_tok, dtype=jnp.int32), TOP_K)
        tok_buf = jnp.zeros((n_rows,), jnp.int32).at[dest.reshape(-1)].set(tok_ids)
        xs = h2.reshape(n_tok, d)[tok_buf]
        ys = _grouped_ffn(block_expert, n_used, xs, moe_w1[i], moe_w3[i], moe_w2[i])
        y_rows = ys[dest]

        ws13 = jnp.concatenate([sh_w1[i], sh_w3[i]], axis=-1).astype(BF16)
        x_all = _moe_out(alpha, xn, h2, y_rows, topw.T, mod, ws13, sh_w2[i].astype(BF16),
                         row(ln_g[i, 1]), row(ln_b[i, 1]), nt, ntx)
    return x_all[:, :seq]
```

```python
import functools

import numpy as np
import jax
import jax.numpy as jnp
from jax import lax
from jax.experimental import pallas as pl
from jax.experimental.pallas import tpu as pltpu
from jax.experimental.pallas import tpu_sc as plsc

F32 = jnp.float32
BF16 = jnp.bfloat16
U32 = jnp.uint32
HIGH_HALF_MASK = np.uint32(0xFFFF0000)

GRID_W = 64
N_MIXERS = 3
LN_EPS = 1e-5
RMS_EPS = 1e-6
N_MOD = 6
MLA_HEADS = 8
QK_NOPE = 128
QK_ROPE = 64
V_HEAD = 128
ROPE_THETA = 10000.0
ATTN_SCALE = (QK_NOPE + QK_ROPE) ** -0.5
TOP_K = 8
N_GROUPS = 8
TOPK_GROUPS = 4
ROUTED_SCALE = 2.5

SEQ_TILE = 256
CONV_HALO = 16
SHORT_HALO = 8
CONV_ROW_CHUNK = 64
LANE = 128
Q_TILE = 512
MOE_BLOCK_ROWS = 512
MOD_COL_TILE = 1536
VMEM_LIMIT = 48 * 1024 * 1024
SC_CORES = 2
SC_SUBCORES = 16
SC_WORKERS = SC_CORES * SC_SUBCORES
SC_WINDOW = 64


def _params(n_axes):
    return pltpu.CompilerParams(dimension_semantics=("arbitrary",) * n_axes,
                                vmem_limit_bytes=VMEM_LIMIT)


def _split_bf16(a):
    hi = a.astype(BF16)
    lo = (a - hi.astype(F32)).astype(BF16)
    return hi, lo


def _dot(a, b):
    return jnp.dot(a, b, preferred_element_type=F32)


def _dot_nt(a, b):
    return lax.dot_general(a, b, (((1,), (1,)), ((), ())), preferred_element_type=F32)


def _pack_halves(v):
    half = v.shape[-1] // 2
    lo = lax.bitcast_convert_type(v[:, :half].astype(BF16).astype(F32), U32) >> 16
    hi = lax.bitcast_convert_type(v[:, half:].astype(BF16).astype(F32), U32) & HIGH_HALF_MASK
    return hi | lo


def _unpack_halves(p):
    lo = lax.bitcast_convert_type(p << 16, F32)
    hi = lax.bitcast_convert_type(p & HIGH_HALF_MASK, F32)
    return lo, hi


def _dot_packed(p, w):
    lo, hi = _unpack_halves(p)
    half = p.shape[-1]
    return _dot(lo.astype(BF16), w[:half]) + _dot(hi.astype(BF16), w[half:])


def _layer_norm(v, g, b):
    mu = jnp.mean(v, axis=-1, keepdims=True)
    c = v - mu
    var = jnp.mean(c * c, axis=-1, keepdims=True)
    return c * lax.rsqrt(var + LN_EPS) * g + b


def _silu(v):
    return v * jax.nn.sigmoid(v)


def _mod_kernel(c_ref, w_ref, b_ref, o_ref):
    a = _silu(c_ref[...])
    a_hi, a_lo = _split_bf16(a)
    w_hi, w_lo = _split_bf16(w_ref[0])
    o_ref[0] = _dot(a_hi, w_hi) + _dot(a_hi, w_lo) + _dot(a_lo, w_hi) + b_ref[0]


def _modulation(c_all, ada_w, ada_b):
    depth, d, n = ada_w.shape
    rows = c_all.shape[0]
    tn = MOD_COL_TILE
    return pl.pallas_call(
        _mod_kernel,
        out_shape=jax.ShapeDtypeStruct((depth, rows, n), F32),
        grid=(depth, n // tn),
        in_specs=[pl.BlockSpec((rows, d), lambda i, j: (0, 0)),
                  pl.BlockSpec((1, d, tn), lambda i, j: (i, 0, j)),
                  pl.BlockSpec((1, 1, tn), lambda i, j: (i, 0, j))],
        out_specs=pl.BlockSpec((1, rows, tn), lambda i, j: (i, 0, j)),
        compiler_params=_params(2),
        name="adaln_mod",
    )(c_all, ada_w, ada_b.reshape(depth, 1, n))


def _tok_spec(d, tl=SEQ_TILE):
    return pl.BlockSpec((1, tl, d), lambda b, l: (b, l, 0))


def _mod_spec(d, n_batch, ntx):
    return pl.BlockSpec((1, N_MOD, d), lambda b, l: (jnp.where(l < ntx, b, n_batch), 0, 0))


def _full_spec(shape):
    zeros = (0,) * len(shape)
    return pl.BlockSpec(shape, lambda b, l: zeros)


def _halo_specs(d, halo, seq_len, tl=SEQ_TILE):
    per_tile = tl // halo
    last = seq_len // halo - 1
    prev = pl.BlockSpec((1, halo, d), lambda b, l: (b, jnp.maximum(l * per_tile - 1, 0), 0))
    nxt = pl.BlockSpec((1, halo, d), lambda b, l: (b, jnp.minimum((l + 1) * per_tile, last), 0))
    return prev, nxt


def _segment_edges(l, ntx, nt):
    first = jnp.logical_or(l == 0, l == ntx)
    last = jnp.logical_or(l == ntx - 1, l == nt - 1)
    return first, last


def _conf_in_kernel(x_ref, mod_ref, w1_ref, b1_ref, u_ref):
    d = x_ref.shape[-1]
    h = x_ref[0] * (1.0 + mod_ref[0, 1:2, :]) + mod_ref[0, 0:1, :]
    z = _dot(h.astype(BF16), w1_ref[...]) + b1_ref[...]
    u_ref[0] = z[:, :d] * jax.nn.sigmoid(z[:, d:])


def _conf_in(x_all, mod, w1, b1, nt, ntx):
    nb, seq_len, d = x_all.shape
    return pl.pallas_call(
        _conf_in_kernel,
        out_shape=jax.ShapeDtypeStruct((nb, nt * SEQ_TILE, d), F32),
        grid=(nb, nt),
        in_specs=[_tok_spec(d), _mod_spec(d, nb, ntx), _full_spec(w1.shape), _full_spec(b1.shape)],
        out_specs=_tok_spec(d),
        compiler_params=_params(2),
        name="conf_in",
    )(x_all, mod, w1, b1)


def _conf_conv_kernel(ntx, nt, up_ref, uc_ref, un_ref, dw_ref, dwb_ref, ng_ref, nb_ref,
                      a_ref, ext_ref, conv_ref):
    tl, d = uc_ref.shape[1], uc_ref.shape[2]
    taps = dw_ref.shape[0]
    lead = CONV_HALO - (taps - 1) // 2
    first, last = _segment_edges(pl.program_id(1), ntx, nt)
    ext_ref[0:CONV_HALO, :] = jnp.where(first, 0.0, up_ref[0])
    ext_ref[CONV_HALO:CONV_HALO + tl, :] = uc_ref[0]
    ext_ref[CONV_HALO + tl:, :] = jnp.where(last, 0.0, un_ref[0])
    for r0 in range(0, tl, CONV_ROW_CHUNK):
        for c0 in range(0, d, LANE):
            acc = jnp.zeros((CONV_ROW_CHUNK, LANE), F32)
            for k in range(taps):
                acc = acc + dw_ref[k:k + 1, c0:c0 + LANE] * ext_ref[r0 + lead + k:r0 + lead + k + CONV_ROW_CHUNK, c0:c0 + LANE]
            conv_ref[r0:r0 + CONV_ROW_CHUNK, c0:c0 + LANE] = acc
    v = _layer_norm(conv_ref[...] + dwb_ref[...], ng_ref[...], nb_ref[...])
    a_ref[0] = _silu(v).astype(BF16)


def _conf_conv(u, dw, dwb, ng, nb_, nt, ntx):
    nb, seq_len, d = u.shape
    prev, nxt = _halo_specs(d, CONV_HALO, seq_len)
    return pl.pallas_call(
        functools.partial(_conf_conv_kernel, ntx, nt),
        out_shape=jax.ShapeDtypeStruct((nb, seq_len, d), BF16),
        grid=(nb, nt),
        in_specs=[prev, _tok_spec(d), nxt, _full_spec(dw.shape), _full_spec(dwb.shape),
                  _full_spec(ng.shape), _full_spec(nb_.shape)],
        out_specs=_tok_spec(d),
        scratch_shapes=[pltpu.VMEM((SEQ_TILE + 2 * CONV_HALO, d), F32),
                        pltpu.VMEM((SEQ_TILE, d), F32)],
        compiler_params=_params(2),
        name="conf_conv",
    )(u, u, u, dw, dwb, ng, nb_)


def _sc_in_kernel(x_ref, mod_ref, w_ref, gb_ref, p_ref):
    d = x_ref.shape[-1]
    h = x_ref[0] * (1.0 + mod_ref[0, 1:2, :]) + mod_ref[0, 0:1, :]
    z = _dot(h.astype(BF16), w_ref[...])
    gb_ref[0] = z[:, :d]
    p_ref[0] = z[:, d:2 * d] * z[:, 2 * d:]


def _sc_in(x_all, mod, w_in, nt, ntx):
    nb, seq_len, d = x_all.shape
    shp = jax.ShapeDtypeStruct((nb, nt * SEQ_TILE, d), F32)
    return pl.pallas_call(
        _sc_in_kernel,
        out_shape=(shp, shp),
        grid=(nb, nt),
        in_specs=[_tok_spec(d), _mod_spec(d, nb, ntx), _full_spec(w_in.shape)],
        out_specs=(_tok_spec(d), _tok_spec(d)),
        compiler_params=_params(2),
        name="sc_in",
    )(x_all, mod, w_in)


def _sc_conv_kernel(ntx, nt, pp_ref, pc_ref, pn_ref, gb_ref, dw_ref, a_ref, ext_ref):
    tl = pc_ref.shape[1]
    taps = dw_ref.shape[0]
    lead = SHORT_HALO - (taps - 1) // 2
    first, last = _segment_edges(pl.program_id(1), ntx, nt)
    ext_ref[0:SHORT_HALO, :] = jnp.where(first, 0.0, pp_ref[0])
    ext_ref[SHORT_HALO:SHORT_HALO + tl, :] = pc_ref[0]
    ext_ref[SHORT_HALO + tl:, :] = jnp.where(last, 0.0, pn_ref[0])
    acc = dw_ref[0:1, :] * ext_ref[lead:lead + tl, :]
    for k in range(1, taps):
        acc = acc + dw_ref[k:k + 1, :] * ext_ref[lead + k:lead + k + tl, :]
    a_ref[0] = (gb_ref[0] * acc).astype(BF16)


def _sc_conv(p, gb, dw, nt, ntx):
    nb, seq_len, d = p.shape
    prev, nxt = _halo_specs(d, SHORT_HALO, seq_len)
    return pl.pallas_call(
        functools.partial(_sc_conv_kernel, ntx, nt),
        out_shape=jax.ShapeDtypeStruct((nb, seq_len, d), BF16),
        grid=(nb, nt),
        in_specs=[prev, _tok_spec(d), nxt, _tok_spec(d), _full_spec(dw.shape)],
        out_specs=_tok_spec(d),
        scratch_shapes=[pltpu.VMEM((SEQ_TILE + 2 * SHORT_HALO, d), F32)],
        compiler_params=_params(2),
        name="sc_conv",
    )(p, p, p, gb, dw)


def _rms(v, g):
    return v * lax.rsqrt(jnp.mean(v * v, axis=-1, keepdims=True) + RMS_EPS) * g


def _mla_proj_kernel(ntx, q_rank,
                     x_ref, mod_ref, wd_ref, wkp_ref, qg_ref, kvg_ref, wqn_ref, wqr_ref, wuk_ref, wuv_ref,
                     cos_ref, sin_ref, q_ref, k_ref, v_ref):
    is_latent = pl.program_id(1) < ntx
    h = (x_ref[0] * (1.0 + mod_ref[0, 1:2, :]) + mod_ref[0, 0:1, :]).astype(BF16)
    dn = _dot(h, wd_ref[...])
    cq = _rms(dn[:, :q_rank], qg_ref[...]).astype(BF16)
    ckv = _rms(dn[:, q_rank:], kvg_ref[...]).astype(BF16)
    cos = jnp.where(is_latent, cos_ref[...], 1.0)
    sin = jnp.where(is_latent, sin_ref[...], 0.0)
    kp2 = _dot(h, wkp_ref[...])
    kp = kp2[:, :LANE] * cos + kp2[:, LANE:] * sin
    kn = _dot(ckv, wuk_ref[...])
    vv = _dot(ckv, wuv_ref[...])
    qn = _dot(cq, wqn_ref[...]) * ATTN_SCALE
    qr2 = _dot(cq, wqr_ref[...])
    hw = MLA_HEADS * LANE
    for hd in range(MLA_HEADS):
        sl = slice(hd * LANE, (hd + 1) * LANE)
        qr = (qr2[:, sl] * cos + qr2[:, hw + hd * LANE:hw + (hd + 1) * LANE] * sin) * ATTN_SCALE
        q_ref[0, hd, :, 0:LANE] = qn[:, sl].astype(BF16)
        q_ref[0, hd, :, LANE:] = qr.astype(BF16)
        k_ref[0, hd, :, 0:LANE] = kn[:, sl].astype(BF16)
        k_ref[0, hd, :, LANE:] = kp.astype(BF16)
        v_ref[0, hd] = vv[:, sl].astype(BF16)


def _mla_proj(x_all, mod, wd, wkp, qg, kvg, wqn, wqr, wuk, wuv, cos_t, sin_t, nt, ntx):
    nb, seq_len, d = x_all.shape
    q_rank = qg.shape[-1]
    tl = SEQ_TILE
    rope_spec = pl.BlockSpec((tl, LANE), lambda b, l: (jnp.minimum(l, ntx - 1), 0))
    qk_shape = jax.ShapeDtypeStruct((nb, MLA_HEADS, nt * tl, 2 * LANE), BF16)
    v_shape = jax.ShapeDtypeStruct((nb, MLA_HEADS, nt * tl, LANE), BF16)
    qk_spec = pl.BlockSpec((1, MLA_HEADS, tl, 2 * LANE), lambda b, l: (b, 0, l, 0))
    v_spec = pl.BlockSpec((1, MLA_HEADS, tl, LANE), lambda b, l: (b, 0, l, 0))
    return pl.pallas_call(
        functools.partial(_mla_proj_kernel, ntx, q_rank),
        out_shape=(qk_shape, qk_shape, v_shape),
        grid=(nb, nt),
        in_specs=[_tok_spec(d), _mod_spec(d, nb, ntx), _full_spec(wd.shape), _full_spec(wkp.shape),
                  _full_spec(qg.shape), _full_spec(kvg.shape), _full_spec(wqn.shape), _full_spec(wqr.shape),
                  _full_spec(wuk.shape), _full_spec(wuv.shape), rope_spec, rope_spec],
        out_specs=(qk_spec, qk_spec, v_spec),
        compiler_params=_params(2),
        name="mla_proj",
    )(x_all, mod, wd, wkp, qg, kvg, wqn, wqr, wuk, wuv, cos_t, sin_t)


def _attn_kernel(q_ref, k_ref, v_ref, o_ref):
    s = _dot_nt(q_ref[0, 0], k_ref[0, 0])
    m = jnp.max(s, axis=-1, keepdims=True)
    p = jnp.exp(s - m)
    denom = jnp.sum(p, axis=-1, keepdims=True)
    o = _dot(p.astype(BF16), v_ref[0, 0])
    o_ref[0] = (o / denom).astype(BF16)


def _attention(q, k, v, n_q):
    nb, nh, n_k, dk = k.shape
    tq = Q_TILE
    return pl.pallas_call(
        _attn_kernel,
        out_shape=jax.ShapeDtypeStruct((nb, n_q, nh * V_HEAD), BF16),
        grid=(nb, nh, n_q // tq),
        in_specs=[pl.BlockSpec((1, 1, tq, dk), lambda b, h, i: (b, h, i, 0)),
                  pl.BlockSpec((1, 1, n_k, dk), lambda b, h, i: (b, h, 0, 0)),
                  pl.BlockSpec((1, 1, n_k, V_HEAD), lambda b, h, i: (b, h, 0, 0))],
        out_specs=pl.BlockSpec((1, tq, V_HEAD), lambda b, h, i: (b, i, h)),
        compiler_params=_params(3),
        name="mla_attn",
    )(q, k, v)


def _route(sel, s):
    n_e, n_t = sel.shape
    per = n_e // N_GROUPS
    sel3 = sel.reshape(N_GROUPS, per, n_t)
    s3 = s.reshape(N_GROUPS, per, n_t)
    iota_p = lax.broadcasted_iota(jnp.int32, (N_GROUPS, per, n_t), 1).astype(F32)
    iota_g = lax.broadcasted_iota(jnp.int32, (N_GROUPS, 1, n_t), 0).astype(F32)
    neg = -jnp.inf
    m1 = jnp.max(sel3, axis=1, keepdims=True)
    first = jnp.min(jnp.where(sel3 == m1, iota_p, float(per)), axis=1, keepdims=True)
    m2 = jnp.max(jnp.where(iota_p == first, neg, sel3), axis=1, keepdims=True)
    gs = m1 + m2
    gsel = jnp.zeros((N_GROUPS, 1, n_t), F32)
    for _ in range(TOPK_GROUPS):
        gm = jnp.max(gs, axis=0, keepdims=True)
        gfirst = jnp.min(jnp.where(gs == gm, iota_g, float(N_GROUPS)), axis=0, keepdims=True)
        pick = iota_g == gfirst
        gsel = jnp.where(pick, 1.0, gsel)
        gs = jnp.where(pick, neg, gs)
    val = jnp.where(gsel > 0.0, sel3, neg)
    iota_e = lax.broadcasted_iota(jnp.int32, (N_GROUPS, per, n_t), 0).astype(F32) * per + iota_p
    chosen = jnp.zeros((N_GROUPS, per, n_t), F32)
    idx, wts = [], []
    for _ in range(TOP_K):
        m = jnp.max(jnp.max(val, axis=1, keepdims=True), axis=0, keepdims=True)
        e = jnp.min(jnp.min(jnp.where(val == m, iota_e, float(n_e)), axis=1, keepdims=True), axis=0, keepdims=True)
        pick = iota_e == e
        wts.append(jnp.sum(jnp.sum(jnp.where(pick, s3, 0.0), axis=1, keepdims=True), axis=0, keepdims=True))
        idx.append(e)
        chosen = jnp.where(pick, 1.0, chosen)
        val = jnp.where(pick, neg, val)
    total = wts[0]
    for w in wts[1:]:
        total = total + w
    wts = [w / total * ROUTED_SCALE for w in wts]
    return idx, wts, chosen, iota_e


def _post_kernel(alpha, a_ref, w_ref, b_ref, x_ref, mod_ref, lng_ref, lnb_ref, rw_hi_ref, rw_lo_ref, rb_ref, tri_ref,
                 xn_ref, h2_ref, topi_ref, topw_ref, rank_ref, cnt_ref, run_ref):
    first_step = jnp.logical_and(pl.program_id(0) == 0, pl.program_id(1) == 0)

    @pl.when(first_step)
    def _():
        run_ref[...] = jnp.zeros_like(run_ref)

    y = _dot(a_ref[0], w_ref[...]) + b_ref[...]
    xn = _layer_norm(alpha * x_ref[0] + mod_ref[0, 2:3, :] * y, lng_ref[...], lnb_ref[...])
    xn_ref[0] = xn
    h2 = xn * (1.0 + mod_ref[0, 4:5, :]) + mod_ref[0, 3:4, :]
    h_hi, h_lo = _split_bf16(h2)
    h2_ref[0] = _pack_halves(h2)
    logits = _dot_nt(rw_hi_ref[...], h_hi) + _dot_nt(rw_hi_ref[...], h_lo) + _dot_nt(rw_lo_ref[...], h_hi)
    s = jax.nn.sigmoid(logits)
    idx, wts, chosen3, iota_e = _route(s + rb_ref[...], s)
    n_e, n_t = s.shape
    chosen = chosen3.reshape(n_e, n_t)
    rank = run_ref[:, 0:1] + _dot(chosen.astype(BF16), tri_ref[...])
    rank3 = rank.reshape(chosen3.shape)
    for k in range(TOP_K):
        rk = jnp.sum(jnp.sum(jnp.where(iota_e == idx[k], rank3, 0.0), axis=1, keepdims=True), axis=0, keepdims=True)
        topi_ref[k:k + 1, :] = idx[k].reshape(1, n_t).astype(jnp.int32)
        topw_ref[k:k + 1, :] = wts[k].reshape(1, n_t)
        rank_ref[k:k + 1, :] = rk.reshape(1, n_t).astype(jnp.int32)
    run_ref[...] = run_ref[...] + jnp.sum(chosen, axis=1, keepdims=True)
    cnt_ref[...] = run_ref[...].astype(jnp.int32)


def _post(alpha, a, w, bias, x_all, mod, lng, lnb, rw_hi, rw_lo, rb, nt, ntx):
    nb, _, dk = a.shape
    d = x_all.shape[-1]
    n_e = rw_hi.shape[0]
    tl = SEQ_TILE
    n_tok = nb * nt * tl
    col_spec = lambda rows: pl.BlockSpec((rows, tl), lambda b, l: (0, b * nt + l))
    row_i = lax.broadcasted_iota(jnp.int32, (tl, tl), 0)
    col_i = lax.broadcasted_iota(jnp.int32, (tl, tl), 1)
    tri = jnp.where(row_i < col_i, 1.0, 0.0).astype(BF16)
    return pl.pallas_call(
        functools.partial(_post_kernel, alpha),
        out_shape=(jax.ShapeDtypeStruct((nb, nt * tl, d), F32),
                   jax.ShapeDtypeStruct((nb, nt * tl, d // 2), U32),
                   jax.ShapeDtypeStruct((TOP_K, n_tok), jnp.int32),
                   jax.ShapeDtypeStruct((TOP_K, n_tok), F32),
                   jax.ShapeDtypeStruct((TOP_K, n_tok), jnp.int32),
                   jax.ShapeDtypeStruct((n_e, LANE), jnp.int32)),
        grid=(nb, nt),
        in_specs=[_tok_spec(dk), _full_spec(w.shape), _full_spec(bias.shape), _tok_spec(d),
                  _mod_spec(d, nb, ntx), _full_spec(lng.shape), _full_spec(lnb.shape),
                  _full_spec(rw_hi.shape), _full_spec(rw_lo.shape), _full_spec(rb.shape), _full_spec(tri.shape)],
        out_specs=(_tok_spec(d), _tok_spec(d // 2), col_spec(TOP_K), col_spec(TOP_K), col_spec(TOP_K),
                   _full_spec((n_e, LANE))),
        scratch_shapes=[pltpu.VMEM((n_e, LANE), F32)],
        compiler_params=_params(2),
        name="mixer_post",
    )(a, w, bias, x_all, mod, lng, lnb, rw_hi, rw_lo, rb, tri)


def _gmm_kernel(be_ref, nu_ref, xs_ref, w1_ref, w3_ref, w2_ref, ys_ref, w13_s, w2_s):
    i = pl.program_id(0)
    de = w2_ref.shape[2]
    changed = jnp.logical_or(i == 0, be_ref[i] != be_ref[jnp.maximum(i - 1, 0)])

    @pl.when(changed)
    def _():
        w13_s[:, :de] = w1_ref[0, 0].astype(BF16)
        w13_s[:, de:] = w3_ref[0, 0].astype(BF16)
        w2_s[...] = w2_ref[0, 0].astype(BF16)

    @pl.when(i < nu_ref[0])
    def _():
        z = _dot_packed(xs_ref[...], w13_s)
        hmid = (_silu(z[:, :de]) * z[:, de:]).astype(BF16)
        ys_ref[...] = _pack_halves(_dot(hmid, w2_s[...]))


def _grouped_ffn(block_expert, n_used, xs, w1, w3, w2, layer):
    n_rows, dp = xs.shape
    d = 2 * dp
    bm = MOE_BLOCK_ROWS
    de = w2.shape[2]
    grid_spec = pltpu.PrefetchScalarGridSpec(
        num_scalar_prefetch=2,
        grid=(n_rows // bm,),
        in_specs=[pl.BlockSpec((bm, dp), lambda i, be, nu: (i, 0)),
                  pl.BlockSpec((1, 1, d, de), lambda i, be, nu: (layer, be[i], 0, 0)),
                  pl.BlockSpec((1, 1, d, de), lambda i, be, nu: (layer, be[i], 0, 0)),
                  pl.BlockSpec((1, 1, de, d), lambda i, be, nu: (layer, be[i], 0, 0))],
        out_specs=pl.BlockSpec((bm, dp), lambda i, be, nu: (i, 0)),
        scratch_shapes=[pltpu.VMEM((d, 2 * de), BF16), pltpu.VMEM((de, d), BF16)],
    )
    return pl.pallas_call(
        _gmm_kernel,
        out_shape=jax.ShapeDtypeStruct((n_rows, dp), U32),
        grid_spec=grid_spec,
        compiler_params=_params(1),
        name="moe_grouped_ffn",
    )(block_expert, n_used, xs, w1, w3, w2)


def _sc_mesh():
    return plsc.VectorSubcoreMesh(core_axis_name="c", subcore_axis_name="s",
                                  num_cores=SC_CORES, num_subcores=SC_SUBCORES)


def _sc_worker():
    return lax.axis_index("s") * SC_CORES + lax.axis_index("c")


def _sc_row_scatter(src, dest, n_out):
    n_k, n_tok = dest.shape
    dp = src.shape[1]
    win = SC_WINDOW
    per_w = n_tok // SC_WORKERS
    n_win = per_w // win
    assert n_tok % (SC_WORKERS * win) == 0 and n_win % 2 == 0
    idx = dest.reshape(n_k, SC_WORKERS, n_win, win).transpose(1, 2, 0, 3)

    def body(src_hbm, idx_hbm, out_hbm, idx_v, rows_v, load_sem, scat_sem):
        wid = _sc_worker()
        base = wid * per_w
        pltpu.sync_copy(idx_hbm.at[wid], idx_v)

        def load(g, slot):
            return pltpu.make_async_copy(src_hbm.at[pl.ds(base + g * win, win)], rows_v.at[slot], load_sem.at[slot])

        def scatter(g, slot, k):
            return pltpu.make_async_copy(rows_v.at[slot], out_hbm.at[idx_v.at[g, k]], scat_sem.at[slot])

        load(0, 0).start()

        @pl.loop(0, n_win, step=2)
        def _(g):
            for slot in range(2):
                cur = g + slot
                load(cur, slot).wait()

                @pl.when(cur + 1 < n_win)
                def _():
                    @pl.when(cur >= 1)
                    def _():
                        for k in range(n_k):
                            scatter(cur - 1, 1 - slot, k).wait()
                    load(cur + 1, 1 - slot).start()

                for k in range(n_k):
                    scatter(cur, slot, k).start()

        for slot in range(2):
            for k in range(n_k):
                scatter(n_win - 2 + slot, slot, k).wait()

    return pl.kernel(
        body, mesh=_sc_mesh(),
        out_type=jax.ShapeDtypeStruct((n_out, dp), src.dtype),
        scratch_types=[pltpu.VMEM((n_win, n_k, win), jnp.int32),
                       pltpu.VMEM((2, win, dp), src.dtype),
                       pltpu.SemaphoreType.DMA((2,)),
                       pltpu.SemaphoreType.DMA((2,))],
        compiler_params=pltpu.CompilerParams(use_tc_tiling_on_sc=True),
        name="sc_dispatch_scatter",
    )(src, idx)


def _sc_row_gather(table, idx):
    n = idx.shape[0]
    dp = table.shape[1]
    win = SC_WINDOW
    per_w = n // SC_WORKERS
    n_win = per_w // win
    assert n % (SC_WORKERS * win) == 0 and n_win % 2 == 0

    def body(table_hbm, idx_hbm, out_hbm, idx_v, rows_v, gather_sem, put_sem):
        wid = _sc_worker()
        base = wid * per_w
        pltpu.sync_copy(idx_hbm.at[pl.ds(base, per_w)], idx_v)

        def gather(g, slot):
            return pltpu.make_async_copy(table_hbm.at[idx_v.at[pl.ds(g * win, win)]], rows_v.at[slot],
                                         gather_sem.at[slot])

        def put(g, slot):
            return pltpu.make_async_copy(rows_v.at[slot], out_hbm.at[pl.ds(base + g * win, win)], put_sem.at[slot])

        gather(0, 0).start()

        @pl.loop(0, n_win, step=2)
        def _(g):
            for slot in range(2):
                cur = g + slot
                gather(cur, slot).wait()

                @pl.when(cur + 1 < n_win)
                def _():
                    @pl.when(cur >= 1)
                    def _():
                        put(cur - 1, 1 - slot).wait()
                    gather(cur + 1, 1 - slot).start()

                put(cur, slot).start()

        for slot in range(2):
            put(n_win - 2 + slot, slot).wait()

    return pl.kernel(
        body, mesh=_sc_mesh(),
        out_type=jax.ShapeDtypeStruct((n, dp), table.dtype),
        scratch_types=[pltpu.VMEM((per_w,), jnp.int32),
                       pltpu.VMEM((2, win, dp), table.dtype),
                       pltpu.SemaphoreType.DMA((2,)),
                       pltpu.SemaphoreType.DMA((2,))],
        compiler_params=pltpu.CompilerParams(use_tc_tiling_on_sc=True),
        name="sc_combine_gather",
    )(table, idx)


def _moe_out_kernel(alpha, x_ref, h2_ref, yg_ref, tw_ref, mod_ref, ws13_ref, ws2_ref, lng_ref, lnb_ref, o_ref):
    de = ws2_ref.shape[0]
    z = _dot_packed(h2_ref[0], ws13_ref)
    hmid = (_silu(z[:, :de]) * z[:, de:]).astype(BF16)
    y = _dot(hmid, ws2_ref[...])
    lo, hi = _unpack_halves(yg_ref[0])
    acc_lo, acc_hi = lo * tw_ref[:, 0:1], hi * tw_ref[:, 0:1]
    for k in range(1, yg_ref.shape[0]):
        lo, hi = _unpack_halves(yg_ref[k])
        acc_lo = acc_lo + lo * tw_ref[:, k:k + 1]
        acc_hi = acc_hi + hi * tw_ref[:, k:k + 1]
    y = y + jnp.concatenate([acc_lo, acc_hi], axis=1)
    o_ref[0] = _layer_norm(alpha * x_ref[0] + mod_ref[0, 5:6, :] * y, lng_ref[...], lnb_ref[...])


def _moe_out(alpha, xn, h2, y_rows, topw_t, mod, ws13, ws2, lng, lnb, nt, ntx):
    nb, _, d = xn.shape
    n_k = y_rows.shape[0]
    tl = SEQ_TILE
    return pl.pallas_call(
        functools.partial(_moe_out_kernel, alpha),
        out_shape=jax.ShapeDtypeStruct((nb, nt * tl, d), F32),
        grid=(nb, nt),
        in_specs=[_tok_spec(d), _tok_spec(d // 2),
                  pl.BlockSpec((n_k, tl, d // 2), lambda b, l: (0, b * nt + l, 0)),
                  pl.BlockSpec((tl, n_k), lambda b, l: (b * nt + l, 0)),
                  _mod_spec(d, nb, ntx),
                  _full_spec(ws13.shape), _full_spec(ws2.shape), _full_spec(lng.shape), _full_spec(lnb.shape)],
        out_specs=_tok_spec(d),
        compiler_params=_params(2),
        name="moe_out",
    )(xn, h2, y_rows, topw_t, mod, ws13, ws2, lng, lnb)


def _dispatch_plan(counts, topi, rank, n_tok):
    n_e = counts.shape[0]
    bm = MOE_BLOCK_ROWS
    padded = (counts + bm - 1) // bm * bm
    pad_end = jnp.cumsum(padded)
    pad_start = pad_end - padded
    onehot = topi[:, None, :] == jnp.arange(n_e, dtype=jnp.int32)[None, :, None]
    dest = rank + jnp.sum(jnp.where(onehot, pad_start[None, :, None], 0), axis=1)
    n_blocks = n_tok * TOP_K // bm + n_e
    block_start = jnp.arange(n_blocks, dtype=jnp.int32) * bm
    block_expert = jnp.sum((pad_end[None, :] <= block_start[:, None]).astype(jnp.int32), axis=1)
    block_expert = jnp.minimum(block_expert, n_e - 1)
    n_used = (pad_end[-1] // bm).astype(jnp.int32).reshape(1)
    return dest.astype(jnp.int32), block_expert, n_used, n_blocks * bm


def _rope_tables(seq):
    n_freq = QK_ROPE // 4
    inv_freq = ROPE_THETA ** (-jnp.arange(n_freq, dtype=F32) / n_freq)
    pos = jnp.arange(seq, dtype=jnp.int32)
    r = (pos // GRID_W).astype(F32)
    col = (pos % GRID_W).astype(F32)
    ang = jnp.concatenate([r[:, None] * inv_freq, col[:, None] * inv_freq], -1)
    cos, sin = jnp.cos(ang), jnp.sin(ang)
    zeros = jnp.zeros((seq, LANE - QK_ROPE), F32)
    cos_slot = jnp.concatenate([cos, cos, zeros], -1)
    sin_slot = jnp.concatenate([-sin, sin, zeros], -1)
    return cos_slot, sin_slot


def _rope_slot_weights(w_rope):
    k, n, _ = w_rope.shape
    half = QK_ROPE // 2
    swapped = jnp.concatenate([w_rope[..., half:], w_rope[..., :half]], -1)
    pad = jnp.zeros((k, n, LANE - QK_ROPE), w_rope.dtype)
    plain = jnp.concatenate([w_rope, pad], -1).reshape(k, n * LANE)
    swp = jnp.concatenate([swapped, pad], -1).reshape(k, n * LANE)
    return jnp.concatenate([plain, swp], -1)


def kernel(x, c, ctx, c_ctx, ada_w, ada_b, ln_g, ln_b, conf_w1, conf_b1, conf_dw, conf_dwb, conf_ng, conf_nb, conf_w2, conf_b2, sc_w_in, sc_dw, sc_w_out, mla_w_dqkv, mla_q_g, mla_kv_g, mla_w_uq, mla_w_uk, mla_w_uv, mla_w_o, moe_router, moe_bias, moe_w1, moe_w3, moe_w2, sh_w1, sh_w3, sh_w2):
    nb, seq, d = x.shape
    l_ctx = ctx.shape[1]
    depth = ada_w.shape[0]
    alpha = (2.0 * depth) ** 0.25
    tl = SEQ_TILE
    assert seq % tl == 0 and l_ctx % tl == 0 and seq % Q_TILE == 0 and seq % GRID_W == 0
    ntx = seq // tl
    nt_all = (seq + l_ctx) // tl
    attn_layers = [i for i in range(depth) if i % N_MIXERS == 2]
    last_ctx_reader = attn_layers[-1] if attn_layers else -1

    rows = -(-(nb + 1) // 8) * 8
    c_all = jnp.zeros((rows, d), F32).at[:nb].set(c).at[nb].set(c_ctx)
    mod_all = _modulation(c_all, ada_w, ada_b).reshape(depth, rows, N_MOD, d)

    x_all = jnp.concatenate([x, ctx], axis=1)
    q_rank, kv_rank = mla_q_g.shape[1], mla_kv_g.shape[1]
    cos_t, sin_t = _rope_tables(seq)

    for i in range(depth):
        need_ctx = i < last_ctx_reader
        kind, j = i % N_MIXERS, i // N_MIXERS
        nt = nt_all if need_ctx else ntx
        mod = mod_all[i]
        row = lambda v: v.reshape(1, -1)

        if kind == 0:
            u = _conf_in(x_all, mod, conf_w1[j].astype(BF16), row(conf_b1[j]), nt, ntx)
            a = _conf_conv(u, conf_dw[j], row(conf_dwb[j]), row(conf_ng[j]), row(conf_nb[j]), nt, ntx)
            w_last, b_last = conf_w2[j].astype(BF16), row(conf_b2[j])
        elif kind == 1:
            gb, p = _sc_in(x_all, mod, sc_w_in[j].astype(BF16), nt, ntx)
            a = _sc_conv(p, gb, sc_dw[j], nt, ntx)
            w_last, b_last = sc_w_out[j].astype(BF16), jnp.zeros((1, d), F32)
        else:
            wdq = mla_w_dqkv[j]
            wd = wdq[:, :q_rank + kv_rank].astype(BF16)
            wkp = _rope_slot_weights(wdq[:, None, q_rank + kv_rank:]).astype(BF16)
            wuq = mla_w_uq[j].reshape(q_rank, MLA_HEADS, QK_NOPE + QK_ROPE)
            wqn = wuq[:, :, :QK_NOPE].reshape(q_rank, MLA_HEADS * QK_NOPE).astype(BF16)
            wqr = _rope_slot_weights(wuq[:, :, QK_NOPE:]).astype(BF16)
            nt_kv = nt_all
            q, k, v = _mla_proj(x_all, mod, wd, wkp, row(mla_q_g[j]), row(mla_kv_g[j]), wqn, wqr,
                                mla_w_uk[j].astype(BF16), mla_w_uv[j].astype(BF16), cos_t, sin_t, nt_kv, ntx)
            a = _attention(q, k, v, nt * tl)
            w_last, b_last = mla_w_o[j].astype(BF16), jnp.zeros((1, d), F32)

        rw_hi, rw_lo = _split_bf16(moe_router[i].T)
        xn, h2, topi, topw, rank, counts = _post(alpha, a, w_last, b_last, x_all, mod, row(ln_g[i, 0]),
                                                 row(ln_b[i, 0]), rw_hi, rw_lo, moe_bias[i].reshape(-1, 1), nt, ntx)

        n_tok = nb * nt * tl
        dest, block_expert, n_used, n_rows = _dispatch_plan(counts[:, 0], topi, rank, n_tok)
        xs = _sc_row_scatter(h2.reshape(n_tok, d // 2), dest, n_rows)
        ys = _grouped_ffn(block_expert, n_used, xs, moe_w1, moe_w3, moe_w2, i)
        y_rows = _sc_row_gather(ys, dest.reshape(-1)).reshape(TOP_K, n_tok, d // 2)

        ws13 = jnp.concatenate([sh_w1[i], sh_w3[i]], axis=-1).astype(BF16)
        x_all = _moe_out(alpha, xn, h2, y_rows, topw.T, mod, ws13, sh_w2[i].astype(BF16),
                         row(ln_g[i, 1]), row(ln_b[i, 1]), nt, ntx)
    return x_all[:, :seq]
```

```python
import functools

import numpy as np
import jax
import jax.numpy as jnp
from jax import lax
from jax.experimental import pallas as pl
from jax.experimental.pallas import tpu as pltpu
from jax.experimental.pallas import tpu_sc as plsc

F32 = jnp.float32
BF16 = jnp.bfloat16
U32 = jnp.uint32
HIGH_HALF_MASK = np.uint32(0xFFFF0000)

GRID_W = 64
N_MIXERS = 3
LN_EPS = 1e-5
RMS_EPS = 1e-6
N_MOD = 6
MLA_HEADS = 8
QK_NOPE = 128
QK_ROPE = 64
V_HEAD = 128
ROPE_THETA = 10000.0
ATTN_SCALE = (QK_NOPE + QK_ROPE) ** -0.5
Q_SCALE = ATTN_SCALE * 1.4426950408889634
TOP_K = 8
N_GROUPS = 8
TOPK_GROUPS = 4
ROUTED_SCALE = 2.5

SEQ_TILE = 256
CONV_HALO = 16
SHORT_HALO = 8
CONV_ROW_CHUNK = 64
LANE = 128
SUBLANES = 8
Q_TILE = 512
ATTN_SUB_ROWS = 256
ATTN_ROW_CHUNK = 16
MOE_BLOCK_ROWS = 512
MOD_COL_TILE = 1536
VMEM_LIMIT = 48 * 1024 * 1024
SC_CORES = 2
SC_SUBCORES = 16
SC_WORKERS = SC_CORES * SC_SUBCORES
SC_WINDOW = 64


def _params(n_axes):
    return pltpu.CompilerParams(dimension_semantics=("arbitrary",) * n_axes,
                                vmem_limit_bytes=VMEM_LIMIT)


def _split_bf16(a):
    hi = a.astype(BF16)
    lo = (a - hi.astype(F32)).astype(BF16)
    return hi, lo


def _dot(a, b):
    return jnp.dot(a, b, preferred_element_type=F32)


def _dot_nt(a, b):
    return lax.dot_general(a, b, (((1,), (1,)), ((), ())), preferred_element_type=F32)


def _pack_halves(v):
    half = v.shape[-1] // 2
    lo = lax.bitcast_convert_type(v[:, :half].astype(BF16).astype(F32), U32) >> 16
    hi = lax.bitcast_convert_type(v[:, half:].astype(BF16).astype(F32), U32) & HIGH_HALF_MASK
    return hi | lo


def _unpack_halves(p):
    lo = lax.bitcast_convert_type(p << 16, F32)
    hi = lax.bitcast_convert_type(p & HIGH_HALF_MASK, F32)
    return lo, hi


def _dot_packed(p, w):
    lo, hi = _unpack_halves(p)
    half = p.shape[-1]
    return _dot(lo.astype(BF16), w[:half]) + _dot(hi.astype(BF16), w[half:])


def _layer_norm(v, g, b):
    mu = jnp.mean(v, axis=-1, keepdims=True)
    c = v - mu
    var = jnp.mean(c * c, axis=-1, keepdims=True)
    return c * lax.rsqrt(var + LN_EPS) * g + b


def _silu(v):
    return v * jax.nn.sigmoid(v)


def _mod_kernel(c_ref, w_ref, b_ref, o_ref):
    a = _silu(c_ref[...])
    a_hi, a_lo = _split_bf16(a)
    w_hi, w_lo = _split_bf16(w_ref[0])
    o_ref[0] = _dot(a_hi, w_hi) + _dot(a_hi, w_lo) + _dot(a_lo, w_hi) + b_ref[0]


def _modulation(c_all, ada_w, ada_b):
    depth, d, n = ada_w.shape
    rows = c_all.shape[0]
    tn = MOD_COL_TILE
    return pl.pallas_call(
        _mod_kernel,
        out_shape=jax.ShapeDtypeStruct((depth, rows, n), F32),
        grid=(depth, n // tn),
        in_specs=[pl.BlockSpec((rows, d), lambda i, j: (0, 0)),
                  pl.BlockSpec((1, d, tn), lambda i, j: (i, 0, j)),
                  pl.BlockSpec((1, 1, tn), lambda i, j: (i, 0, j))],
        out_specs=pl.BlockSpec((1, rows, tn), lambda i, j: (i, 0, j)),
        compiler_params=_params(2),
        name="adaln_mod",
    )(c_all, ada_w, ada_b.reshape(depth, 1, n))


def _tok_spec(d, tl=SEQ_TILE):
    return pl.BlockSpec((1, tl, d), lambda b, l: (b, l, 0))


def _mod_spec(d, n_batch, ntx):
    return pl.BlockSpec((1, N_MOD, d), lambda b, l: (jnp.where(l < ntx, b, n_batch), 0, 0))


def _full_spec(shape):
    zeros = (0,) * len(shape)
    return pl.BlockSpec(shape, lambda b, l: zeros)


def _halo_specs(d, halo, seq_len, tl=SEQ_TILE):
    per_tile = tl // halo
    last = seq_len // halo - 1
    prev = pl.BlockSpec((1, halo, d), lambda b, l: (b, jnp.maximum(l * per_tile - 1, 0), 0))
    nxt = pl.BlockSpec((1, halo, d), lambda b, l: (b, jnp.minimum((l + 1) * per_tile, last), 0))
    return prev, nxt


def _segment_edges(l, ntx, nt):
    first = jnp.logical_or(l == 0, l == ntx)
    last = jnp.logical_or(l == ntx - 1, l == nt - 1)
    return first, last


def _conf_in_kernel(x_ref, mod_ref, w1_ref, b1_ref, u_ref):
    d = x_ref.shape[-1]
    h = x_ref[0] * (1.0 + mod_ref[0, 1:2, :]) + mod_ref[0, 0:1, :]
    z = _dot(h.astype(BF16), w1_ref[...]) + b1_ref[...]
    u_ref[0] = z[:, :d] * jax.nn.sigmoid(z[:, d:])


def _conf_in(x_all, mod, w1, b1, nt, ntx):
    nb, seq_len, d = x_all.shape
    return pl.pallas_call(
        _conf_in_kernel,
        out_shape=jax.ShapeDtypeStruct((nb, nt * SEQ_TILE, d), F32),
        grid=(nb, nt),
        in_specs=[_tok_spec(d), _mod_spec(d, nb, ntx), _full_spec(w1.shape), _full_spec(b1.shape)],
        out_specs=_tok_spec(d),
        compiler_params=_params(2),
        name="conf_in",
    )(x_all, mod, w1, b1)


def _conf_conv_kernel(ntx, nt, up_ref, uc_ref, un_ref, dw_ref, dwb_ref, ng_ref, nb_ref,
                      a_ref, sh_ref, conv_ref):
    tl, d = uc_ref.shape[1], uc_ref.shape[2]
    taps = dw_ref.shape[0]
    lead = CONV_HALO - (taps - 1) // 2
    first, last = _segment_edges(pl.program_id(1), ntx, nt)
    sh_ref[0, 0:CONV_HALO, :] = jnp.where(first, 0.0, up_ref[0])
    sh_ref[0, CONV_HALO:CONV_HALO + tl, :] = uc_ref[0]
    sh_ref[0, CONV_HALO + tl:, :] = jnp.where(last, 0.0, un_ref[0])
    span = tl + 2 * CONV_HALO - SUBLANES
    for s in range(1, SUBLANES):
        sh_ref[s, 0:span, :] = sh_ref[0, s:s + span, :]
    groups = CONV_ROW_CHUNK // SUBLANES

    def row_chunk(i, carry):
        r0 = pl.multiple_of(i * CONV_ROW_CHUNK, CONV_ROW_CHUNK)
        for c0 in range(0, d, LANE):
            accs = [jnp.zeros((SUBLANES, LANE), F32) for _ in range(groups)]
            for k in range(taps):
                res = (lead + k) % SUBLANES
                off = lead + k - res
                w = dw_ref[k, :, c0:c0 + LANE]
                for g in range(groups):
                    lo = r0 + (off + g * SUBLANES)
                    accs[g] = accs[g] + w * sh_ref[res, pl.ds(lo, SUBLANES), c0:c0 + LANE]
            for g in range(groups):
                conv_ref[pl.ds(r0 + g * SUBLANES, SUBLANES), c0:c0 + LANE] = accs[g]
        return carry

    lax.fori_loop(0, tl // CONV_ROW_CHUNK, row_chunk, 0)
    v = _layer_norm(conv_ref[...] + dwb_ref[...], ng_ref[...], nb_ref[...])
    a_ref[0] = _silu(v).astype(BF16)


def _conf_conv(u, dw, dwb, ng, nb_, nt, ntx):
    nb, seq_len, d = u.shape
    prev, nxt = _halo_specs(d, CONV_HALO, seq_len)
    return pl.pallas_call(
        functools.partial(_conf_conv_kernel, ntx, nt),
        out_shape=jax.ShapeDtypeStruct((nb, seq_len, d), BF16),
        grid=(nb, nt),
        in_specs=[prev, _tok_spec(d), nxt, _full_spec(dw.shape), _full_spec(dwb.shape),
                  _full_spec(ng.shape), _full_spec(nb_.shape)],
        out_specs=_tok_spec(d),
        scratch_shapes=[pltpu.VMEM((SUBLANES, SEQ_TILE + 2 * CONV_HALO, d), F32),
                        pltpu.VMEM((SEQ_TILE, d), F32)],
        compiler_params=_params(2),
        name="conf_conv",
    )(u, u, u, dw, dwb, ng, nb_)


def _sc_in_kernel(x_ref, mod_ref, w_ref, gb_ref, p_ref):
    d = x_ref.shape[-1]
    h = x_ref[0] * (1.0 + mod_ref[0, 1:2, :]) + mod_ref[0, 0:1, :]
    z = _dot(h.astype(BF16), w_ref[...])
    gb_ref[0] = z[:, :d]
    p_ref[0] = z[:, d:2 * d] * z[:, 2 * d:]


def _sc_in(x_all, mod, w_in, nt, ntx):
    nb, seq_len, d = x_all.shape
    shp = jax.ShapeDtypeStruct((nb, nt * SEQ_TILE, d), F32)
    return pl.pallas_call(
        _sc_in_kernel,
        out_shape=(shp, shp),
        grid=(nb, nt),
        in_specs=[_tok_spec(d), _mod_spec(d, nb, ntx), _full_spec(w_in.shape)],
        out_specs=(_tok_spec(d), _tok_spec(d)),
        compiler_params=_params(2),
        name="sc_in",
    )(x_all, mod, w_in)


def _sc_conv_kernel(ntx, nt, pp_ref, pc_ref, pn_ref, gb_ref, dw_ref, a_ref, ext_ref):
    tl = pc_ref.shape[1]
    taps = dw_ref.shape[0]
    lead = SHORT_HALO - (taps - 1) // 2
    first, last = _segment_edges(pl.program_id(1), ntx, nt)
    ext_ref[0:SHORT_HALO, :] = jnp.where(first, 0.0, pp_ref[0])
    ext_ref[SHORT_HALO:SHORT_HALO + tl, :] = pc_ref[0]
    ext_ref[SHORT_HALO + tl:, :] = jnp.where(last, 0.0, pn_ref[0])
    acc = dw_ref[0:1, :] * ext_ref[lead:lead + tl, :]
    for k in range(1, taps):
        acc = acc + dw_ref[k:k + 1, :] * ext_ref[lead + k:lead + k + tl, :]
    a_ref[0] = (gb_ref[0] * acc).astype(BF16)


def _sc_conv(p, gb, dw, nt, ntx):
    nb, seq_len, d = p.shape
    prev, nxt = _halo_specs(d, SHORT_HALO, seq_len)
    return pl.pallas_call(
        functools.partial(_sc_conv_kernel, ntx, nt),
        out_shape=jax.ShapeDtypeStruct((nb, seq_len, d), BF16),
        grid=(nb, nt),
        in_specs=[prev, _tok_spec(d), nxt, _tok_spec(d), _full_spec(dw.shape)],
        out_specs=_tok_spec(d),
        scratch_shapes=[pltpu.VMEM((SEQ_TILE + 2 * SHORT_HALO, d), F32)],
        compiler_params=_params(2),
        name="sc_conv",
    )(p, p, p, gb, dw)


def _rms(v, g):
    return v * lax.rsqrt(jnp.mean(v * v, axis=-1, keepdims=True) + RMS_EPS) * g


def _mla_proj_kernel(ntx, q_rank,
                     x_ref, mod_ref, wd_ref, wkp_ref, qg_ref, kvg_ref, wqn_ref, wqr_ref, wuk_ref, wuv_ref,
                     cos_ref, sin_ref, q_ref, k_ref, v_ref):
    is_latent = pl.program_id(1) < ntx
    h = (x_ref[0] * (1.0 + mod_ref[0, 1:2, :]) + mod_ref[0, 0:1, :]).astype(BF16)
    dn = _dot(h, wd_ref[...])
    cq = _rms(dn[:, :q_rank], qg_ref[...]).astype(BF16)
    ckv = _rms(dn[:, q_rank:], kvg_ref[...]).astype(BF16)
    cos = jnp.where(is_latent, cos_ref[...], 1.0)
    sin = jnp.where(is_latent, sin_ref[...], 0.0)
    kp2 = _dot(h, wkp_ref[...])
    kp = kp2[:, :LANE] * cos + kp2[:, LANE:] * sin
    kn = _dot(ckv, wuk_ref[...])
    vv = _dot(ckv, wuv_ref[...])
    qn = _dot(cq, wqn_ref[...]) * Q_SCALE
    qr2 = _dot(cq, wqr_ref[...])
    hw = MLA_HEADS * LANE
    for hd in range(MLA_HEADS):
        sl = slice(hd * LANE, (hd + 1) * LANE)
        qr = (qr2[:, sl] * cos + qr2[:, hw + hd * LANE:hw + (hd + 1) * LANE] * sin) * Q_SCALE
        q_ref[0, hd, :, 0:LANE] = qn[:, sl].astype(BF16)
        q_ref[0, hd, :, LANE:] = qr.astype(BF16)
        k_ref[0, hd, :, 0:LANE] = kn[:, sl].astype(BF16)
        k_ref[0, hd, :, LANE:] = kp.astype(BF16)
        v_ref[0, hd] = vv[:, sl].astype(BF16)


def _mla_proj(x_all, mod, wd, wkp, qg, kvg, wqn, wqr, wuk, wuv, cos_t, sin_t, nt, ntx):
    nb, seq_len, d = x_all.shape
    q_rank = qg.shape[-1]
    tl = SEQ_TILE
    rope_spec = pl.BlockSpec((tl, LANE), lambda b, l: (jnp.minimum(l, ntx - 1), 0))
    qk_shape = jax.ShapeDtypeStruct((nb, MLA_HEADS, nt * tl, 2 * LANE), BF16)
    v_shape = jax.ShapeDtypeStruct((nb, MLA_HEADS, nt * tl, LANE), BF16)
    qk_spec = pl.BlockSpec((1, MLA_HEADS, tl, 2 * LANE), lambda b, l: (b, 0, l, 0))
    v_spec = pl.BlockSpec((1, MLA_HEADS, tl, LANE), lambda b, l: (b, 0, l, 0))
    return pl.pallas_call(
        functools.partial(_mla_proj_kernel, ntx, q_rank),
        out_shape=(qk_shape, qk_shape, v_shape),
        grid=(nb, nt),
        in_specs=[_tok_spec(d), _mod_spec(d, nb, ntx), _full_spec(wd.shape), _full_spec(wkp.shape),
                  _full_spec(qg.shape), _full_spec(kvg.shape), _full_spec(wqn.shape), _full_spec(wqr.shape),
                  _full_spec(wuk.shape), _full_spec(wuv.shape), rope_spec, rope_spec],
        out_specs=(qk_spec, qk_spec, v_spec),
        compiler_params=_params(2),
        name="mla_proj",
    )(x_all, mod, wd, wkp, qg, kvg, wqn, wqr, wuk, wuv, cos_t, sin_t)


def _attn_kernel(q_ref, k_ref, v_ref, o_ref, s_ref, p_ref, l_ref):
    tq = q_ref.shape[2]
    for r0 in range(0, tq, ATTN_SUB_ROWS):
        rows = slice(r0, r0 + ATTN_SUB_ROWS)
        s_ref[rows, :] = _dot_nt(q_ref[0, 0, rows, :], k_ref[0, 0])
        for c0 in range(r0, r0 + ATTN_SUB_ROWS, ATTN_ROW_CHUNK):
            chunk = slice(c0, c0 + ATTN_ROW_CHUNK)
            s = s_ref[chunk, :]
            p = jnp.exp2(s - jnp.max(s, axis=-1, keepdims=True))
            l_ref[chunk, :] = jnp.broadcast_to(jnp.sum(p, axis=-1, keepdims=True), (ATTN_ROW_CHUNK, V_HEAD))
            p_ref[chunk, :] = p.astype(BF16)
        o = _dot(p_ref[rows, :], v_ref[0, 0])
        o_ref[0, rows, :] = (o / l_ref[rows, :]).astype(BF16)


def _attention(q, k, v, n_q):
    nb, nh, n_k, dk = k.shape
    tq = Q_TILE
    return pl.pallas_call(
        _attn_kernel,
        out_shape=jax.ShapeDtypeStruct((nb, n_q, nh * V_HEAD), BF16),
        grid=(nb, nh, n_q // tq),
        in_specs=[pl.BlockSpec((1, 1, tq, dk), lambda b, h, i: (b, h, i, 0)),
                  pl.BlockSpec((1, 1, n_k, dk), lambda b, h, i: (b, h, 0, 0)),
                  pl.BlockSpec((1, 1, n_k, V_HEAD), lambda b, h, i: (b, h, 0, 0))],
        out_specs=pl.BlockSpec((1, tq, V_HEAD), lambda b, h, i: (b, i, h)),
        scratch_shapes=[pltpu.VMEM((tq, n_k), F32), pltpu.VMEM((tq, n_k), BF16), pltpu.VMEM((tq, V_HEAD), F32)],
        compiler_params=_params(3),
        name="mla_attn",
    )(q, k, v)


def _route(sel, s):
    n_e, n_t = sel.shape
    per = n_e // N_GROUPS
    sel3 = sel.reshape(N_GROUPS, per, n_t)
    s3 = s.reshape(N_GROUPS, per, n_t)
    iota_p = lax.broadcasted_iota(jnp.int32, (N_GROUPS, per, n_t), 1).astype(F32)
    iota_g = lax.broadcasted_iota(jnp.int32, (N_GROUPS, 1, n_t), 0).astype(F32)
    neg = -jnp.inf
    m1 = jnp.max(sel3, axis=1, keepdims=True)
    first = jnp.min(jnp.where(sel3 == m1, iota_p, float(per)), axis=1, keepdims=True)
    m2 = jnp.max(jnp.where(iota_p == first, neg, sel3), axis=1, keepdims=True)
    gs = m1 + m2
    gsel = jnp.zeros((N_GROUPS, 1, n_t), F32)
    for _ in range(TOPK_GROUPS):
        gm = jnp.max(gs, axis=0, keepdims=True)
        gfirst = jnp.min(jnp.where(gs == gm, iota_g, float(N_GROUPS)), axis=0, keepdims=True)
        pick = iota_g == gfirst
        gsel = jnp.where(pick, 1.0, gsel)
        gs = jnp.where(pick, neg, gs)
    val = jnp.where(gsel > 0.0, sel3, neg)
    iota_e = lax.broadcasted_iota(jnp.int32, (N_GROUPS, per, n_t), 0).astype(F32) * per + iota_p
    chosen = jnp.zeros((N_GROUPS, per, n_t), F32)
    idx, wts = [], []
    for _ in range(TOP_K):
        m = jnp.max(jnp.max(val, axis=1, keepdims=True), axis=0, keepdims=True)
        e = jnp.min(jnp.min(jnp.where(val == m, iota_e, float(n_e)), axis=1, keepdims=True), axis=0, keepdims=True)
        pick = iota_e == e
        wts.append(jnp.sum(jnp.sum(jnp.where(pick, s3, 0.0), axis=1, keepdims=True), axis=0, keepdims=True))
        idx.append(e)
        chosen = jnp.where(pick, 1.0, chosen)
        val = jnp.where(pick, neg, val)
    total = wts[0]
    for w in wts[1:]:
        total = total + w
    wts = [w / total * ROUTED_SCALE for w in wts]
    return idx, wts, chosen, iota_e


def _post_kernel(alpha, a_ref, w_ref, b_ref, x_ref, mod_ref, lng_ref, lnb_ref, rw_hi_ref, rw_lo_ref, rb_ref, tri_ref,
                 xn_ref, h2_ref, topi_ref, topw_ref, rank_ref, cnt_ref, run_ref):
    first_step = jnp.logical_and(pl.program_id(0) == 0, pl.program_id(1) == 0)

    @pl.when(first_step)
    def _():
        run_ref[...] = jnp.zeros_like(run_ref)

    y = _dot(a_ref[0], w_ref[...]) + b_ref[...]
    xn = _layer_norm(alpha * x_ref[0] + mod_ref[0, 2:3, :] * y, lng_ref[...], lnb_ref[...])
    xn_ref[0] = xn
    h2 = xn * (1.0 + mod_ref[0, 4:5, :]) + mod_ref[0, 3:4, :]
    h_hi, h_lo = _split_bf16(h2)
    h2_ref[0] = _pack_halves(h2)
    logits = _dot_nt(rw_hi_ref[...], h_hi) + _dot_nt(rw_hi_ref[...], h_lo) + _dot_nt(rw_lo_ref[...], h_hi)
    s = jax.nn.sigmoid(logits)
    idx, wts, chosen3, iota_e = _route(s + rb_ref[...], s)
    n_e, n_t = s.shape
    chosen = chosen3.reshape(n_e, n_t)
    rank = run_ref[:, 0:1] + _dot(chosen.astype(BF16), tri_ref[...])
    rank3 = rank.reshape(chosen3.shape)
    for k in range(TOP_K):
        rk = jnp.sum(jnp.sum(jnp.where(iota_e == idx[k], rank3, 0.0), axis=1, keepdims=True), axis=0, keepdims=True)
        topi_ref[k:k + 1, :] = idx[k].reshape(1, n_t).astype(jnp.int32)
        topw_ref[k:k + 1, :] = wts[k].reshape(1, n_t)
        rank_ref[k:k + 1, :] = rk.reshape(1, n_t).astype(jnp.int32)
    run_ref[...] = run_ref[...] + jnp.sum(chosen, axis=1, keepdims=True)
    cnt_ref[...] = run_ref[...].astype(jnp.int32)


def _post(alpha, a, w, bias, x_all, mod, lng, lnb, rw_hi, rw_lo, rb, nt, ntx):
    nb, _, dk = a.shape
    d = x_all.shape[-1]
    n_e = rw_hi.shape[0]
    tl = SEQ_TILE
    n_tok = nb * nt * tl
    col_spec = lambda rows: pl.BlockSpec((rows, tl), lambda b, l: (0, b * nt + l))
    row_i = lax.broadcasted_iota(jnp.int32, (tl, tl), 0)
    col_i = lax.broadcasted_iota(jnp.int32, (tl, tl), 1)
    tri = jnp.where(row_i < col_i, 1.0, 0.0).astype(BF16)
    return pl.pallas_call(
        functools.partial(_post_kernel, alpha),
        out_shape=(jax.ShapeDtypeStruct((nb, nt * tl, d), F32),
                   jax.ShapeDtypeStruct((nb, nt * tl, d // 2), U32),
                   jax.ShapeDtypeStruct((TOP_K, n_tok), jnp.int32),
                   jax.ShapeDtypeStruct((TOP_K, n_tok), F32),
                   jax.ShapeDtypeStruct((TOP_K, n_tok), jnp.int32),
                   jax.ShapeDtypeStruct((n_e, LANE), jnp.int32)),
        grid=(nb, nt),
        in_specs=[_tok_spec(dk), _full_spec(w.shape), _full_spec(bias.shape), _tok_spec(d),
                  _mod_spec(d, nb, ntx), _full_spec(lng.shape), _full_spec(lnb.shape),
                  _full_spec(rw_hi.shape), _full_spec(rw_lo.shape), _full_spec(rb.shape), _full_spec(tri.shape)],
        out_specs=(_tok_spec(d), _tok_spec(d // 2), col_spec(TOP_K), col_spec(TOP_K), col_spec(TOP_K),
                   _full_spec((n_e, LANE))),
        scratch_shapes=[pltpu.VMEM((n_e, LANE), F32)],
        compiler_params=_params(2),
        name="mixer_post",
    )(a, w, bias, x_all, mod, lng, lnb, rw_hi, rw_lo, rb, tri)


def _gmm_kernel(be_ref, nu_ref, xs_ref, w1_ref, w3_ref, w2_ref, ys_ref, w13_s, w2_s):
    i = pl.program_id(0)
    de = w2_ref.shape[2]
    changed = jnp.logical_or(i == 0, be_ref[i] != be_ref[jnp.maximum(i - 1, 0)])

    @pl.when(changed)
    def _():
        w13_s[:, :de] = w1_ref[0, 0].astype(BF16)
        w13_s[:, de:] = w3_ref[0, 0].astype(BF16)
        w2_s[...] = w2_ref[0, 0].astype(BF16)

    @pl.when(i < nu_ref[0])
    def _():
        z = _dot_packed(xs_ref[...], w13_s)
        hmid = (_silu(z[:, :de]) * z[:, de:]).astype(BF16)
        ys_ref[...] = _pack_halves(_dot(hmid, w2_s[...]))


def _grouped_ffn(block_expert, n_used, xs, w1, w3, w2, layer):
    n_rows, dp = xs.shape
    d = 2 * dp
    bm = MOE_BLOCK_ROWS
    de = w2.shape[2]
    grid_spec = pltpu.PrefetchScalarGridSpec(
        num_scalar_prefetch=2,
        grid=(n_rows // bm,),
        in_specs=[pl.BlockSpec((bm, dp), lambda i, be, nu: (i, 0)),
                  pl.BlockSpec((1, 1, d, de), lambda i, be, nu: (layer, be[i], 0, 0)),
                  pl.BlockSpec((1, 1, d, de), lambda i, be, nu: (layer, be[i], 0, 0)),
                  pl.BlockSpec((1, 1, de, d), lambda i, be, nu: (layer, be[i], 0, 0))],
        out_specs=pl.BlockSpec((bm, dp), lambda i, be, nu: (i, 0)),
        scratch_shapes=[pltpu.VMEM((d, 2 * de), BF16), pltpu.VMEM((de, d), BF16)],
    )
    return pl.pallas_call(
        _gmm_kernel,
        out_shape=jax.ShapeDtypeStruct((n_rows, dp), U32),
        grid_spec=grid_spec,
        compiler_params=_params(1),
        name="moe_grouped_ffn",
    )(block_expert, n_used, xs, w1, w3, w2)


def _sc_mesh():
    return plsc.VectorSubcoreMesh(core_axis_name="c", subcore_axis_name="s",
                                  num_cores=SC_CORES, num_subcores=SC_SUBCORES)


def _sc_worker():
    return lax.axis_index("s") * SC_CORES + lax.axis_index("c")


def _sc_row_scatter(src, dest, n_out):
    n_k, n_tok = dest.shape
    dp = src.shape[1]
    win = SC_WINDOW
    per_w = n_tok // SC_WORKERS
    n_win = per_w // win
    assert n_tok % (SC_WORKERS * win) == 0 and n_win % 2 == 0
    idx = dest.reshape(n_k, SC_WORKERS, n_win, win).transpose(1, 2, 0, 3)

    def body(src_hbm, idx_hbm, out_hbm, idx_v, rows_v, load_sem, scat_sem):
        wid = _sc_worker()
        base = wid * per_w
        pltpu.sync_copy(idx_hbm.at[wid], idx_v)

        def load(g, slot):
            return pltpu.make_async_copy(src_hbm.at[pl.ds(base + g * win, win)], rows_v.at[slot], load_sem.at[slot])

        def scatter(g, slot, k):
            return pltpu.make_async_copy(rows_v.at[slot], out_hbm.at[idx_v.at[g, k]], scat_sem.at[slot])

        load(0, 0).start()

        @pl.loop(0, n_win, step=2)
        def _(g):
            for slot in range(2):
                cur = g + slot
                load(cur, slot).wait()

                @pl.when(cur + 1 < n_win)
                def _():
                    @pl.when(cur >= 1)
                    def _():
                        for k in range(n_k):
                            scatter(cur - 1, 1 - slot, k).wait()
                    load(cur + 1, 1 - slot).start()

                for k in range(n_k):
                    scatter(cur, slot, k).start()

        for slot in range(2):
            for k in range(n_k):
                scatter(n_win - 2 + slot, slot, k).wait()

    return pl.kernel(
        body, mesh=_sc_mesh(),
        out_type=jax.ShapeDtypeStruct((n_out, dp), src.dtype),
        scratch_types=[pltpu.VMEM((n_win, n_k, win), jnp.int32),
                       pltpu.VMEM((2, win, dp), src.dtype),
                       pltpu.SemaphoreType.DMA((2,)),
                       pltpu.SemaphoreType.DMA((2,))],
        compiler_params=pltpu.CompilerParams(use_tc_tiling_on_sc=True),
        name="sc_dispatch_scatter",
    )(src, idx)


def _sc_row_gather(table, idx):
    n = idx.shape[0]
    dp = table.shape[1]
    win = SC_WINDOW
    per_w = n // SC_WORKERS
    n_win = per_w // win
    assert n % (SC_WORKERS * win) == 0 and n_win % 2 == 0

    def body(table_hbm, idx_hbm, out_hbm, idx_v, rows_v, gather_sem, put_sem):
        wid = _sc_worker()
        base = wid * per_w
        pltpu.sync_copy(idx_hbm.at[pl.ds(base, per_w)], idx_v)

        def gather(g, slot):
            return pltpu.make_async_copy(table_hbm.at[idx_v.at[pl.ds(g * win, win)]], rows_v.at[slot],
                                         gather_sem.at[slot])

        def put(g, slot):
            return pltpu.make_async_copy(rows_v.at[slot], out_hbm.at[pl.ds(base + g * win, win)], put_sem.at[slot])

        gather(0, 0).start()

        @pl.loop(0, n_win, step=2)
        def _(g):
            for slot in range(2):
                cur = g + slot
                gather(cur, slot).wait()

                @pl.when(cur + 1 < n_win)
                def _():
                    @pl.when(cur >= 1)
                    def _():
                        put(cur - 1, 1 - slot).wait()
                    gather(cur + 1, 1 - slot).start()

                put(cur, slot).start()

        for slot in range(2):
            put(n_win - 2 + slot, slot).wait()

    return pl.kernel(
        body, mesh=_sc_mesh(),
        out_type=jax.ShapeDtypeStruct((n, dp), table.dtype),
        scratch_types=[pltpu.VMEM((per_w,), jnp.int32),
                       pltpu.VMEM((2, win, dp), table.dtype),
                       pltpu.SemaphoreType.DMA((2,)),
                       pltpu.SemaphoreType.DMA((2,))],
        compiler_params=pltpu.CompilerParams(use_tc_tiling_on_sc=True),
        name="sc_combine_gather",
    )(table, idx)


def _moe_out_kernel(alpha, x_ref, h2_ref, yg_ref, tw_ref, mod_ref, ws13_ref, ws2_ref, lng_ref, lnb_ref, o_ref):
    de = ws2_ref.shape[0]
    z = _dot_packed(h2_ref[0], ws13_ref)
    hmid = (_silu(z[:, :de]) * z[:, de:]).astype(BF16)
    y = _dot(hmid, ws2_ref[...])
    lo, hi = _unpack_halves(yg_ref[0])
    acc_lo, acc_hi = lo * tw_ref[:, 0:1], hi * tw_ref[:, 0:1]
    for k in range(1, yg_ref.shape[0]):
        lo, hi = _unpack_halves(yg_ref[k])
        acc_lo = acc_lo + lo * tw_ref[:, k:k + 1]
        acc_hi = acc_hi + hi * tw_ref[:, k:k + 1]
    y = y + jnp.concatenate([acc_lo, acc_hi], axis=1)
    o_ref[0] = _layer_norm(alpha * x_ref[0] + mod_ref[0, 5:6, :] * y, lng_ref[...], lnb_ref[...])


def _moe_out(alpha, xn, h2, y_rows, topw_t, mod, ws13, ws2, lng, lnb, nt, ntx):
    nb, _, d = xn.shape
    n_k = y_rows.shape[0]
    tl = SEQ_TILE
    return pl.pallas_call(
        functools.partial(_moe_out_kernel, alpha),
        out_shape=jax.ShapeDtypeStruct((nb, nt * tl, d), F32),
        grid=(nb, nt),
        in_specs=[_tok_spec(d), _tok_spec(d // 2),
                  pl.BlockSpec((n_k, tl, d // 2), lambda b, l: (0, b * nt + l, 0)),
                  pl.BlockSpec((tl, n_k), lambda b, l: (b * nt + l, 0)),
                  _mod_spec(d, nb, ntx),
                  _full_spec(ws13.shape), _full_spec(ws2.shape), _full_spec(lng.shape), _full_spec(lnb.shape)],
        out_specs=_tok_spec(d),
        compiler_params=_params(2),
        name="moe_out",
    )(xn, h2, y_rows, topw_t, mod, ws13, ws2, lng, lnb)


def _dispatch_plan(counts, topi, rank, n_tok):
    n_e = counts.shape[0]
    bm = MOE_BLOCK_ROWS
    padded = (counts + bm - 1) // bm * bm
    pad_end = jnp.cumsum(padded)
    pad_start = pad_end - padded
    onehot = topi[:, None, :] == jnp.arange(n_e, dtype=jnp.int32)[None, :, None]
    dest = rank + jnp.sum(jnp.where(onehot, pad_start[None, :, None], 0), axis=1)
    n_blocks = n_tok * TOP_K // bm + n_e
    block_start = jnp.arange(n_blocks, dtype=jnp.int32) * bm
    block_expert = jnp.sum((pad_end[None, :] <= block_start[:, None]).astype(jnp.int32), axis=1)
    block_expert = jnp.minimum(block_expert, n_e - 1)
    n_used = (pad_end[-1] // bm).astype(jnp.int32).reshape(1)
    return dest.astype(jnp.int32), block_expert, n_used, n_blocks * bm


def _rope_tables(seq):
    n_freq = QK_ROPE // 4
    inv_freq = ROPE_THETA ** (-jnp.arange(n_freq, dtype=F32) / n_freq)
    pos = jnp.arange(seq, dtype=jnp.int32)
    r = (pos // GRID_W).astype(F32)
    col = (pos % GRID_W).astype(F32)
    ang = jnp.concatenate([r[:, None] * inv_freq, col[:, None] * inv_freq], -1)
    cos, sin = jnp.cos(ang), jnp.sin(ang)
    zeros = jnp.zeros((seq, LANE - QK_ROPE), F32)
    cos_slot = jnp.concatenate([cos, cos, zeros], -1)
    sin_slot = jnp.concatenate([-sin, sin, zeros], -1)
    return cos_slot, sin_slot


def _rope_slot_weights(w_rope):
    k, n, _ = w_rope.shape
    half = QK_ROPE // 2
    swapped = jnp.concatenate([w_rope[..., half:], w_rope[..., :half]], -1)
    pad = jnp.zeros((k, n, LANE - QK_ROPE), w_rope.dtype)
    plain = jnp.concatenate([w_rope, pad], -1).reshape(k, n * LANE)
    swp = jnp.concatenate([swapped, pad], -1).reshape(k, n * LANE)
    return jnp.concatenate([plain, swp], -1)


def kernel(x, c, ctx, c_ctx, ada_w, ada_b, ln_g, ln_b, conf_w1, conf_b1, conf_dw, conf_dwb, conf_ng, conf_nb, conf_w2, conf_b2, sc_w_in, sc_dw, sc_w_out, mla_w_dqkv, mla_q_g, mla_kv_g, mla_w_uq, mla_w_uk, mla_w_uv, mla_w_o, moe_router, moe_bias, moe_w1, moe_w3, moe_w2, sh_w1, sh_w3, sh_w2):
    nb, seq, d = x.shape
    l_ctx = ctx.shape[1]
    depth = ada_w.shape[0]
    alpha = (2.0 * depth) ** 0.25
    tl = SEQ_TILE
    assert seq % tl == 0 and l_ctx % tl == 0 and seq % Q_TILE == 0 and seq % GRID_W == 0
    ntx = seq // tl
    nt_all = (seq + l_ctx) // tl
    attn_layers = [i for i in range(depth) if i % N_MIXERS == 2]
    last_ctx_reader = attn_layers[-1] if attn_layers else -1

    rows = -(-(nb + 1) // 8) * 8
    c_all = jnp.zeros((rows, d), F32).at[:nb].set(c).at[nb].set(c_ctx)
    mod_all = _modulation(c_all, ada_w, ada_b).reshape(depth, rows, N_MOD, d)

    x_all = jnp.concatenate([x, ctx], axis=1)
    q_rank, kv_rank = mla_q_g.shape[1], mla_kv_g.shape[1]
    cos_t, sin_t = _rope_tables(seq)

    for i in range(depth):
        need_ctx = i < last_ctx_reader
        kind, j = i % N_MIXERS, i // N_MIXERS
        nt = nt_all if need_ctx else ntx
        mod = mod_all[i]
        row = lambda v: v.reshape(1, -1)

        if kind == 0:
            u = _conf_in(x_all, mod, conf_w1[j].astype(BF16), row(conf_b1[j]), nt, ntx)
            dw_tiles = jnp.broadcast_to(conf_dw[j][:, None, :], (conf_dw.shape[1], SUBLANES, d))
            a = _conf_conv(u, dw_tiles, row(conf_dwb[j]), row(conf_ng[j]), row(conf_nb[j]), nt, ntx)
            w_last, b_last = conf_w2[j].astype(BF16), row(conf_b2[j])
        elif kind == 1:
            gb, p = _sc_in(x_all, mod, sc_w_in[j].astype(BF16), nt, ntx)
            a = _sc_conv(p, gb, sc_dw[j], nt, ntx)
            w_last, b_last = sc_w_out[j].astype(BF16), jnp.zeros((1, d), F32)
        else:
            wdq = mla_w_dqkv[j]
            wd = wdq[:, :q_rank + kv_rank].astype(BF16)
            wkp = _rope_slot_weights(wdq[:, None, q_rank + kv_rank:]).astype(BF16)
            wuq = mla_w_uq[j].reshape(q_rank, MLA_HEADS, QK_NOPE + QK_ROPE)
            wqn = wuq[:, :, :QK_NOPE].reshape(q_rank, MLA_HEADS * QK_NOPE).astype(BF16)
            wqr = _rope_slot_weights(wuq[:, :, QK_NOPE:]).astype(BF16)
            nt_kv = nt_all
            q, k, v = _mla_proj(x_all, mod, wd, wkp, row(mla_q_g[j]), row(mla_kv_g[j]), wqn, wqr,
                                mla_w_uk[j].astype(BF16), mla_w_uv[j].astype(BF16), cos_t, sin_t, nt_kv, ntx)
            a = _attention(q, k, v, nt * tl)
            w_last, b_last = mla_w_o[j].astype(BF16), jnp.zeros((1, d), F32)

        rw_hi, rw_lo = _split_bf16(moe_router[i].T)
        xn, h2, topi, topw, rank, counts = _post(alpha, a, w_last, b_last, x_all, mod, row(ln_g[i, 0]),
                                                 row(ln_b[i, 0]), rw_hi, rw_lo, moe_bias[i].reshape(-1, 1), nt, ntx)

        n_tok = nb * nt * tl
        dest, block_expert, n_used, n_rows = _dispatch_plan(counts[:, 0], topi, rank, n_tok)
        xs = _sc_row_scatter(h2.reshape(n_tok, d // 2), dest, n_rows)
        ys = _grouped_ffn(block_expert, n_used, xs, moe_w1, moe_w3, moe_w2, i)
        y_rows = _sc_row_gather(ys, dest.reshape(-1)).reshape(TOP_K, n_tok, d // 2)

        ws13 = jnp.concatenate([sh_w1[i], sh_w3[i]], axis=-1).astype(BF16)
        x_all = _moe_out(alpha, xn, h2, y_rows, topw.T, mod, ws13, sh_w2[i].astype(BF16),
                         row(ln_g[i, 1]), row(ln_b[i, 1]), nt, ntx)
    return x_all[:, :seq]
```

```python
import functools

import numpy as np
import jax
import jax.numpy as jnp
from jax import lax
from jax.experimental import pallas as pl
from jax.experimental.pallas import tpu as pltpu
from jax.experimental.pallas import tpu_sc as plsc

F32 = jnp.float32
BF16 = jnp.bfloat16
U32 = jnp.uint32
HIGH_HALF_MASK = np.uint32(0xFFFF0000)

GRID_W = 64
N_MIXERS = 3
LN_EPS = 1e-5
RMS_EPS = 1e-6
N_MOD = 6
MLA_HEADS = 8
QK_NOPE = 128
QK_ROPE = 64
V_HEAD = 128
ROPE_THETA = 10000.0
ATTN_SCALE = (QK_NOPE + QK_ROPE) ** -0.5
Q_SCALE = ATTN_SCALE * 1.4426950408889634
TOP_K = 8
N_GROUPS = 8
TOPK_GROUPS = 4
ROUTED_SCALE = 2.5

SEQ_TILE = 256
CONV_HALO = 16
SHORT_HALO = 8
CONV_ROW_CHUNK = 64
LANE = 128
SUBLANES = 8
Q_TILE = 512
ATTN_SUB_ROWS = 256
ATTN_ROW_CHUNK = 16
MOE_BLOCK_ROWS = 512
MOD_COL_TILE = 1536
VMEM_LIMIT = 48 * 1024 * 1024
SC_CORES = 2
SC_SUBCORES = 16
SC_WORKERS = SC_CORES * SC_SUBCORES
SC_WINDOWS = (64, 32, 16)
N_CHAINS = 2


def _params(n_axes):
    return pltpu.CompilerParams(dimension_semantics=("arbitrary",) * n_axes,
                                vmem_limit_bytes=VMEM_LIMIT)


def _split_bf16(a):
    hi = a.astype(BF16)
    lo = (a - hi.astype(F32)).astype(BF16)
    return hi, lo


def _dot(a, b):
    return jnp.dot(a, b, preferred_element_type=F32)


def _dot_nt(a, b):
    return lax.dot_general(a, b, (((1,), (1,)), ((), ())), preferred_element_type=F32)


def _pack_halves(v):
    half = v.shape[-1] // 2
    lo = lax.bitcast_convert_type(v[:, :half].astype(BF16).astype(F32), U32) >> 16
    hi = lax.bitcast_convert_type(v[:, half:].astype(BF16).astype(F32), U32) & HIGH_HALF_MASK
    return hi | lo


def _unpack_halves(p):
    lo = lax.bitcast_convert_type(p << 16, F32)
    hi = lax.bitcast_convert_type(p & HIGH_HALF_MASK, F32)
    return lo, hi


def _dot_packed(p, w):
    lo, hi = _unpack_halves(p)
    half = p.shape[-1]
    return _dot(lo.astype(BF16), w[:half]) + _dot(hi.astype(BF16), w[half:])


def _layer_norm(v, g, b):
    mu = jnp.mean(v, axis=-1, keepdims=True)
    c = v - mu
    var = jnp.mean(c * c, axis=-1, keepdims=True)
    return c * lax.rsqrt(var + LN_EPS) * g + b


def _silu(v):
    return v * jax.nn.sigmoid(v)


def _mod_kernel(c_ref, w_ref, b_ref, o_ref):
    a = _silu(c_ref[...])
    a_hi, a_lo = _split_bf16(a)
    w_hi, w_lo = _split_bf16(w_ref[0])
    o_ref[0] = _dot(a_hi, w_hi) + _dot(a_hi, w_lo) + _dot(a_lo, w_hi) + b_ref[0]


def _modulation(c_all, ada_w, ada_b):
    depth, d, n = ada_w.shape
    rows = c_all.shape[0]
    tn = MOD_COL_TILE
    return pl.pallas_call(
        _mod_kernel,
        out_shape=jax.ShapeDtypeStruct((depth, rows, n), F32),
        grid=(depth, n // tn),
        in_specs=[pl.BlockSpec((rows, d), lambda i, j: (0, 0)),
                  pl.BlockSpec((1, d, tn), lambda i, j: (i, 0, j)),
                  pl.BlockSpec((1, 1, tn), lambda i, j: (i, 0, j))],
        out_specs=pl.BlockSpec((1, rows, tn), lambda i, j: (i, 0, j)),
        compiler_params=_params(2),
        name="adaln_mod",
    )(c_all, ada_w, ada_b.reshape(depth, 1, n))


def _tok_spec(d, tl=SEQ_TILE):
    return pl.BlockSpec((1, tl, d), lambda b, l: (b, l, 0))


def _mod_spec(d, n_batch, ntx):
    return pl.BlockSpec((1, N_MOD, d), lambda b, l: (jnp.where(l < ntx, b, n_batch), 0, 0))


def _full_spec(shape):
    zeros = (0,) * len(shape)
    return pl.BlockSpec(shape, lambda b, l: zeros)


def _halo_specs(d, halo, seq_len, tl=SEQ_TILE):
    per_tile = tl // halo
    last = seq_len // halo - 1
    prev = pl.BlockSpec((1, halo, d), lambda b, l: (b, jnp.maximum(l * per_tile - 1, 0), 0))
    nxt = pl.BlockSpec((1, halo, d), lambda b, l: (b, jnp.minimum((l + 1) * per_tile, last), 0))
    return prev, nxt


def _segment_edges(l, ntx, nt):
    first = jnp.logical_or(l == 0, l == ntx)
    last = jnp.logical_or(l == ntx - 1, l == nt - 1)
    return first, last


def _conf_in_kernel(x_ref, mod_ref, w1_ref, b1_ref, u_ref):
    d = x_ref.shape[-1]
    h = x_ref[0] * (1.0 + mod_ref[0, 1:2, :]) + mod_ref[0, 0:1, :]
    z = _dot(h.astype(BF16), w1_ref[...]) + b1_ref[...]
    u_ref[0] = z[:, :d] * jax.nn.sigmoid(z[:, d:])


def _conf_in(x_all, mod, w1, b1, nt, ntx):
    nb, seq_len, d = x_all.shape
    return pl.pallas_call(
        _conf_in_kernel,
        out_shape=jax.ShapeDtypeStruct((nb, nt * SEQ_TILE, d), F32),
        grid=(nb, nt),
        in_specs=[_tok_spec(d), _mod_spec(d, nb, ntx), _full_spec(w1.shape), _full_spec(b1.shape)],
        out_specs=_tok_spec(d),
        compiler_params=_params(2),
        name="conf_in",
    )(x_all, mod, w1, b1)


def _conf_conv_kernel(ntx, nt, up_ref, uc_ref, un_ref, dw_ref, dwb_ref, ng_ref, nb_ref,
                      a_ref, sh_ref, conv_ref):
    tl, d = uc_ref.shape[1], uc_ref.shape[2]
    taps = dw_ref.shape[0]
    lead = CONV_HALO - (taps - 1) // 2
    first, last = _segment_edges(pl.program_id(1), ntx, nt)
    sh_ref[0, 0:CONV_HALO, :] = jnp.where(first, 0.0, up_ref[0])
    sh_ref[0, CONV_HALO:CONV_HALO + tl, :] = uc_ref[0]
    sh_ref[0, CONV_HALO + tl:, :] = jnp.where(last, 0.0, un_ref[0])
    span = tl + 2 * CONV_HALO - SUBLANES
    for s in range(1, SUBLANES):
        sh_ref[s, 0:span, :] = sh_ref[0, s:s + span, :]
    groups = CONV_ROW_CHUNK // SUBLANES

    def row_chunk(i, carry):
        r0 = pl.multiple_of(i * CONV_ROW_CHUNK, CONV_ROW_CHUNK)
        for c0 in range(0, d, LANE):
            accs = [jnp.zeros((SUBLANES, LANE), F32) for _ in range(groups)]
            for k in range(taps):
                res = (lead + k) % SUBLANES
                off = lead + k - res
                w = dw_ref[k, :, c0:c0 + LANE]
                for g in range(groups):
                    lo = r0 + (off + g * SUBLANES)
                    accs[g] = accs[g] + w * sh_ref[res, pl.ds(lo, SUBLANES), c0:c0 + LANE]
            for g in range(groups):
                conv_ref[pl.ds(r0 + g * SUBLANES, SUBLANES), c0:c0 + LANE] = accs[g]
        return carry

    lax.fori_loop(0, tl // CONV_ROW_CHUNK, row_chunk, 0)
    v = _layer_norm(conv_ref[...] + dwb_ref[...], ng_ref[...], nb_ref[...])
    a_ref[0] = _silu(v).astype(BF16)


def _conf_conv(u, dw, dwb, ng, nb_, nt, ntx):
    nb, seq_len, d = u.shape
    prev, nxt = _halo_specs(d, CONV_HALO, seq_len)
    return pl.pallas_call(
        functools.partial(_conf_conv_kernel, ntx, nt),
        out_shape=jax.ShapeDtypeStruct((nb, seq_len, d), BF16),
        grid=(nb, nt),
        in_specs=[prev, _tok_spec(d), nxt, _full_spec(dw.shape), _full_spec(dwb.shape),
                  _full_spec(ng.shape), _full_spec(nb_.shape)],
        out_specs=_tok_spec(d),
        scratch_shapes=[pltpu.VMEM((SUBLANES, SEQ_TILE + 2 * CONV_HALO, d), F32),
                        pltpu.VMEM((SEQ_TILE, d), F32)],
        compiler_params=_params(2),
        name="conf_conv",
    )(u, u, u, dw, dwb, ng, nb_)


def _sc_in_kernel(x_ref, mod_ref, w_ref, gb_ref, p_ref):
    d = x_ref.shape[-1]
    h = x_ref[0] * (1.0 + mod_ref[0, 1:2, :]) + mod_ref[0, 0:1, :]
    z = _dot(h.astype(BF16), w_ref[...])
    gb_ref[0] = z[:, :d]
    p_ref[0] = z[:, d:2 * d] * z[:, 2 * d:]


def _sc_in(x_all, mod, w_in, nt, ntx):
    nb, seq_len, d = x_all.shape
    shp = jax.ShapeDtypeStruct((nb, nt * SEQ_TILE, d), F32)
    return pl.pallas_call(
        _sc_in_kernel,
        out_shape=(shp, shp),
        grid=(nb, nt),
        in_specs=[_tok_spec(d), _mod_spec(d, nb, ntx), _full_spec(w_in.shape)],
        out_specs=(_tok_spec(d), _tok_spec(d)),
        compiler_params=_params(2),
        name="sc_in",
    )(x_all, mod, w_in)


def _sc_conv_kernel(ntx, nt, pp_ref, pc_ref, pn_ref, gb_ref, dw_ref, a_ref, ext_ref):
    tl = pc_ref.shape[1]
    taps = dw_ref.shape[0]
    lead = SHORT_HALO - (taps - 1) // 2
    first, last = _segment_edges(pl.program_id(1), ntx, nt)
    ext_ref[0:SHORT_HALO, :] = jnp.where(first, 0.0, pp_ref[0])
    ext_ref[SHORT_HALO:SHORT_HALO + tl, :] = pc_ref[0]
    ext_ref[SHORT_HALO + tl:, :] = jnp.where(last, 0.0, pn_ref[0])
    acc = dw_ref[0:1, :] * ext_ref[lead:lead + tl, :]
    for k in range(1, taps):
        acc = acc + dw_ref[k:k + 1, :] * ext_ref[lead + k:lead + k + tl, :]
    a_ref[0] = (gb_ref[0] * acc).astype(BF16)


def _sc_conv(p, gb, dw, nt, ntx):
    nb, seq_len, d = p.shape
    prev, nxt = _halo_specs(d, SHORT_HALO, seq_len)
    return pl.pallas_call(
        functools.partial(_sc_conv_kernel, ntx, nt),
        out_shape=jax.ShapeDtypeStruct((nb, seq_len, d), BF16),
        grid=(nb, nt),
        in_specs=[prev, _tok_spec(d), nxt, _tok_spec(d), _full_spec(dw.shape)],
        out_specs=_tok_spec(d),
        scratch_shapes=[pltpu.VMEM((SEQ_TILE + 2 * SHORT_HALO, d), F32)],
        compiler_params=_params(2),
        name="sc_conv",
    )(p, p, p, gb, dw)


def _rms(v, g):
    return v * lax.rsqrt(jnp.mean(v * v, axis=-1, keepdims=True) + RMS_EPS) * g


def _mla_proj_kernel(ntx, q_rank,
                     x_ref, mod_ref, wd_ref, wkp_ref, qg_ref, kvg_ref, wqn_ref, wqr_ref, wuk_ref, wuv_ref,
                     cos_ref, sin_ref, q_ref, k_ref, v_ref):
    is_latent = pl.program_id(1) < ntx
    h = (x_ref[0] * (1.0 + mod_ref[0, 1:2, :]) + mod_ref[0, 0:1, :]).astype(BF16)
    dn = _dot(h, wd_ref[...])
    cq = _rms(dn[:, :q_rank], qg_ref[...]).astype(BF16)
    ckv = _rms(dn[:, q_rank:], kvg_ref[...]).astype(BF16)
    cos = jnp.where(is_latent, cos_ref[...], 1.0)
    sin = jnp.where(is_latent, sin_ref[...], 0.0)
    kp2 = _dot(h, wkp_ref[...])
    kp = kp2[:, :LANE] * cos + kp2[:, LANE:] * sin
    kn = _dot(ckv, wuk_ref[...])
    vv = _dot(ckv, wuv_ref[...])
    qn = _dot(cq, wqn_ref[...]) * Q_SCALE
    qr2 = _dot(cq, wqr_ref[...])
    hw = MLA_HEADS * LANE
    for hd in range(MLA_HEADS):
        sl = slice(hd * LANE, (hd + 1) * LANE)
        qr = (qr2[:, sl] * cos + qr2[:, hw + hd * LANE:hw + (hd + 1) * LANE] * sin) * Q_SCALE
        q_ref[0, hd, :, 0:LANE] = qn[:, sl].astype(BF16)
        q_ref[0, hd, :, LANE:] = qr.astype(BF16)
        k_ref[0, hd, :, 0:LANE] = kn[:, sl].astype(BF16)
        k_ref[0, hd, :, LANE:] = kp.astype(BF16)
        v_ref[0, hd] = vv[:, sl].astype(BF16)


def _mla_proj(x_all, mod, wd, wkp, qg, kvg, wqn, wqr, wuk, wuv, cos_t, sin_t, nt, ntx):
    nb, seq_len, d = x_all.shape
    q_rank = qg.shape[-1]
    tl = SEQ_TILE
    rope_spec = pl.BlockSpec((tl, LANE), lambda b, l: (jnp.minimum(l, ntx - 1), 0))
    qk_shape = jax.ShapeDtypeStruct((nb, MLA_HEADS, nt * tl, 2 * LANE), BF16)
    v_shape = jax.ShapeDtypeStruct((nb, MLA_HEADS, nt * tl, LANE), BF16)
    qk_spec = pl.BlockSpec((1, MLA_HEADS, tl, 2 * LANE), lambda b, l: (b, 0, l, 0))
    v_spec = pl.BlockSpec((1, MLA_HEADS, tl, LANE), lambda b, l: (b, 0, l, 0))
    return pl.pallas_call(
        functools.partial(_mla_proj_kernel, ntx, q_rank),
        out_shape=(qk_shape, qk_shape, v_shape),
        grid=(nb, nt),
        in_specs=[_tok_spec(d), _mod_spec(d, nb, ntx), _full_spec(wd.shape), _full_spec(wkp.shape),
                  _full_spec(qg.shape), _full_spec(kvg.shape), _full_spec(wqn.shape), _full_spec(wqr.shape),
                  _full_spec(wuk.shape), _full_spec(wuv.shape), rope_spec, rope_spec],
        out_specs=(qk_spec, qk_spec, v_spec),
        compiler_params=_params(2),
        name="mla_proj",
    )(x_all, mod, wd, wkp, qg, kvg, wqn, wqr, wuk, wuv, cos_t, sin_t)


def _attn_kernel(q_ref, k_ref, v_ref, o_ref, s_ref, p_ref, l_ref):
    tq = q_ref.shape[2]
    for r0 in range(0, tq, ATTN_SUB_ROWS):
        rows = slice(r0, r0 + ATTN_SUB_ROWS)
        s_ref[rows, :] = _dot_nt(q_ref[0, 0, rows, :], k_ref[0, 0])
        for c0 in range(r0, r0 + ATTN_SUB_ROWS, ATTN_ROW_CHUNK):
            chunk = slice(c0, c0 + ATTN_ROW_CHUNK)
            s = s_ref[chunk, :]
            p = jnp.exp2(s - jnp.max(s, axis=-1, keepdims=True))
            l_ref[chunk, :] = jnp.broadcast_to(jnp.sum(p, axis=-1, keepdims=True), (ATTN_ROW_CHUNK, V_HEAD))
            p_ref[chunk, :] = p.astype(BF16)
        o = _dot(p_ref[rows, :], v_ref[0, 0])
        o_ref[0, rows, :] = (o / l_ref[rows, :]).astype(BF16)


def _attention(q, k, v, n_q):
    nb, nh, n_k, dk = k.shape
    tq = Q_TILE
    return pl.pallas_call(
        _attn_kernel,
        out_shape=jax.ShapeDtypeStruct((nb, n_q, nh * V_HEAD), BF16),
        grid=(nb, nh, n_q // tq),
        in_specs=[pl.BlockSpec((1, 1, tq, dk), lambda b, h, i: (b, h, i, 0)),
                  pl.BlockSpec((1, 1, n_k, dk), lambda b, h, i: (b, h, 0, 0)),
                  pl.BlockSpec((1, 1, n_k, V_HEAD), lambda b, h, i: (b, h, 0, 0))],
        out_specs=pl.BlockSpec((1, tq, V_HEAD), lambda b, h, i: (b, i, h)),
        scratch_shapes=[pltpu.VMEM((tq, n_k), F32), pltpu.VMEM((tq, n_k), BF16), pltpu.VMEM((tq, V_HEAD), F32)],
        compiler_params=_params(3),
        name="mla_attn",
    )(q, k, v)


def _route(sel, s):
    n_e, n_t = sel.shape
    per = n_e // N_GROUPS
    sel3 = sel.reshape(N_GROUPS, per, n_t)
    s3 = s.reshape(N_GROUPS, per, n_t)
    iota_p = lax.broadcasted_iota(jnp.int32, (N_GROUPS, per, n_t), 1).astype(F32)
    iota_g = lax.broadcasted_iota(jnp.int32, (N_GROUPS, 1, n_t), 0).astype(F32)
    neg = -jnp.inf
    m1 = jnp.max(sel3, axis=1, keepdims=True)
    first = jnp.min(jnp.where(sel3 == m1, iota_p, float(per)), axis=1, keepdims=True)
    m2 = jnp.max(jnp.where(iota_p == first, neg, sel3), axis=1, keepdims=True)
    gs = m1 + m2
    gsel = jnp.zeros((N_GROUPS, 1, n_t), F32)
    for _ in range(TOPK_GROUPS):
        gm = jnp.max(gs, axis=0, keepdims=True)
        gfirst = jnp.min(jnp.where(gs == gm, iota_g, float(N_GROUPS)), axis=0, keepdims=True)
        pick = iota_g == gfirst
        gsel = jnp.where(pick, 1.0, gsel)
        gs = jnp.where(pick, neg, gs)
    val = jnp.where(gsel > 0.0, sel3, neg)
    iota_e = lax.broadcasted_iota(jnp.int32, (N_GROUPS, per, n_t), 0).astype(F32) * per + iota_p
    chosen = jnp.zeros((N_GROUPS, per, n_t), F32)
    idx, wts = [], []
    for _ in range(TOP_K):
        m = jnp.max(jnp.max(val, axis=1, keepdims=True), axis=0, keepdims=True)
        e = jnp.min(jnp.min(jnp.where(val == m, iota_e, float(n_e)), axis=1, keepdims=True), axis=0, keepdims=True)
        pick = iota_e == e
        wts.append(jnp.sum(jnp.sum(jnp.where(pick, s3, 0.0), axis=1, keepdims=True), axis=0, keepdims=True))
        idx.append(e)
        chosen = jnp.where(pick, 1.0, chosen)
        val = jnp.where(pick, neg, val)
    total = wts[0]
    for w in wts[1:]:
        total = total + w
    wts = [w / total * ROUTED_SCALE for w in wts]
    return idx, wts, chosen, iota_e


def _post_kernel(alpha, a_ref, w_ref, b_ref, x_ref, mod_ref, lng_ref, lnb_ref, rw_hi_ref, rw_lo_ref, rb_ref, tri_ref,
                 xn_ref, h2_ref, topi_ref, topw_ref, rank_ref, cnt_ref, run_ref):
    first_step = jnp.logical_and(pl.program_id(0) == 0, pl.program_id(1) == 0)

    @pl.when(first_step)
    def _():
        run_ref[...] = jnp.zeros_like(run_ref)

    y = _dot(a_ref[0], w_ref[...]) + b_ref[...]
    xn = _layer_norm(alpha * x_ref[0] + mod_ref[0, 2:3, :] * y, lng_ref[...], lnb_ref[...])
    xn_ref[0] = xn
    h2 = xn * (1.0 + mod_ref[0, 4:5, :]) + mod_ref[0, 3:4, :]
    h_hi, h_lo = _split_bf16(h2)
    h2_ref[0] = _pack_halves(h2)
    logits = _dot_nt(rw_hi_ref[...], h_hi) + _dot_nt(rw_hi_ref[...], h_lo) + _dot_nt(rw_lo_ref[...], h_hi)
    s = jax.nn.sigmoid(logits)
    idx, wts, chosen3, iota_e = _route(s + rb_ref[...], s)
    n_e, n_t = s.shape
    chosen = chosen3.reshape(n_e, n_t)
    rank = run_ref[:, 0:1] + _dot(chosen.astype(BF16), tri_ref[...])
    rank3 = rank.reshape(chosen3.shape)
    for k in range(TOP_K):
        rk = jnp.sum(jnp.sum(jnp.where(iota_e == idx[k], rank3, 0.0), axis=1, keepdims=True), axis=0, keepdims=True)
        topi_ref[k:k + 1, :] = idx[k].reshape(1, n_t).astype(jnp.int32)
        topw_ref[k:k + 1, :] = wts[k].reshape(1, n_t)
        rank_ref[k:k + 1, :] = rk.reshape(1, n_t).astype(jnp.int32)
    run_ref[...] = run_ref[...] + jnp.sum(chosen, axis=1, keepdims=True)
    cnt_ref[...] = run_ref[...].astype(jnp.int32)


def _post(alpha, a, w, bias, x_all, mod, lng, lnb, rw_hi, rw_lo, rb, nt, ntx):
    nb, _, dk = a.shape
    d = x_all.shape[-1]
    n_e = rw_hi.shape[0]
    tl = SEQ_TILE
    n_tok = nb * nt * tl
    col_spec = lambda rows: pl.BlockSpec((rows, tl), lambda b, l: (0, b * nt + l))
    row_i = lax.broadcasted_iota(jnp.int32, (tl, tl), 0)
    col_i = lax.broadcasted_iota(jnp.int32, (tl, tl), 1)
    tri = jnp.where(row_i < col_i, 1.0, 0.0).astype(BF16)
    return pl.pallas_call(
        functools.partial(_post_kernel, alpha),
        out_shape=(jax.ShapeDtypeStruct((nb, nt * tl, d), F32),
                   jax.ShapeDtypeStruct((nb, nt * tl, d // 2), U32),
                   jax.ShapeDtypeStruct((TOP_K, n_tok), jnp.int32),
                   jax.ShapeDtypeStruct((TOP_K, n_tok), F32),
                   jax.ShapeDtypeStruct((TOP_K, n_tok), jnp.int32),
                   jax.ShapeDtypeStruct((n_e, LANE), jnp.int32)),
        grid=(nb, nt),
        in_specs=[_tok_spec(dk), _full_spec(w.shape), _full_spec(bias.shape), _tok_spec(d),
                  _mod_spec(d, nb, ntx), _full_spec(lng.shape), _full_spec(lnb.shape),
                  _full_spec(rw_hi.shape), _full_spec(rw_lo.shape), _full_spec(rb.shape), _full_spec(tri.shape)],
        out_specs=(_tok_spec(d), _tok_spec(d // 2), col_spec(TOP_K), col_spec(TOP_K), col_spec(TOP_K),
                   _full_spec((n_e, LANE))),
        scratch_shapes=[pltpu.VMEM((n_e, LANE), F32)],
        compiler_params=_params(2),
        name="mixer_post",
    )(a, w, bias, x_all, mod, lng, lnb, rw_hi, rw_lo, rb, tri)


def _gmm_kernel(be_ref, nu_ref, xs_ref, w1_ref, w3_ref, w2_ref, ys_ref, w13_s, w2_s):
    i = pl.program_id(0)
    de = w2_ref.shape[2]
    changed = jnp.logical_or(i == 0, be_ref[i] != be_ref[jnp.maximum(i - 1, 0)])

    @pl.when(changed)
    def _():
        w13_s[:, :de] = w1_ref[0, 0].astype(BF16)
        w13_s[:, de:] = w3_ref[0, 0].astype(BF16)
        w2_s[...] = w2_ref[0, 0].astype(BF16)

    @pl.when(i < nu_ref[0])
    def _():
        z = _dot_packed(xs_ref[...], w13_s)
        hmid = (_silu(z[:, :de]) * z[:, de:]).astype(BF16)
        ys_ref[...] = _pack_halves(_dot(hmid, w2_s[...]))


def _grouped_ffn(block_expert, n_used, xs, w1, w3, w2, layer):
    n_rows, dp = xs.shape
    d = 2 * dp
    bm = MOE_BLOCK_ROWS
    de = w2.shape[2]
    grid_spec = pltpu.PrefetchScalarGridSpec(
        num_scalar_prefetch=2,
        grid=(n_rows // bm,),
        in_specs=[pl.BlockSpec((bm, dp), lambda i, be, nu: (i, 0)),
                  pl.BlockSpec((1, 1, d, de), lambda i, be, nu: (layer, be[i], 0, 0)),
                  pl.BlockSpec((1, 1, d, de), lambda i, be, nu: (layer, be[i], 0, 0)),
                  pl.BlockSpec((1, 1, de, d), lambda i, be, nu: (layer, be[i], 0, 0))],
        out_specs=pl.BlockSpec((bm, dp), lambda i, be, nu: (i, 0)),
        scratch_shapes=[pltpu.VMEM((d, 2 * de), BF16), pltpu.VMEM((de, d), BF16)],
    )
    return pl.pallas_call(
        _gmm_kernel,
        out_shape=jax.ShapeDtypeStruct((n_rows, dp), U32),
        grid_spec=grid_spec,
        compiler_params=_params(1),
        name="moe_grouped_ffn",
    )(block_expert, n_used, xs, w1, w3, w2)


def _sc_mesh():
    return plsc.VectorSubcoreMesh(core_axis_name="c", subcore_axis_name="s",
                                  num_cores=SC_CORES, num_subcores=SC_SUBCORES)


def _sc_worker():
    return lax.axis_index("s") * SC_CORES + lax.axis_index("c")


def _sc_window(rows_per_worker):
    for win in SC_WINDOWS:
        if rows_per_worker % (2 * win) == 0:
            return win
    raise ValueError(f"no SparseCore window divides {rows_per_worker} rows per worker")


def _sc_row_scatter(src, dest, n_out):
    n_k, n_tok = dest.shape
    dp = src.shape[1]
    assert n_tok % SC_WORKERS == 0
    per_w = n_tok // SC_WORKERS
    win = _sc_window(per_w)
    n_win = per_w // win
    idx = dest.reshape(n_k, SC_WORKERS, n_win, win).transpose(1, 2, 0, 3)

    def body(src_hbm, idx_hbm, out_hbm, idx_v, rows_v, load_sem, scat_sem):
        wid = _sc_worker()
        base = wid * per_w
        pltpu.sync_copy(idx_hbm.at[wid], idx_v)

        def load(g, slot):
            return pltpu.make_async_copy(src_hbm.at[pl.ds(base + g * win, win)], rows_v.at[slot], load_sem.at[slot])

        def scatter(g, slot, k):
            return pltpu.make_async_copy(rows_v.at[slot], out_hbm.at[idx_v.at[g, k]], scat_sem.at[slot])

        load(0, 0).start()

        @pl.loop(0, n_win, step=2)
        def _(g):
            for slot in range(2):
                cur = g + slot
                load(cur, slot).wait()

                @pl.when(cur + 1 < n_win)
                def _():
                    @pl.when(cur >= 1)
                    def _():
                        for k in range(n_k):
                            scatter(cur - 1, 1 - slot, k).wait()
                    load(cur + 1, 1 - slot).start()

                for k in range(n_k):
                    scatter(cur, slot, k).start()

        for slot in range(2):
            for k in range(n_k):
                scatter(n_win - 2 + slot, slot, k).wait()

    return pl.kernel(
        body, mesh=_sc_mesh(),
        out_type=jax.ShapeDtypeStruct((n_out, dp), src.dtype),
        scratch_types=[pltpu.VMEM((n_win, n_k, win), jnp.int32),
                       pltpu.VMEM((2, win, dp), src.dtype),
                       pltpu.SemaphoreType.DMA((2,)),
                       pltpu.SemaphoreType.DMA((2,))],
        compiler_params=pltpu.CompilerParams(use_tc_tiling_on_sc=True),
        name="sc_dispatch_scatter",
    )(src, idx)


def _sc_row_gather(table, idx):
    n = idx.shape[0]
    dp = table.shape[1]
    assert n % SC_WORKERS == 0
    per_w = n // SC_WORKERS
    win = _sc_window(per_w)
    n_win = per_w // win

    def body(table_hbm, idx_hbm, out_hbm, idx_v, rows_v, gather_sem, put_sem):
        wid = _sc_worker()
        base = wid * per_w
        pltpu.sync_copy(idx_hbm.at[pl.ds(base, per_w)], idx_v)

        def gather(g, slot):
            return pltpu.make_async_copy(table_hbm.at[idx_v.at[pl.ds(g * win, win)]], rows_v.at[slot],
                                         gather_sem.at[slot])

        def put(g, slot):
            return pltpu.make_async_copy(rows_v.at[slot], out_hbm.at[pl.ds(base + g * win, win)], put_sem.at[slot])

        gather(0, 0).start()

        @pl.loop(0, n_win, step=2)
        def _(g):
            for slot in range(2):
                cur = g + slot
                gather(cur, slot).wait()

                @pl.when(cur + 1 < n_win)
                def _():
                    @pl.when(cur >= 1)
                    def _():
                        put(cur - 1, 1 - slot).wait()
                    gather(cur + 1, 1 - slot).start()

                put(cur, slot).start()

        for slot in range(2):
            put(n_win - 2 + slot, slot).wait()

    return pl.kernel(
        body, mesh=_sc_mesh(),
        out_type=jax.ShapeDtypeStruct((n, dp), table.dtype),
        scratch_types=[pltpu.VMEM((per_w,), jnp.int32),
                       pltpu.VMEM((2, win, dp), table.dtype),
                       pltpu.SemaphoreType.DMA((2,)),
                       pltpu.SemaphoreType.DMA((2,))],
        compiler_params=pltpu.CompilerParams(use_tc_tiling_on_sc=True),
        name="sc_combine_gather",
    )(table, idx)


def _moe_out_kernel(alpha, x_ref, h2_ref, yg_ref, tw_ref, mod_ref, ws13_ref, ws2_ref, lng_ref, lnb_ref, o_ref):
    de = ws2_ref.shape[0]
    z = _dot_packed(h2_ref[0], ws13_ref)
    hmid = (_silu(z[:, :de]) * z[:, de:]).astype(BF16)
    y = _dot(hmid, ws2_ref[...])
    lo, hi = _unpack_halves(yg_ref[0])
    acc_lo, acc_hi = lo * tw_ref[:, 0:1], hi * tw_ref[:, 0:1]
    for k in range(1, yg_ref.shape[0]):
        lo, hi = _unpack_halves(yg_ref[k])
        acc_lo = acc_lo + lo * tw_ref[:, k:k + 1]
        acc_hi = acc_hi + hi * tw_ref[:, k:k + 1]
    y = y + jnp.concatenate([acc_lo, acc_hi], axis=1)
    o_ref[0] = _layer_norm(alpha * x_ref[0] + mod_ref[0, 5:6, :] * y, lng_ref[...], lnb_ref[...])


def _moe_out(alpha, xn, h2, y_rows, topw_t, mod, ws13, ws2, lng, lnb, nt, ntx):
    nb, _, d = xn.shape
    n_k = y_rows.shape[0]
    tl = SEQ_TILE
    return pl.pallas_call(
        functools.partial(_moe_out_kernel, alpha),
        out_shape=jax.ShapeDtypeStruct((nb, nt * tl, d), F32),
        grid=(nb, nt),
        in_specs=[_tok_spec(d), _tok_spec(d // 2),
                  pl.BlockSpec((n_k, tl, d // 2), lambda b, l: (0, b * nt + l, 0)),
                  pl.BlockSpec((tl, n_k), lambda b, l: (b * nt + l, 0)),
                  _mod_spec(d, nb, ntx),
                  _full_spec(ws13.shape), _full_spec(ws2.shape), _full_spec(lng.shape), _full_spec(lnb.shape)],
        out_specs=_tok_spec(d),
        compiler_params=_params(2),
        name="moe_out",
    )(xn, h2, y_rows, topw_t, mod, ws13, ws2, lng, lnb)


def _dispatch_plan(counts, topi, rank, n_tok):
    n_e = counts.shape[0]
    bm = MOE_BLOCK_ROWS
    padded = (counts + bm - 1) // bm * bm
    pad_end = jnp.cumsum(padded)
    pad_start = pad_end - padded
    onehot = topi[:, None, :] == jnp.arange(n_e, dtype=jnp.int32)[None, :, None]
    dest = rank + jnp.sum(jnp.where(onehot, pad_start[None, :, None], 0), axis=1)
    n_blocks = n_tok * TOP_K // bm + n_e
    block_start = jnp.arange(n_blocks, dtype=jnp.int32) * bm
    block_expert = jnp.sum((pad_end[None, :] <= block_start[:, None]).astype(jnp.int32), axis=1)
    block_expert = jnp.minimum(block_expert, n_e - 1)
    n_used = (pad_end[-1] // bm).astype(jnp.int32).reshape(1)
    return dest.astype(jnp.int32), block_expert, n_used, n_blocks * bm


def _rope_tables(seq):
    n_freq = QK_ROPE // 4
    inv_freq = ROPE_THETA ** (-jnp.arange(n_freq, dtype=F32) / n_freq)
    pos = jnp.arange(seq, dtype=jnp.int32)
    r = (pos // GRID_W).astype(F32)
    col = (pos % GRID_W).astype(F32)
    ang = jnp.concatenate([r[:, None] * inv_freq, col[:, None] * inv_freq], -1)
    cos, sin = jnp.cos(ang), jnp.sin(ang)
    zeros = jnp.zeros((seq, LANE - QK_ROPE), F32)
    cos_slot = jnp.concatenate([cos, cos, zeros], -1)
    sin_slot = jnp.concatenate([-sin, sin, zeros], -1)
    return cos_slot, sin_slot


def _rope_slot_weights(w_rope):
    k, n, _ = w_rope.shape
    half = QK_ROPE // 2
    swapped = jnp.concatenate([w_rope[..., half:], w_rope[..., :half]], -1)
    pad = jnp.zeros((k, n, LANE - QK_ROPE), w_rope.dtype)
    plain = jnp.concatenate([w_rope, pad], -1).reshape(k, n * LANE)
    swp = jnp.concatenate([swapped, pad], -1).reshape(k, n * LANE)
    return jnp.concatenate([plain, swp], -1)


def kernel(x, c, ctx, c_ctx, ada_w, ada_b, ln_g, ln_b, conf_w1, conf_b1, conf_dw, conf_dwb, conf_ng, conf_nb, conf_w2, conf_b2, sc_w_in, sc_dw, sc_w_out, mla_w_dqkv, mla_q_g, mla_kv_g, mla_w_uq, mla_w_uk, mla_w_uv, mla_w_o, moe_router, moe_bias, moe_w1, moe_w3, moe_w2, sh_w1, sh_w3, sh_w2):
    nb, seq, d = x.shape
    l_ctx = ctx.shape[1]
    depth = ada_w.shape[0]
    alpha = (2.0 * depth) ** 0.25
    tl = SEQ_TILE
    assert seq % tl == 0 and l_ctx % tl == 0 and seq % Q_TILE == 0 and seq % GRID_W == 0
    ntx = seq // tl
    nt_all = (seq + l_ctx) // tl
    attn_layers = [i for i in range(depth) if i % N_MIXERS == 2]
    last_ctx_reader = attn_layers[-1] if attn_layers else -1

    rows = -(-(nb + 1) // 8) * 8
    c_all = jnp.zeros((rows, d), F32).at[:nb].set(c).at[nb].set(c_ctx)
    mod_all = _modulation(c_all, ada_w, ada_b).reshape(depth, rows, N_MOD, d)

    assert nb % N_CHAINS == 0
    nbc = nb // N_CHAINS
    chains = [jnp.concatenate([x[c0:c0 + nbc], ctx[c0:c0 + nbc]], axis=1) for c0 in range(0, nb, nbc)]
    q_rank, kv_rank = mla_q_g.shape[1], mla_kv_g.shape[1]
    cos_t, sin_t = _rope_tables(seq)
    row = lambda v: v.reshape(1, -1)

    for i in range(depth):
        need_ctx = i < last_ctx_reader
        kind, j = i % N_MIXERS, i // N_MIXERS
        nt = nt_all if need_ctx else ntx
        n_tok = nbc * nt * tl

        if kind == 0:
            w_first = conf_w1[j].astype(BF16)
            dw_tiles = jnp.broadcast_to(conf_dw[j][:, None, :], (conf_dw.shape[1], SUBLANES, d))
            w_last, b_last = conf_w2[j].astype(BF16), row(conf_b2[j])
        elif kind == 1:
            w_first = sc_w_in[j].astype(BF16)
            w_last, b_last = sc_w_out[j].astype(BF16), jnp.zeros((1, d), F32)
        else:
            wdq = mla_w_dqkv[j]
            wd = wdq[:, :q_rank + kv_rank].astype(BF16)
            wkp = _rope_slot_weights(wdq[:, None, q_rank + kv_rank:]).astype(BF16)
            wuq = mla_w_uq[j].reshape(q_rank, MLA_HEADS, QK_NOPE + QK_ROPE)
            wqn = wuq[:, :, :QK_NOPE].reshape(q_rank, MLA_HEADS * QK_NOPE).astype(BF16)
            wqr = _rope_slot_weights(wuq[:, :, QK_NOPE:]).astype(BF16)
            wuk, wuv = mla_w_uk[j].astype(BF16), mla_w_uv[j].astype(BF16)
            w_last, b_last = mla_w_o[j].astype(BF16), jnp.zeros((1, d), F32)
        rw_hi, rw_lo = _split_bf16(moe_router[i].T)
        ws13 = jnp.concatenate([sh_w1[i], sh_w3[i]], axis=-1).astype(BF16)
        ws2 = sh_w2[i].astype(BF16)

        for ci in range(N_CHAINS):
            x_all = chains[ci]
            mod = jnp.concatenate([mod_all[i, ci * nbc:(ci + 1) * nbc], mod_all[i, nb:nb + 1]], axis=0)

            if kind == 0:
                u = _conf_in(x_all, mod, w_first, row(conf_b1[j]), nt, ntx)
                a = _conf_conv(u, dw_tiles, row(conf_dwb[j]), row(conf_ng[j]), row(conf_nb[j]), nt, ntx)
            elif kind == 1:
                gb, p = _sc_in(x_all, mod, w_first, nt, ntx)
                a = _sc_conv(p, gb, sc_dw[j], nt, ntx)
            else:
                q, k, v = _mla_proj(x_all, mod, wd, wkp, row(mla_q_g[j]), row(mla_kv_g[j]), wqn, wqr,
                                    wuk, wuv, cos_t, sin_t, nt_all, ntx)
                a = _attention(q, k, v, nt * tl)

            xn, h2, topi, topw, rank, counts = _post(alpha, a, w_last, b_last, x_all, mod, row(ln_g[i, 0]),
                                                     row(ln_b[i, 0]), rw_hi, rw_lo, moe_bias[i].reshape(-1, 1),
                                                     nt, ntx)

            dest, block_expert, n_used, n_rows = _dispatch_plan(counts[:, 0], topi, rank, n_tok)
            xs = _sc_row_scatter(h2.reshape(n_tok, d // 2), dest, n_rows)
            ys = _grouped_ffn(block_expert, n_used, xs, moe_w1, moe_w3, moe_w2, i)
            y_rows = _sc_row_gather(ys, dest.reshape(-1)).reshape(TOP_K, n_tok, d // 2)

            chains[ci] = _moe_out(alpha, xn, h2, y_rows, topw.T, mod, ws13, ws2,
                                  row(ln_g[i, 1]), row(ln_b[i, 1]), nt, ntx)
    return jnp.concatenate([xc[:, :seq] for xc in chains], axis=0)
```

```python
import functools

import numpy as np
import jax
import jax.numpy as jnp
from jax import lax
from jax.experimental import pallas as pl
from jax.experimental.pallas import tpu as pltpu
from jax.experimental.pallas import tpu_sc as plsc

F32 = jnp.float32
BF16 = jnp.bfloat16
U32 = jnp.uint32
HIGH_HALF_MASK = np.uint32(0xFFFF0000)

GRID_W = 64
N_MIXERS = 3
LN_EPS = 1e-5
RMS_EPS = 1e-6
N_MOD = 6
MLA_HEADS = 8
QK_NOPE = 128
QK_ROPE = 64
V_HEAD = 128
ROPE_THETA = 10000.0
ATTN_SCALE = (QK_NOPE + QK_ROPE) ** -0.5
Q_SCALE = ATTN_SCALE * 1.4426950408889634
TOP_K = 8
N_GROUPS = 8
TOPK_GROUPS = 4
ROUTED_SCALE = 2.5

SEQ_TILE = 256
CONV_HALO = 16
SHORT_HALO = 8
CONV_ROW_CHUNK = 64
LANE = 128
SUBLANES = 8
Q_TILE = 1024
ATTN_SUB_ROWS = 256
ATTN_ROW_CHUNK = 16
MOE_BLOCK_ROWS = 1024
MOE_SUB_ROWS = 1024
MOD_COL_TILE = 1536
VMEM_LIMIT = 48 * 1024 * 1024
SC_CORES = 2
SC_SUBCORES = 16
SC_WORKERS = SC_CORES * SC_SUBCORES
SC_WINDOWS = (64, 32, 16)
N_CHAINS = 1


def _params(n_axes):
    return pltpu.CompilerParams(dimension_semantics=("arbitrary",) * n_axes,
                                vmem_limit_bytes=VMEM_LIMIT)


def _split_bf16(a):
    hi = a.astype(BF16)
    lo = (a - hi.astype(F32)).astype(BF16)
    return hi, lo


def _dot(a, b):
    return jnp.dot(a, b, preferred_element_type=F32)


def _dot_nt(a, b):
    return lax.dot_general(a, b, (((1,), (1,)), ((), ())), preferred_element_type=F32)


def _pack_halves(v):
    half = v.shape[-1] // 2
    lo = lax.bitcast_convert_type(v[:, :half].astype(BF16).astype(F32), U32) >> 16
    hi = lax.bitcast_convert_type(v[:, half:].astype(BF16).astype(F32), U32) & HIGH_HALF_MASK
    return hi | lo


def _unpack_halves(p):
    lo = lax.bitcast_convert_type(p << 16, F32)
    hi = lax.bitcast_convert_type(p & HIGH_HALF_MASK, F32)
    return lo, hi


def _dot_packed(p, w):
    lo, hi = _unpack_halves(p)
    half = p.shape[-1]
    return _dot(lo.astype(BF16), w[:half]) + _dot(hi.astype(BF16), w[half:])


def _layer_norm(v, g, b):
    mu = jnp.mean(v, axis=-1, keepdims=True)
    c = v - mu
    var = jnp.mean(c * c, axis=-1, keepdims=True)
    return c * lax.rsqrt(var + LN_EPS) * g + b


def _silu(v):
    return v * jax.nn.sigmoid(v)


def _mod_kernel(c_ref, w_ref, b_ref, o_ref):
    a = _silu(c_ref[...])
    a_hi, a_lo = _split_bf16(a)
    w_hi, w_lo = _split_bf16(w_ref[0])
    o_ref[0] = _dot(a_hi, w_hi) + _dot(a_hi, w_lo) + _dot(a_lo, w_hi) + b_ref[0]


def _modulation(c_all, ada_w, ada_b):
    depth, d, n = ada_w.shape
    rows = c_all.shape[0]
    tn = MOD_COL_TILE
    return pl.pallas_call(
        _mod_kernel,
        out_shape=jax.ShapeDtypeStruct((depth, rows, n), F32),
        grid=(depth, n // tn),
        in_specs=[pl.BlockSpec((rows, d), lambda i, j: (0, 0)),
                  pl.BlockSpec((1, d, tn), lambda i, j: (i, 0, j)),
                  pl.BlockSpec((1, 1, tn), lambda i, j: (i, 0, j))],
        out_specs=pl.BlockSpec((1, rows, tn), lambda i, j: (i, 0, j)),
        compiler_params=_params(2),
        name="adaln_mod",
    )(c_all, ada_w, ada_b.reshape(depth, 1, n))


def _tok_spec(d, tl=SEQ_TILE):
    return pl.BlockSpec((1, tl, d), lambda b, l: (b, l, 0))


def _mod_spec(d, n_batch, ntx):
    return pl.BlockSpec((1, N_MOD, d), lambda b, l: (jnp.where(l < ntx, b, n_batch), 0, 0))


def _full_spec(shape):
    zeros = (0,) * len(shape)
    return pl.BlockSpec(shape, lambda b, l: zeros)


def _halo_specs(d, halo, seq_len, tl=SEQ_TILE):
    per_tile = tl // halo
    last = seq_len // halo - 1
    prev = pl.BlockSpec((1, halo, d), lambda b, l: (b, jnp.maximum(l * per_tile - 1, 0), 0))
    nxt = pl.BlockSpec((1, halo, d), lambda b, l: (b, jnp.minimum((l + 1) * per_tile, last), 0))
    return prev, nxt


def _segment_edges(l, ntx, nt):
    first = jnp.logical_or(l == 0, l == ntx)
    last = jnp.logical_or(l == ntx - 1, l == nt - 1)
    return first, last


def _conf_in_kernel(x_ref, mod_ref, w1_ref, b1_ref, u_ref):
    d = x_ref.shape[-1]
    h = x_ref[0] * (1.0 + mod_ref[0, 1:2, :]) + mod_ref[0, 0:1, :]
    z = _dot(h.astype(BF16), w1_ref[...]) + b1_ref[...]
    u_ref[0] = z[:, :d] * jax.nn.sigmoid(z[:, d:])


def _conf_in(x_all, mod, w1, b1, nt, ntx):
    nb, seq_len, d = x_all.shape
    return pl.pallas_call(
        _conf_in_kernel,
        out_shape=jax.ShapeDtypeStruct((nb, nt * SEQ_TILE, d), F32),
        grid=(nb, nt),
        in_specs=[_tok_spec(d), _mod_spec(d, nb, ntx), _full_spec(w1.shape), _full_spec(b1.shape)],
        out_specs=_tok_spec(d),
        compiler_params=_params(2),
        name="conf_in",
    )(x_all, mod, w1, b1)


def _conf_conv_kernel(ntx, nt, up_ref, uc_ref, un_ref, dw_ref, dwb_ref, ng_ref, nb_ref,
                      a_ref, sh_ref, conv_ref):
    tl, d = uc_ref.shape[1], uc_ref.shape[2]
    taps = dw_ref.shape[0]
    lead = CONV_HALO - (taps - 1) // 2
    first, last = _segment_edges(pl.program_id(1), ntx, nt)
    sh_ref[0, 0:CONV_HALO, :] = jnp.where(first, 0.0, up_ref[0])
    sh_ref[0, CONV_HALO:CONV_HALO + tl, :] = uc_ref[0]
    sh_ref[0, CONV_HALO + tl:, :] = jnp.where(last, 0.0, un_ref[0])
    span = tl + 2 * CONV_HALO - SUBLANES
    for s in range(1, SUBLANES):
        sh_ref[s, 0:span, :] = sh_ref[0, s:s + span, :]
    groups = CONV_ROW_CHUNK // SUBLANES

    def row_chunk(i, carry):
        r0 = pl.multiple_of(i * CONV_ROW_CHUNK, CONV_ROW_CHUNK)
        for c0 in range(0, d, LANE):
            accs = [jnp.zeros((SUBLANES, LANE), F32) for _ in range(groups)]
            for k in range(taps):
                res = (lead + k) % SUBLANES
                off = lead + k - res
                w = dw_ref[k, :, c0:c0 + LANE]
                for g in range(groups):
                    lo = r0 + (off + g * SUBLANES)
                    accs[g] = accs[g] + w * sh_ref[res, pl.ds(lo, SUBLANES), c0:c0 + LANE]
            for g in range(groups):
                conv_ref[pl.ds(r0 + g * SUBLANES, SUBLANES), c0:c0 + LANE] = accs[g]
        return carry

    lax.fori_loop(0, tl // CONV_ROW_CHUNK, row_chunk, 0)
    v = _layer_norm(conv_ref[...] + dwb_ref[...], ng_ref[...], nb_ref[...])
    a_ref[0] = _silu(v).astype(BF16)


def _conf_conv(u, dw, dwb, ng, nb_, nt, ntx):
    nb, seq_len, d = u.shape
    prev, nxt = _halo_specs(d, CONV_HALO, seq_len)
    return pl.pallas_call(
        functools.partial(_conf_conv_kernel, ntx, nt),
        out_shape=jax.ShapeDtypeStruct((nb, seq_len, d), BF16),
        grid=(nb, nt),
        in_specs=[prev, _tok_spec(d), nxt, _full_spec(dw.shape), _full_spec(dwb.shape),
                  _full_spec(ng.shape), _full_spec(nb_.shape)],
        out_specs=_tok_spec(d),
        scratch_shapes=[pltpu.VMEM((SUBLANES, SEQ_TILE + 2 * CONV_HALO, d), F32),
                        pltpu.VMEM((SEQ_TILE, d), F32)],
        compiler_params=_params(2),
        name="conf_conv",
    )(u, u, u, dw, dwb, ng, nb_)


def _sc_in_kernel(x_ref, mod_ref, w_ref, gb_ref, p_ref):
    d = x_ref.shape[-1]
    h = x_ref[0] * (1.0 + mod_ref[0, 1:2, :]) + mod_ref[0, 0:1, :]
    z = _dot(h.astype(BF16), w_ref[...])
    gb_ref[0] = z[:, :d]
    p_ref[0] = z[:, d:2 * d] * z[:, 2 * d:]


def _sc_in(x_all, mod, w_in, nt, ntx):
    nb, seq_len, d = x_all.shape
    shp = jax.ShapeDtypeStruct((nb, nt * SEQ_TILE, d), F32)
    return pl.pallas_call(
        _sc_in_kernel,
        out_shape=(shp, shp),
        grid=(nb, nt),
        in_specs=[_tok_spec(d), _mod_spec(d, nb, ntx), _full_spec(w_in.shape)],
        out_specs=(_tok_spec(d), _tok_spec(d)),
        compiler_params=_params(2),
        name="sc_in",
    )(x_all, mod, w_in)


def _sc_conv_kernel(ntx, nt, pp_ref, pc_ref, pn_ref, gb_ref, dw_ref, a_ref, ext_ref):
    tl = pc_ref.shape[1]
    taps = dw_ref.shape[0]
    lead = SHORT_HALO - (taps - 1) // 2
    first, last = _segment_edges(pl.program_id(1), ntx, nt)
    ext_ref[0:SHORT_HALO, :] = jnp.where(first, 0.0, pp_ref[0])
    ext_ref[SHORT_HALO:SHORT_HALO + tl, :] = pc_ref[0]
    ext_ref[SHORT_HALO + tl:, :] = jnp.where(last, 0.0, pn_ref[0])
    acc = dw_ref[0:1, :] * ext_ref[lead:lead + tl, :]
    for k in range(1, taps):
        acc = acc + dw_ref[k:k + 1, :] * ext_ref[lead + k:lead + k + tl, :]
    a_ref[0] = (gb_ref[0] * acc).astype(BF16)


def _sc_conv(p, gb, dw, nt, ntx):
    nb, seq_len, d = p.shape
    prev, nxt = _halo_specs(d, SHORT_HALO, seq_len)
    return pl.pallas_call(
        functools.partial(_sc_conv_kernel, ntx, nt),
        out_shape=jax.ShapeDtypeStruct((nb, seq_len, d), BF16),
        grid=(nb, nt),
        in_specs=[prev, _tok_spec(d), nxt, _tok_spec(d), _full_spec(dw.shape)],
        out_specs=_tok_spec(d),
        scratch_shapes=[pltpu.VMEM((SEQ_TILE + 2 * SHORT_HALO, d), F32)],
        compiler_params=_params(2),
        name="sc_conv",
    )(p, p, p, gb, dw)


def _rms(v, g):
    return v * lax.rsqrt(jnp.mean(v * v, axis=-1, keepdims=True) + RMS_EPS) * g


def _mla_proj_kernel(ntx, q_rank,
                     x_ref, mod_ref, wd_ref, wkp_ref, qg_ref, kvg_ref, wqn_ref, wqr_ref, wuk_ref, wuv_ref,
                     cos_ref, sin_ref, q_ref, k_ref, v_ref):
    is_latent = pl.program_id(1) < ntx
    h = (x_ref[0] * (1.0 + mod_ref[0, 1:2, :]) + mod_ref[0, 0:1, :]).astype(BF16)
    dn = _dot(h, wd_ref[...])
    cq = _rms(dn[:, :q_rank], qg_ref[...]).astype(BF16)
    ckv = _rms(dn[:, q_rank:], kvg_ref[...]).astype(BF16)
    cos = jnp.where(is_latent, cos_ref[...], 1.0)
    sin = jnp.where(is_latent, sin_ref[...], 0.0)
    kp2 = _dot(h, wkp_ref[...])
    kp = kp2[:, :LANE] * cos + kp2[:, LANE:] * sin
    kn = _dot(ckv, wuk_ref[...])
    vv = _dot(ckv, wuv_ref[...])
    qn = _dot(cq, wqn_ref[...]) * Q_SCALE
    qr2 = _dot(cq, wqr_ref[...])
    hw = MLA_HEADS * LANE
    for hd in range(MLA_HEADS):
        sl = slice(hd * LANE, (hd + 1) * LANE)
        qr = (qr2[:, sl] * cos + qr2[:, hw + hd * LANE:hw + (hd + 1) * LANE] * sin) * Q_SCALE
        q_ref[0, hd, :, 0:LANE] = qn[:, sl].astype(BF16)
        q_ref[0, hd, :, LANE:] = qr.astype(BF16)
        k_ref[0, hd, :, 0:LANE] = kn[:, sl].astype(BF16)
        k_ref[0, hd, :, LANE:] = kp.astype(BF16)
        v_ref[0, hd] = vv[:, sl].astype(BF16)


def _mla_proj(x_all, mod, wd, wkp, qg, kvg, wqn, wqr, wuk, wuv, cos_t, sin_t, nt, ntx):
    nb, seq_len, d = x_all.shape
    q_rank = qg.shape[-1]
    tl = SEQ_TILE
    rope_spec = pl.BlockSpec((tl, LANE), lambda b, l: (jnp.minimum(l, ntx - 1), 0))
    qk_shape = jax.ShapeDtypeStruct((nb, MLA_HEADS, nt * tl, 2 * LANE), BF16)
    v_shape = jax.ShapeDtypeStruct((nb, MLA_HEADS, nt * tl, LANE), BF16)
    qk_spec = pl.BlockSpec((1, MLA_HEADS, tl, 2 * LANE), lambda b, l: (b, 0, l, 0))
    v_spec = pl.BlockSpec((1, MLA_HEADS, tl, LANE), lambda b, l: (b, 0, l, 0))
    return pl.pallas_call(
        functools.partial(_mla_proj_kernel, ntx, q_rank),
        out_shape=(qk_shape, qk_shape, v_shape),
        grid=(nb, nt),
        in_specs=[_tok_spec(d), _mod_spec(d, nb, ntx), _full_spec(wd.shape), _full_spec(wkp.shape),
                  _full_spec(qg.shape), _full_spec(kvg.shape), _full_spec(wqn.shape), _full_spec(wqr.shape),
                  _full_spec(wuk.shape), _full_spec(wuv.shape), rope_spec, rope_spec],
        out_specs=(qk_spec, qk_spec, v_spec),
        compiler_params=_params(2),
        name="mla_proj",
    )(x_all, mod, wd, wkp, qg, kvg, wqn, wqr, wuk, wuv, cos_t, sin_t)


def _attn_kernel(q_ref, k_ref, v_ref, o_ref, s_ref, p_ref, l_ref):
    tq = q_ref.shape[2]
    for r0 in range(0, tq, ATTN_SUB_ROWS):
        rows = slice(r0, r0 + ATTN_SUB_ROWS)
        s_ref[rows, :] = _dot_nt(q_ref[0, 0, rows, :], k_ref[0, 0])
        for c0 in range(r0, r0 + ATTN_SUB_ROWS, ATTN_ROW_CHUNK):
            chunk = slice(c0, c0 + ATTN_ROW_CHUNK)
            s = s_ref[chunk, :]
            p = jnp.exp2(s - jnp.max(s, axis=-1, keepdims=True))
            l_ref[chunk, :] = jnp.broadcast_to(jnp.sum(p, axis=-1, keepdims=True), (ATTN_ROW_CHUNK, V_HEAD))
            p_ref[chunk, :] = p.astype(BF16)
        o = _dot(p_ref[rows, :], v_ref[0, 0])
        o_ref[0, rows, :] = (o / l_ref[rows, :]).astype(BF16)


def _attention(q, k, v, n_q):
    nb, nh, n_k, dk = k.shape
    tq = Q_TILE
    return pl.pallas_call(
        _attn_kernel,
        out_shape=jax.ShapeDtypeStruct((nb, n_q, nh * V_HEAD), BF16),
        grid=(nb, nh, n_q // tq),
        in_specs=[pl.BlockSpec((1, 1, tq, dk), lambda b, h, i: (b, h, i, 0)),
                  pl.BlockSpec((1, 1, n_k, dk), lambda b, h, i: (b, h, 0, 0)),
                  pl.BlockSpec((1, 1, n_k, V_HEAD), lambda b, h, i: (b, h, 0, 0))],
        out_specs=pl.BlockSpec((1, tq, V_HEAD), lambda b, h, i: (b, i, h)),
        scratch_shapes=[pltpu.VMEM((tq, n_k), F32), pltpu.VMEM((tq, n_k), BF16), pltpu.VMEM((tq, V_HEAD), F32)],
        compiler_params=_params(3),
        name="mla_attn",
    )(q, k, v)


def _route(sel, s):
    n_e, n_t = sel.shape
    per = n_e // N_GROUPS
    sel3 = sel.reshape(N_GROUPS, per, n_t)
    s3 = s.reshape(N_GROUPS, per, n_t)
    iota_p = lax.broadcasted_iota(jnp.int32, (N_GROUPS, per, n_t), 1).astype(F32)
    iota_g = lax.broadcasted_iota(jnp.int32, (N_GROUPS, 1, n_t), 0).astype(F32)
    neg = -jnp.inf
    m1 = jnp.max(sel3, axis=1, keepdims=True)
    first = jnp.min(jnp.where(sel3 == m1, iota_p, float(per)), axis=1, keepdims=True)
    m2 = jnp.max(jnp.where(iota_p == first, neg, sel3), axis=1, keepdims=True)
    gs = m1 + m2
    gsel = jnp.zeros((N_GROUPS, 1, n_t), F32)
    for _ in range(TOPK_GROUPS):
        gm = jnp.max(gs, axis=0, keepdims=True)
        gfirst = jnp.min(jnp.where(gs == gm, iota_g, float(N_GROUPS)), axis=0, keepdims=True)
        pick = iota_g == gfirst
        gsel = jnp.where(pick, 1.0, gsel)
        gs = jnp.where(pick, neg, gs)
    val = jnp.where(gsel > 0.0, sel3, neg)
    iota_e = lax.broadcasted_iota(jnp.int32, (N_GROUPS, per, n_t), 0).astype(F32) * per + iota_p
    chosen = jnp.zeros((N_GROUPS, per, n_t), F32)
    idx, wts = [], []
    for _ in range(TOP_K):
        m = jnp.max(jnp.max(val, axis=1, keepdims=True), axis=0, keepdims=True)
        e = jnp.min(jnp.min(jnp.where(val == m, iota_e, float(n_e)), axis=1, keepdims=True), axis=0, keepdims=True)
        pick = iota_e == e
        wts.append(jnp.sum(jnp.sum(jnp.where(pick, s3, 0.0), axis=1, keepdims=True), axis=0, keepdims=True))
        idx.append(e)
        chosen = jnp.where(pick, 1.0, chosen)
        val = jnp.where(pick, neg, val)
    total = wts[0]
    for w in wts[1:]:
        total = total + w
    wts = [w / total * ROUTED_SCALE for w in wts]
    return idx, wts, chosen, iota_e


def _post_kernel(alpha, a_ref, w_ref, b_ref, x_ref, mod_ref, lng_ref, lnb_ref, rw_hi_ref, rw_lo_ref, rb_ref, tri_ref,
                 xn_ref, h2_ref, topi_ref, topw_ref, rank_ref, cnt_ref, run_ref):
    first_step = jnp.logical_and(pl.program_id(0) == 0, pl.program_id(1) == 0)

    @pl.when(first_step)
    def _():
        run_ref[...] = jnp.zeros_like(run_ref)

    y = _dot(a_ref[0], w_ref[...]) + b_ref[...]
    xn = _layer_norm(alpha * x_ref[0] + mod_ref[0, 2:3, :] * y, lng_ref[...], lnb_ref[...])
    xn_ref[0] = xn
    h2 = xn * (1.0 + mod_ref[0, 4:5, :]) + mod_ref[0, 3:4, :]
    h_hi, h_lo = _split_bf16(h2)
    h2_ref[0] = _pack_halves(h2)
    logits = _dot_nt(rw_hi_ref[...], h_hi) + _dot_nt(rw_hi_ref[...], h_lo) + _dot_nt(rw_lo_ref[...], h_hi)
    s = jax.nn.sigmoid(logits)
    idx, wts, chosen3, iota_e = _route(s + rb_ref[...], s)
    n_e, n_t = s.shape
    chosen = chosen3.reshape(n_e, n_t)
    rank = run_ref[:, 0:1] + _dot(chosen.astype(BF16), tri_ref[...])
    rank3 = rank.reshape(chosen3.shape)
    for k in range(TOP_K):
        rk = jnp.sum(jnp.sum(jnp.where(iota_e == idx[k], rank3, 0.0), axis=1, keepdims=True), axis=0, keepdims=True)
        topi_ref[k:k + 1, :] = idx[k].reshape(1, n_t).astype(jnp.int32)
        topw_ref[k:k + 1, :] = wts[k].reshape(1, n_t)
        rank_ref[k:k + 1, :] = rk.reshape(1, n_t).astype(jnp.int32)
    run_ref[...] = run_ref[...] + jnp.sum(chosen, axis=1, keepdims=True)
    cnt_ref[...] = run_ref[...].astype(jnp.int32)


def _post(alpha, a, w, bias, x_all, mod, lng, lnb, rw_hi, rw_lo, rb, nt, ntx):
    nb, _, dk = a.shape
    d = x_all.shape[-1]
    n_e = rw_hi.shape[0]
    tl = SEQ_TILE
    n_tok = nb * nt * tl
    col_spec = lambda rows: pl.BlockSpec((rows, tl), lambda b, l: (0, b * nt + l))
    row_i = lax.broadcasted_iota(jnp.int32, (tl, tl), 0)
    col_i = lax.broadcasted_iota(jnp.int32, (tl, tl), 1)
    tri = jnp.where(row_i < col_i, 1.0, 0.0).astype(BF16)
    return pl.pallas_call(
        functools.partial(_post_kernel, alpha),
        out_shape=(jax.ShapeDtypeStruct((nb, nt * tl, d), F32),
                   jax.ShapeDtypeStruct((nb, nt * tl, d // 2), U32),
                   jax.ShapeDtypeStruct((TOP_K, n_tok), jnp.int32),
                   jax.ShapeDtypeStruct((TOP_K, n_tok), F32),
                   jax.ShapeDtypeStruct((TOP_K, n_tok), jnp.int32),
                   jax.ShapeDtypeStruct((n_e, LANE), jnp.int32)),
        grid=(nb, nt),
        in_specs=[_tok_spec(dk), _full_spec(w.shape), _full_spec(bias.shape), _tok_spec(d),
                  _mod_spec(d, nb, ntx), _full_spec(lng.shape), _full_spec(lnb.shape),
                  _full_spec(rw_hi.shape), _full_spec(rw_lo.shape), _full_spec(rb.shape), _full_spec(tri.shape)],
        out_specs=(_tok_spec(d), _tok_spec(d // 2), col_spec(TOP_K), col_spec(TOP_K), col_spec(TOP_K),
                   _full_spec((n_e, LANE))),
        scratch_shapes=[pltpu.VMEM((n_e, LANE), F32)],
        compiler_params=_params(2),
        name="mixer_post",
    )(a, w, bias, x_all, mod, lng, lnb, rw_hi, rw_lo, rb, tri)


def _gmm_kernel(be_ref, nu_ref, xs_ref, w1_ref, w3_ref, w2_ref, ys_ref, w13_s, w2_s):
    i = pl.program_id(0)
    de = w2_ref.shape[2]
    changed = jnp.logical_or(i == 0, be_ref[i] != be_ref[jnp.maximum(i - 1, 0)])

    @pl.when(changed)
    def _():
        w13_s[:, :de] = w1_ref[0, 0].astype(BF16)
        w13_s[:, de:] = w3_ref[0, 0].astype(BF16)
        w2_s[...] = w2_ref[0, 0].astype(BF16)

    @pl.when(i < nu_ref[0])
    def _():
        for r0 in range(0, xs_ref.shape[0], MOE_SUB_ROWS):
            rows = slice(r0, r0 + MOE_SUB_ROWS)
            z = _dot_packed(xs_ref[rows, :], w13_s)
            hmid = (_silu(z[:, :de]) * z[:, de:]).astype(BF16)
            ys_ref[rows, :] = _pack_halves(_dot(hmid, w2_s[...]))


def _grouped_ffn(block_expert, n_used, xs, w1, w3, w2, layer):
    n_rows, dp = xs.shape
    d = 2 * dp
    bm = MOE_BLOCK_ROWS
    de = w2.shape[2]
    grid_spec = pltpu.PrefetchScalarGridSpec(
        num_scalar_prefetch=2,
        grid=(n_rows // bm,),
        in_specs=[pl.BlockSpec((bm, dp), lambda i, be, nu: (i, 0)),
                  pl.BlockSpec((1, 1, d, de), lambda i, be, nu: (layer, be[i], 0, 0)),
                  pl.BlockSpec((1, 1, d, de), lambda i, be, nu: (layer, be[i], 0, 0)),
                  pl.BlockSpec((1, 1, de, d), lambda i, be, nu: (layer, be[i], 0, 0))],
        out_specs=pl.BlockSpec((bm, dp), lambda i, be, nu: (i, 0)),
        scratch_shapes=[pltpu.VMEM((d, 2 * de), BF16), pltpu.VMEM((de, d), BF16)],
    )
    return pl.pallas_call(
        _gmm_kernel,
        out_shape=jax.ShapeDtypeStruct((n_rows, dp), U32),
        grid_spec=grid_spec,
        compiler_params=_params(1),
        name="moe_grouped_ffn",
    )(block_expert, n_used, xs, w1, w3, w2)


def _sc_mesh():
    return plsc.VectorSubcoreMesh(core_axis_name="c", subcore_axis_name="s",
                                  num_cores=SC_CORES, num_subcores=SC_SUBCORES)


def _sc_worker():
    return lax.axis_index("s") * SC_CORES + lax.axis_index("c")


def _sc_window(rows_per_worker):
    for win in SC_WINDOWS:
        if rows_per_worker % (2 * win) == 0:
            return win
    raise ValueError(f"no SparseCore window divides {rows_per_worker} rows per worker")


def _sc_row_scatter(src, dest, n_out):
    n_k, n_tok = dest.shape
    dp = src.shape[1]
    assert n_tok % SC_WORKERS == 0
    per_w = n_tok // SC_WORKERS
    win = _sc_window(per_w)
    n_win = per_w // win
    idx = dest.reshape(n_k, SC_WORKERS, n_win, win).transpose(1, 2, 0, 3)

    def body(src_hbm, idx_hbm, out_hbm, idx_v, rows_v, load_sem, scat_sem):
        wid = _sc_worker()
        base = wid * per_w
        pltpu.sync_copy(idx_hbm.at[wid], idx_v)

        def load(g, slot):
            return pltpu.make_async_copy(src_hbm.at[pl.ds(base + g * win, win)], rows_v.at[slot], load_sem.at[slot])

        def scatter(g, slot, k):
            return pltpu.make_async_copy(rows_v.at[slot], out_hbm.at[idx_v.at[g, k]], scat_sem.at[slot])

        load(0, 0).start()

        @pl.loop(0, n_win, step=2)
        def _(g):
            for slot in range(2):
                cur = g + slot
                load(cur, slot).wait()

                @pl.when(cur + 1 < n_win)
                def _():
                    @pl.when(cur >= 1)
                    def _():
                        for k in range(n_k):
                            scatter(cur - 1, 1 - slot, k).wait()
                    load(cur + 1, 1 - slot).start()

                for k in range(n_k):
                    scatter(cur, slot, k).start()

        for slot in range(2):
            for k in range(n_k):
                scatter(n_win - 2 + slot, slot, k).wait()

    return pl.kernel(
        body, mesh=_sc_mesh(),
        out_type=jax.ShapeDtypeStruct((n_out, dp), src.dtype),
        scratch_types=[pltpu.VMEM((n_win, n_k, win), jnp.int32),
                       pltpu.VMEM((2, win, dp), src.dtype),
                       pltpu.SemaphoreType.DMA((2,)),
                       pltpu.SemaphoreType.DMA((2,))],
        compiler_params=pltpu.CompilerParams(use_tc_tiling_on_sc=True),
        name="sc_dispatch_scatter",
    )(src, idx)


def _sc_row_gather(table, idx):
    n = idx.shape[0]
    dp = table.shape[1]
    assert n % SC_WORKERS == 0
    per_w = n // SC_WORKERS
    win = _sc_window(per_w)
    n_win = per_w // win

    def body(table_hbm, idx_hbm, out_hbm, idx_v, rows_v, gather_sem, put_sem):
        wid = _sc_worker()
        base = wid * per_w
        pltpu.sync_copy(idx_hbm.at[pl.ds(base, per_w)], idx_v)

        def gather(g, slot):
            return pltpu.make_async_copy(table_hbm.at[idx_v.at[pl.ds(g * win, win)]], rows_v.at[slot],
                                         gather_sem.at[slot])

        def put(g, slot):
            return pltpu.make_async_copy(rows_v.at[slot], out_hbm.at[pl.ds(base + g * win, win)], put_sem.at[slot])

        gather(0, 0).start()

        @pl.loop(0, n_win, step=2)
        def _(g):
            for slot in range(2):
                cur = g + slot
                gather(cur, slot).wait()

                @pl.when(cur + 1 < n_win)
                def _():
                    @pl.when(cur >= 1)
                    def _():
                        put(cur - 1, 1 - slot).wait()
                    gather(cur + 1, 1 - slot).start()

                put(cur, slot).start()

        for slot in range(2):
            put(n_win - 2 + slot, slot).wait()

    return pl.kernel(
        body, mesh=_sc_mesh(),
        out_type=jax.ShapeDtypeStruct((n, dp), table.dtype),
        scratch_types=[pltpu.VMEM((per_w,), jnp.int32),
                       pltpu.VMEM((2, win, dp), table.dtype),
                       pltpu.SemaphoreType.DMA((2,)),
                       pltpu.SemaphoreType.DMA((2,))],
        compiler_params=pltpu.CompilerParams(use_tc_tiling_on_sc=True),
        name="sc_combine_gather",
    )(table, idx)


def _moe_out_kernel(alpha, x_ref, h2_ref, yg_ref, tw_ref, mod_ref, ws13_ref, ws2_ref, lng_ref, lnb_ref, o_ref):
    de = ws2_ref.shape[0]
    z = _dot_packed(h2_ref[0], ws13_ref)
    hmid = (_silu(z[:, :de]) * z[:, de:]).astype(BF16)
    y = _dot(hmid, ws2_ref[...])
    lo, hi = _unpack_halves(yg_ref[0])
    acc_lo, acc_hi = lo * tw_ref[:, 0:1], hi * tw_ref[:, 0:1]
    for k in range(1, yg_ref.shape[0]):
        lo, hi = _unpack_halves(yg_ref[k])
        acc_lo = acc_lo + lo * tw_ref[:, k:k + 1]
        acc_hi = acc_hi + hi * tw_ref[:, k:k + 1]
    y = y + jnp.concatenate([acc_lo, acc_hi], axis=1)
    o_ref[0] = _layer_norm(alpha * x_ref[0] + mod_ref[0, 5:6, :] * y, lng_ref[...], lnb_ref[...])


def _moe_out(alpha, xn, h2, y_rows, topw_t, mod, ws13, ws2, lng, lnb, nt, ntx):
    nb, _, d = xn.shape
    n_k = y_rows.shape[0]
    tl = SEQ_TILE
    return pl.pallas_call(
        functools.partial(_moe_out_kernel, alpha),
        out_shape=jax.ShapeDtypeStruct((nb, nt * tl, d), F32),
        grid=(nb, nt),
        in_specs=[_tok_spec(d), _tok_spec(d // 2),
                  pl.BlockSpec((n_k, tl, d // 2), lambda b, l: (0, b * nt + l, 0)),
                  pl.BlockSpec((tl, n_k), lambda b, l: (b * nt + l, 0)),
                  _mod_spec(d, nb, ntx),
                  _full_spec(ws13.shape), _full_spec(ws2.shape), _full_spec(lng.shape), _full_spec(lnb.shape)],
        out_specs=_tok_spec(d),
        compiler_params=_params(2),
        name="moe_out",
    )(xn, h2, y_rows, topw_t, mod, ws13, ws2, lng, lnb)


def _dispatch_plan(counts, topi, rank, n_tok):
    n_e = counts.shape[0]
    bm = MOE_BLOCK_ROWS
    padded = (counts + bm - 1) // bm * bm
    pad_end = jnp.cumsum(padded)
    pad_start = pad_end - padded
    onehot = topi[:, None, :] == jnp.arange(n_e, dtype=jnp.int32)[None, :, None]
    dest = rank + jnp.sum(jnp.where(onehot, pad_start[None, :, None], 0), axis=1)
    n_blocks = n_tok * TOP_K // bm + n_e
    block_start = jnp.arange(n_blocks, dtype=jnp.int32) * bm
    block_expert = jnp.sum((pad_end[None, :] <= block_start[:, None]).astype(jnp.int32), axis=1)
    block_expert = jnp.minimum(block_expert, n_e - 1)
    n_used = (pad_end[-1] // bm).astype(jnp.int32).reshape(1)
    return dest.astype(jnp.int32), block_expert, n_used, n_blocks * bm


def _rope_tables(seq):
    n_freq = QK_ROPE // 4
    inv_freq = ROPE_THETA ** (-jnp.arange(n_freq, dtype=F32) / n_freq)
    pos = jnp.arange(seq, dtype=jnp.int32)
    r = (pos // GRID_W).astype(F32)
    col = (pos % GRID_W).astype(F32)
    ang = jnp.concatenate([r[:, None] * inv_freq, col[:, None] * inv_freq], -1)
    cos, sin = jnp.cos(ang), jnp.sin(ang)
    zeros = jnp.zeros((seq, LANE - QK_ROPE), F32)
    cos_slot = jnp.concatenate([cos, cos, zeros], -1)
    sin_slot = jnp.concatenate([-sin, sin, zeros], -1)
    return cos_slot, sin_slot


def _rope_slot_weights(w_rope):
    k, n, _ = w_rope.shape
    half = QK_ROPE // 2
    swapped = jnp.concatenate([w_rope[..., half:], w_rope[..., :half]], -1)
    pad = jnp.zeros((k, n, LANE - QK_ROPE), w_rope.dtype)
    plain = jnp.concatenate([w_rope, pad], -1).reshape(k, n * LANE)
    swp = jnp.concatenate([swapped, pad], -1).reshape(k, n * LANE)
    return jnp.concatenate([plain, swp], -1)


def kernel(x, c, ctx, c_ctx, ada_w, ada_b, ln_g, ln_b, conf_w1, conf_b1, conf_dw, conf_dwb, conf_ng, conf_nb, conf_w2, conf_b2, sc_w_in, sc_dw, sc_w_out, mla_w_dqkv, mla_q_g, mla_kv_g, mla_w_uq, mla_w_uk, mla_w_uv, mla_w_o, moe_router, moe_bias, moe_w1, moe_w3, moe_w2, sh_w1, sh_w3, sh_w2):
    nb, seq, d = x.shape
    l_ctx = ctx.shape[1]
    depth = ada_w.shape[0]
    alpha = (2.0 * depth) ** 0.25
    tl = SEQ_TILE
    assert seq % tl == 0 and l_ctx % tl == 0 and seq % Q_TILE == 0 and seq % GRID_W == 0
    ntx = seq // tl
    nt_all = (seq + l_ctx) // tl
    attn_layers = [i for i in range(depth) if i % N_MIXERS == 2]
    last_ctx_reader = attn_layers[-1] if attn_layers else -1

    rows = -(-(nb + 1) // 8) * 8
    c_all = jnp.zeros((rows, d), F32).at[:nb].set(c).at[nb].set(c_ctx)
    mod_all = _modulation(c_all, ada_w, ada_b).reshape(depth, rows, N_MOD, d)

    assert nb % N_CHAINS == 0
    nbc = nb // N_CHAINS
    chains = [jnp.concatenate([x[c0:c0 + nbc], ctx[c0:c0 + nbc]], axis=1) for c0 in range(0, nb, nbc)]
    q_rank, kv_rank = mla_q_g.shape[1], mla_kv_g.shape[1]
    cos_t, sin_t = _rope_tables(seq)
    row = lambda v: v.reshape(1, -1)

    for i in range(depth):
        need_ctx = i < last_ctx_reader
        kind, j = i % N_MIXERS, i // N_MIXERS
        nt = nt_all if need_ctx else ntx
        n_tok = nbc * nt * tl

        if kind == 0:
            w_first = conf_w1[j].astype(BF16)
            dw_tiles = jnp.broadcast_to(conf_dw[j][:, None, :], (conf_dw.shape[1], SUBLANES, d))
            w_last, b_last = conf_w2[j].astype(BF16), row(conf_b2[j])
        elif kind == 1:
            w_first = sc_w_in[j].astype(BF16)
            w_last, b_last = sc_w_out[j].astype(BF16), jnp.zeros((1, d), F32)
        else:
            wdq = mla_w_dqkv[j]
            wd = wdq[:, :q_rank + kv_rank].astype(BF16)
            wkp = _rope_slot_weights(wdq[:, None, q_rank + kv_rank:]).astype(BF16)
            wuq = mla_w_uq[j].reshape(q_rank, MLA_HEADS, QK_NOPE + QK_ROPE)
            wqn = wuq[:, :, :QK_NOPE].reshape(q_rank, MLA_HEADS * QK_NOPE).astype(BF16)
            wqr = _rope_slot_weights(wuq[:, :, QK_NOPE:]).astype(BF16)
            wuk, wuv = mla_w_uk[j].astype(BF16), mla_w_uv[j].astype(BF16)
            w_last, b_last = mla_w_o[j].astype(BF16), jnp.zeros((1, d), F32)
        rw_hi, rw_lo = _split_bf16(moe_router[i].T)
        ws13 = jnp.concatenate([sh_w1[i], sh_w3[i]], axis=-1).astype(BF16)
        ws2 = sh_w2[i].astype(BF16)

        for ci in range(N_CHAINS):
            x_all = chains[ci]
            mod = jnp.concatenate([mod_all[i, ci * nbc:(ci + 1) * nbc], mod_all[i, nb:nb + 1]], axis=0)

            if kind == 0:
                u = _conf_in(x_all, mod, w_first, row(conf_b1[j]), nt, ntx)
                a = _conf_conv(u, dw_tiles, row(conf_dwb[j]), row(conf_ng[j]), row(conf_nb[j]), nt, ntx)
            elif kind == 1:
                gb, p = _sc_in(x_all, mod, w_first, nt, ntx)
                a = _sc_conv(p, gb, sc_dw[j], nt, ntx)
            else:
                q, k, v = _mla_proj(x_all, mod, wd, wkp, row(mla_q_g[j]), row(mla_kv_g[j]), wqn, wqr,
                                    wuk, wuv, cos_t, sin_t, nt_all, ntx)
                a = _attention(q, k, v, nt * tl)

            xn, h2, topi, topw, rank, counts = _post(alpha, a, w_last, b_last, x_all, mod, row(ln_g[i, 0]),
                                                     row(ln_b[i, 0]), rw_hi, rw_lo, moe_bias[i].reshape(-1, 1),
                                                     nt, ntx)

            dest, block_expert, n_used, n_rows = _dispatch_plan(counts[:, 0], topi, rank, n_tok)
            xs = _sc_row_scatter(h2.reshape(n_tok, d // 2), dest, n_rows)
            ys = _grouped_ffn(block_expert, n_used, xs, moe_w1, moe_w3, moe_w2, i)
            y_rows = _sc_row_gather(ys, dest.reshape(-1)).reshape(TOP_K, n_tok, d // 2)

            chains[ci] = _moe_out(alpha, xn, h2, y_rows, topw.T, mod, ws13, ws2,
                                  row(ln_g[i, 1]), row(ln_b[i, 1]), nt, ntx)
    return jnp.concatenate([xc[:, :seq] for xc in chains], axis=0)
```

```python
import functools

import numpy as np
import jax
import jax.numpy as jnp
from jax import lax
from jax.experimental import pallas as pl
from jax.experimental.pallas import tpu as pltpu
from jax.experimental.pallas import tpu_sc as plsc

F32 = jnp.float32
BF16 = jnp.bfloat16
U32 = jnp.uint32
HIGH_HALF_MASK = np.uint32(0xFFFF0000)

GRID_W = 64
N_MIXERS = 3
LN_EPS = 1e-5
RMS_EPS = 1e-6
N_MOD = 6
MLA_HEADS = 8
QK_NOPE = 128
QK_ROPE = 64
V_HEAD = 128
ROPE_THETA = 10000.0
ATTN_SCALE = (QK_NOPE + QK_ROPE) ** -0.5
Q_SCALE = ATTN_SCALE * 1.4426950408889634
TOP_K = 8
N_GROUPS = 8
TOPK_GROUPS = 4
ROUTED_SCALE = 2.5

SEQ_TILE = 256
CONV_HALO = 16
SHORT_HALO = 8
CONV_ROW_CHUNK = 64
LANE = 128
SUBLANES = 8
Q_TILE = 1024
ATTN_SUB_ROWS = 256
ATTN_ROW_CHUNK = 16
MOE_BLOCK_ROWS = 1024
MOE_SUB_ROWS = 1024
ROUTE_SUB_TOKENS = 128
MOD_COL_TILE = 1536
VMEM_LIMIT = 48 * 1024 * 1024
SC_CORES = 2
SC_SUBCORES = 16
SC_WORKERS = SC_CORES * SC_SUBCORES
SC_WINDOWS = (64, 32, 16)
N_CHAINS = 1


def _params(n_axes):
    return pltpu.CompilerParams(dimension_semantics=("arbitrary",) * n_axes,
                                vmem_limit_bytes=VMEM_LIMIT)


def _split_bf16(a):
    hi = a.astype(BF16)
    lo = (a - hi.astype(F32)).astype(BF16)
    return hi, lo


def _dot(a, b):
    return jnp.dot(a, b, preferred_element_type=F32)


def _dot_nt(a, b):
    return lax.dot_general(a, b, (((1,), (1,)), ((), ())), preferred_element_type=F32)


def _pack_halves(v):
    half = v.shape[-1] // 2
    lo = lax.bitcast_convert_type(v[:, :half].astype(BF16).astype(F32), U32) >> 16
    hi = lax.bitcast_convert_type(v[:, half:].astype(BF16).astype(F32), U32) & HIGH_HALF_MASK
    return hi | lo


def _unpack_halves(p):
    lo = lax.bitcast_convert_type(p << 16, F32)
    hi = lax.bitcast_convert_type(p & HIGH_HALF_MASK, F32)
    return lo, hi


def _dot_packed(p, w):
    lo, hi = _unpack_halves(p)
    half = p.shape[-1]
    return _dot(lo.astype(BF16), w[:half]) + _dot(hi.astype(BF16), w[half:])


def _layer_norm(v, g, b):
    mu = jnp.mean(v, axis=-1, keepdims=True)
    c = v - mu
    var = jnp.mean(c * c, axis=-1, keepdims=True)
    return c * lax.rsqrt(var + LN_EPS) * g + b


def _silu(v):
    return v * jax.nn.sigmoid(v)


def _mod_kernel(c_ref, w_ref, b_ref, o_ref):
    a = _silu(c_ref[...])
    a_hi, a_lo = _split_bf16(a)
    w_hi, w_lo = _split_bf16(w_ref[0])
    o_ref[0] = _dot(a_hi, w_hi) + _dot(a_hi, w_lo) + _dot(a_lo, w_hi) + b_ref[0]


def _modulation(c_all, ada_w, ada_b):
    depth, d, n = ada_w.shape
    rows = c_all.shape[0]
    tn = MOD_COL_TILE
    return pl.pallas_call(
        _mod_kernel,
        out_shape=jax.ShapeDtypeStruct((depth, rows, n), F32),
        grid=(depth, n // tn),
        in_specs=[pl.BlockSpec((rows, d), lambda i, j: (0, 0)),
                  pl.BlockSpec((1, d, tn), lambda i, j: (i, 0, j)),
                  pl.BlockSpec((1, 1, tn), lambda i, j: (i, 0, j))],
        out_specs=pl.BlockSpec((1, rows, tn), lambda i, j: (i, 0, j)),
        compiler_params=_params(2),
        name="adaln_mod",
    )(c_all, ada_w, ada_b.reshape(depth, 1, n))


def _tok_spec(d, tl=SEQ_TILE):
    return pl.BlockSpec((1, tl, d), lambda b, l: (b, l, 0))


def _mod_spec(d, n_batch, ntx):
    return pl.BlockSpec((1, N_MOD, d), lambda b, l: (jnp.where(l < ntx, b, n_batch), 0, 0))


def _full_spec(shape):
    zeros = (0,) * len(shape)
    return pl.BlockSpec(shape, lambda b, l: zeros)


def _halo_specs(d, halo, seq_len, tl=SEQ_TILE):
    per_tile = tl // halo
    last = seq_len // halo - 1
    prev = pl.BlockSpec((1, halo, d), lambda b, l: (b, jnp.maximum(l * per_tile - 1, 0), 0))
    nxt = pl.BlockSpec((1, halo, d), lambda b, l: (b, jnp.minimum((l + 1) * per_tile, last), 0))
    return prev, nxt


def _segment_edges(l, ntx, nt):
    first = jnp.logical_or(l == 0, l == ntx)
    last = jnp.logical_or(l == ntx - 1, l == nt - 1)
    return first, last


def _conf_in_kernel(x_ref, mod_ref, w1_ref, b1_ref, u_ref):
    d = x_ref.shape[-1]
    h = x_ref[0] * (1.0 + mod_ref[0, 1:2, :]) + mod_ref[0, 0:1, :]
    z = _dot(h.astype(BF16), w1_ref[...]) + b1_ref[...]
    u_ref[0] = z[:, :d] * jax.nn.sigmoid(z[:, d:])


def _conf_in(x_all, mod, w1, b1, nt, ntx):
    nb, seq_len, d = x_all.shape
    return pl.pallas_call(
        _conf_in_kernel,
        out_shape=jax.ShapeDtypeStruct((nb, nt * SEQ_TILE, d), F32),
        grid=(nb, nt),
        in_specs=[_tok_spec(d), _mod_spec(d, nb, ntx), _full_spec(w1.shape), _full_spec(b1.shape)],
        out_specs=_tok_spec(d),
        compiler_params=_params(2),
        name="conf_in",
    )(x_all, mod, w1, b1)


def _conf_conv_kernel(ntx, nt, up_ref, uc_ref, un_ref, dw_ref, dwb_ref, ng_ref, nb_ref,
                      a_ref, sh_ref, conv_ref):
    tl, d = uc_ref.shape[1], uc_ref.shape[2]
    taps = dw_ref.shape[0]
    lead = CONV_HALO - (taps - 1) // 2
    first, last = _segment_edges(pl.program_id(1), ntx, nt)
    sh_ref[0, 0:CONV_HALO, :] = jnp.where(first, 0.0, up_ref[0])
    sh_ref[0, CONV_HALO:CONV_HALO + tl, :] = uc_ref[0]
    sh_ref[0, CONV_HALO + tl:, :] = jnp.where(last, 0.0, un_ref[0])
    span = tl + 2 * CONV_HALO - SUBLANES
    for s in range(1, SUBLANES):
        sh_ref[s, 0:span, :] = sh_ref[0, s:s + span, :]
    groups = CONV_ROW_CHUNK // SUBLANES

    def row_chunk(i, carry):
        r0 = pl.multiple_of(i * CONV_ROW_CHUNK, CONV_ROW_CHUNK)
        for c0 in range(0, d, LANE):
            accs = [jnp.zeros((SUBLANES, LANE), F32) for _ in range(groups)]
            for k in range(taps):
                res = (lead + k) % SUBLANES
                off = lead + k - res
                w = dw_ref[k, :, c0:c0 + LANE]
                for g in range(groups):
                    lo = r0 + (off + g * SUBLANES)
                    accs[g] = accs[g] + w * sh_ref[res, pl.ds(lo, SUBLANES), c0:c0 + LANE]
            for g in range(groups):
                conv_ref[pl.ds(r0 + g * SUBLANES, SUBLANES), c0:c0 + LANE] = accs[g]
        return carry

    lax.fori_loop(0, tl // CONV_ROW_CHUNK, row_chunk, 0)
    v = _layer_norm(conv_ref[...] + dwb_ref[...], ng_ref[...], nb_ref[...])
    a_ref[0] = _silu(v).astype(BF16)


def _conf_conv(u, dw, dwb, ng, nb_, nt, ntx):
    nb, seq_len, d = u.shape
    prev, nxt = _halo_specs(d, CONV_HALO, seq_len)
    return pl.pallas_call(
        functools.partial(_conf_conv_kernel, ntx, nt),
        out_shape=jax.ShapeDtypeStruct((nb, seq_len, d), BF16),
        grid=(nb, nt),
        in_specs=[prev, _tok_spec(d), nxt, _full_spec(dw.shape), _full_spec(dwb.shape),
                  _full_spec(ng.shape), _full_spec(nb_.shape)],
        out_specs=_tok_spec(d),
        scratch_shapes=[pltpu.VMEM((SUBLANES, SEQ_TILE + 2 * CONV_HALO, d), F32),
                        pltpu.VMEM((SEQ_TILE, d), F32)],
        compiler_params=_params(2),
        name="conf_conv",
    )(u, u, u, dw, dwb, ng, nb_)


def _sc_in_kernel(x_ref, mod_ref, w_ref, gb_ref, p_ref):
    d = x_ref.shape[-1]
    h = x_ref[0] * (1.0 + mod_ref[0, 1:2, :]) + mod_ref[0, 0:1, :]
    z = _dot(h.astype(BF16), w_ref[...])
    gb_ref[0] = z[:, :d]
    p_ref[0] = z[:, d:2 * d] * z[:, 2 * d:]


def _sc_in(x_all, mod, w_in, nt, ntx):
    nb, seq_len, d = x_all.shape
    shp = jax.ShapeDtypeStruct((nb, nt * SEQ_TILE, d), F32)
    return pl.pallas_call(
        _sc_in_kernel,
        out_shape=(shp, shp),
        grid=(nb, nt),
        in_specs=[_tok_spec(d), _mod_spec(d, nb, ntx), _full_spec(w_in.shape)],
        out_specs=(_tok_spec(d), _tok_spec(d)),
        compiler_params=_params(2),
        name="sc_in",
    )(x_all, mod, w_in)


def _sc_conv_kernel(ntx, nt, pp_ref, pc_ref, pn_ref, gb_ref, dw_ref, a_ref, ext_ref):
    tl = pc_ref.shape[1]
    taps = dw_ref.shape[0]
    lead = SHORT_HALO - (taps - 1) // 2
    first, last = _segment_edges(pl.program_id(1), ntx, nt)
    ext_ref[0:SHORT_HALO, :] = jnp.where(first, 0.0, pp_ref[0])
    ext_ref[SHORT_HALO:SHORT_HALO + tl, :] = pc_ref[0]
    ext_ref[SHORT_HALO + tl:, :] = jnp.where(last, 0.0, pn_ref[0])
    acc = dw_ref[0:1, :] * ext_ref[lead:lead + tl, :]
    for k in range(1, taps):
        acc = acc + dw_ref[k:k + 1, :] * ext_ref[lead + k:lead + k + tl, :]
    a_ref[0] = (gb_ref[0] * acc).astype(BF16)


def _sc_conv(p, gb, dw, nt, ntx):
    nb, seq_len, d = p.shape
    prev, nxt = _halo_specs(d, SHORT_HALO, seq_len)
    return pl.pallas_call(
        functools.partial(_sc_conv_kernel, ntx, nt),
        out_shape=jax.ShapeDtypeStruct((nb, seq_len, d), BF16),
        grid=(nb, nt),
        in_specs=[prev, _tok_spec(d), nxt, _tok_spec(d), _full_spec(dw.shape)],
        out_specs=_tok_spec(d),
        scratch_shapes=[pltpu.VMEM((SEQ_TILE + 2 * SHORT_HALO, d), F32)],
        compiler_params=_params(2),
        name="sc_conv",
    )(p, p, p, gb, dw)


def _rms(v, g):
    return v * lax.rsqrt(jnp.mean(v * v, axis=-1, keepdims=True) + RMS_EPS) * g


def _mla_proj_kernel(ntx, q_rank,
                     x_ref, mod_ref, wd_ref, wkp_ref, qg_ref, kvg_ref, wqn_ref, wqr_ref, wuk_ref, wuv_ref,
                     cos_ref, sin_ref, q_ref, k_ref, v_ref):
    is_latent = pl.program_id(1) < ntx
    h = (x_ref[0] * (1.0 + mod_ref[0, 1:2, :]) + mod_ref[0, 0:1, :]).astype(BF16)
    dn = _dot(h, wd_ref[...])
    cq = _rms(dn[:, :q_rank], qg_ref[...]).astype(BF16)
    ckv = _rms(dn[:, q_rank:], kvg_ref[...]).astype(BF16)
    cos = jnp.where(is_latent, cos_ref[...], 1.0)
    sin = jnp.where(is_latent, sin_ref[...], 0.0)
    kp2 = _dot(h, wkp_ref[...])
    kp = kp2[:, :LANE] * cos + kp2[:, LANE:] * sin
    kn = _dot(ckv, wuk_ref[...])
    vv = _dot(ckv, wuv_ref[...])
    qn = _dot(cq, wqn_ref[...]) * Q_SCALE
    qr2 = _dot(cq, wqr_ref[...])
    hw = MLA_HEADS * LANE
    for hd in range(MLA_HEADS):
        sl = slice(hd * LANE, (hd + 1) * LANE)
        qr = (qr2[:, sl] * cos + qr2[:, hw + hd * LANE:hw + (hd + 1) * LANE] * sin) * Q_SCALE
        q_ref[0, hd, :, 0:LANE] = qn[:, sl].astype(BF16)
        q_ref[0, hd, :, LANE:] = qr.astype(BF16)
        k_ref[0, hd, :, 0:LANE] = kn[:, sl].astype(BF16)
        k_ref[0, hd, :, LANE:] = kp.astype(BF16)
        v_ref[0, hd] = vv[:, sl].astype(BF16)


def _mla_proj(x_all, mod, wd, wkp, qg, kvg, wqn, wqr, wuk, wuv, cos_t, sin_t, nt, ntx):
    nb, seq_len, d = x_all.shape
    q_rank = qg.shape[-1]
    tl = SEQ_TILE
    rope_spec = pl.BlockSpec((tl, LANE), lambda b, l: (jnp.minimum(l, ntx - 1), 0))
    qk_shape = jax.ShapeDtypeStruct((nb, MLA_HEADS, nt * tl, 2 * LANE), BF16)
    v_shape = jax.ShapeDtypeStruct((nb, MLA_HEADS, nt * tl, LANE), BF16)
    qk_spec = pl.BlockSpec((1, MLA_HEADS, tl, 2 * LANE), lambda b, l: (b, 0, l, 0))
    v_spec = pl.BlockSpec((1, MLA_HEADS, tl, LANE), lambda b, l: (b, 0, l, 0))
    return pl.pallas_call(
        functools.partial(_mla_proj_kernel, ntx, q_rank),
        out_shape=(qk_shape, qk_shape, v_shape),
        grid=(nb, nt),
        in_specs=[_tok_spec(d), _mod_spec(d, nb, ntx), _full_spec(wd.shape), _full_spec(wkp.shape),
                  _full_spec(qg.shape), _full_spec(kvg.shape), _full_spec(wqn.shape), _full_spec(wqr.shape),
                  _full_spec(wuk.shape), _full_spec(wuv.shape), rope_spec, rope_spec],
        out_specs=(qk_spec, qk_spec, v_spec),
        compiler_params=_params(2),
        name="mla_proj",
    )(x_all, mod, wd, wkp, qg, kvg, wqn, wqr, wuk, wuv, cos_t, sin_t)


def _attn_kernel(q_ref, k_ref, v_ref, o_ref, s_ref, p_ref, l_ref):
    tq = q_ref.shape[2]
    for r0 in range(0, tq, ATTN_SUB_ROWS):
        rows = slice(r0, r0 + ATTN_SUB_ROWS)
        s_ref[rows, :] = _dot_nt(q_ref[0, 0, rows, :], k_ref[0, 0])
        for c0 in range(r0, r0 + ATTN_SUB_ROWS, ATTN_ROW_CHUNK):
            chunk = slice(c0, c0 + ATTN_ROW_CHUNK)
            s = s_ref[chunk, :]
            p = jnp.exp2(s - jnp.max(s, axis=-1, keepdims=True))
            l_ref[chunk, :] = jnp.broadcast_to(jnp.sum(p, axis=-1, keepdims=True), (ATTN_ROW_CHUNK, V_HEAD))
            p_ref[chunk, :] = p.astype(BF16)
        o = _dot(p_ref[rows, :], v_ref[0, 0])
        o_ref[0, rows, :] = (o / l_ref[rows, :]).astype(BF16)


def _attention(q, k, v, n_q):
    nb, nh, n_k, dk = k.shape
    tq = Q_TILE
    return pl.pallas_call(
        _attn_kernel,
        out_shape=jax.ShapeDtypeStruct((nb, n_q, nh * V_HEAD), BF16),
        grid=(nb, nh, n_q // tq),
        in_specs=[pl.BlockSpec((1, 1, tq, dk), lambda b, h, i: (b, h, i, 0)),
                  pl.BlockSpec((1, 1, n_k, dk), lambda b, h, i: (b, h, 0, 0)),
                  pl.BlockSpec((1, 1, n_k, V_HEAD), lambda b, h, i: (b, h, 0, 0))],
        out_specs=pl.BlockSpec((1, tq, V_HEAD), lambda b, h, i: (b, i, h)),
        scratch_shapes=[pltpu.VMEM((tq, n_k), F32), pltpu.VMEM((tq, n_k), BF16), pltpu.VMEM((tq, V_HEAD), F32)],
        compiler_params=_params(3),
        name="mla_attn",
    )(q, k, v)


def _route(sel, s):
    n_e, n_t = sel.shape
    per = n_e // N_GROUPS
    sel3 = sel.reshape(N_GROUPS, per, n_t)
    s3 = s.reshape(N_GROUPS, per, n_t)
    iota_p = lax.broadcasted_iota(jnp.int32, (N_GROUPS, per, n_t), 1).astype(F32)
    iota_g = lax.broadcasted_iota(jnp.int32, (N_GROUPS, 1, n_t), 0).astype(F32)
    neg = -jnp.inf
    m1 = jnp.max(sel3, axis=1, keepdims=True)
    first = jnp.min(jnp.where(sel3 == m1, iota_p, float(per)), axis=1, keepdims=True)
    m2 = jnp.max(jnp.where(iota_p == first, neg, sel3), axis=1, keepdims=True)
    gs = m1 + m2
    gsel = jnp.zeros((N_GROUPS, 1, n_t), F32)
    for _ in range(TOPK_GROUPS):
        gm = jnp.max(gs, axis=0, keepdims=True)
        gfirst = jnp.min(jnp.where(gs == gm, iota_g, float(N_GROUPS)), axis=0, keepdims=True)
        pick = iota_g == gfirst
        gsel = jnp.where(pick, 1.0, gsel)
        gs = jnp.where(pick, neg, gs)
    val = jnp.where(gsel > 0.0, sel3, neg)
    iota_e = lax.broadcasted_iota(jnp.int32, (N_GROUPS, per, n_t), 0).astype(F32) * per + iota_p
    chosen = jnp.zeros((N_GROUPS, per, n_t), F32)
    idx, wts = [], []
    for _ in range(TOP_K):
        m = jnp.max(jnp.max(val, axis=1, keepdims=True), axis=0, keepdims=True)
        e = jnp.min(jnp.min(jnp.where(val == m, iota_e, float(n_e)), axis=1, keepdims=True), axis=0, keepdims=True)
        pick = iota_e == e
        wts.append(jnp.sum(jnp.sum(jnp.where(pick, s3, 0.0), axis=1, keepdims=True), axis=0, keepdims=True))
        idx.append(e)
        chosen = jnp.where(pick, 1.0, chosen)
        val = jnp.where(pick, neg, val)
    total = wts[0]
    for w in wts[1:]:
        total = total + w
    wts = [w / total * ROUTED_SCALE for w in wts]
    return idx, wts, chosen, iota_e


def _post_kernel(alpha, a_ref, w_ref, b_ref, x_ref, mod_ref, lng_ref, lnb_ref, rw_hi_ref, rw_lo_ref, rb_ref, tri_ref,
                 xn_ref, h2_ref, topi_ref, topw_ref, rank_ref, cnt_ref, run_ref):
    first_step = jnp.logical_and(pl.program_id(0) == 0, pl.program_id(1) == 0)

    @pl.when(first_step)
    def _():
        run_ref[...] = jnp.zeros_like(run_ref)

    y = _dot(a_ref[0], w_ref[...]) + b_ref[...]
    run = run_ref[...]
    for t0 in range(0, y.shape[0], ROUTE_SUB_TOKENS):
        rows = slice(t0, t0 + ROUTE_SUB_TOKENS)
        xn = _layer_norm(alpha * x_ref[0, rows, :] + mod_ref[0, 2:3, :] * y[rows], lng_ref[...], lnb_ref[...])
        xn_ref[0, rows, :] = xn
        h2 = xn * (1.0 + mod_ref[0, 4:5, :]) + mod_ref[0, 3:4, :]
        h_hi, h_lo = _split_bf16(h2)
        h2_ref[0, rows, :] = _pack_halves(h2)
        logits = _dot_nt(rw_hi_ref[...], h_hi) + _dot_nt(rw_hi_ref[...], h_lo) + _dot_nt(rw_lo_ref[...], h_hi)
        s = jax.nn.sigmoid(logits)
        idx, wts, chosen3, iota_e = _route(s + rb_ref[...], s)
        n_e, n_t = s.shape
        chosen = chosen3.reshape(n_e, n_t)
        rank = run[:, 0:1] + _dot(chosen.astype(BF16), tri_ref[...])
        rank3 = rank.reshape(chosen3.shape)
        for k in range(TOP_K):
            rk = jnp.sum(jnp.sum(jnp.where(iota_e == idx[k], rank3, 0.0), axis=1, keepdims=True), axis=0,
                         keepdims=True)
            topi_ref[k:k + 1, rows] = idx[k].reshape(1, n_t).astype(jnp.int32)
            topw_ref[k:k + 1, rows] = wts[k].reshape(1, n_t)
            rank_ref[k:k + 1, rows] = rk.reshape(1, n_t).astype(jnp.int32)
        run = run + jnp.sum(chosen, axis=1, keepdims=True)
    run_ref[...] = run
    cnt_ref[...] = run.astype(jnp.int32)


def _post(alpha, a, w, bias, x_all, mod, lng, lnb, rw_hi, rw_lo, rb, nt, ntx):
    nb, _, dk = a.shape
    d = x_all.shape[-1]
    n_e = rw_hi.shape[0]
    tl = SEQ_TILE
    n_tok = nb * nt * tl
    col_spec = lambda rows: pl.BlockSpec((rows, tl), lambda b, l: (0, b * nt + l))
    sub = ROUTE_SUB_TOKENS
    row_i = lax.broadcasted_iota(jnp.int32, (sub, sub), 0)
    col_i = lax.broadcasted_iota(jnp.int32, (sub, sub), 1)
    tri = jnp.where(row_i < col_i, 1.0, 0.0).astype(BF16)
    return pl.pallas_call(
        functools.partial(_post_kernel, alpha),
        out_shape=(jax.ShapeDtypeStruct((nb, nt * tl, d), F32),
                   jax.ShapeDtypeStruct((nb, nt * tl, d // 2), U32),
                   jax.ShapeDtypeStruct((TOP_K, n_tok), jnp.int32),
                   jax.ShapeDtypeStruct((TOP_K, n_tok), F32),
                   jax.ShapeDtypeStruct((TOP_K, n_tok), jnp.int32),
                   jax.ShapeDtypeStruct((n_e, LANE), jnp.int32)),
        grid=(nb, nt),
        in_specs=[_tok_spec(dk), _full_spec(w.shape), _full_spec(bias.shape), _tok_spec(d),
                  _mod_spec(d, nb, ntx), _full_spec(lng.shape), _full_spec(lnb.shape),
                  _full_spec(rw_hi.shape), _full_spec(rw_lo.shape), _full_spec(rb.shape), _full_spec(tri.shape)],
        out_specs=(_tok_spec(d), _tok_spec(d // 2), col_spec(TOP_K), col_spec(TOP_K), col_spec(TOP_K),
                   _full_spec((n_e, LANE))),
        scratch_shapes=[pltpu.VMEM((n_e, LANE), F32)],
        compiler_params=_params(2),
        name="mixer_post",
    )(a, w, bias, x_all, mod, lng, lnb, rw_hi, rw_lo, rb, tri)


def _gmm_kernel(be_ref, nu_ref, xs_ref, w1_ref, w3_ref, w2_ref, ys_ref, w13_s, w2_s):
    i = pl.program_id(0)
    de = w2_ref.shape[2]
    changed = jnp.logical_or(i == 0, be_ref[i] != be_ref[jnp.maximum(i - 1, 0)])

    @pl.when(changed)
    def _():
        w13_s[:, :de] = w1_ref[0, 0].astype(BF16)
        w13_s[:, de:] = w3_ref[0, 0].astype(BF16)
        w2_s[...] = w2_ref[0, 0].astype(BF16)

    @pl.when(i < nu_ref[0])
    def _():
        for r0 in range(0, xs_ref.shape[0], MOE_SUB_ROWS):
            rows = slice(r0, r0 + MOE_SUB_ROWS)
            z = _dot_packed(xs_ref[rows, :], w13_s)
            hmid = (_silu(z[:, :de]) * z[:, de:]).astype(BF16)
            ys_ref[rows, :] = _pack_halves(_dot(hmid, w2_s[...]))


def _grouped_ffn(block_expert, n_used, xs, w1, w3, w2, layer):
    n_rows, dp = xs.shape
    d = 2 * dp
    bm = MOE_BLOCK_ROWS
    de = w2.shape[2]
    grid_spec = pltpu.PrefetchScalarGridSpec(
        num_scalar_prefetch=2,
        grid=(n_rows // bm,),
        in_specs=[pl.BlockSpec((bm, dp), lambda i, be, nu: (i, 0)),
                  pl.BlockSpec((1, 1, d, de), lambda i, be, nu: (layer, be[i], 0, 0)),
                  pl.BlockSpec((1, 1, d, de), lambda i, be, nu: (layer, be[i], 0, 0)),
                  pl.BlockSpec((1, 1, de, d), lambda i, be, nu: (layer, be[i], 0, 0))],
        out_specs=pl.BlockSpec((bm, dp), lambda i, be, nu: (i, 0)),
        scratch_shapes=[pltpu.VMEM((d, 2 * de), BF16), pltpu.VMEM((de, d), BF16)],
    )
    return pl.pallas_call(
        _gmm_kernel,
        out_shape=jax.ShapeDtypeStruct((n_rows, dp), U32),
        grid_spec=grid_spec,
        compiler_params=_params(1),
        name="moe_grouped_ffn",
    )(block_expert, n_used, xs, w1, w3, w2)


def _sc_mesh():
    return plsc.VectorSubcoreMesh(core_axis_name="c", subcore_axis_name="s",
                                  num_cores=SC_CORES, num_subcores=SC_SUBCORES)


def _sc_worker():
    return lax.axis_index("s") * SC_CORES + lax.axis_index("c")


def _sc_window(rows_per_worker):
    for win in SC_WINDOWS:
        if rows_per_worker % (2 * win) == 0:
            return win
    raise ValueError(f"no SparseCore window divides {rows_per_worker} rows per worker")


def _sc_row_scatter(src, dest, n_out):
    n_k, n_tok = dest.shape
    dp = src.shape[1]
    assert n_tok % SC_WORKERS == 0
    per_w = n_tok // SC_WORKERS
    win = _sc_window(per_w)
    n_win = per_w // win
    idx = dest.reshape(n_k, SC_WORKERS, n_win, win).transpose(1, 2, 0, 3)

    def body(src_hbm, idx_hbm, out_hbm, idx_v, rows_v, load_sem, scat_sem):
        wid = _sc_worker()
        base = wid * per_w
        pltpu.sync_copy(idx_hbm.at[wid], idx_v)

        def load(g, slot):
            return pltpu.make_async_copy(src_hbm.at[pl.ds(base + g * win, win)], rows_v.at[slot], load_sem.at[slot])

        def scatter(g, slot, k):
            return pltpu.make_async_copy(rows_v.at[slot], out_hbm.at[idx_v.at[g, k]], scat_sem.at[slot])

        load(0, 0).start()

        @pl.loop(0, n_win, step=2)
        def _(g):
            for slot in range(2):
                cur = g + slot
                load(cur, slot).wait()

                @pl.when(cur + 1 < n_win)
                def _():
                    @pl.when(cur >= 1)
                    def _():
                        for k in range(n_k):
                            scatter(cur - 1, 1 - slot, k).wait()
                    load(cur + 1, 1 - slot).start()

                for k in range(n_k):
                    scatter(cur, slot, k).start()

        for slot in range(2):
            for k in range(n_k):
                scatter(n_win - 2 + slot, slot, k).wait()

    return pl.kernel(
        body, mesh=_sc_mesh(),
        out_type=jax.ShapeDtypeStruct((n_out, dp), src.dtype),
        scratch_types=[pltpu.VMEM((n_win, n_k, win), jnp.int32),
                       pltpu.VMEM((2, win, dp), src.dtype),
                       pltpu.SemaphoreType.DMA((2,)),
                       pltpu.SemaphoreType.DMA((2,))],
        compiler_params=pltpu.CompilerParams(use_tc_tiling_on_sc=True),
        name="sc_dispatch_scatter",
    )(src, idx)


def _sc_row_gather(table, idx):
    n = idx.shape[0]
    dp = table.shape[1]
    assert n % SC_WORKERS == 0
    per_w = n // SC_WORKERS
    win = _sc_window(per_w)
    n_win = per_w // win

    def body(table_hbm, idx_hbm, out_hbm, idx_v, rows_v, gather_sem, put_sem):
        wid = _sc_worker()
        base = wid * per_w
        pltpu.sync_copy(idx_hbm.at[pl.ds(base, per_w)], idx_v)

        def gather(g, slot):
            return pltpu.make_async_copy(table_hbm.at[idx_v.at[pl.ds(g * win, win)]], rows_v.at[slot],
                                         gather_sem.at[slot])

        def put(g, slot):
            return pltpu.make_async_copy(rows_v.at[slot], out_hbm.at[pl.ds(base + g * win, win)], put_sem.at[slot])

        gather(0, 0).start()

        @pl.loop(0, n_win, step=2)
        def _(g):
            for slot in range(2):
                cur = g + slot
                gather(cur, slot).wait()

                @pl.when(cur + 1 < n_win)
                def _():
                    @pl.when(cur >= 1)
                    def _():
                        put(cur - 1, 1 - slot).wait()
                    gather(cur + 1, 1 - slot).start()

                put(cur, slot).start()

        for slot in range(2):
            put(n_win - 2 + slot, slot).wait()

    return pl.kernel(
        body, mesh=_sc_mesh(),
        out_type=jax.ShapeDtypeStruct((n, dp), table.dtype),
        scratch_types=[pltpu.VMEM((per_w,), jnp.int32),
                       pltpu.VMEM((2, win, dp), table.dtype),
                       pltpu.SemaphoreType.DMA((2,)),
                       pltpu.SemaphoreType.DMA((2,))],
        compiler_params=pltpu.CompilerParams(use_tc_tiling_on_sc=True),
        name="sc_combine_gather",
    )(table, idx)


def _moe_out_kernel(alpha, x_ref, h2_ref, yg_ref, tw_ref, mod_ref, ws13_ref, ws2_ref, lng_ref, lnb_ref, o_ref):
    de = ws2_ref.shape[0]
    z = _dot_packed(h2_ref[0], ws13_ref)
    hmid = (_silu(z[:, :de]) * z[:, de:]).astype(BF16)
    y = _dot(hmid, ws2_ref[...])
    lo, hi = _unpack_halves(yg_ref[0])
    acc_lo, acc_hi = lo * tw_ref[:, 0:1], hi * tw_ref[:, 0:1]
    for k in range(1, yg_ref.shape[0]):
        lo, hi = _unpack_halves(yg_ref[k])
        acc_lo = acc_lo + lo * tw_ref[:, k:k + 1]
        acc_hi = acc_hi + hi * tw_ref[:, k:k + 1]
    y = y + jnp.concatenate([acc_lo, acc_hi], axis=1)
    o_ref[0] = _layer_norm(alpha * x_ref[0] + mod_ref[0, 5:6, :] * y, lng_ref[...], lnb_ref[...])


def _moe_out(alpha, xn, h2, y_rows, topw_t, mod, ws13, ws2, lng, lnb, nt, ntx):
    nb, _, d = xn.shape
    n_k = y_rows.shape[0]
    tl = SEQ_TILE
    return pl.pallas_call(
        functools.partial(_moe_out_kernel, alpha),
        out_shape=jax.ShapeDtypeStruct((nb, nt * tl, d), F32),
        grid=(nb, nt),
        in_specs=[_tok_spec(d), _tok_spec(d // 2),
                  pl.BlockSpec((n_k, tl, d // 2), lambda b, l: (0, b * nt + l, 0)),
                  pl.BlockSpec((tl, n_k), lambda b, l: (b * nt + l, 0)),
                  _mod_spec(d, nb, ntx),
                  _full_spec(ws13.shape), _full_spec(ws2.shape), _full_spec(lng.shape), _full_spec(lnb.shape)],
        out_specs=_tok_spec(d),
        compiler_params=_params(2),
        name="moe_out",
    )(xn, h2, y_rows, topw_t, mod, ws13, ws2, lng, lnb)


def _dispatch_plan(counts, topi, rank, n_tok):
    n_e = counts.shape[0]
    bm = MOE_BLOCK_ROWS
    padded = (counts + bm - 1) // bm * bm
    pad_end = jnp.cumsum(padded)
    pad_start = pad_end - padded
    onehot = topi[:, None, :] == jnp.arange(n_e, dtype=jnp.int32)[None, :, None]
    dest = rank + jnp.sum(jnp.where(onehot, pad_start[None, :, None], 0), axis=1)
    n_blocks = n_tok * TOP_K // bm + n_e
    block_start = jnp.arange(n_blocks, dtype=jnp.int32) * bm
    block_expert = jnp.sum((pad_end[None, :] <= block_start[:, None]).astype(jnp.int32), axis=1)
    block_expert = jnp.minimum(block_expert, n_e - 1)
    n_used = (pad_end[-1] // bm).astype(jnp.int32).reshape(1)
    return dest.astype(jnp.int32), block_expert, n_used, n_blocks * bm


def _rope_tables(seq):
    n_freq = QK_ROPE // 4
    inv_freq = ROPE_THETA ** (-jnp.arange(n_freq, dtype=F32) / n_freq)
    pos = jnp.arange(seq, dtype=jnp.int32)
    r = (pos // GRID_W).astype(F32)
    col = (pos % GRID_W).astype(F32)
    ang = jnp.concatenate([r[:, None] * inv_freq, col[:, None] * inv_freq], -1)
    cos, sin = jnp.cos(ang), jnp.sin(ang)
    zeros = jnp.zeros((seq, LANE - QK_ROPE), F32)
    cos_slot = jnp.concatenate([cos, cos, zeros], -1)
    sin_slot = jnp.concatenate([-sin, sin, zeros], -1)
    return cos_slot, sin_slot


def _rope_slot_weights(w_rope):
    k, n, _ = w_rope.shape
    half = QK_ROPE // 2
    swapped = jnp.concatenate([w_rope[..., half:], w_rope[..., :half]], -1)
    pad = jnp.zeros((k, n, LANE - QK_ROPE), w_rope.dtype)
    plain = jnp.concatenate([w_rope, pad], -1).reshape(k, n * LANE)
    swp = jnp.concatenate([swapped, pad], -1).reshape(k, n * LANE)
    return jnp.concatenate([plain, swp], -1)


def kernel(x, c, ctx, c_ctx, ada_w, ada_b, ln_g, ln_b, conf_w1, conf_b1, conf_dw, conf_dwb, conf_ng, conf_nb, conf_w2, conf_b2, sc_w_in, sc_dw, sc_w_out, mla_w_dqkv, mla_q_g, mla_kv_g, mla_w_uq, mla_w_uk, mla_w_uv, mla_w_o, moe_router, moe_bias, moe_w1, moe_w3, moe_w2, sh_w1, sh_w3, sh_w2):
    nb, seq, d = x.shape
    l_ctx = ctx.shape[1]
    depth = ada_w.shape[0]
    alpha = (2.0 * depth) ** 0.25
    tl = SEQ_TILE
    assert seq % tl == 0 and l_ctx % tl == 0 and seq % Q_TILE == 0 and seq % GRID_W == 0
    ntx = seq // tl
    nt_all = (seq + l_ctx) // tl
    attn_layers = [i for i in range(depth) if i % N_MIXERS == 2]
    last_ctx_reader = attn_layers[-1] if attn_layers else -1

    rows = -(-(nb + 1) // 8) * 8
    c_all = jnp.zeros((rows, d), F32).at[:nb].set(c).at[nb].set(c_ctx)
    mod_all = _modulation(c_all, ada_w, ada_b).reshape(depth, rows, N_MOD, d)

    assert nb % N_CHAINS == 0
    nbc = nb // N_CHAINS
    chains = [jnp.concatenate([x[c0:c0 + nbc], ctx[c0:c0 + nbc]], axis=1) for c0 in range(0, nb, nbc)]
    q_rank, kv_rank = mla_q_g.shape[1], mla_kv_g.shape[1]
    cos_t, sin_t = _rope_tables(seq)
    row = lambda v: v.reshape(1, -1)

    for i in range(depth):
        need_ctx = i < last_ctx_reader
        kind, j = i % N_MIXERS, i // N_MIXERS
        nt = nt_all if need_ctx else ntx
        n_tok = nbc * nt * tl

        if kind == 0:
            w_first = conf_w1[j].astype(BF16)
            dw_tiles = jnp.broadcast_to(conf_dw[j][:, None, :], (conf_dw.shape[1], SUBLANES, d))
            w_last, b_last = conf_w2[j].astype(BF16), row(conf_b2[j])
        elif kind == 1:
            w_first = sc_w_in[j].astype(BF16)
            w_last, b_last = sc_w_out[j].astype(BF16), jnp.zeros((1, d), F32)
        else:
            wdq = mla_w_dqkv[j]
            wd = wdq[:, :q_rank + kv_rank].astype(BF16)
            wkp = _rope_slot_weights(wdq[:, None, q_rank + kv_rank:]).astype(BF16)
            wuq = mla_w_uq[j].reshape(q_rank, MLA_HEADS, QK_NOPE + QK_ROPE)
            wqn = wuq[:, :, :QK_NOPE].reshape(q_rank, MLA_HEADS * QK_NOPE).astype(BF16)
            wqr = _rope_slot_weights(wuq[:, :, QK_NOPE:]).astype(BF16)
            wuk, wuv = mla_w_uk[j].astype(BF16), mla_w_uv[j].astype(BF16)
            w_last, b_last = mla_w_o[j].astype(BF16), jnp.zeros((1, d), F32)
        rw_hi, rw_lo = _split_bf16(moe_router[i].T)
        ws13 = jnp.concatenate([sh_w1[i], sh_w3[i]], axis=-1).astype(BF16)
        ws2 = sh_w2[i].astype(BF16)

        for ci in range(N_CHAINS):
            x_all = chains[ci]
            mod = jnp.concatenate([mod_all[i, ci * nbc:(ci + 1) * nbc], mod_all[i, nb:nb + 1]], axis=0)

            if kind == 0:
                u = _conf_in(x_all, mod, w_first, row(conf_b1[j]), nt, ntx)
                a = _conf_conv(u, dw_tiles, row(conf_dwb[j]), row(conf_ng[j]), row(conf_nb[j]), nt, ntx)
            elif kind == 1:
                gb, p = _sc_in(x_all, mod, w_first, nt, ntx)
                a = _sc_conv(p, gb, sc_dw[j], nt, ntx)
            else:
                q, k, v = _mla_proj(x_all, mod, wd, wkp, row(mla_q_g[j]), row(mla_kv_g[j]), wqn, wqr,
                                    wuk, wuv, cos_t, sin_t, nt_all, ntx)
                a = _attention(q, k, v, nt * tl)

            xn, h2, topi, topw, rank, counts = _post(alpha, a, w_last, b_last, x_all, mod, row(ln_g[i, 0]),
                                                     row(ln_b[i, 0]), rw_hi, rw_lo, moe_bias[i].reshape(-1, 1),
                                                     nt, ntx)

            dest, block_expert, n_used, n_rows = _dispatch_plan(counts[:, 0], topi, rank, n_tok)
            xs = _sc_row_scatter(h2.reshape(n_tok, d // 2), dest, n_rows)
            ys = _grouped_ffn(block_expert, n_used, xs, moe_w1, moe_w3, moe_w2, i)
            y_rows = _sc_row_gather(ys, dest.reshape(-1)).reshape(TOP_K, n_tok, d // 2)

            chains[ci] = _moe_out(alpha, xn, h2, y_rows, topw.T, mod, ws13, ws2,
                                  row(ln_g[i, 1]), row(ln_b[i, 1]), nt, ntx)
    return jnp.concatenate([xc[:, :seq] for xc in chains], axis=0)
```

```python
import functools

import numpy as np
import jax
import jax.numpy as jnp
from jax import lax
from jax.experimental import pallas as pl
from jax.experimental.pallas import tpu as pltpu
from jax.experimental.pallas import tpu_sc as plsc

F32 = jnp.float32
BF16 = jnp.bfloat16
U32 = jnp.uint32
HIGH_HALF_MASK = np.uint32(0xFFFF0000)

GRID_W = 64
N_MIXERS = 3
LN_EPS = 1e-5
RMS_EPS = 1e-6
N_MOD = 6
MLA_HEADS = 8
QK_NOPE = 128
QK_ROPE = 64
V_HEAD = 128
ROPE_THETA = 10000.0
ATTN_SCALE = (QK_NOPE + QK_ROPE) ** -0.5
Q_SCALE = ATTN_SCALE * 1.4426950408889634
TOP_K = 8
N_GROUPS = 8
TOPK_GROUPS = 4
ROUTED_SCALE = 2.5

SEQ_TILE = 256
CONV_HALO = 16
SHORT_HALO = 8
CONV_ROW_CHUNK = 64
LANE = 128
SUBLANES = 8
Q_TILE = 1024
ATTN_SUB_ROWS = 256
ATTN_ROW_CHUNK = 16
MOE_BLOCK_ROWS = 1024
MOE_SUB_ROWS = 1024
ROUTE_SUB_TOKENS = 128
MOD_COL_TILE = 1536
VMEM_LIMIT = 48 * 1024 * 1024
SC_CORES = 2
SC_SUBCORES = 16
SC_WORKERS = SC_CORES * SC_SUBCORES
SC_WINDOWS = (64, 32, 16)
N_CHAINS = 1


def _params(n_axes):
    return pltpu.CompilerParams(dimension_semantics=("arbitrary",) * n_axes,
                                vmem_limit_bytes=VMEM_LIMIT)


def _split_bf16(a):
    hi = a.astype(BF16)
    lo = (a - hi.astype(F32)).astype(BF16)
    return hi, lo


def _dot(a, b):
    return jnp.dot(a, b, preferred_element_type=F32)


def _dot_nt(a, b):
    return lax.dot_general(a, b, (((1,), (1,)), ((), ())), preferred_element_type=F32)


def _pack_halves(v):
    half = v.shape[-1] // 2
    lo = lax.bitcast_convert_type(v[:, :half].astype(BF16).astype(F32), U32) >> 16
    hi = lax.bitcast_convert_type(v[:, half:].astype(BF16).astype(F32), U32) & HIGH_HALF_MASK
    return hi | lo


def _unpack_halves(p):
    lo = lax.bitcast_convert_type(p << 16, F32)
    hi = lax.bitcast_convert_type(p & HIGH_HALF_MASK, F32)
    return lo, hi


def _dot_packed(p, w):
    lo, hi = _unpack_halves(p)
    return _dot(jnp.concatenate([lo.astype(BF16), hi.astype(BF16)], axis=1), w[...])


def _layer_norm(v, g, b):
    mu = jnp.mean(v, axis=-1, keepdims=True)
    c = v - mu
    var = jnp.mean(c * c, axis=-1, keepdims=True)
    return c * lax.rsqrt(var + LN_EPS) * g + b


def _silu(v):
    return v * jax.nn.sigmoid(v)


def _mod_kernel(c_ref, w_ref, b_ref, o_ref):
    a = _silu(c_ref[...])
    a_hi, a_lo = _split_bf16(a)
    w_hi, w_lo = _split_bf16(w_ref[0])
    o_ref[0] = _dot(a_hi, w_hi) + _dot(a_hi, w_lo) + _dot(a_lo, w_hi) + b_ref[0]


def _modulation(c_all, ada_w, ada_b):
    depth, d, n = ada_w.shape
    rows = c_all.shape[0]
    tn = MOD_COL_TILE
    return pl.pallas_call(
        _mod_kernel,
        out_shape=jax.ShapeDtypeStruct((depth, rows, n), F32),
        grid=(depth, n // tn),
        in_specs=[pl.BlockSpec((rows, d), lambda i, j: (0, 0)),
                  pl.BlockSpec((1, d, tn), lambda i, j: (i, 0, j)),
                  pl.BlockSpec((1, 1, tn), lambda i, j: (i, 0, j))],
        out_specs=pl.BlockSpec((1, rows, tn), lambda i, j: (i, 0, j)),
        compiler_params=_params(2),
        name="adaln_mod",
    )(c_all, ada_w, ada_b.reshape(depth, 1, n))


def _tok_spec(d, tl=SEQ_TILE):
    return pl.BlockSpec((1, tl, d), lambda b, l: (b, l, 0))


def _mod_spec(d, n_batch, ntx):
    return pl.BlockSpec((1, N_MOD, d), lambda b, l: (jnp.where(l < ntx, b, n_batch), 0, 0))


def _full_spec(shape):
    zeros = (0,) * len(shape)
    return pl.BlockSpec(shape, lambda b, l: zeros)


def _halo_specs(d, halo, seq_len, tl=SEQ_TILE):
    per_tile = tl // halo
    last = seq_len // halo - 1
    prev = pl.BlockSpec((1, halo, d), lambda b, l: (b, jnp.maximum(l * per_tile - 1, 0), 0))
    nxt = pl.BlockSpec((1, halo, d), lambda b, l: (b, jnp.minimum((l + 1) * per_tile, last), 0))
    return prev, nxt


def _segment_edges(l, ntx, nt):
    first = jnp.logical_or(l == 0, l == ntx)
    last = jnp.logical_or(l == ntx - 1, l == nt - 1)
    return first, last


def _window_rows(prev_ref, cur_ref, next_ref, win_ref):
    halo, tl = prev_ref.shape[1], cur_ref.shape[1]
    win_ref[0:halo, :] = prev_ref[0]
    win_ref[halo:halo + tl, :] = cur_ref[0]
    win_ref[halo + tl:, :] = next_ref[0]
    return win_ref[...]


def _edge_mask(n_rows, halo, first, last):
    r = lax.broadcasted_iota(jnp.int32, (n_rows, 1), 0)
    outside = jnp.logical_or(jnp.logical_and(first, r < halo), jnp.logical_and(last, r >= n_rows - halo))
    return jnp.where(outside, 0.0, 1.0)


def _conf_conv_kernel(ntx, nt, xp_ref, xc_ref, xn_ref, mod_ref, w1_ref, b1_ref, dw_ref, dwb_ref, ng_ref, nb_ref,
                      a_ref, win_ref, sh_ref, conv_ref):
    tl, d = xc_ref.shape[1], xc_ref.shape[2]
    taps = dw_ref.shape[0]
    lead = CONV_HALO - (taps - 1) // 2
    first, last = _segment_edges(pl.program_id(1), ntx, nt)
    xw = _window_rows(xp_ref, xc_ref, xn_ref, win_ref)
    h = xw * (1.0 + mod_ref[0, 1:2, :]) + mod_ref[0, 0:1, :]
    z = _dot(h.astype(BF16), w1_ref[...]) + b1_ref[...]
    u = z[:, :d] * jax.nn.sigmoid(z[:, d:])
    sh_ref[0] = u * _edge_mask(u.shape[0], CONV_HALO, first, last)
    span = tl + 2 * CONV_HALO - SUBLANES
    for s in range(1, SUBLANES):
        sh_ref[s, 0:span, :] = sh_ref[0, s:s + span, :]
    groups = CONV_ROW_CHUNK // SUBLANES

    def row_chunk(i, carry):
        r0 = pl.multiple_of(i * CONV_ROW_CHUNK, CONV_ROW_CHUNK)
        for c0 in range(0, d, LANE):
            accs = [jnp.zeros((SUBLANES, LANE), F32) for _ in range(groups)]
            for k in range(taps):
                res = (lead + k) % SUBLANES
                off = lead + k - res
                w = dw_ref[k, :, c0:c0 + LANE]
                for g in range(groups):
                    lo = r0 + (off + g * SUBLANES)
                    accs[g] = accs[g] + w * sh_ref[res, pl.ds(lo, SUBLANES), c0:c0 + LANE]
            for g in range(groups):
                conv_ref[pl.ds(r0 + g * SUBLANES, SUBLANES), c0:c0 + LANE] = accs[g]
        return carry

    lax.fori_loop(0, tl // CONV_ROW_CHUNK, row_chunk, 0)
    v = _layer_norm(conv_ref[...] + dwb_ref[...], ng_ref[...], nb_ref[...])
    a_ref[0] = _silu(v).astype(BF16)


def _conf_conv(x_all, mod, w1, b1, dw, dwb, ng, nb_, nt, ntx):
    nb, seq_len, d = x_all.shape
    prev, nxt = _halo_specs(d, CONV_HALO, seq_len)
    return pl.pallas_call(
        functools.partial(_conf_conv_kernel, ntx, nt),
        out_shape=jax.ShapeDtypeStruct((nb, nt * SEQ_TILE, d), BF16),
        grid=(nb, nt),
        in_specs=[prev, _tok_spec(d), nxt, _mod_spec(d, nb, ntx), _full_spec(w1.shape), _full_spec(b1.shape),
                  _full_spec(dw.shape), _full_spec(dwb.shape), _full_spec(ng.shape), _full_spec(nb_.shape)],
        out_specs=_tok_spec(d),
        scratch_shapes=[pltpu.VMEM((SEQ_TILE + 2 * CONV_HALO, d), F32),
                        pltpu.VMEM((SUBLANES, SEQ_TILE + 2 * CONV_HALO, d), F32),
                        pltpu.VMEM((SEQ_TILE, d), F32)],
        compiler_params=_params(2),
        name="conf_conv",
    )(x_all, x_all, x_all, mod, w1, b1, dw, dwb, ng, nb_)


def _sc_conv_kernel(ntx, nt, xp_ref, xc_ref, xn_ref, mod_ref, w_ref, dw_ref, a_ref, win_ref, ext_ref):
    tl, d = xc_ref.shape[1], xc_ref.shape[2]
    taps = dw_ref.shape[0]
    lead = SHORT_HALO - (taps - 1) // 2
    first, last = _segment_edges(pl.program_id(1), ntx, nt)
    xw = _window_rows(xp_ref, xc_ref, xn_ref, win_ref)
    h = xw * (1.0 + mod_ref[0, 1:2, :]) + mod_ref[0, 0:1, :]
    z = _dot(h.astype(BF16), w_ref[...])
    gb = z[SHORT_HALO:SHORT_HALO + tl, :d]
    ext_ref[...] = z[:, d:2 * d] * z[:, 2 * d:] * _edge_mask(z.shape[0], SHORT_HALO, first, last)
    acc = dw_ref[0:1, :] * ext_ref[lead:lead + tl, :]
    for k in range(1, taps):
        acc = acc + dw_ref[k:k + 1, :] * ext_ref[lead + k:lead + k + tl, :]
    a_ref[0] = (gb * acc).astype(BF16)


def _sc_conv(x_all, mod, w_in, dw, nt, ntx):
    nb, seq_len, d = x_all.shape
    prev, nxt = _halo_specs(d, SHORT_HALO, seq_len)
    win = pltpu.VMEM((SEQ_TILE + 2 * SHORT_HALO, d), F32)
    return pl.pallas_call(
        functools.partial(_sc_conv_kernel, ntx, nt),
        out_shape=jax.ShapeDtypeStruct((nb, nt * SEQ_TILE, d), BF16),
        grid=(nb, nt),
        in_specs=[prev, _tok_spec(d), nxt, _mod_spec(d, nb, ntx), _full_spec(w_in.shape), _full_spec(dw.shape)],
        out_specs=_tok_spec(d),
        scratch_shapes=[win, win],
        compiler_params=_params(2),
        name="sc_conv",
    )(x_all, x_all, x_all, mod, w_in, dw)


def _rms(v, g):
    return v * lax.rsqrt(jnp.mean(v * v, axis=-1, keepdims=True) + RMS_EPS) * g


def _mla_proj_kernel(ntx, q_rank,
                     x_ref, mod_ref, wd_ref, wkp_ref, qg_ref, kvg_ref, wqn_ref, wqr_ref, wuk_ref, wuv_ref,
                     cos_ref, sin_ref, q_ref, k_ref, v_ref):
    is_latent = pl.program_id(1) < ntx
    h = (x_ref[0] * (1.0 + mod_ref[0, 1:2, :]) + mod_ref[0, 0:1, :]).astype(BF16)
    dn = _dot(h, wd_ref[...])
    cq = _rms(dn[:, :q_rank], qg_ref[...]).astype(BF16)
    ckv = _rms(dn[:, q_rank:], kvg_ref[...]).astype(BF16)
    cos = jnp.where(is_latent, cos_ref[...], 1.0)
    sin = jnp.where(is_latent, sin_ref[...], 0.0)
    kp2 = _dot(h, wkp_ref[...])
    kp = kp2[:, :LANE] * cos + kp2[:, LANE:] * sin
    kn = _dot(ckv, wuk_ref[...])
    vv = _dot(ckv, wuv_ref[...])
    qn = _dot(cq, wqn_ref[...]) * Q_SCALE
    qr2 = _dot(cq, wqr_ref[...])
    hw = MLA_HEADS * LANE
    for hd in range(MLA_HEADS):
        sl = slice(hd * LANE, (hd + 1) * LANE)
        qr = (qr2[:, sl] * cos + qr2[:, hw + hd * LANE:hw + (hd + 1) * LANE] * sin) * Q_SCALE
        q_ref[0, hd, :, 0:LANE] = qn[:, sl].astype(BF16)
        q_ref[0, hd, :, LANE:] = qr.astype(BF16)
        k_ref[0, hd, :, 0:LANE] = kn[:, sl].astype(BF16)
        k_ref[0, hd, :, LANE:] = kp.astype(BF16)
        v_ref[0, hd] = vv[:, sl].astype(BF16)


def _mla_proj(x_all, mod, wd, wkp, qg, kvg, wqn, wqr, wuk, wuv, cos_t, sin_t, nt, ntx):
    nb, seq_len, d = x_all.shape
    q_rank = qg.shape[-1]
    tl = SEQ_TILE
    rope_spec = pl.BlockSpec((tl, LANE), lambda b, l: (jnp.minimum(l, ntx - 1), 0))
    qk_shape = jax.ShapeDtypeStruct((nb, MLA_HEADS, nt * tl, 2 * LANE), BF16)
    v_shape = jax.ShapeDtypeStruct((nb, MLA_HEADS, nt * tl, LANE), BF16)
    qk_spec = pl.BlockSpec((1, MLA_HEADS, tl, 2 * LANE), lambda b, l: (b, 0, l, 0))
    v_spec = pl.BlockSpec((1, MLA_HEADS, tl, LANE), lambda b, l: (b, 0, l, 0))
    return pl.pallas_call(
        functools.partial(_mla_proj_kernel, ntx, q_rank),
        out_shape=(qk_shape, qk_shape, v_shape),
        grid=(nb, nt),
        in_specs=[_tok_spec(d), _mod_spec(d, nb, ntx), _full_spec(wd.shape), _full_spec(wkp.shape),
                  _full_spec(qg.shape), _full_spec(kvg.shape), _full_spec(wqn.shape), _full_spec(wqr.shape),
                  _full_spec(wuk.shape), _full_spec(wuv.shape), rope_spec, rope_spec],
        out_specs=(qk_spec, qk_spec, v_spec),
        compiler_params=_params(2),
        name="mla_proj",
    )(x_all, mod, wd, wkp, qg, kvg, wqn, wqr, wuk, wuv, cos_t, sin_t)


def _attn_kernel(q_ref, k_ref, v_ref, o_ref, s_ref, p_ref, l_ref):
    tq = q_ref.shape[2]
    for r0 in range(0, tq, ATTN_SUB_ROWS):
        rows = slice(r0, r0 + ATTN_SUB_ROWS)
        s_ref[rows, :] = _dot_nt(q_ref[0, 0, rows, :], k_ref[0, 0])
        for c0 in range(r0, r0 + ATTN_SUB_ROWS, ATTN_ROW_CHUNK):
            chunk = slice(c0, c0 + ATTN_ROW_CHUNK)
            s = s_ref[chunk, :]
            p = jnp.exp2(s - jnp.max(s, axis=-1, keepdims=True))
            l_ref[chunk, :] = jnp.broadcast_to(jnp.sum(p, axis=-1, keepdims=True), (ATTN_ROW_CHUNK, V_HEAD))
            p_ref[chunk, :] = p.astype(BF16)
        o = _dot(p_ref[rows, :], v_ref[0, 0])
        o_ref[0, rows, :] = (o / l_ref[rows, :]).astype(BF16)


def _attention(q, k, v, n_q):
    nb, nh, n_k, dk = k.shape
    tq = Q_TILE
    return pl.pallas_call(
        _attn_kernel,
        out_shape=jax.ShapeDtypeStruct((nb, n_q, nh * V_HEAD), BF16),
        grid=(nb, nh, n_q // tq),
        in_specs=[pl.BlockSpec((1, 1, tq, dk), lambda b, h, i: (b, h, i, 0)),
                  pl.BlockSpec((1, 1, n_k, dk), lambda b, h, i: (b, h, 0, 0)),
                  pl.BlockSpec((1, 1, n_k, V_HEAD), lambda b, h, i: (b, h, 0, 0))],
        out_specs=pl.BlockSpec((1, tq, V_HEAD), lambda b, h, i: (b, i, h)),
        scratch_shapes=[pltpu.VMEM((tq, n_k), F32), pltpu.VMEM((tq, n_k), BF16), pltpu.VMEM((tq, V_HEAD), F32)],
        compiler_params=_params(3),
        name="mla_attn",
    )(q, k, v)


def _route(sel, s):
    n_e, n_t = sel.shape
    per = n_e // N_GROUPS
    sel3 = sel.reshape(N_GROUPS, per, n_t)
    s3 = s.reshape(N_GROUPS, per, n_t)
    iota_p = lax.broadcasted_iota(jnp.int32, (N_GROUPS, per, n_t), 1).astype(F32)
    iota_g = lax.broadcasted_iota(jnp.int32, (N_GROUPS, 1, n_t), 0).astype(F32)
    neg = -jnp.inf
    m1 = jnp.max(sel3, axis=1, keepdims=True)
    first = jnp.min(jnp.where(sel3 == m1, iota_p, float(per)), axis=1, keepdims=True)
    m2 = jnp.max(jnp.where(iota_p == first, neg, sel3), axis=1, keepdims=True)
    gs = m1 + m2
    gsel = jnp.zeros((N_GROUPS, 1, n_t), F32)
    for _ in range(TOPK_GROUPS):
        gm = jnp.max(gs, axis=0, keepdims=True)
        gfirst = jnp.min(jnp.where(gs == gm, iota_g, float(N_GROUPS)), axis=0, keepdims=True)
        pick = iota_g == gfirst
        gsel = jnp.where(pick, 1.0, gsel)
        gs = jnp.where(pick, neg, gs)
    val = jnp.where(gsel > 0.0, sel3, neg)
    iota_e = lax.broadcasted_iota(jnp.int32, (N_GROUPS, per, n_t), 0).astype(F32) * per + iota_p
    chosen = jnp.zeros((N_GROUPS, per, n_t), F32)
    idx, wts = [], []
    for _ in range(TOP_K):
        m = jnp.max(jnp.max(val, axis=1, keepdims=True), axis=0, keepdims=True)
        e = jnp.min(jnp.min(jnp.where(val == m, iota_e, float(n_e)), axis=1, keepdims=True), axis=0, keepdims=True)
        pick = iota_e == e
        wts.append(jnp.sum(jnp.sum(jnp.where(pick, s3, 0.0), axis=1, keepdims=True), axis=0, keepdims=True))
        idx.append(e)
        chosen = jnp.where(pick, 1.0, chosen)
        val = jnp.where(pick, neg, val)
    total = wts[0]
    for w in wts[1:]:
        total = total + w
    wts = [w / total * ROUTED_SCALE for w in wts]
    return idx, wts, chosen, iota_e


def _post_kernel(alpha, a_ref, w_ref, b_ref, x_ref, mod_ref, lng_ref, lnb_ref, rw_hi_ref, rw_lo_ref, rb_ref, tri_ref,
                 xn_ref, h2_ref, topi_ref, topw_ref, rank_ref, cnt_ref, run_ref):
    first_step = jnp.logical_and(pl.program_id(0) == 0, pl.program_id(1) == 0)

    @pl.when(first_step)
    def _():
        run_ref[...] = jnp.zeros_like(run_ref)

    y = _dot(a_ref[0], w_ref[...]) + b_ref[...]
    run = run_ref[...]
    for t0 in range(0, y.shape[0], ROUTE_SUB_TOKENS):
        rows = slice(t0, t0 + ROUTE_SUB_TOKENS)
        xn = _layer_norm(alpha * x_ref[0, rows, :] + mod_ref[0, 2:3, :] * y[rows], lng_ref[...], lnb_ref[...])
        xn_ref[0, rows, :] = xn
        h2 = xn * (1.0 + mod_ref[0, 4:5, :]) + mod_ref[0, 3:4, :]
        h_hi, h_lo = _split_bf16(h2)
        h2_ref[0, rows, :] = _pack_halves(h2)
        logits = _dot_nt(rw_hi_ref[...], h_hi) + _dot_nt(rw_hi_ref[...], h_lo) + _dot_nt(rw_lo_ref[...], h_hi)
        s = jax.nn.sigmoid(logits)
        idx, wts, chosen3, iota_e = _route(s + rb_ref[...], s)
        n_e, n_t = s.shape
        chosen = chosen3.reshape(n_e, n_t)
        rank = run[:, 0:1] + _dot(chosen.astype(BF16), tri_ref[...])
        rank3 = rank.reshape(chosen3.shape)
        for k in range(TOP_K):
            rk = jnp.sum(jnp.sum(jnp.where(iota_e == idx[k], rank3, 0.0), axis=1, keepdims=True), axis=0,
                         keepdims=True)
            topi_ref[k:k + 1, rows] = idx[k].reshape(1, n_t).astype(jnp.int32)
            topw_ref[k:k + 1, rows] = wts[k].reshape(1, n_t)
            rank_ref[k:k + 1, rows] = rk.reshape(1, n_t).astype(jnp.int32)
        run = run + jnp.sum(chosen, axis=1, keepdims=True)
    run_ref[...] = run
    cnt_ref[...] = run.astype(jnp.int32)


def _post(alpha, a, w, bias, x_all, mod, lng, lnb, rw_hi, rw_lo, rb, nt, ntx):
    nb, _, dk = a.shape
    d = x_all.shape[-1]
    n_e = rw_hi.shape[0]
    tl = SEQ_TILE
    n_tok = nb * nt * tl
    col_spec = lambda rows: pl.BlockSpec((rows, tl), lambda b, l: (0, b * nt + l))
    sub = ROUTE_SUB_TOKENS
    row_i = lax.broadcasted_iota(jnp.int32, (sub, sub), 0)
    col_i = lax.broadcasted_iota(jnp.int32, (sub, sub), 1)
    tri = jnp.where(row_i < col_i, 1.0, 0.0).astype(BF16)
    return pl.pallas_call(
        functools.partial(_post_kernel, alpha),
        out_shape=(jax.ShapeDtypeStruct((nb, nt * tl, d), F32),
                   jax.ShapeDtypeStruct((nb, nt * tl, d // 2), U32),
                   jax.ShapeDtypeStruct((TOP_K, n_tok), jnp.int32),
                   jax.ShapeDtypeStruct((TOP_K, n_tok), F32),
                   jax.ShapeDtypeStruct((TOP_K, n_tok), jnp.int32),
                   jax.ShapeDtypeStruct((n_e, LANE), jnp.int32)),
        grid=(nb, nt),
        in_specs=[_tok_spec(dk), _full_spec(w.shape), _full_spec(bias.shape), _tok_spec(d),
                  _mod_spec(d, nb, ntx), _full_spec(lng.shape), _full_spec(lnb.shape),
                  _full_spec(rw_hi.shape), _full_spec(rw_lo.shape), _full_spec(rb.shape), _full_spec(tri.shape)],
        out_specs=(_tok_spec(d), _tok_spec(d // 2), col_spec(TOP_K), col_spec(TOP_K), col_spec(TOP_K),
                   _full_spec((n_e, LANE))),
        scratch_shapes=[pltpu.VMEM((n_e, LANE), F32)],
        compiler_params=_params(2),
        name="mixer_post",
    )(a, w, bias, x_all, mod, lng, lnb, rw_hi, rw_lo, rb, tri)


def _gmm_kernel(be_ref, nu_ref, xs_ref, w1_ref, w3_ref, w2_ref, ys_ref, w13_s, w2_s):
    i = pl.program_id(0)
    de = w2_ref.shape[2]
    changed = jnp.logical_or(i == 0, be_ref[i] != be_ref[jnp.maximum(i - 1, 0)])

    @pl.when(changed)
    def _():
        w13_s[:, :de] = w1_ref[0, 0].astype(BF16)
        w13_s[:, de:] = w3_ref[0, 0].astype(BF16)
        w2_s[...] = w2_ref[0, 0].astype(BF16)

    @pl.when(i < nu_ref[0])
    def _():
        for r0 in range(0, xs_ref.shape[0], MOE_SUB_ROWS):
            rows = slice(r0, r0 + MOE_SUB_ROWS)
            z = _dot_packed(xs_ref[rows, :], w13_s)
            hmid = (_silu(z[:, :de]) * z[:, de:]).astype(BF16)
            ys_ref[rows, :] = _pack_halves(_dot(hmid, w2_s[...]))


def _grouped_ffn(block_expert, n_used, xs, w1, w3, w2, layer):
    n_rows, dp = xs.shape
    d = 2 * dp
    bm = MOE_BLOCK_ROWS
    de = w2.shape[2]
    grid_spec = pltpu.PrefetchScalarGridSpec(
        num_scalar_prefetch=2,
        grid=(n_rows // bm,),
        in_specs=[pl.BlockSpec((bm, dp), lambda i, be, nu: (i, 0)),
                  pl.BlockSpec((1, 1, d, de), lambda i, be, nu: (layer, be[i], 0, 0)),
                  pl.BlockSpec((1, 1, d, de), lambda i, be, nu: (layer, be[i], 0, 0)),
                  pl.BlockSpec((1, 1, de, d), lambda i, be, nu: (layer, be[i], 0, 0))],
        out_specs=pl.BlockSpec((bm, dp), lambda i, be, nu: (i, 0)),
        scratch_shapes=[pltpu.VMEM((d, 2 * de), BF16), pltpu.VMEM((de, d), BF16)],
    )
    return pl.pallas_call(
        _gmm_kernel,
        out_shape=jax.ShapeDtypeStruct((n_rows, dp), U32),
        grid_spec=grid_spec,
        compiler_params=_params(1),
        name="moe_grouped_ffn",
    )(block_expert, n_used, xs, w1, w3, w2)


def _sc_mesh():
    return plsc.VectorSubcoreMesh(core_axis_name="c", subcore_axis_name="s",
                                  num_cores=SC_CORES, num_subcores=SC_SUBCORES)


def _sc_worker():
    return lax.axis_index("s") * SC_CORES + lax.axis_index("c")


def _sc_window(rows_per_worker):
    for win in SC_WINDOWS:
        if rows_per_worker % (2 * win) == 0:
            return win
    raise ValueError(f"no SparseCore window divides {rows_per_worker} rows per worker")


def _sc_row_scatter(src, dest, n_out):
    n_k, n_tok = dest.shape
    dp = src.shape[1]
    assert n_tok % SC_WORKERS == 0
    per_w = n_tok // SC_WORKERS
    win = _sc_window(per_w)
    n_win = per_w // win
    idx = dest.reshape(n_k, SC_WORKERS, n_win, win).transpose(1, 2, 0, 3)

    def body(src_hbm, idx_hbm, out_hbm, idx_v, rows_v, load_sem, scat_sem):
        wid = _sc_worker()
        base = wid * per_w
        pltpu.sync_copy(idx_hbm.at[wid], idx_v)

        def load(g, slot):
            return pltpu.make_async_copy(src_hbm.at[pl.ds(base + g * win, win)], rows_v.at[slot], load_sem.at[slot])

        def scatter(g, slot, k):
            return pltpu.make_async_copy(rows_v.at[slot], out_hbm.at[idx_v.at[g, k]], scat_sem.at[slot])

        load(0, 0).start()

        @pl.loop(0, n_win, step=2)
        def _(g):
            for slot in range(2):
                cur = g + slot
                load(cur, slot).wait()

                @pl.when(cur + 1 < n_win)
                def _():
                    @pl.when(cur >= 1)
                    def _():
                        for k in range(n_k):
                            scatter(cur - 1, 1 - slot, k).wait()
                    load(cur + 1, 1 - slot).start()

                for k in range(n_k):
                    scatter(cur, slot, k).start()

        for slot in range(2):
            for k in range(n_k):
                scatter(n_win - 2 + slot, slot, k).wait()

    return pl.kernel(
        body, mesh=_sc_mesh(),
        out_type=jax.ShapeDtypeStruct((n_out, dp), src.dtype),
        scratch_types=[pltpu.VMEM((n_win, n_k, win), jnp.int32),
                       pltpu.VMEM((2, win, dp), src.dtype),
                       pltpu.SemaphoreType.DMA((2,)),
                       pltpu.SemaphoreType.DMA((2,))],
        compiler_params=pltpu.CompilerParams(use_tc_tiling_on_sc=True),
        name="sc_dispatch_scatter",
    )(src, idx)


def _sc_row_gather(table, idx):
    n = idx.shape[0]
    dp = table.shape[1]
    assert n % SC_WORKERS == 0
    per_w = n // SC_WORKERS
    win = _sc_window(per_w)
    n_win = per_w // win

    def body(table_hbm, idx_hbm, out_hbm, idx_v, rows_v, gather_sem, put_sem):
        wid = _sc_worker()
        base = wid * per_w
        pltpu.sync_copy(idx_hbm.at[pl.ds(base, per_w)], idx_v)

        def gather(g, slot):
            return pltpu.make_async_copy(table_hbm.at[idx_v.at[pl.ds(g * win, win)]], rows_v.at[slot],
                                         gather_sem.at[slot])

        def put(g, slot):
            return pltpu.make_async_copy(rows_v.at[slot], out_hbm.at[pl.ds(base + g * win, win)], put_sem.at[slot])

        gather(0, 0).start()

        @pl.loop(0, n_win, step=2)
        def _(g):
            for slot in range(2):
                cur = g + slot
                gather(cur, slot).wait()

                @pl.when(cur + 1 < n_win)
                def _():
                    @pl.when(cur >= 1)
                    def _():
                        put(cur - 1, 1 - slot).wait()
                    gather(cur + 1, 1 - slot).start()

                put(cur, slot).start()

        for slot in range(2):
            put(n_win - 2 + slot, slot).wait()

    return pl.kernel(
        body, mesh=_sc_mesh(),
        out_type=jax.ShapeDtypeStruct((n, dp), table.dtype),
        scratch_types=[pltpu.VMEM((per_w,), jnp.int32),
                       pltpu.VMEM((2, win, dp), table.dtype),
                       pltpu.SemaphoreType.DMA((2,)),
                       pltpu.SemaphoreType.DMA((2,))],
        compiler_params=pltpu.CompilerParams(use_tc_tiling_on_sc=True),
        name="sc_combine_gather",
    )(table, idx)


def _moe_out_kernel(alpha, x_ref, h2_ref, yg_ref, tw_ref, mod_ref, ws13_ref, ws2_ref, lng_ref, lnb_ref, o_ref):
    de = ws2_ref.shape[0]
    z = _dot_packed(h2_ref[0], ws13_ref)
    hmid = (_silu(z[:, :de]) * z[:, de:]).astype(BF16)
    y = _dot(hmid, ws2_ref[...])
    lo, hi = _unpack_halves(yg_ref[0])
    acc_lo, acc_hi = lo * tw_ref[:, 0:1], hi * tw_ref[:, 0:1]
    for k in range(1, yg_ref.shape[0]):
        lo, hi = _unpack_halves(yg_ref[k])
        acc_lo = acc_lo + lo * tw_ref[:, k:k + 1]
        acc_hi = acc_hi + hi * tw_ref[:, k:k + 1]
    y = y + jnp.concatenate([acc_lo, acc_hi], axis=1)
    o_ref[0] = _layer_norm(alpha * x_ref[0] + mod_ref[0, 5:6, :] * y, lng_ref[...], lnb_ref[...])


def _moe_out(alpha, xn, h2, y_rows, topw_t, mod, ws13, ws2, lng, lnb, nt, ntx):
    nb, _, d = xn.shape
    n_k = y_rows.shape[0]
    tl = SEQ_TILE
    return pl.pallas_call(
        functools.partial(_moe_out_kernel, alpha),
        out_shape=jax.ShapeDtypeStruct((nb, nt * tl, d), F32),
        grid=(nb, nt),
        in_specs=[_tok_spec(d), _tok_spec(d // 2),
                  pl.BlockSpec((n_k, tl, d // 2), lambda b, l: (0, b * nt + l, 0)),
                  pl.BlockSpec((tl, n_k), lambda b, l: (b * nt + l, 0)),
                  _mod_spec(d, nb, ntx),
                  _full_spec(ws13.shape), _full_spec(ws2.shape), _full_spec(lng.shape), _full_spec(lnb.shape)],
        out_specs=_tok_spec(d),
        compiler_params=_params(2),
        name="moe_out",
    )(xn, h2, y_rows, topw_t, mod, ws13, ws2, lng, lnb)


def _dispatch_plan(counts, topi, rank, n_tok):
    n_e = counts.shape[0]
    bm = MOE_BLOCK_ROWS
    padded = (counts + bm - 1) // bm * bm
    pad_end = jnp.cumsum(padded)
    pad_start = pad_end - padded
    onehot = topi[:, None, :] == jnp.arange(n_e, dtype=jnp.int32)[None, :, None]
    dest = rank + jnp.sum(jnp.where(onehot, pad_start[None, :, None], 0), axis=1)
    n_blocks = n_tok * TOP_K // bm + n_e
    block_start = jnp.arange(n_blocks, dtype=jnp.int32) * bm
    block_expert = jnp.sum((pad_end[None, :] <= block_start[:, None]).astype(jnp.int32), axis=1)
    block_expert = jnp.minimum(block_expert, n_e - 1)
    n_used = (pad_end[-1] // bm).astype(jnp.int32).reshape(1)
    return dest.astype(jnp.int32), block_expert, n_used, n_blocks * bm


def _rope_tables(seq):
    n_freq = QK_ROPE // 4
    inv_freq = ROPE_THETA ** (-jnp.arange(n_freq, dtype=F32) / n_freq)
    pos = jnp.arange(seq, dtype=jnp.int32)
    r = (pos // GRID_W).astype(F32)
    col = (pos % GRID_W).astype(F32)
    ang = jnp.concatenate([r[:, None] * inv_freq, col[:, None] * inv_freq], -1)
    cos, sin = jnp.cos(ang), jnp.sin(ang)
    zeros = jnp.zeros((seq, LANE - QK_ROPE), F32)
    cos_slot = jnp.concatenate([cos, cos, zeros], -1)
    sin_slot = jnp.concatenate([-sin, sin, zeros], -1)
    return cos_slot, sin_slot


def _rope_slot_weights(w_rope):
    k, n, _ = w_rope.shape
    half = QK_ROPE // 2
    swapped = jnp.concatenate([w_rope[..., half:], w_rope[..., :half]], -1)
    pad = jnp.zeros((k, n, LANE - QK_ROPE), w_rope.dtype)
    plain = jnp.concatenate([w_rope, pad], -1).reshape(k, n * LANE)
    swp = jnp.concatenate([swapped, pad], -1).reshape(k, n * LANE)
    return jnp.concatenate([plain, swp], -1)


def kernel(x, c, ctx, c_ctx, ada_w, ada_b, ln_g, ln_b, conf_w1, conf_b1, conf_dw, conf_dwb, conf_ng, conf_nb, conf_w2, conf_b2, sc_w_in, sc_dw, sc_w_out, mla_w_dqkv, mla_q_g, mla_kv_g, mla_w_uq, mla_w_uk, mla_w_uv, mla_w_o, moe_router, moe_bias, moe_w1, moe_w3, moe_w2, sh_w1, sh_w3, sh_w2):
    nb, seq, d = x.shape
    l_ctx = ctx.shape[1]
    depth = ada_w.shape[0]
    alpha = (2.0 * depth) ** 0.25
    tl = SEQ_TILE
    assert seq % tl == 0 and l_ctx % tl == 0 and seq % Q_TILE == 0 and seq % GRID_W == 0
    ntx = seq // tl
    nt_all = (seq + l_ctx) // tl
    attn_layers = [i for i in range(depth) if i % N_MIXERS == 2]
    last_ctx_reader = attn_layers[-1] if attn_layers else -1

    rows = -(-(nb + 1) // 8) * 8
    c_all = jnp.zeros((rows, d), F32).at[:nb].set(c).at[nb].set(c_ctx)
    mod_all = _modulation(c_all, ada_w, ada_b).reshape(depth, rows, N_MOD, d)

    assert nb % N_CHAINS == 0
    nbc = nb // N_CHAINS
    chains = [jnp.concatenate([x[c0:c0 + nbc], ctx[c0:c0 + nbc]], axis=1) for c0 in range(0, nb, nbc)]
    q_rank, kv_rank = mla_q_g.shape[1], mla_kv_g.shape[1]
    cos_t, sin_t = _rope_tables(seq)
    row = lambda v: v.reshape(1, -1)

    for i in range(depth):
        need_ctx = i < last_ctx_reader
        kind, j = i % N_MIXERS, i // N_MIXERS
        nt = nt_all if need_ctx else ntx
        n_tok = nbc * nt * tl

        if kind == 0:
            w_first = conf_w1[j].astype(BF16)
            dw_tiles = jnp.broadcast_to(conf_dw[j][:, None, :], (conf_dw.shape[1], SUBLANES, d))
            w_last, b_last = conf_w2[j].astype(BF16), row(conf_b2[j])
        elif kind == 1:
            w_first = sc_w_in[j].astype(BF16)
            w_last, b_last = sc_w_out[j].astype(BF16), jnp.zeros((1, d), F32)
        else:
            wdq = mla_w_dqkv[j]
            wd = wdq[:, :q_rank + kv_rank].astype(BF16)
            wkp = _rope_slot_weights(wdq[:, None, q_rank + kv_rank:]).astype(BF16)
            wuq = mla_w_uq[j].reshape(q_rank, MLA_HEADS, QK_NOPE + QK_ROPE)
            wqn = wuq[:, :, :QK_NOPE].reshape(q_rank, MLA_HEADS * QK_NOPE).astype(BF16)
            wqr = _rope_slot_weights(wuq[:, :, QK_NOPE:]).astype(BF16)
            wuk, wuv = mla_w_uk[j].astype(BF16), mla_w_uv[j].astype(BF16)
            w_last, b_last = mla_w_o[j].astype(BF16), jnp.zeros((1, d), F32)
        rw_hi, rw_lo = _split_bf16(moe_router[i].T)
        ws13 = jnp.concatenate([sh_w1[i], sh_w3[i]], axis=-1).astype(BF16)
        ws2 = sh_w2[i].astype(BF16)

        for ci in range(N_CHAINS):
            x_all = chains[ci]
            mod = jnp.concatenate([mod_all[i, ci * nbc:(ci + 1) * nbc], mod_all[i, nb:nb + 1]], axis=0)

            if kind == 0:
                a = _conf_conv(x_all, mod, w_first, row(conf_b1[j]), dw_tiles, row(conf_dwb[j]),
                               row(conf_ng[j]), row(conf_nb[j]), nt, ntx)
            elif kind == 1:
                a = _sc_conv(x_all, mod, w_first, sc_dw[j], nt, ntx)
            else:
                q, k, v = _mla_proj(x_all, mod, wd, wkp, row(mla_q_g[j]), row(mla_kv_g[j]), wqn, wqr,
                                    wuk, wuv, cos_t, sin_t, nt_all, ntx)
                a = _attention(q, k, v, nt * tl)

            xn, h2, topi, topw, rank, counts = _post(alpha, a, w_last, b_last, x_all, mod, row(ln_g[i, 0]),
                                                     row(ln_b[i, 0]), rw_hi, rw_lo, moe_bias[i].reshape(-1, 1),
                                                     nt, ntx)

            dest, block_expert, n_used, n_rows = _dispatch_plan(counts[:, 0], topi, rank, n_tok)
            xs = _sc_row_scatter(h2.reshape(n_tok, d // 2), dest, n_rows)
            ys = _grouped_ffn(block_expert, n_used, xs, moe_w1, moe_w3, moe_w2, i)
            y_rows = _sc_row_gather(ys, dest.reshape(-1)).reshape(TOP_K, n_tok, d // 2)

            chains[ci] = _moe_out(alpha, xn, h2, y_rows, topw.T, mod, ws13, ws2,
                                  row(ln_g[i, 1]), row(ln_b[i, 1]), nt, ntx)
    return jnp.concatenate([xc[:, :seq] for xc in chains], axis=0)
```

```python
import functools

import numpy as np
import jax
import jax.numpy as jnp
from jax import lax
from jax.experimental import pallas as pl
from jax.experimental.pallas import tpu as pltpu
from jax.experimental.pallas import tpu_sc as plsc

F32 = jnp.float32
BF16 = jnp.bfloat16
U32 = jnp.uint32
HIGH_HALF_MASK = np.uint32(0xFFFF0000)

GRID_W = 64
N_MIXERS = 3
LN_EPS = 1e-5
RMS_EPS = 1e-6
N_MOD = 6
MLA_HEADS = 8
QK_NOPE = 128
QK_ROPE = 64
V_HEAD = 128
ROPE_THETA = 10000.0
ATTN_SCALE = (QK_NOPE + QK_ROPE) ** -0.5
Q_SCALE = ATTN_SCALE * 1.4426950408889634
TOP_K = 8
N_GROUPS = 8
TOPK_GROUPS = 4
ROUTED_SCALE = 2.5

SEQ_TILE = 256
CONV_HALO = 16
SHORT_HALO = 8
CONV_ROW_CHUNK = 64
LANE = 128
SUBLANES = 8
Q_TILE = 1024
ATTN_SUB_ROWS = 256
ATTN_ROW_CHUNK = 16
MOE_BLOCK_ROWS = 1024
MOE_SUB_ROWS = 1024
ROUTE_SUB_TOKENS = 128
MOD_COL_TILE = 1536
VMEM_LIMIT = 48 * 1024 * 1024
SC_CORES = 2
SC_SUBCORES = 16
SC_WORKERS = SC_CORES * SC_SUBCORES
SC_WINDOWS = (64, 32, 16)
N_CHAINS = 1


def _params(n_axes):
    return pltpu.CompilerParams(dimension_semantics=("arbitrary",) * n_axes,
                                vmem_limit_bytes=VMEM_LIMIT)


def _split_bf16(a):
    hi = a.astype(BF16)
    lo = (a - hi.astype(F32)).astype(BF16)
    return hi, lo


def _dot(a, b):
    return jnp.dot(a, b, preferred_element_type=F32)


def _dot_nt(a, b):
    return lax.dot_general(a, b, (((1,), (1,)), ((), ())), preferred_element_type=F32)


def _pack_halves(v):
    half = v.shape[-1] // 2
    lo = lax.bitcast_convert_type(v[:, :half].astype(BF16).astype(F32), U32) >> 16
    hi = lax.bitcast_convert_type(v[:, half:].astype(BF16).astype(F32), U32) & HIGH_HALF_MASK
    return hi | lo


def _unpack_halves(p):
    lo = lax.bitcast_convert_type(p << 16, F32)
    hi = lax.bitcast_convert_type(p & HIGH_HALF_MASK, F32)
    return lo, hi


def _dot_packed(p, w):
    lo, hi = _unpack_halves(p)
    return _dot(jnp.concatenate([lo.astype(BF16), hi.astype(BF16)], axis=1), w[...])


def _layer_norm(v, g, b):
    mu = jnp.mean(v, axis=-1, keepdims=True)
    c = v - mu
    var = jnp.mean(c * c, axis=-1, keepdims=True)
    return c * lax.rsqrt(var + LN_EPS) * g + b


def _silu(v):
    return v * jax.nn.sigmoid(v)


def _mod_kernel(c_ref, w_ref, b_ref, o_ref):
    a = _silu(c_ref[...])
    a_hi, a_lo = _split_bf16(a)
    w_hi, w_lo = _split_bf16(w_ref[0])
    o_ref[0] = _dot(a_hi, w_hi) + _dot(a_hi, w_lo) + _dot(a_lo, w_hi) + b_ref[0]


def _modulation(c_all, ada_w, ada_b):
    depth, d, n = ada_w.shape
    rows = c_all.shape[0]
    tn = MOD_COL_TILE
    return pl.pallas_call(
        _mod_kernel,
        out_shape=jax.ShapeDtypeStruct((depth, rows, n), F32),
        grid=(depth, n // tn),
        in_specs=[pl.BlockSpec((rows, d), lambda i, j: (0, 0)),
                  pl.BlockSpec((1, d, tn), lambda i, j: (i, 0, j)),
                  pl.BlockSpec((1, 1, tn), lambda i, j: (i, 0, j))],
        out_specs=pl.BlockSpec((1, rows, tn), lambda i, j: (i, 0, j)),
        compiler_params=_params(2),
        name="adaln_mod",
    )(c_all, ada_w, ada_b.reshape(depth, 1, n))


def _tok_spec(d, tl=SEQ_TILE):
    return pl.BlockSpec((1, tl, d), lambda b, l: (b, l, 0))


def _mod_spec(d, n_batch, ntx):
    return pl.BlockSpec((1, N_MOD, d), lambda b, l: (jnp.where(l < ntx, b, n_batch), 0, 0))


def _full_spec(shape):
    zeros = (0,) * len(shape)
    return pl.BlockSpec(shape, lambda b, l: zeros)


def _halo_specs(d, halo, seq_len, tl=SEQ_TILE):
    per_tile = tl // halo
    last = seq_len // halo - 1
    prev = pl.BlockSpec((1, halo, d), lambda b, l: (b, jnp.maximum(l * per_tile - 1, 0), 0))
    nxt = pl.BlockSpec((1, halo, d), lambda b, l: (b, jnp.minimum((l + 1) * per_tile, last), 0))
    return prev, nxt


def _segment_edges(l, ntx, nt):
    first = jnp.logical_or(l == 0, l == ntx)
    last = jnp.logical_or(l == ntx - 1, l == nt - 1)
    return first, last


def _window_rows(prev_ref, cur_ref, next_ref, win_ref):
    halo, tl = prev_ref.shape[1], cur_ref.shape[1]
    win_ref[0:halo, :] = prev_ref[0]
    win_ref[halo:halo + tl, :] = cur_ref[0]
    win_ref[halo + tl:, :] = next_ref[0]
    return win_ref[...]


def _edge_mask(n_rows, halo, first, last):
    r = lax.broadcasted_iota(jnp.int32, (n_rows, 1), 0)
    outside = jnp.logical_or(jnp.logical_and(first, r < halo), jnp.logical_and(last, r >= n_rows - halo))
    return jnp.where(outside, 0.0, 1.0)


def _conf_in_kernel(x_ref, mod_ref, w1_ref, b1_ref, u_ref):
    d = x_ref.shape[-1]
    h = x_ref[0] * (1.0 + mod_ref[0, 1:2, :]) + mod_ref[0, 0:1, :]
    z = _dot(h.astype(BF16), w1_ref[...]) + b1_ref[...]
    u_ref[0] = z[:, :d] * jax.nn.sigmoid(z[:, d:])


def _conf_in(x_all, mod, w1, b1, nt, ntx):
    nb, seq_len, d = x_all.shape
    return pl.pallas_call(
        _conf_in_kernel,
        out_shape=jax.ShapeDtypeStruct((nb, nt * SEQ_TILE, d), F32),
        grid=(nb, nt),
        in_specs=[_tok_spec(d), _mod_spec(d, nb, ntx), _full_spec(w1.shape), _full_spec(b1.shape)],
        out_specs=_tok_spec(d),
        compiler_params=_params(2),
        name="conf_in",
    )(x_all, mod, w1, b1)


def _conf_conv_kernel(ntx, nt, up_ref, uc_ref, un_ref, dw_ref, dwb_ref, ng_ref, nb_ref,
                      a_ref, sh_ref, conv_ref):
    tl, d = uc_ref.shape[1], uc_ref.shape[2]
    taps = dw_ref.shape[0]
    lead = CONV_HALO - (taps - 1) // 2
    first, last = _segment_edges(pl.program_id(1), ntx, nt)
    sh_ref[0, 0:CONV_HALO, :] = jnp.where(first, 0.0, up_ref[0])
    sh_ref[0, CONV_HALO:CONV_HALO + tl, :] = uc_ref[0]
    sh_ref[0, CONV_HALO + tl:, :] = jnp.where(last, 0.0, un_ref[0])
    span = tl + 2 * CONV_HALO - SUBLANES
    for s in range(1, SUBLANES):
        sh_ref[s, 0:span, :] = sh_ref[0, s:s + span, :]
    groups = CONV_ROW_CHUNK // SUBLANES

    def row_chunk(i, carry):
        r0 = pl.multiple_of(i * CONV_ROW_CHUNK, CONV_ROW_CHUNK)
        for c0 in range(0, d, LANE):
            accs = [jnp.zeros((SUBLANES, LANE), F32) for _ in range(groups)]
            for k in range(taps):
                res = (lead + k) % SUBLANES
                off = lead + k - res
                w = dw_ref[k, :, c0:c0 + LANE]
                for g in range(groups):
                    lo = r0 + (off + g * SUBLANES)
                    accs[g] = accs[g] + w * sh_ref[res, pl.ds(lo, SUBLANES), c0:c0 + LANE]
            for g in range(groups):
                conv_ref[pl.ds(r0 + g * SUBLANES, SUBLANES), c0:c0 + LANE] = accs[g]
        return carry

    lax.fori_loop(0, tl // CONV_ROW_CHUNK, row_chunk, 0)
    v = _layer_norm(conv_ref[...] + dwb_ref[...], ng_ref[...], nb_ref[...])
    a_ref[0] = _silu(v).astype(BF16)


def _conf_conv(u, dw, dwb, ng, nb_, nt, ntx):
    nb, seq_len, d = u.shape
    prev, nxt = _halo_specs(d, CONV_HALO, seq_len)
    return pl.pallas_call(
        functools.partial(_conf_conv_kernel, ntx, nt),
        out_shape=jax.ShapeDtypeStruct((nb, seq_len, d), BF16),
        grid=(nb, nt),
        in_specs=[prev, _tok_spec(d), nxt, _full_spec(dw.shape), _full_spec(dwb.shape),
                  _full_spec(ng.shape), _full_spec(nb_.shape)],
        out_specs=_tok_spec(d),
        scratch_shapes=[pltpu.VMEM((SUBLANES, SEQ_TILE + 2 * CONV_HALO, d), F32),
                        pltpu.VMEM((SEQ_TILE, d), F32)],
        compiler_params=_params(2),
        name="conf_conv",
    )(u, u, u, dw, dwb, ng, nb_)


def _sc_conv_kernel(ntx, nt, xp_ref, xc_ref, xn_ref, mod_ref, w_ref, dw_ref, a_ref, win_ref, ext_ref):
    tl, d = xc_ref.shape[1], xc_ref.shape[2]
    taps = dw_ref.shape[0]
    lead = SHORT_HALO - (taps - 1) // 2
    first, last = _segment_edges(pl.program_id(1), ntx, nt)
    xw = _window_rows(xp_ref, xc_ref, xn_ref, win_ref)
    h = xw * (1.0 + mod_ref[0, 1:2, :]) + mod_ref[0, 0:1, :]
    z = _dot(h.astype(BF16), w_ref[...])
    gb = z[SHORT_HALO:SHORT_HALO + tl, :d]
    ext_ref[...] = z[:, d:2 * d] * z[:, 2 * d:] * _edge_mask(z.shape[0], SHORT_HALO, first, last)
    acc = dw_ref[0:1, :] * ext_ref[lead:lead + tl, :]
    for k in range(1, taps):
        acc = acc + dw_ref[k:k + 1, :] * ext_ref[lead + k:lead + k + tl, :]
    a_ref[0] = (gb * acc).astype(BF16)


def _sc_conv(x_all, mod, w_in, dw, nt, ntx):
    nb, seq_len, d = x_all.shape
    prev, nxt = _halo_specs(d, SHORT_HALO, seq_len)
    win = pltpu.VMEM((SEQ_TILE + 2 * SHORT_HALO, d), F32)
    return pl.pallas_call(
        functools.partial(_sc_conv_kernel, ntx, nt),
        out_shape=jax.ShapeDtypeStruct((nb, nt * SEQ_TILE, d), BF16),
        grid=(nb, nt),
        in_specs=[prev, _tok_spec(d), nxt, _mod_spec(d, nb, ntx), _full_spec(w_in.shape), _full_spec(dw.shape)],
        out_specs=_tok_spec(d),
        scratch_shapes=[win, win],
        compiler_params=_params(2),
        name="sc_conv",
    )(x_all, x_all, x_all, mod, w_in, dw)


def _rms(v, g):
    return v * lax.rsqrt(jnp.mean(v * v, axis=-1, keepdims=True) + RMS_EPS) * g


def _mla_proj_kernel(ntx, q_rank,
                     x_ref, mod_ref, wd_ref, wkp_ref, qg_ref, kvg_ref, wqn_ref, wqr_ref, wuk_ref, wuv_ref,
                     cos_ref, sin_ref, q_ref, k_ref, v_ref):
    is_latent = pl.program_id(1) < ntx
    h = (x_ref[0] * (1.0 + mod_ref[0, 1:2, :]) + mod_ref[0, 0:1, :]).astype(BF16)
    dn = _dot(h, wd_ref[...])
    cq = _rms(dn[:, :q_rank], qg_ref[...]).astype(BF16)
    ckv = _rms(dn[:, q_rank:], kvg_ref[...]).astype(BF16)
    cos = jnp.where(is_latent, cos_ref[...], 1.0)
    sin = jnp.where(is_latent, sin_ref[...], 0.0)
    kp2 = _dot(h, wkp_ref[...])
    kp = kp2[:, :LANE] * cos + kp2[:, LANE:] * sin
    kn = _dot(ckv, wuk_ref[...])
    vv = _dot(ckv, wuv_ref[...])
    qn = _dot(cq, wqn_ref[...]) * Q_SCALE
    qr2 = _dot(cq, wqr_ref[...])
    hw = MLA_HEADS * LANE
    for hd in range(MLA_HEADS):
        sl = slice(hd * LANE, (hd + 1) * LANE)
        qr = (qr2[:, sl] * cos + qr2[:, hw + hd * LANE:hw + (hd + 1) * LANE] * sin) * Q_SCALE
        q_ref[0, hd, :, 0:LANE] = qn[:, sl].astype(BF16)
        q_ref[0, hd, :, LANE:] = qr.astype(BF16)
        k_ref[0, hd, :, 0:LANE] = kn[:, sl].astype(BF16)
        k_ref[0, hd, :, LANE:] = kp.astype(BF16)
        v_ref[0, hd] = vv[:, sl].astype(BF16)


def _mla_proj(x_all, mod, wd, wkp, qg, kvg, wqn, wqr, wuk, wuv, cos_t, sin_t, nt, ntx):
    nb, seq_len, d = x_all.shape
    q_rank = qg.shape[-1]
    tl = SEQ_TILE
    rope_spec = pl.BlockSpec((tl, LANE), lambda b, l: (jnp.minimum(l, ntx - 1), 0))
    qk_shape = jax.ShapeDtypeStruct((nb, MLA_HEADS, nt * tl, 2 * LANE), BF16)
    v_shape = jax.ShapeDtypeStruct((nb, MLA_HEADS, nt * tl, LANE), BF16)
    qk_spec = pl.BlockSpec((1, MLA_HEADS, tl, 2 * LANE), lambda b, l: (b, 0, l, 0))
    v_spec = pl.BlockSpec((1, MLA_HEADS, tl, LANE), lambda b, l: (b, 0, l, 0))
    return pl.pallas_call(
        functools.partial(_mla_proj_kernel, ntx, q_rank),
        out_shape=(qk_shape, qk_shape, v_shape),
        grid=(nb, nt),
        in_specs=[_tok_spec(d), _mod_spec(d, nb, ntx), _full_spec(wd.shape), _full_spec(wkp.shape),
                  _full_spec(qg.shape), _full_spec(kvg.shape), _full_spec(wqn.shape), _full_spec(wqr.shape),
                  _full_spec(wuk.shape), _full_spec(wuv.shape), rope_spec, rope_spec],
        out_specs=(qk_spec, qk_spec, v_spec),
        compiler_params=_params(2),
        name="mla_proj",
    )(x_all, mod, wd, wkp, qg, kvg, wqn, wqr, wuk, wuv, cos_t, sin_t)


def _attn_kernel(q_ref, k_ref, v_ref, o_ref, s_ref, p_ref, l_ref):
    tq = q_ref.shape[2]
    for r0 in range(0, tq, ATTN_SUB_ROWS):
        rows = slice(r0, r0 + ATTN_SUB_ROWS)
        s_ref[rows, :] = _dot_nt(q_ref[0, 0, rows, :], k_ref[0, 0])
        for c0 in range(r0, r0 + ATTN_SUB_ROWS, ATTN_ROW_CHUNK):
            chunk = slice(c0, c0 + ATTN_ROW_CHUNK)
            s = s_ref[chunk, :]
            p = jnp.exp2(s - jnp.max(s, axis=-1, keepdims=True))
            l_ref[chunk, :] = jnp.broadcast_to(jnp.sum(p, axis=-1, keepdims=True), (ATTN_ROW_CHUNK, V_HEAD))
            p_ref[chunk, :] = p.astype(BF16)
        o = _dot(p_ref[rows, :], v_ref[0, 0])
        o_ref[0, rows, :] = (o / l_ref[rows, :]).astype(BF16)


def _attention(q, k, v, n_q):
    nb, nh, n_k, dk = k.shape
    tq = Q_TILE
    return pl.pallas_call(
        _attn_kernel,
        out_shape=jax.ShapeDtypeStruct((nb, n_q, nh * V_HEAD), BF16),
        grid=(nb, nh, n_q // tq),
        in_specs=[pl.BlockSpec((1, 1, tq, dk), lambda b, h, i: (b, h, i, 0)),
                  pl.BlockSpec((1, 1, n_k, dk), lambda b, h, i: (b, h, 0, 0)),
                  pl.BlockSpec((1, 1, n_k, V_HEAD), lambda b, h, i: (b, h, 0, 0))],
        out_specs=pl.BlockSpec((1, tq, V_HEAD), lambda b, h, i: (b, i, h)),
        scratch_shapes=[pltpu.VMEM((tq, n_k), F32), pltpu.VMEM((tq, n_k), BF16), pltpu.VMEM((tq, V_HEAD), F32)],
        compiler_params=_params(3),
        name="mla_attn",
    )(q, k, v)


def _route(sel, s):
    n_e, n_t = sel.shape
    per = n_e // N_GROUPS
    sel3 = sel.reshape(N_GROUPS, per, n_t)
    s3 = s.reshape(N_GROUPS, per, n_t)
    iota_p = lax.broadcasted_iota(jnp.int32, (N_GROUPS, per, n_t), 1).astype(F32)
    iota_g = lax.broadcasted_iota(jnp.int32, (N_GROUPS, 1, n_t), 0).astype(F32)
    neg = -jnp.inf
    m1 = jnp.max(sel3, axis=1, keepdims=True)
    first = jnp.min(jnp.where(sel3 == m1, iota_p, float(per)), axis=1, keepdims=True)
    m2 = jnp.max(jnp.where(iota_p == first, neg, sel3), axis=1, keepdims=True)
    gs = m1 + m2
    gsel = jnp.zeros((N_GROUPS, 1, n_t), F32)
    for _ in range(TOPK_GROUPS):
        gm = jnp.max(gs, axis=0, keepdims=True)
        gfirst = jnp.min(jnp.where(gs == gm, iota_g, float(N_GROUPS)), axis=0, keepdims=True)
        pick = iota_g == gfirst
        gsel = jnp.where(pick, 1.0, gsel)
        gs = jnp.where(pick, neg, gs)
    val = jnp.where(gsel > 0.0, sel3, neg)
    iota_e = lax.broadcasted_iota(jnp.int32, (N_GROUPS, per, n_t), 0).astype(F32) * per + iota_p
    chosen = jnp.zeros((N_GROUPS, per, n_t), F32)
    idx, wts = [], []
    for _ in range(TOP_K):
        m = jnp.max(jnp.max(val, axis=1, keepdims=True), axis=0, keepdims=True)
        e = jnp.min(jnp.min(jnp.where(val == m, iota_e, float(n_e)), axis=1, keepdims=True), axis=0, keepdims=True)
        pick = iota_e == e
        wts.append(jnp.sum(jnp.sum(jnp.where(pick, s3, 0.0), axis=1, keepdims=True), axis=0, keepdims=True))
        idx.append(e)
        chosen = jnp.where(pick, 1.0, chosen)
        val = jnp.where(pick, neg, val)
    total = wts[0]
    for w in wts[1:]:
        total = total + w
    wts = [w / total * ROUTED_SCALE for w in wts]
    return idx, wts, chosen, iota_e


def _post_kernel(alpha, a_ref, w_ref, b_ref, x_ref, mod_ref, lng_ref, lnb_ref, rw_hi_ref, rw_lo_ref, rb_ref, tri_ref,
                 xn_ref, h2_ref, topi_ref, topw_ref, rank_ref, cnt_ref, run_ref):
    first_step = jnp.logical_and(pl.program_id(0) == 0, pl.program_id(1) == 0)

    @pl.when(first_step)
    def _():
        run_ref[...] = jnp.zeros_like(run_ref)

    y = _dot(a_ref[0], w_ref[...]) + b_ref[...]
    run = run_ref[...]
    for t0 in range(0, y.shape[0], ROUTE_SUB_TOKENS):
        rows = slice(t0, t0 + ROUTE_SUB_TOKENS)
        xn = _layer_norm(alpha * x_ref[0, rows, :] + mod_ref[0, 2:3, :] * y[rows], lng_ref[...], lnb_ref[...])
        xn_ref[0, rows, :] = xn
        h2 = xn * (1.0 + mod_ref[0, 4:5, :]) + mod_ref[0, 3:4, :]
        h_hi, h_lo = _split_bf16(h2)
        h2_ref[0, rows, :] = _pack_halves(h2)
        logits = _dot_nt(rw_hi_ref[...], h_hi) + _dot_nt(rw_hi_ref[...], h_lo) + _dot_nt(rw_lo_ref[...], h_hi)
        s = jax.nn.sigmoid(logits)
        idx, wts, chosen3, iota_e = _route(s + rb_ref[...], s)
        n_e, n_t = s.shape
        chosen = chosen3.reshape(n_e, n_t)
        rank = run[:, 0:1] + _dot(chosen.astype(BF16), tri_ref[...])
        rank3 = rank.reshape(chosen3.shape)
        for k in range(TOP_K):
            rk = jnp.sum(jnp.sum(jnp.where(iota_e == idx[k], rank3, 0.0), axis=1, keepdims=True), axis=0,
                         keepdims=True)
            topi_ref[k:k + 1, rows] = idx[k].reshape(1, n_t).astype(jnp.int32)
            topw_ref[k:k + 1, rows] = wts[k].reshape(1, n_t)
            rank_ref[k:k + 1, rows] = rk.reshape(1, n_t).astype(jnp.int32)
        run = run + jnp.sum(chosen, axis=1, keepdims=True)
    run_ref[...] = run
    cnt_ref[...] = run.astype(jnp.int32)


def _post(alpha, a, w, bias, x_all, mod, lng, lnb, rw_hi, rw_lo, rb, nt, ntx):
    nb, _, dk = a.shape
    d = x_all.shape[-1]
    n_e = rw_hi.shape[0]
    tl = SEQ_TILE
    n_tok = nb * nt * tl
    col_spec = lambda rows: pl.BlockSpec((rows, tl), lambda b, l: (0, b * nt + l))
    sub = ROUTE_SUB_TOKENS
    row_i = lax.broadcasted_iota(jnp.int32, (sub, sub), 0)
    col_i = lax.broadcasted_iota(jnp.int32, (sub, sub), 1)
    tri = jnp.where(row_i < col_i, 1.0, 0.0).astype(BF16)
    return pl.pallas_call(
        functools.partial(_post_kernel, alpha),
        out_shape=(jax.ShapeDtypeStruct((nb, nt * tl, d), F32),
                   jax.ShapeDtypeStruct((nb, nt * tl, d // 2), U32),
                   jax.ShapeDtypeStruct((TOP_K, n_tok), jnp.int32),
                   jax.ShapeDtypeStruct((TOP_K, n_tok), F32),
                   jax.ShapeDtypeStruct((TOP_K, n_tok), jnp.int32),
                   jax.ShapeDtypeStruct((n_e, LANE), jnp.int32)),
        grid=(nb, nt),
        in_specs=[_tok_spec(dk), _full_spec(w.shape), _full_spec(bias.shape), _tok_spec(d),
                  _mod_spec(d, nb, ntx), _full_spec(lng.shape), _full_spec(lnb.shape),
                  _full_spec(rw_hi.shape), _full_spec(rw_lo.shape), _full_spec(rb.shape), _full_spec(tri.shape)],
        out_specs=(_tok_spec(d), _tok_spec(d // 2), col_spec(TOP_K), col_spec(TOP_K), col_spec(TOP_K),
                   _full_spec((n_e, LANE))),
        scratch_shapes=[pltpu.VMEM((n_e, LANE), F32)],
        compiler_params=_params(2),
        name="mixer_post",
    )(a, w, bias, x_all, mod, lng, lnb, rw_hi, rw_lo, rb, tri)


def _gmm_kernel(be_ref, nu_ref, xs_ref, w1_ref, w3_ref, w2_ref, ys_ref, w13_s, w2_s):
    i = pl.program_id(0)
    de = w2_ref.shape[2]
    changed = jnp.logical_or(i == 0, be_ref[i] != be_ref[jnp.maximum(i - 1, 0)])

    @pl.when(changed)
    def _():
        w13_s[:, :de] = w1_ref[0, 0].astype(BF16)
        w13_s[:, de:] = w3_ref[0, 0].astype(BF16)
        w2_s[...] = w2_ref[0, 0].astype(BF16)

    @pl.when(i < nu_ref[0])
    def _():
        for r0 in range(0, xs_ref.shape[0], MOE_SUB_ROWS):
            rows = slice(r0, r0 + MOE_SUB_ROWS)
            z = _dot_packed(xs_ref[rows, :], w13_s)
            hmid = (_silu(z[:, :de]) * z[:, de:]).astype(BF16)
            ys_ref[rows, :] = _pack_halves(_dot(hmid, w2_s[...]))


def _grouped_ffn(block_expert, n_used, xs, w1, w3, w2, layer):
    n_rows, dp = xs.shape
    d = 2 * dp
    bm = MOE_BLOCK_ROWS
    de = w2.shape[2]
    grid_spec = pltpu.PrefetchScalarGridSpec(
        num_scalar_prefetch=2,
        grid=(n_rows // bm,),
        in_specs=[pl.BlockSpec((bm, dp), lambda i, be, nu: (i, 0)),
                  pl.BlockSpec((1, 1, d, de), lambda i, be, nu: (layer, be[i], 0, 0)),
                  pl.BlockSpec((1, 1, d, de), lambda i, be, nu: (layer, be[i], 0, 0)),
                  pl.BlockSpec((1, 1, de, d), lambda i, be, nu: (layer, be[i], 0, 0))],
        out_specs=pl.BlockSpec((bm, dp), lambda i, be, nu: (i, 0)),
        scratch_shapes=[pltpu.VMEM((d, 2 * de), BF16), pltpu.VMEM((de, d), BF16)],
    )
    return pl.pallas_call(
        _gmm_kernel,
        out_shape=jax.ShapeDtypeStruct((n_rows, dp), U32),
        grid_spec=grid_spec,
        compiler_params=_params(1),
        name="moe_grouped_ffn",
    )(block_expert, n_used, xs, w1, w3, w2)


def _sc_mesh():
    return plsc.VectorSubcoreMesh(core_axis_name="c", subcore_axis_name="s",
                                  num_cores=SC_CORES, num_subcores=SC_SUBCORES)


def _sc_worker():
    return lax.axis_index("s") * SC_CORES + lax.axis_index("c")


def _sc_window(rows_per_worker):
    for win in SC_WINDOWS:
        if rows_per_worker % (2 * win) == 0:
            return win
    raise ValueError(f"no SparseCore window divides {rows_per_worker} rows per worker")


def _sc_row_scatter(src, dest, n_out):
    n_k, n_tok = dest.shape
    dp = src.shape[1]
    assert n_tok % SC_WORKERS == 0
    per_w = n_tok // SC_WORKERS
    win = _sc_window(per_w)
    n_win = per_w // win
    idx = dest.reshape(n_k, SC_WORKERS, n_win, win).transpose(1, 2, 0, 3)

    def body(src_hbm, idx_hbm, out_hbm, idx_v, rows_v, load_sem, scat_sem):
        wid = _sc_worker()
        base = wid * per_w
        pltpu.sync_copy(idx_hbm.at[wid], idx_v)

        def load(g, slot):
            return pltpu.make_async_copy(src_hbm.at[pl.ds(base + g * win, win)], rows_v.at[slot], load_sem.at[slot])

        def scatter(g, slot, k):
            return pltpu.make_async_copy(rows_v.at[slot], out_hbm.at[idx_v.at[g, k]], scat_sem.at[slot])

        load(0, 0).start()

        @pl.loop(0, n_win, step=2)
        def _(g):
            for slot in range(2):
                cur = g + slot
                load(cur, slot).wait()

                @pl.when(cur + 1 < n_win)
                def _():
                    @pl.when(cur >= 1)
                    def _():
                        for k in range(n_k):
                            scatter(cur - 1, 1 - slot, k).wait()
                    load(cur + 1, 1 - slot).start()

                for k in range(n_k):
                    scatter(cur, slot, k).start()

        for slot in range(2):
            for k in range(n_k):
                scatter(n_win - 2 + slot, slot, k).wait()

    return pl.kernel(
        body, mesh=_sc_mesh(),
        out_type=jax.ShapeDtypeStruct((n_out, dp), src.dtype),
        scratch_types=[pltpu.VMEM((n_win, n_k, win), jnp.int32),
                       pltpu.VMEM((2, win, dp), src.dtype),
                       pltpu.SemaphoreType.DMA((2,)),
                       pltpu.SemaphoreType.DMA((2,))],
        compiler_params=pltpu.CompilerParams(use_tc_tiling_on_sc=True),
        name="sc_dispatch_scatter",
    )(src, idx)


def _sc_row_gather(table, idx):
    n = idx.shape[0]
    dp = table.shape[1]
    assert n % SC_WORKERS == 0
    per_w = n // SC_WORKERS
    win = _sc_window(per_w)
    n_win = per_w // win

    def body(table_hbm, idx_hbm, out_hbm, idx_v, rows_v, gather_sem, put_sem):
        wid = _sc_worker()
        base = wid * per_w
        pltpu.sync_copy(idx_hbm.at[pl.ds(base, per_w)], idx_v)

        def gather(g, slot):
            return pltpu.make_async_copy(table_hbm.at[idx_v.at[pl.ds(g * win, win)]], rows_v.at[slot],
                                         gather_sem.at[slot])

        def put(g, slot):
            return pltpu.make_async_copy(rows_v.at[slot], out_hbm.at[pl.ds(base + g * win, win)], put_sem.at[slot])

        gather(0, 0).start()

        @pl.loop(0, n_win, step=2)
        def _(g):
            for slot in range(2):
                cur = g + slot
                gather(cur, slot).wait()

                @pl.when(cur + 1 < n_win)
                def _():
                    @pl.when(cur >= 1)
                    def _():
                        put(cur - 1, 1 - slot).wait()
                    gather(cur + 1, 1 - slot).start()

                put(cur, slot).start()

        for slot in range(2):
            put(n_win - 2 + slot, slot).wait()

    return pl.kernel(
        body, mesh=_sc_mesh(),
        out_type=jax.ShapeDtypeStruct((n, dp), table.dtype),
        scratch_types=[pltpu.VMEM((per_w,), jnp.int32),
                       pltpu.VMEM((2, win, dp), table.dtype),
                       pltpu.SemaphoreType.DMA((2,)),
                       pltpu.SemaphoreType.DMA((2,))],
        compiler_params=pltpu.CompilerParams(use_tc_tiling_on_sc=True),
        name="sc_combine_gather",
    )(table, idx)


def _moe_out_kernel(alpha, x_ref, h2_ref, yg_ref, tw_ref, mod_ref, ws13_ref, ws2_ref, lng_ref, lnb_ref, o_ref):
    de = ws2_ref.shape[0]
    z = _dot_packed(h2_ref[0], ws13_ref)
    hmid = (_silu(z[:, :de]) * z[:, de:]).astype(BF16)
    y = _dot(hmid, ws2_ref[...])
    lo, hi = _unpack_halves(yg_ref[0])
    acc_lo, acc_hi = lo * tw_ref[:, 0:1], hi * tw_ref[:, 0:1]
    for k in range(1, yg_ref.shape[0]):
        lo, hi = _unpack_halves(yg_ref[k])
        acc_lo = acc_lo + lo * tw_ref[:, k:k + 1]
        acc_hi = acc_hi + hi * tw_ref[:, k:k + 1]
    y = y + jnp.concatenate([acc_lo, acc_hi], axis=1)
    o_ref[0] = _layer_norm(alpha * x_ref[0] + mod_ref[0, 5:6, :] * y, lng_ref[...], lnb_ref[...])


def _moe_out(alpha, xn, h2, y_rows, topw_t, mod, ws13, ws2, lng, lnb, nt, ntx):
    nb, _, d = xn.shape
    n_k = y_rows.shape[0]
    tl = SEQ_TILE
    return pl.pallas_call(
        functools.partial(_moe_out_kernel, alpha),
        out_shape=jax.ShapeDtypeStruct((nb, nt * tl, d), F32),
        grid=(nb, nt),
        in_specs=[_tok_spec(d), _tok_spec(d // 2),
                  pl.BlockSpec((n_k, tl, d // 2), lambda b, l: (0, b * nt + l, 0)),
                  pl.BlockSpec((tl, n_k), lambda b, l: (b * nt + l, 0)),
                  _mod_spec(d, nb, ntx),
                  _full_spec(ws13.shape), _full_spec(ws2.shape), _full_spec(lng.shape), _full_spec(lnb.shape)],
        out_specs=_tok_spec(d),
        compiler_params=_params(2),
        name="moe_out",
    )(xn, h2, y_rows, topw_t, mod, ws13, ws2, lng, lnb)


def _dispatch_plan(counts, topi, rank, n_tok):
    n_e = counts.shape[0]
    bm = MOE_BLOCK_ROWS
    padded = (counts + bm - 1) // bm * bm
    pad_end = jnp.cumsum(padded)
    pad_start = pad_end - padded
    onehot = topi[:, None, :] == jnp.arange(n_e, dtype=jnp.int32)[None, :, None]
    dest = rank + jnp.sum(jnp.where(onehot, pad_start[None, :, None], 0), axis=1)
    n_blocks = n_tok * TOP_K // bm + n_e
    block_start = jnp.arange(n_blocks, dtype=jnp.int32) * bm
    block_expert = jnp.sum((pad_end[None, :] <= block_start[:, None]).astype(jnp.int32), axis=1)
    block_expert = jnp.minimum(block_expert, n_e - 1)
    n_used = (pad_end[-1] // bm).astype(jnp.int32).reshape(1)
    return dest.astype(jnp.int32), block_expert, n_used, n_blocks * bm


def _rope_tables(seq):
    n_freq = QK_ROPE // 4
    inv_freq = ROPE_THETA ** (-jnp.arange(n_freq, dtype=F32) / n_freq)
    pos = jnp.arange(seq, dtype=jnp.int32)
    r = (pos // GRID_W).astype(F32)
    col = (pos % GRID_W).astype(F32)
    ang = jnp.concatenate([r[:, None] * inv_freq, col[:, None] * inv_freq], -1)
    cos, sin = jnp.cos(ang), jnp.sin(ang)
    zeros = jnp.zeros((seq, LANE - QK_ROPE), F32)
    cos_slot = jnp.concatenate([cos, cos, zeros], -1)
    sin_slot = jnp.concatenate([-sin, sin, zeros], -1)
    return cos_slot, sin_slot


def _rope_slot_weights(w_rope):
    k, n, _ = w_rope.shape
    half = QK_ROPE // 2
    swapped = jnp.concatenate([w_rope[..., half:], w_rope[..., :half]], -1)
    pad = jnp.zeros((k, n, LANE - QK_ROPE), w_rope.dtype)
    plain = jnp.concatenate([w_rope, pad], -1).reshape(k, n * LANE)
    swp = jnp.concatenate([swapped, pad], -1).reshape(k, n * LANE)
    return jnp.concatenate([plain, swp], -1)


def kernel(x, c, ctx, c_ctx, ada_w, ada_b, ln_g, ln_b, conf_w1, conf_b1, conf_dw, conf_dwb, conf_ng, conf_nb, conf_w2, conf_b2, sc_w_in, sc_dw, sc_w_out, mla_w_dqkv, mla_q_g, mla_kv_g, mla_w_uq, mla_w_uk, mla_w_uv, mla_w_o, moe_router, moe_bias, moe_w1, moe_w3, moe_w2, sh_w1, sh_w3, sh_w2):
    nb, seq, d = x.shape
    l_ctx = ctx.shape[1]
    depth = ada_w.shape[0]
    alpha = (2.0 * depth) ** 0.25
    tl = SEQ_TILE
    assert seq % tl == 0 and l_ctx % tl == 0 and seq % Q_TILE == 0 and seq % GRID_W == 0
    ntx = seq // tl
    nt_all = (seq + l_ctx) // tl
    attn_layers = [i for i in range(depth) if i % N_MIXERS == 2]
    last_ctx_reader = attn_layers[-1] if attn_layers else -1

    rows = -(-(nb + 1) // 8) * 8
    c_all = jnp.zeros((rows, d), F32).at[:nb].set(c).at[nb].set(c_ctx)
    mod_all = _modulation(c_all, ada_w, ada_b).reshape(depth, rows, N_MOD, d)

    assert nb % N_CHAINS == 0
    nbc = nb // N_CHAINS
    chains = [jnp.concatenate([x[c0:c0 + nbc], ctx[c0:c0 + nbc]], axis=1) for c0 in range(0, nb, nbc)]
    q_rank, kv_rank = mla_q_g.shape[1], mla_kv_g.shape[1]
    cos_t, sin_t = _rope_tables(seq)
    row = lambda v: v.reshape(1, -1)

    for i in range(depth):
        need_ctx = i < last_ctx_reader
        kind, j = i % N_MIXERS, i // N_MIXERS
        nt = nt_all if need_ctx else ntx
        n_tok = nbc * nt * tl

        if kind == 0:
            w_first = conf_w1[j].astype(BF16)
            dw_tiles = jnp.broadcast_to(conf_dw[j][:, None, :], (conf_dw.shape[1], SUBLANES, d))
            w_last, b_last = conf_w2[j].astype(BF16), row(conf_b2[j])
        elif kind == 1:
            w_first = sc_w_in[j].astype(BF16)
            w_last, b_last = sc_w_out[j].astype(BF16), jnp.zeros((1, d), F32)
        else:
            wdq = mla_w_dqkv[j]
            wd = wdq[:, :q_rank + kv_rank].astype(BF16)
            wkp = _rope_slot_weights(wdq[:, None, q_rank + kv_rank:]).astype(BF16)
            wuq = mla_w_uq[j].reshape(q_rank, MLA_HEADS, QK_NOPE + QK_ROPE)
            wqn = wuq[:, :, :QK_NOPE].reshape(q_rank, MLA_HEADS * QK_NOPE).astype(BF16)
            wqr = _rope_slot_weights(wuq[:, :, QK_NOPE:]).astype(BF16)
            wuk, wuv = mla_w_uk[j].astype(BF16), mla_w_uv[j].astype(BF16)
            w_last, b_last = mla_w_o[j].astype(BF16), jnp.zeros((1, d), F32)
        rw_hi, rw_lo = _split_bf16(moe_router[i].T)
        ws13 = jnp.concatenate([sh_w1[i], sh_w3[i]], axis=-1).astype(BF16)
        ws2 = sh_w2[i].astype(BF16)

        for ci in range(N_CHAINS):
            x_all = chains[ci]
            mod = jnp.concatenate([mod_all[i, ci * nbc:(ci + 1) * nbc], mod_all[i, nb:nb + 1]], axis=0)

            if kind == 0:
                u = _conf_in(x_all, mod, w_first, row(conf_b1[j]), nt, ntx)
                a = _conf_conv(u, dw_tiles, row(conf_dwb[j]), row(conf_ng[j]), row(conf_nb[j]), nt, ntx)
            elif kind == 1:
                a = _sc_conv(x_all, mod, w_first, sc_dw[j], nt, ntx)
            else:
                q, k, v = _mla_proj(x_all, mod, wd, wkp, row(mla_q_g[j]), row(mla_kv_g[j]), wqn, wqr,
                                    wuk, wuv, cos_t, sin_t, nt_all, ntx)
                a = _attention(q, k, v, nt * tl)

            xn, h2, topi, topw, rank, counts = _post(alpha, a, w_last, b_last, x_all, mod, row(ln_g[i, 0]),
                                                     row(ln_b[i, 0]), rw_hi, rw_lo, moe_bias[i].reshape(-1, 1),
                                                     nt, ntx)

            dest, block_expert, n_used, n_rows = _dispatch_plan(counts[:, 0], topi, rank, n_tok)
            xs = _sc_row_scatter(h2.reshape(n_tok, d // 2), dest, n_rows)
            ys = _grouped_ffn(block_expert, n_used, xs, moe_w1, moe_w3, moe_w2, i)
            y_rows = _sc_row_gather(ys, dest.reshape(-1)).reshape(TOP_K, n_tok, d // 2)

            chains[ci] = _moe_out(alpha, xn, h2, y_rows, topw.T, mod, ws13, ws2,
                                  row(ln_g[i, 1]), row(ln_b[i, 1]), nt, ntx)
    return jnp.concatenate([xc[:, :seq] for xc in chains], axis=0)
```

```python
import functools

import numpy as np
import jax
import jax.numpy as jnp
from jax import lax
from jax.experimental import pallas as pl
from jax.experimental.pallas import tpu as pltpu
from jax.experimental.pallas import tpu_sc as plsc

F32 = jnp.float32
BF16 = jnp.bfloat16
U32 = jnp.uint32
HIGH_HALF_MASK = np.uint32(0xFFFF0000)

GRID_W = 64
N_MIXERS = 3
LN_EPS = 1e-5
RMS_EPS = 1e-6
N_MOD = 6
MLA_HEADS = 8
QK_NOPE = 128
QK_ROPE = 64
V_HEAD = 128
ROPE_THETA = 10000.0
ATTN_SCALE = (QK_NOPE + QK_ROPE) ** -0.5
Q_SCALE = ATTN_SCALE * 1.4426950408889634
TOP_K = 8
N_GROUPS = 8
TOPK_GROUPS = 4
ROUTED_SCALE = 2.5

SEQ_TILE = 256
CONV_HALO = 16
SHORT_HALO = 8
CONV_ROW_CHUNK = 64
LANE = 128
SUBLANES = 8
Q_TILE = 1024
ATTN_SUB_ROWS = 256
ATTN_ROW_CHUNK = 16
MOE_BLOCK_ROWS = 1024
MOE_SUB_ROWS = 1024
ROUTE_SUB_TOKENS = 128
MOD_COL_TILE = 1536
VMEM_LIMIT = 48 * 1024 * 1024
SC_CORES = 2
SC_SUBCORES = 16
SC_WORKERS = SC_CORES * SC_SUBCORES
SC_LANES = 16
SC_COMBINE_TOKENS = 8
SC_WINDOWS = (64, 32, 16)
N_CHAINS = 1


def _params(n_axes):
    return pltpu.CompilerParams(dimension_semantics=("arbitrary",) * n_axes,
                                vmem_limit_bytes=VMEM_LIMIT)


def _split_bf16(a):
    hi = a.astype(BF16)
    lo = (a - hi.astype(F32)).astype(BF16)
    return hi, lo


def _dot(a, b):
    return jnp.dot(a, b, preferred_element_type=F32)


def _dot_nt(a, b):
    return lax.dot_general(a, b, (((1,), (1,)), ((), ())), preferred_element_type=F32)


def _pack_halves(v):
    half = v.shape[-1] // 2
    lo = lax.bitcast_convert_type(v[:, :half].astype(BF16).astype(F32), U32) >> 16
    hi = lax.bitcast_convert_type(v[:, half:].astype(BF16).astype(F32), U32) & HIGH_HALF_MASK
    return hi | lo


def _unpack_halves(p):
    lo = lax.bitcast_convert_type(p << 16, F32)
    hi = lax.bitcast_convert_type(p & HIGH_HALF_MASK, F32)
    return lo, hi


def _dot_packed(p, w):
    lo, hi = _unpack_halves(p)
    return _dot(jnp.concatenate([lo.astype(BF16), hi.astype(BF16)], axis=1), w[...])


def _layer_norm(v, g, b):
    mu = jnp.mean(v, axis=-1, keepdims=True)
    c = v - mu
    var = jnp.mean(c * c, axis=-1, keepdims=True)
    return c * lax.rsqrt(var + LN_EPS) * g + b


def _silu(v):
    return v * jax.nn.sigmoid(v)


def _mod_kernel(c_ref, w_ref, b_ref, o_ref):
    a = _silu(c_ref[...])
    a_hi, a_lo = _split_bf16(a)
    w_hi, w_lo = _split_bf16(w_ref[0])
    o_ref[0] = _dot(a_hi, w_hi) + _dot(a_hi, w_lo) + _dot(a_lo, w_hi) + b_ref[0]


def _modulation(c_all, ada_w, ada_b):
    depth, d, n = ada_w.shape
    rows = c_all.shape[0]
    tn = MOD_COL_TILE
    return pl.pallas_call(
        _mod_kernel,
        out_shape=jax.ShapeDtypeStruct((depth, rows, n), F32),
        grid=(depth, n // tn),
        in_specs=[pl.BlockSpec((rows, d), lambda i, j: (0, 0)),
                  pl.BlockSpec((1, d, tn), lambda i, j: (i, 0, j)),
                  pl.BlockSpec((1, 1, tn), lambda i, j: (i, 0, j))],
        out_specs=pl.BlockSpec((1, rows, tn), lambda i, j: (i, 0, j)),
        compiler_params=_params(2),
        name="adaln_mod",
    )(c_all, ada_w, ada_b.reshape(depth, 1, n))


def _tok_spec(d, tl=SEQ_TILE):
    return pl.BlockSpec((1, tl, d), lambda b, l: (b, l, 0))


def _mod_spec(d, n_batch, ntx):
    return pl.BlockSpec((1, N_MOD, d), lambda b, l: (jnp.where(l < ntx, b, n_batch), 0, 0))


def _full_spec(shape):
    zeros = (0,) * len(shape)
    return pl.BlockSpec(shape, lambda b, l: zeros)


def _halo_specs(d, halo, seq_len, tl=SEQ_TILE):
    per_tile = tl // halo
    last = seq_len // halo - 1
    prev = pl.BlockSpec((1, halo, d), lambda b, l: (b, jnp.maximum(l * per_tile - 1, 0), 0))
    nxt = pl.BlockSpec((1, halo, d), lambda b, l: (b, jnp.minimum((l + 1) * per_tile, last), 0))
    return prev, nxt


def _segment_edges(l, ntx, nt):
    first = jnp.logical_or(l == 0, l == ntx)
    last = jnp.logical_or(l == ntx - 1, l == nt - 1)
    return first, last


def _window_rows(prev_ref, cur_ref, next_ref, win_ref):
    halo, tl = prev_ref.shape[1], cur_ref.shape[1]
    win_ref[0:halo, :] = prev_ref[0]
    win_ref[halo:halo + tl, :] = cur_ref[0]
    win_ref[halo + tl:, :] = next_ref[0]
    return win_ref[...]


def _edge_mask(n_rows, halo, first, last):
    r = lax.broadcasted_iota(jnp.int32, (n_rows, 1), 0)
    outside = jnp.logical_or(jnp.logical_and(first, r < halo), jnp.logical_and(last, r >= n_rows - halo))
    return jnp.where(outside, 0.0, 1.0)


def _conf_in_kernel(x_ref, mod_ref, w1_ref, b1_ref, u_ref):
    d = x_ref.shape[-1]
    h = x_ref[0] * (1.0 + mod_ref[0, 1:2, :]) + mod_ref[0, 0:1, :]
    z = _dot(h.astype(BF16), w1_ref[...]) + b1_ref[...]
    u_ref[0] = z[:, :d] * jax.nn.sigmoid(z[:, d:])


def _conf_in(x_all, mod, w1, b1, nt, ntx):
    nb, seq_len, d = x_all.shape
    return pl.pallas_call(
        _conf_in_kernel,
        out_shape=jax.ShapeDtypeStruct((nb, nt * SEQ_TILE, d), F32),
        grid=(nb, nt),
        in_specs=[_tok_spec(d), _mod_spec(d, nb, ntx), _full_spec(w1.shape), _full_spec(b1.shape)],
        out_specs=_tok_spec(d),
        compiler_params=_params(2),
        name="conf_in",
    )(x_all, mod, w1, b1)


def _conf_conv_kernel(ntx, nt, up_ref, uc_ref, un_ref, dw_ref, dwb_ref, ng_ref, nb_ref,
                      a_ref, sh_ref, conv_ref):
    tl, d = uc_ref.shape[1], uc_ref.shape[2]
    taps = dw_ref.shape[0]
    lead = CONV_HALO - (taps - 1) // 2
    first, last = _segment_edges(pl.program_id(1), ntx, nt)
    sh_ref[0, 0:CONV_HALO, :] = jnp.where(first, 0.0, up_ref[0])
    sh_ref[0, CONV_HALO:CONV_HALO + tl, :] = uc_ref[0]
    sh_ref[0, CONV_HALO + tl:, :] = jnp.where(last, 0.0, un_ref[0])
    span = tl + 2 * CONV_HALO - SUBLANES
    for s in range(1, SUBLANES):
        sh_ref[s, 0:span, :] = sh_ref[0, s:s + span, :]
    groups = CONV_ROW_CHUNK // SUBLANES

    def row_chunk(i, carry):
        r0 = pl.multiple_of(i * CONV_ROW_CHUNK, CONV_ROW_CHUNK)
        for c0 in range(0, d, LANE):
            accs = [jnp.zeros((SUBLANES, LANE), F32) for _ in range(groups)]
            for k in range(taps):
                res = (lead + k) % SUBLANES
                off = lead + k - res
                w = dw_ref[k, :, c0:c0 + LANE]
                for g in range(groups):
                    lo = r0 + (off + g * SUBLANES)
                    accs[g] = accs[g] + w * sh_ref[res, pl.ds(lo, SUBLANES), c0:c0 + LANE]
            for g in range(groups):
                conv_ref[pl.ds(r0 + g * SUBLANES, SUBLANES), c0:c0 + LANE] = accs[g]
        return carry

    lax.fori_loop(0, tl // CONV_ROW_CHUNK, row_chunk, 0)
    v = _layer_norm(conv_ref[...] + dwb_ref[...], ng_ref[...], nb_ref[...])
    a_ref[0] = _silu(v).astype(BF16)


def _conf_conv(u, dw, dwb, ng, nb_, nt, ntx):
    nb, seq_len, d = u.shape
    prev, nxt = _halo_specs(d, CONV_HALO, seq_len)
    return pl.pallas_call(
        functools.partial(_conf_conv_kernel, ntx, nt),
        out_shape=jax.ShapeDtypeStruct((nb, seq_len, d), BF16),
        grid=(nb, nt),
        in_specs=[prev, _tok_spec(d), nxt, _full_spec(dw.shape), _full_spec(dwb.shape),
                  _full_spec(ng.shape), _full_spec(nb_.shape)],
        out_specs=_tok_spec(d),
        scratch_shapes=[pltpu.VMEM((SUBLANES, SEQ_TILE + 2 * CONV_HALO, d), F32),
                        pltpu.VMEM((SEQ_TILE, d), F32)],
        compiler_params=_params(2),
        name="conf_conv",
    )(u, u, u, dw, dwb, ng, nb_)


def _sc_conv_kernel(ntx, nt, xp_ref, xc_ref, xn_ref, mod_ref, w_ref, dw_ref, a_ref, win_ref, ext_ref):
    tl, d = xc_ref.shape[1], xc_ref.shape[2]
    taps = dw_ref.shape[0]
    lead = SHORT_HALO - (taps - 1) // 2
    first, last = _segment_edges(pl.program_id(1), ntx, nt)
    xw = _window_rows(xp_ref, xc_ref, xn_ref, win_ref)
    h = xw * (1.0 + mod_ref[0, 1:2, :]) + mod_ref[0, 0:1, :]
    z = _dot(h.astype(BF16), w_ref[...])
    gb = z[SHORT_HALO:SHORT_HALO + tl, :d]
    ext_ref[...] = z[:, d:2 * d] * z[:, 2 * d:] * _edge_mask(z.shape[0], SHORT_HALO, first, last)
    acc = dw_ref[0:1, :] * ext_ref[lead:lead + tl, :]
    for k in range(1, taps):
        acc = acc + dw_ref[k:k + 1, :] * ext_ref[lead + k:lead + k + tl, :]
    a_ref[0] = (gb * acc).astype(BF16)


def _sc_conv(x_all, mod, w_in, dw, nt, ntx):
    nb, seq_len, d = x_all.shape
    prev, nxt = _halo_specs(d, SHORT_HALO, seq_len)
    win = pltpu.VMEM((SEQ_TILE + 2 * SHORT_HALO, d), F32)
    return pl.pallas_call(
        functools.partial(_sc_conv_kernel, ntx, nt),
        out_shape=jax.ShapeDtypeStruct((nb, nt * SEQ_TILE, d), BF16),
        grid=(nb, nt),
        in_specs=[prev, _tok_spec(d), nxt, _mod_spec(d, nb, ntx), _full_spec(w_in.shape), _full_spec(dw.shape)],
        out_specs=_tok_spec(d),
        scratch_shapes=[win, win],
        compiler_params=_params(2),
        name="sc_conv",
    )(x_all, x_all, x_all, mod, w_in, dw)


def _rms(v, g):
    return v * lax.rsqrt(jnp.mean(v * v, axis=-1, keepdims=True) + RMS_EPS) * g


def _mla_proj_kernel(ntx, q_rank,
                     x_ref, mod_ref, wd_ref, wkp_ref, qg_ref, kvg_ref, wqn_ref, wqr_ref, wuk_ref, wuv_ref,
                     cos_ref, sin_ref, q_ref, k_ref, v_ref):
    is_latent = pl.program_id(1) < ntx
    h = (x_ref[0] * (1.0 + mod_ref[0, 1:2, :]) + mod_ref[0, 0:1, :]).astype(BF16)
    dn = _dot(h, wd_ref[...])
    cq = _rms(dn[:, :q_rank], qg_ref[...]).astype(BF16)
    ckv = _rms(dn[:, q_rank:], kvg_ref[...]).astype(BF16)
    cos = jnp.where(is_latent, cos_ref[...], 1.0)
    sin = jnp.where(is_latent, sin_ref[...], 0.0)
    kp2 = _dot(h, wkp_ref[...])
    kp = kp2[:, :LANE] * cos + kp2[:, LANE:] * sin
    kn = _dot(ckv, wuk_ref[...])
    vv = _dot(ckv, wuv_ref[...])
    qn = _dot(cq, wqn_ref[...]) * Q_SCALE
    qr2 = _dot(cq, wqr_ref[...])
    hw = MLA_HEADS * LANE
    for hd in range(MLA_HEADS):
        sl = slice(hd * LANE, (hd + 1) * LANE)
        qr = (qr2[:, sl] * cos + qr2[:, hw + hd * LANE:hw + (hd + 1) * LANE] * sin) * Q_SCALE
        q_ref[0, hd, :, 0:LANE] = qn[:, sl].astype(BF16)
        q_ref[0, hd, :, LANE:] = qr.astype(BF16)
        k_ref[0, hd, :, 0:LANE] = kn[:, sl].astype(BF16)
        k_ref[0, hd, :, LANE:] = kp.astype(BF16)
        v_ref[0, hd] = vv[:, sl].astype(BF16)


def _mla_proj(x_all, mod, wd, wkp, qg, kvg, wqn, wqr, wuk, wuv, cos_t, sin_t, nt, ntx):
    nb, seq_len, d = x_all.shape
    q_rank = qg.shape[-1]
    tl = SEQ_TILE
    rope_spec = pl.BlockSpec((tl, LANE), lambda b, l: (jnp.minimum(l, ntx - 1), 0))
    qk_shape = jax.ShapeDtypeStruct((nb, MLA_HEADS, nt * tl, 2 * LANE), BF16)
    v_shape = jax.ShapeDtypeStruct((nb, MLA_HEADS, nt * tl, LANE), BF16)
    qk_spec = pl.BlockSpec((1, MLA_HEADS, tl, 2 * LANE), lambda b, l: (b, 0, l, 0))
    v_spec = pl.BlockSpec((1, MLA_HEADS, tl, LANE), lambda b, l: (b, 0, l, 0))
    return pl.pallas_call(
        functools.partial(_mla_proj_kernel, ntx, q_rank),
        out_shape=(qk_shape, qk_shape, v_shape),
        grid=(nb, nt),
        in_specs=[_tok_spec(d), _mod_spec(d, nb, ntx), _full_spec(wd.shape), _full_spec(wkp.shape),
                  _full_spec(qg.shape), _full_spec(kvg.shape), _full_spec(wqn.shape), _full_spec(wqr.shape),
                  _full_spec(wuk.shape), _full_spec(wuv.shape), rope_spec, rope_spec],
        out_specs=(qk_spec, qk_spec, v_spec),
        compiler_params=_params(2),
        name="mla_proj",
    )(x_all, mod, wd, wkp, qg, kvg, wqn, wqr, wuk, wuv, cos_t, sin_t)


def _attn_kernel(q_ref, k_ref, v_ref, o_ref, s_ref, p_ref, l_ref):
    tq = q_ref.shape[2]
    for r0 in range(0, tq, ATTN_SUB_ROWS):
        rows = slice(r0, r0 + ATTN_SUB_ROWS)
        s_ref[rows, :] = _dot_nt(q_ref[0, 0, rows, :], k_ref[0, 0])
        for c0 in range(r0, r0 + ATTN_SUB_ROWS, ATTN_ROW_CHUNK):
            chunk = slice(c0, c0 + ATTN_ROW_CHUNK)
            s = s_ref[chunk, :]
            p = jnp.exp2(s - jnp.max(s, axis=-1, keepdims=True))
            l_ref[chunk, :] = jnp.broadcast_to(jnp.sum(p, axis=-1, keepdims=True), (ATTN_ROW_CHUNK, V_HEAD))
            p_ref[chunk, :] = p.astype(BF16)
        o = _dot(p_ref[rows, :], v_ref[0, 0])
        o_ref[0, rows, :] = (o / l_ref[rows, :]).astype(BF16)


def _attention(q, k, v, n_q):
    nb, nh, n_k, dk = k.shape
    tq = Q_TILE
    return pl.pallas_call(
        _attn_kernel,
        out_shape=jax.ShapeDtypeStruct((nb, n_q, nh * V_HEAD), BF16),
        grid=(nb, nh, n_q // tq),
        in_specs=[pl.BlockSpec((1, 1, tq, dk), lambda b, h, i: (b, h, i, 0)),
                  pl.BlockSpec((1, 1, n_k, dk), lambda b, h, i: (b, h, 0, 0)),
                  pl.BlockSpec((1, 1, n_k, V_HEAD), lambda b, h, i: (b, h, 0, 0))],
        out_specs=pl.BlockSpec((1, tq, V_HEAD), lambda b, h, i: (b, i, h)),
        scratch_shapes=[pltpu.VMEM((tq, n_k), F32), pltpu.VMEM((tq, n_k), BF16), pltpu.VMEM((tq, V_HEAD), F32)],
        compiler_params=_params(3),
        name="mla_attn",
    )(q, k, v)


def _route(sel, s):
    n_e, n_t = sel.shape
    per = n_e // N_GROUPS
    sel3 = sel.reshape(N_GROUPS, per, n_t)
    s3 = s.reshape(N_GROUPS, per, n_t)
    iota_p = lax.broadcasted_iota(jnp.int32, (N_GROUPS, per, n_t), 1).astype(F32)
    iota_g = lax.broadcasted_iota(jnp.int32, (N_GROUPS, 1, n_t), 0).astype(F32)
    neg = -jnp.inf
    m1 = jnp.max(sel3, axis=1, keepdims=True)
    first = jnp.min(jnp.where(sel3 == m1, iota_p, float(per)), axis=1, keepdims=True)
    m2 = jnp.max(jnp.where(iota_p == first, neg, sel3), axis=1, keepdims=True)
    gs = m1 + m2
    gsel = jnp.zeros((N_GROUPS, 1, n_t), F32)
    for _ in range(TOPK_GROUPS):
        gm = jnp.max(gs, axis=0, keepdims=True)
        gfirst = jnp.min(jnp.where(gs == gm, iota_g, float(N_GROUPS)), axis=0, keepdims=True)
        pick = iota_g == gfirst
        gsel = jnp.where(pick, 1.0, gsel)
        gs = jnp.where(pick, neg, gs)
    val = jnp.where(gsel > 0.0, sel3, neg)
    iota_e = lax.broadcasted_iota(jnp.int32, (N_GROUPS, per, n_t), 0).astype(F32) * per + iota_p
    chosen = jnp.zeros((N_GROUPS, per, n_t), F32)
    idx, wts = [], []
    for _ in range(TOP_K):
        m = jnp.max(jnp.max(val, axis=1, keepdims=True), axis=0, keepdims=True)
        e = jnp.min(jnp.min(jnp.where(val == m, iota_e, float(n_e)), axis=1, keepdims=True), axis=0, keepdims=True)
        pick = iota_e == e
        wts.append(jnp.sum(jnp.sum(jnp.where(pick, s3, 0.0), axis=1, keepdims=True), axis=0, keepdims=True))
        idx.append(e)
        chosen = jnp.where(pick, 1.0, chosen)
        val = jnp.where(pick, neg, val)
    total = wts[0]
    for w in wts[1:]:
        total = total + w
    wts = [w / total * ROUTED_SCALE for w in wts]
    return idx, wts, chosen, iota_e


def _post_kernel(alpha, a_ref, w_ref, b_ref, x_ref, mod_ref, lng_ref, lnb_ref, rw_hi_ref, rw_lo_ref, rb_ref, tri_ref,
                 xn_ref, h2_ref, topi_ref, topw_ref, rank_ref, cnt_ref, run_ref):
    first_step = jnp.logical_and(pl.program_id(0) == 0, pl.program_id(1) == 0)

    @pl.when(first_step)
    def _():
        run_ref[...] = jnp.zeros_like(run_ref)

    y = _dot(a_ref[0], w_ref[...]) + b_ref[...]
    run = run_ref[...]
    for t0 in range(0, y.shape[0], ROUTE_SUB_TOKENS):
        rows = slice(t0, t0 + ROUTE_SUB_TOKENS)
        xn = _layer_norm(alpha * x_ref[0, rows, :] + mod_ref[0, 2:3, :] * y[rows], lng_ref[...], lnb_ref[...])
        xn_ref[0, rows, :] = xn
        h2 = xn * (1.0 + mod_ref[0, 4:5, :]) + mod_ref[0, 3:4, :]
        h_hi, h_lo = _split_bf16(h2)
        h2_ref[0, rows, :] = _pack_halves(h2)
        logits = _dot_nt(rw_hi_ref[...], h_hi) + _dot_nt(rw_hi_ref[...], h_lo) + _dot_nt(rw_lo_ref[...], h_hi)
        s = jax.nn.sigmoid(logits)
        idx, wts, chosen3, iota_e = _route(s + rb_ref[...], s)
        n_e, n_t = s.shape
        chosen = chosen3.reshape(n_e, n_t)
        rank = run[:, 0:1] + _dot(chosen.astype(BF16), tri_ref[...])
        rank3 = rank.reshape(chosen3.shape)
        for k in range(TOP_K):
            rk = jnp.sum(jnp.sum(jnp.where(iota_e == idx[k], rank3, 0.0), axis=1, keepdims=True), axis=0,
                         keepdims=True)
            topi_ref[k:k + 1, rows] = idx[k].reshape(1, n_t).astype(jnp.int32)
            topw_ref[k:k + 1, rows] = wts[k].reshape(1, n_t)
            rank_ref[k:k + 1, rows] = rk.reshape(1, n_t).astype(jnp.int32)
        run = run + jnp.sum(chosen, axis=1, keepdims=True)
    run_ref[...] = run
    cnt_ref[...] = run.astype(jnp.int32)


def _post(alpha, a, w, bias, x_all, mod, lng, lnb, rw_hi, rw_lo, rb, nt, ntx):
    nb, _, dk = a.shape
    d = x_all.shape[-1]
    n_e = rw_hi.shape[0]
    tl = SEQ_TILE
    n_tok = nb * nt * tl
    col_spec = lambda rows: pl.BlockSpec((rows, tl), lambda b, l: (0, b * nt + l))
    sub = ROUTE_SUB_TOKENS
    row_i = lax.broadcasted_iota(jnp.int32, (sub, sub), 0)
    col_i = lax.broadcasted_iota(jnp.int32, (sub, sub), 1)
    tri = jnp.where(row_i < col_i, 1.0, 0.0).astype(BF16)
    return pl.pallas_call(
        functools.partial(_post_kernel, alpha),
        out_shape=(jax.ShapeDtypeStruct((nb, nt * tl, d), F32),
                   jax.ShapeDtypeStruct((nb, nt * tl, d // 2), U32),
                   jax.ShapeDtypeStruct((TOP_K, n_tok), jnp.int32),
                   jax.ShapeDtypeStruct((TOP_K, n_tok), F32),
                   jax.ShapeDtypeStruct((TOP_K, n_tok), jnp.int32),
                   jax.ShapeDtypeStruct((n_e, LANE), jnp.int32)),
        grid=(nb, nt),
        in_specs=[_tok_spec(dk), _full_spec(w.shape), _full_spec(bias.shape), _tok_spec(d),
                  _mod_spec(d, nb, ntx), _full_spec(lng.shape), _full_spec(lnb.shape),
                  _full_spec(rw_hi.shape), _full_spec(rw_lo.shape), _full_spec(rb.shape), _full_spec(tri.shape)],
        out_specs=(_tok_spec(d), _tok_spec(d // 2), col_spec(TOP_K), col_spec(TOP_K), col_spec(TOP_K),
                   _full_spec((n_e, LANE))),
        scratch_shapes=[pltpu.VMEM((n_e, LANE), F32)],
        compiler_params=_params(2),
        name="mixer_post",
    )(a, w, bias, x_all, mod, lng, lnb, rw_hi, rw_lo, rb, tri)


def _gmm_kernel(be_ref, nu_ref, xs_ref, w1_ref, w3_ref, w2_ref, ys_ref, w13_s, w2_s):
    i = pl.program_id(0)
    de = w2_ref.shape[2]
    changed = jnp.logical_or(i == 0, be_ref[i] != be_ref[jnp.maximum(i - 1, 0)])

    @pl.when(changed)
    def _():
        w13_s[:, :de] = w1_ref[0, 0].astype(BF16)
        w13_s[:, de:] = w3_ref[0, 0].astype(BF16)
        w2_s[...] = w2_ref[0, 0].astype(BF16)

    @pl.when(i < nu_ref[0])
    def _():
        for r0 in range(0, xs_ref.shape[0], MOE_SUB_ROWS):
            rows = slice(r0, r0 + MOE_SUB_ROWS)
            z = _dot_packed(xs_ref[rows, :], w13_s)
            hmid = (_silu(z[:, :de]) * z[:, de:]).astype(BF16)
            ys_ref[rows, :] = _pack_halves(_dot(hmid, w2_s[...]))


def _grouped_ffn(block_expert, n_used, xs, w1, w3, w2, layer):
    n_rows, dp = xs.shape
    d = 2 * dp
    bm = MOE_BLOCK_ROWS
    de = w2.shape[2]
    grid_spec = pltpu.PrefetchScalarGridSpec(
        num_scalar_prefetch=2,
        grid=(n_rows // bm,),
        in_specs=[pl.BlockSpec((bm, dp), lambda i, be, nu: (i, 0)),
                  pl.BlockSpec((1, 1, d, de), lambda i, be, nu: (layer, be[i], 0, 0)),
                  pl.BlockSpec((1, 1, d, de), lambda i, be, nu: (layer, be[i], 0, 0)),
                  pl.BlockSpec((1, 1, de, d), lambda i, be, nu: (layer, be[i], 0, 0))],
        out_specs=pl.BlockSpec((bm, dp), lambda i, be, nu: (i, 0)),
        scratch_shapes=[pltpu.VMEM((d, 2 * de), BF16), pltpu.VMEM((de, d), BF16)],
    )
    return pl.pallas_call(
        _gmm_kernel,
        out_shape=jax.ShapeDtypeStruct((n_rows, dp), U32),
        grid_spec=grid_spec,
        compiler_params=_params(1),
        name="moe_grouped_ffn",
    )(block_expert, n_used, xs, w1, w3, w2)


def _sc_mesh():
    return plsc.VectorSubcoreMesh(core_axis_name="c", subcore_axis_name="s",
                                  num_cores=SC_CORES, num_subcores=SC_SUBCORES)


def _sc_worker():
    return lax.axis_index("s") * SC_CORES + lax.axis_index("c")


def _sc_window(rows_per_worker):
    for win in SC_WINDOWS:
        if rows_per_worker % (2 * win) == 0:
            return win
    raise ValueError(f"no SparseCore window divides {rows_per_worker} rows per worker")


def _sc_row_scatter(src, dest, n_out):
    n_k, n_tok = dest.shape
    dp = src.shape[1]
    assert n_tok % SC_WORKERS == 0
    per_w = n_tok // SC_WORKERS
    win = _sc_window(per_w)
    n_win = per_w // win
    idx = dest.reshape(n_k, SC_WORKERS, n_win, win).transpose(1, 2, 0, 3)

    def body(src_hbm, idx_hbm, out_hbm, idx_v, rows_v, load_sem, scat_sem):
        wid = _sc_worker()
        base = wid * per_w
        pltpu.sync_copy(idx_hbm.at[wid], idx_v)

        def load(g, slot):
            return pltpu.make_async_copy(src_hbm.at[pl.ds(base + g * win, win)], rows_v.at[slot], load_sem.at[slot])

        def scatter(g, slot, k):
            return pltpu.make_async_copy(rows_v.at[slot], out_hbm.at[idx_v.at[g, k]], scat_sem.at[slot])

        load(0, 0).start()

        @pl.loop(0, n_win, step=2)
        def _(g):
            for slot in range(2):
                cur = g + slot
                load(cur, slot).wait()

                @pl.when(cur + 1 < n_win)
                def _():
                    @pl.when(cur >= 1)
                    def _():
                        for k in range(n_k):
                            scatter(cur - 1, 1 - slot, k).wait()
                    load(cur + 1, 1 - slot).start()

                for k in range(n_k):
                    scatter(cur, slot, k).start()

        for slot in range(2):
            for k in range(n_k):
                scatter(n_win - 2 + slot, slot, k).wait()

    return pl.kernel(
        body, mesh=_sc_mesh(),
        out_type=jax.ShapeDtypeStruct((n_out, dp), src.dtype),
        scratch_types=[pltpu.VMEM((n_win, n_k, win), jnp.int32),
                       pltpu.VMEM((2, win, dp), src.dtype),
                       pltpu.SemaphoreType.DMA((2,)),
                       pltpu.SemaphoreType.DMA((2,))],
        compiler_params=pltpu.CompilerParams(use_tc_tiling_on_sc=True),
        name="sc_dispatch_scatter",
    )(src, idx)


def _sc_combine(table, dest, gates):
    n_k, n_tok = dest.shape
    dp = table.shape[1]
    d = 2 * dp
    lanes = SC_LANES
    assert n_tok % SC_WORKERS == 0
    per_w = n_tok // SC_WORKERS
    tok_win = SC_COMBINE_TOKENS
    rows = tok_win * n_k
    n_win = per_w // tok_win
    assert per_w % (2 * tok_win) == 0 and rows <= 128
    idx = dest.T.reshape(-1)
    gate_rows = jnp.broadcast_to(gates.T.reshape(-1, 1), (n_tok * n_k, lanes))

    def body(table_hbm, idx_hbm, gate_hbm, out_hbm, idx_v, rows_v, gate_v, out_v, gather_sem, gate_sem, put_sem):
        wid = _sc_worker()
        tok0 = wid * per_w
        pltpu.sync_copy(idx_hbm.at[pl.ds(tok0 * n_k, per_w * n_k)], idx_v)

        def gather(g, slot):
            return pltpu.make_async_copy(table_hbm.at[idx_v.at[pl.ds(g * rows, rows)]], rows_v.at[slot],
                                         gather_sem.at[slot])

        def load_gates(g, slot):
            return pltpu.make_async_copy(gate_hbm.at[pl.ds((tok0 + g * tok_win) * n_k, rows)], gate_v.at[slot],
                                         gate_sem.at[slot])

        def put(g, slot):
            return pltpu.make_async_copy(out_v.at[slot], out_hbm.at[pl.ds(tok0 + g * tok_win, tok_win)],
                                         put_sem.at[slot])

        def reduce_window(slot):
            @pl.loop(0, tok_win)
            def _(t):
                g_k = [gate_v[slot, t * n_k + k, :] for k in range(n_k)]

                @pl.loop(0, dp // lanes)
                def _(v):
                    words = pl.ds(v * lanes, lanes)
                    acc_lo = jnp.zeros((lanes,), F32)
                    acc_hi = jnp.zeros((lanes,), F32)
                    for k in range(n_k):
                        w = rows_v[slot, t * n_k + k, words]
                        acc_lo = acc_lo + g_k[k] * plsc.bitcast(w << 16, F32)
                        acc_hi = acc_hi + g_k[k] * plsc.bitcast(w & HIGH_HALF_MASK, F32)
                    out_v[slot, t, words] = acc_lo
                    out_v[slot, t, pl.ds(dp + v * lanes, lanes)] = acc_hi

        gather(0, 0).start()
        load_gates(0, 0).start()

        @pl.loop(0, n_win, step=2)
        def _(g):
            for slot in range(2):
                cur = g + slot
                gather(cur, slot).wait()
                load_gates(cur, slot).wait()

                @pl.when(cur + 1 < n_win)
                def _():
                    gather(cur + 1, 1 - slot).start()
                    load_gates(cur + 1, 1 - slot).start()

                @pl.when(cur >= 2)
                def _():
                    put(cur - 2, slot).wait()

                reduce_window(slot)
                put(cur, slot).start()

        for slot in range(2):
            put(n_win - 2 + slot, slot).wait()

    return pl.kernel(
        body, mesh=_sc_mesh(),
        out_type=jax.ShapeDtypeStruct((n_tok, d), F32),
        scratch_types=[pltpu.VMEM((per_w * n_k,), jnp.int32),
                       pltpu.VMEM((2, rows, dp), table.dtype),
                       pltpu.VMEM((2, rows, lanes), F32),
                       pltpu.VMEM((2, tok_win, d), F32),
                       pltpu.SemaphoreType.DMA((2,)),
                       pltpu.SemaphoreType.DMA((2,)),
                       pltpu.SemaphoreType.DMA((2,))],
        compiler_params=pltpu.CompilerParams(use_tc_tiling_on_sc=True, needs_layout_passes=False),
        name="sc_combine_reduce",
    )(table, idx, gate_rows)


def _moe_out_kernel(alpha, x_ref, h2_ref, r_ref, mod_ref, ws13_ref, ws2_ref, lng_ref, lnb_ref, o_ref):
    de = ws2_ref.shape[0]
    z = _dot_packed(h2_ref[0], ws13_ref)
    hmid = (_silu(z[:, :de]) * z[:, de:]).astype(BF16)
    y = _dot(hmid, ws2_ref[...]) + r_ref[0]
    o_ref[0] = _layer_norm(alpha * x_ref[0] + mod_ref[0, 5:6, :] * y, lng_ref[...], lnb_ref[...])


def _moe_out(alpha, xn, h2, routed, mod, ws13, ws2, lng, lnb, nt, ntx):
    nb, _, d = xn.shape
    return pl.pallas_call(
        functools.partial(_moe_out_kernel, alpha),
        out_shape=jax.ShapeDtypeStruct((nb, nt * SEQ_TILE, d), F32),
        grid=(nb, nt),
        in_specs=[_tok_spec(d), _tok_spec(d // 2), _tok_spec(d), _mod_spec(d, nb, ntx),
                  _full_spec(ws13.shape), _full_spec(ws2.shape), _full_spec(lng.shape), _full_spec(lnb.shape)],
        out_specs=_tok_spec(d),
        compiler_params=_params(2),
        name="moe_out",
    )(xn, h2, routed, mod, ws13, ws2, lng, lnb)


def _dispatch_plan(counts, topi, rank, n_tok):
    n_e = counts.shape[0]
    bm = MOE_BLOCK_ROWS
    padded = (counts + bm - 1) // bm * bm
    pad_end = jnp.cumsum(padded)
    pad_start = pad_end - padded
    onehot = topi[:, None, :] == jnp.arange(n_e, dtype=jnp.int32)[None, :, None]
    dest = rank + jnp.sum(jnp.where(onehot, pad_start[None, :, None], 0), axis=1)
    n_blocks = n_tok * TOP_K // bm + n_e
    block_start = jnp.arange(n_blocks, dtype=jnp.int32) * bm
    block_expert = jnp.sum((pad_end[None, :] <= block_start[:, None]).astype(jnp.int32), axis=1)
    block_expert = jnp.minimum(block_expert, n_e - 1)
    n_used = (pad_end[-1] // bm).astype(jnp.int32).reshape(1)
    return dest.astype(jnp.int32), block_expert, n_used, n_blocks * bm


def _rope_tables(seq):
    n_freq = QK_ROPE // 4
    inv_freq = ROPE_THETA ** (-jnp.arange(n_freq, dtype=F32) / n_freq)
    pos = jnp.arange(seq, dtype=jnp.int32)
    r = (pos // GRID_W).astype(F32)
    col = (pos % GRID_W).astype(F32)
    ang = jnp.concatenate([r[:, None] * inv_freq, col[:, None] * inv_freq], -1)
    cos, sin = jnp.cos(ang), jnp.sin(ang)
    zeros = jnp.zeros((seq, LANE - QK_ROPE), F32)
    cos_slot = jnp.concatenate([cos, cos, zeros], -1)
    sin_slot = jnp.concatenate([-sin, sin, zeros], -1)
    return cos_slot, sin_slot


def _rope_slot_weights(w_rope):
    k, n, _ = w_rope.shape
    half = QK_ROPE // 2
    swapped = jnp.concatenate([w_rope[..., half:], w_rope[..., :half]], -1)
    pad = jnp.zeros((k, n, LANE - QK_ROPE), w_rope.dtype)
    plain = jnp.concatenate([w_rope, pad], -1).reshape(k, n * LANE)
    swp = jnp.concatenate([swapped, pad], -1).reshape(k, n * LANE)
    return jnp.concatenate([plain, swp], -1)


def kernel(x, c, ctx, c_ctx, ada_w, ada_b, ln_g, ln_b, conf_w1, conf_b1, conf_dw, conf_dwb, conf_ng, conf_nb, conf_w2, conf_b2, sc_w_in, sc_dw, sc_w_out, mla_w_dqkv, mla_q_g, mla_kv_g, mla_w_uq, mla_w_uk, mla_w_uv, mla_w_o, moe_router, moe_bias, moe_w1, moe_w3, moe_w2, sh_w1, sh_w3, sh_w2):
    nb, seq, d = x.shape
    l_ctx = ctx.shape[1]
    depth = ada_w.shape[0]
    alpha = (2.0 * depth) ** 0.25
    tl = SEQ_TILE
    assert seq % tl == 0 and l_ctx % tl == 0 and seq % Q_TILE == 0 and seq % GRID_W == 0
    ntx = seq // tl
    nt_all = (seq + l_ctx) // tl
    attn_layers = [i for i in range(depth) if i % N_MIXERS == 2]
    last_ctx_reader = attn_layers[-1] if attn_layers else -1

    rows = -(-(nb + 1) // 8) * 8
    c_all = jnp.zeros((rows, d), F32).at[:nb].set(c).at[nb].set(c_ctx)
    mod_all = _modulation(c_all, ada_w, ada_b).reshape(depth, rows, N_MOD, d)

    assert nb % N_CHAINS == 0
    nbc = nb // N_CHAINS
    chains = [jnp.concatenate([x[c0:c0 + nbc], ctx[c0:c0 + nbc]], axis=1) for c0 in range(0, nb, nbc)]
    q_rank, kv_rank = mla_q_g.shape[1], mla_kv_g.shape[1]
    cos_t, sin_t = _rope_tables(seq)
    row = lambda v: v.reshape(1, -1)

    for i in range(depth):
        need_ctx = i < last_ctx_reader
        kind, j = i % N_MIXERS, i // N_MIXERS
        nt = nt_all if need_ctx else ntx
        n_tok = nbc * nt * tl

        if kind == 0:
            w_first = conf_w1[j].astype(BF16)
            dw_tiles = jnp.broadcast_to(conf_dw[j][:, None, :], (conf_dw.shape[1], SUBLANES, d))
            w_last, b_last = conf_w2[j].astype(BF16), row(conf_b2[j])
        elif kind == 1:
            w_first = sc_w_in[j].astype(BF16)
            w_last, b_last = sc_w_out[j].astype(BF16), jnp.zeros((1, d), F32)
        else:
            wdq = mla_w_dqkv[j]
            wd = wdq[:, :q_rank + kv_rank].astype(BF16)
            wkp = _rope_slot_weights(wdq[:, None, q_rank + kv_rank:]).astype(BF16)
            wuq = mla_w_uq[j].reshape(q_rank, MLA_HEADS, QK_NOPE + QK_ROPE)
            wqn = wuq[:, :, :QK_NOPE].reshape(q_rank, MLA_HEADS * QK_NOPE).astype(BF16)
            wqr = _rope_slot_weights(wuq[:, :, QK_NOPE:]).astype(BF16)
            wuk, wuv = mla_w_uk[j].astype(BF16), mla_w_uv[j].astype(BF16)
            w_last, b_last = mla_w_o[j].astype(BF16), jnp.zeros((1, d), F32)
        rw_hi, rw_lo = _split_bf16(moe_router[i].T)
        ws13 = jnp.concatenate([sh_w1[i], sh_w3[i]], axis=-1).astype(BF16)
        ws2 = sh_w2[i].astype(BF16)

        for ci in range(N_CHAINS):
            x_all = chains[ci]
            mod = jnp.concatenate([mod_all[i, ci * nbc:(ci + 1) * nbc], mod_all[i, nb:nb + 1]], axis=0)

            if kind == 0:
                u = _conf_in(x_all, mod, w_first, row(conf_b1[j]), nt, ntx)
                a = _conf_conv(u, dw_tiles, row(conf_dwb[j]), row(conf_ng[j]), row(conf_nb[j]), nt, ntx)
            elif kind == 1:
                a = _sc_conv(x_all, mod, w_first, sc_dw[j], nt, ntx)
            else:
                q, k, v = _mla_proj(x_all, mod, wd, wkp, row(mla_q_g[j]), row(mla_kv_g[j]), wqn, wqr,
                                    wuk, wuv, cos_t, sin_t, nt_all, ntx)
                a = _attention(q, k, v, nt * tl)

            xn, h2, topi, topw, rank, counts = _post(alpha, a, w_last, b_last, x_all, mod, row(ln_g[i, 0]),
                                                     row(ln_b[i, 0]), rw_hi, rw_lo, moe_bias[i].reshape(-1, 1),
                                                     nt, ntx)

            dest, block_expert, n_used, n_rows = _dispatch_plan(counts[:, 0], topi, rank, n_tok)
            xs = _sc_row_scatter(h2.reshape(n_tok, d // 2), dest, n_rows)
            ys = _grouped_ffn(block_expert, n_used, xs, moe_w1, moe_w3, moe_w2, i)
            routed = _sc_combine(ys, dest, topw).reshape(nbc, nt * tl, d)

            chains[ci] = _moe_out(alpha, xn, h2, routed, mod, ws13, ws2,
                                  row(ln_g[i, 1]), row(ln_b[i, 1]), nt, ntx)
    return jnp.concatenate([xc[:, :seq] for xc in chains], axis=0)
```

```python
import functools

import numpy as np
import jax
import jax.numpy as jnp
from jax import lax
from jax.experimental import pallas as pl
from jax.experimental.pallas import tpu as pltpu
from jax.experimental.pallas import tpu_sc as plsc

F32 = jnp.float32
BF16 = jnp.bfloat16
U32 = jnp.uint32
HIGH_HALF_MASK = np.uint32(0xFFFF0000)

GRID_W = 64
N_MIXERS = 3
LN_EPS = 1e-5
RMS_EPS = 1e-6
N_MOD = 6
MLA_HEADS = 8
QK_NOPE = 128
QK_ROPE = 64
V_HEAD = 128
ROPE_THETA = 10000.0
ATTN_SCALE = (QK_NOPE + QK_ROPE) ** -0.5
Q_SCALE = ATTN_SCALE * 1.4426950408889634
TOP_K = 8
N_GROUPS = 8
TOPK_GROUPS = 4
ROUTED_SCALE = 2.5

SEQ_TILE = 256
CONV_HALO = 16
SHORT_HALO = 8
CONV_ROW_CHUNK = 64
LANE = 128
SUBLANES = 8
Q_TILE = 1024
ATTN_SUB_ROWS = 256
ATTN_ROW_CHUNK = 16
MOE_BLOCK_ROWS = 1024
MOE_SUB_ROWS = 1024
ROUTE_SUB_TOKENS = 128
MOD_COL_TILE = 1536
VMEM_LIMIT = 48 * 1024 * 1024
SC_CORES = 2
SC_SUBCORES = 16
SC_WORKERS = SC_CORES * SC_SUBCORES
SC_LANES = 16
SC_COMBINE_UNROLL = 4
SC_COMBINE_TOKENS = 8
SC_WINDOWS = (64, 32, 16)
N_CHAINS = 1


def _params(n_axes):
    return pltpu.CompilerParams(dimension_semantics=("arbitrary",) * n_axes,
                                vmem_limit_bytes=VMEM_LIMIT)


def _split_bf16(a):
    hi = a.astype(BF16)
    lo = (a - hi.astype(F32)).astype(BF16)
    return hi, lo


def _dot(a, b):
    return jnp.dot(a, b, preferred_element_type=F32)


def _dot_nt(a, b):
    return lax.dot_general(a, b, (((1,), (1,)), ((), ())), preferred_element_type=F32)


def _pack_halves(v):
    half = v.shape[-1] // 2
    lo = lax.bitcast_convert_type(v[:, :half].astype(BF16).astype(F32), U32) >> 16
    hi = lax.bitcast_convert_type(v[:, half:].astype(BF16).astype(F32), U32) & HIGH_HALF_MASK
    return hi | lo


def _unpack_halves(p):
    lo = lax.bitcast_convert_type(p << 16, F32)
    hi = lax.bitcast_convert_type(p & HIGH_HALF_MASK, F32)
    return lo, hi


def _dot_packed(p, w):
    lo, hi = _unpack_halves(p)
    return _dot(jnp.concatenate([lo.astype(BF16), hi.astype(BF16)], axis=1), w[...])


def _layer_norm(v, g, b):
    mu = jnp.mean(v, axis=-1, keepdims=True)
    c = v - mu
    var = jnp.mean(c * c, axis=-1, keepdims=True)
    return c * lax.rsqrt(var + LN_EPS) * g + b


def _silu(v):
    return v * jax.nn.sigmoid(v)


def _mod_kernel(c_ref, w_ref, b_ref, o_ref):
    a = _silu(c_ref[...])
    a_hi, a_lo = _split_bf16(a)
    w_hi, w_lo = _split_bf16(w_ref[0])
    o_ref[0] = _dot(a_hi, w_hi) + _dot(a_hi, w_lo) + _dot(a_lo, w_hi) + b_ref[0]


def _modulation(c_all, ada_w, ada_b):
    depth, d, n = ada_w.shape
    rows = c_all.shape[0]
    tn = MOD_COL_TILE
    return pl.pallas_call(
        _mod_kernel,
        out_shape=jax.ShapeDtypeStruct((depth, rows, n), F32),
        grid=(depth, n // tn),
        in_specs=[pl.BlockSpec((rows, d), lambda i, j: (0, 0)),
                  pl.BlockSpec((1, d, tn), lambda i, j: (i, 0, j)),
                  pl.BlockSpec((1, 1, tn), lambda i, j: (i, 0, j))],
        out_specs=pl.BlockSpec((1, rows, tn), lambda i, j: (i, 0, j)),
        compiler_params=_params(2),
        name="adaln_mod",
    )(c_all, ada_w, ada_b.reshape(depth, 1, n))


def _tok_spec(d, tl=SEQ_TILE):
    return pl.BlockSpec((1, tl, d), lambda b, l: (b, l, 0))


def _mod_spec(d, n_batch, ntx):
    return pl.BlockSpec((1, N_MOD, d), lambda b, l: (jnp.where(l < ntx, b, n_batch), 0, 0))


def _full_spec(shape):
    zeros = (0,) * len(shape)
    return pl.BlockSpec(shape, lambda b, l: zeros)


def _halo_specs(d, halo, seq_len, tl=SEQ_TILE):
    per_tile = tl // halo
    last = seq_len // halo - 1
    prev = pl.BlockSpec((1, halo, d), lambda b, l: (b, jnp.maximum(l * per_tile - 1, 0), 0))
    nxt = pl.BlockSpec((1, halo, d), lambda b, l: (b, jnp.minimum((l + 1) * per_tile, last), 0))
    return prev, nxt


def _segment_edges(l, ntx, nt):
    first = jnp.logical_or(l == 0, l == ntx)
    last = jnp.logical_or(l == ntx - 1, l == nt - 1)
    return first, last


def _window_rows(prev_ref, cur_ref, next_ref, win_ref):
    halo, tl = prev_ref.shape[1], cur_ref.shape[1]
    win_ref[0:halo, :] = prev_ref[0]
    win_ref[halo:halo + tl, :] = cur_ref[0]
    win_ref[halo + tl:, :] = next_ref[0]
    return win_ref[...]


def _edge_mask(n_rows, halo, first, last):
    r = lax.broadcasted_iota(jnp.int32, (n_rows, 1), 0)
    outside = jnp.logical_or(jnp.logical_and(first, r < halo), jnp.logical_and(last, r >= n_rows - halo))
    return jnp.where(outside, 0.0, 1.0)


def _conf_in_kernel(x_ref, mod_ref, w1_ref, b1_ref, u_ref):
    d = x_ref.shape[-1]
    h = x_ref[0] * (1.0 + mod_ref[0, 1:2, :]) + mod_ref[0, 0:1, :]
    z = _dot(h.astype(BF16), w1_ref[...]) + b1_ref[...]
    u_ref[0] = z[:, :d] * jax.nn.sigmoid(z[:, d:])


def _conf_in(x_all, mod, w1, b1, nt, ntx):
    nb, seq_len, d = x_all.shape
    return pl.pallas_call(
        _conf_in_kernel,
        out_shape=jax.ShapeDtypeStruct((nb, nt * SEQ_TILE, d), F32),
        grid=(nb, nt),
        in_specs=[_tok_spec(d), _mod_spec(d, nb, ntx), _full_spec(w1.shape), _full_spec(b1.shape)],
        out_specs=_tok_spec(d),
        compiler_params=_params(2),
        name="conf_in",
    )(x_all, mod, w1, b1)


def _conf_conv_kernel(ntx, nt, up_ref, uc_ref, un_ref, dw_ref, dwb_ref, ng_ref, nb_ref,
                      a_ref, sh_ref, conv_ref):
    tl, d = uc_ref.shape[1], uc_ref.shape[2]
    taps = dw_ref.shape[0]
    lead = CONV_HALO - (taps - 1) // 2
    first, last = _segment_edges(pl.program_id(1), ntx, nt)
    sh_ref[0, 0:CONV_HALO, :] = jnp.where(first, 0.0, up_ref[0])
    sh_ref[0, CONV_HALO:CONV_HALO + tl, :] = uc_ref[0]
    sh_ref[0, CONV_HALO + tl:, :] = jnp.where(last, 0.0, un_ref[0])
    span = tl + 2 * CONV_HALO - SUBLANES
    for s in range(1, SUBLANES):
        sh_ref[s, 0:span, :] = sh_ref[0, s:s + span, :]
    groups = CONV_ROW_CHUNK // SUBLANES

    def row_chunk(i, carry):
        r0 = pl.multiple_of(i * CONV_ROW_CHUNK, CONV_ROW_CHUNK)
        for c0 in range(0, d, LANE):
            accs = [jnp.zeros((SUBLANES, LANE), F32) for _ in range(groups)]
            for k in range(taps):
                res = (lead + k) % SUBLANES
                off = lead + k - res
                w = dw_ref[k, :, c0:c0 + LANE]
                for g in range(groups):
                    lo = r0 + (off + g * SUBLANES)
                    accs[g] = accs[g] + w * sh_ref[res, pl.ds(lo, SUBLANES), c0:c0 + LANE]
            for g in range(groups):
                conv_ref[pl.ds(r0 + g * SUBLANES, SUBLANES), c0:c0 + LANE] = accs[g]
        return carry

    lax.fori_loop(0, tl // CONV_ROW_CHUNK, row_chunk, 0)
    v = _layer_norm(conv_ref[...] + dwb_ref[...], ng_ref[...], nb_ref[...])
    a_ref[0] = _silu(v).astype(BF16)


def _conf_conv(u, dw, dwb, ng, nb_, nt, ntx):
    nb, seq_len, d = u.shape
    prev, nxt = _halo_specs(d, CONV_HALO, seq_len)
    return pl.pallas_call(
        functools.partial(_conf_conv_kernel, ntx, nt),
        out_shape=jax.ShapeDtypeStruct((nb, seq_len, d), BF16),
        grid=(nb, nt),
        in_specs=[prev, _tok_spec(d), nxt, _full_spec(dw.shape), _full_spec(dwb.shape),
                  _full_spec(ng.shape), _full_spec(nb_.shape)],
        out_specs=_tok_spec(d),
        scratch_shapes=[pltpu.VMEM((SUBLANES, SEQ_TILE + 2 * CONV_HALO, d), F32),
                        pltpu.VMEM((SEQ_TILE, d), F32)],
        compiler_params=_params(2),
        name="conf_conv",
    )(u, u, u, dw, dwb, ng, nb_)


def _sc_conv_kernel(ntx, nt, xp_ref, xc_ref, xn_ref, mod_ref, w_ref, dw_ref, a_ref, win_ref, ext_ref):
    tl, d = xc_ref.shape[1], xc_ref.shape[2]
    taps = dw_ref.shape[0]
    lead = SHORT_HALO - (taps - 1) // 2
    first, last = _segment_edges(pl.program_id(1), ntx, nt)
    xw = _window_rows(xp_ref, xc_ref, xn_ref, win_ref)
    h = xw * (1.0 + mod_ref[0, 1:2, :]) + mod_ref[0, 0:1, :]
    z = _dot(h.astype(BF16), w_ref[...])
    gb = z[SHORT_HALO:SHORT_HALO + tl, :d]
    ext_ref[...] = z[:, d:2 * d] * z[:, 2 * d:] * _edge_mask(z.shape[0], SHORT_HALO, first, last)
    acc = dw_ref[0:1, :] * ext_ref[lead:lead + tl, :]
    for k in range(1, taps):
        acc = acc + dw_ref[k:k + 1, :] * ext_ref[lead + k:lead + k + tl, :]
    a_ref[0] = (gb * acc).astype(BF16)


def _sc_conv(x_all, mod, w_in, dw, nt, ntx):
    nb, seq_len, d = x_all.shape
    prev, nxt = _halo_specs(d, SHORT_HALO, seq_len)
    win = pltpu.VMEM((SEQ_TILE + 2 * SHORT_HALO, d), F32)
    return pl.pallas_call(
        functools.partial(_sc_conv_kernel, ntx, nt),
        out_shape=jax.ShapeDtypeStruct((nb, nt * SEQ_TILE, d), BF16),
        grid=(nb, nt),
        in_specs=[prev, _tok_spec(d), nxt, _mod_spec(d, nb, ntx), _full_spec(w_in.shape), _full_spec(dw.shape)],
        out_specs=_tok_spec(d),
        scratch_shapes=[win, win],
        compiler_params=_params(2),
        name="sc_conv",
    )(x_all, x_all, x_all, mod, w_in, dw)


def _rms(v, g):
    return v * lax.rsqrt(jnp.mean(v * v, axis=-1, keepdims=True) + RMS_EPS) * g


def _mla_proj_kernel(ntx, q_rank,
                     x_ref, mod_ref, wd_ref, wkp_ref, qg_ref, kvg_ref, wqn_ref, wqr_ref, wuk_ref, wuv_ref,
                     cos_ref, sin_ref, q_ref, k_ref, v_ref):
    is_latent = pl.program_id(1) < ntx
    h = (x_ref[0] * (1.0 + mod_ref[0, 1:2, :]) + mod_ref[0, 0:1, :]).astype(BF16)
    dn = _dot(h, wd_ref[...])
    cq = _rms(dn[:, :q_rank], qg_ref[...]).astype(BF16)
    ckv = _rms(dn[:, q_rank:], kvg_ref[...]).astype(BF16)
    cos = jnp.where(is_latent, cos_ref[...], 1.0)
    sin = jnp.where(is_latent, sin_ref[...], 0.0)
    kp2 = _dot(h, wkp_ref[...])
    kp = kp2[:, :LANE] * cos + kp2[:, LANE:] * sin
    kn = _dot(ckv, wuk_ref[...])
    vv = _dot(ckv, wuv_ref[...])
    qn = _dot(cq, wqn_ref[...]) * Q_SCALE
    qr2 = _dot(cq, wqr_ref[...])
    hw = MLA_HEADS * LANE
    for hd in range(MLA_HEADS):
        sl = slice(hd * LANE, (hd + 1) * LANE)
        qr = (qr2[:, sl] * cos + qr2[:, hw + hd * LANE:hw + (hd + 1) * LANE] * sin) * Q_SCALE
        q_ref[0, hd, :, 0:LANE] = qn[:, sl].astype(BF16)
        q_ref[0, hd, :, LANE:] = qr.astype(BF16)
        k_ref[0, hd, :, 0:LANE] = kn[:, sl].astype(BF16)
        k_ref[0, hd, :, LANE:] = kp.astype(BF16)
        v_ref[0, hd] = vv[:, sl].astype(BF16)


def _mla_proj(x_all, mod, wd, wkp, qg, kvg, wqn, wqr, wuk, wuv, cos_t, sin_t, nt, ntx):
    nb, seq_len, d = x_all.shape
    q_rank = qg.shape[-1]
    tl = SEQ_TILE
    rope_spec = pl.BlockSpec((tl, LANE), lambda b, l: (jnp.minimum(l, ntx - 1), 0))
    qk_shape = jax.ShapeDtypeStruct((nb, MLA_HEADS, nt * tl, 2 * LANE), BF16)
    v_shape = jax.ShapeDtypeStruct((nb, MLA_HEADS, nt * tl, LANE), BF16)
    qk_spec = pl.BlockSpec((1, MLA_HEADS, tl, 2 * LANE), lambda b, l: (b, 0, l, 0))
    v_spec = pl.BlockSpec((1, MLA_HEADS, tl, LANE), lambda b, l: (b, 0, l, 0))
    return pl.pallas_call(
        functools.partial(_mla_proj_kernel, ntx, q_rank),
        out_shape=(qk_shape, qk_shape, v_shape),
        grid=(nb, nt),
        in_specs=[_tok_spec(d), _mod_spec(d, nb, ntx), _full_spec(wd.shape), _full_spec(wkp.shape),
                  _full_spec(qg.shape), _full_spec(kvg.shape), _full_spec(wqn.shape), _full_spec(wqr.shape),
                  _full_spec(wuk.shape), _full_spec(wuv.shape), rope_spec, rope_spec],
        out_specs=(qk_spec, qk_spec, v_spec),
        compiler_params=_params(2),
        name="mla_proj",
    )(x_all, mod, wd, wkp, qg, kvg, wqn, wqr, wuk, wuv, cos_t, sin_t)


def _attn_kernel(q_ref, k_ref, v_ref, o_ref, s_ref, p_ref, l_ref):
    tq = q_ref.shape[2]
    for r0 in range(0, tq, ATTN_SUB_ROWS):
        rows = slice(r0, r0 + ATTN_SUB_ROWS)
        s_ref[rows, :] = _dot_nt(q_ref[0, 0, rows, :], k_ref[0, 0])
        for c0 in range(r0, r0 + ATTN_SUB_ROWS, ATTN_ROW_CHUNK):
            chunk = slice(c0, c0 + ATTN_ROW_CHUNK)
            s = s_ref[chunk, :]
            p = jnp.exp2(s - jnp.max(s, axis=-1, keepdims=True))
            l_ref[chunk, :] = jnp.broadcast_to(jnp.sum(p, axis=-1, keepdims=True), (ATTN_ROW_CHUNK, V_HEAD))
            p_ref[chunk, :] = p.astype(BF16)
        o = _dot(p_ref[rows, :], v_ref[0, 0])
        o_ref[0, rows, :] = (o / l_ref[rows, :]).astype(BF16)


def _attention(q, k, v, n_q):
    nb, nh, n_k, dk = k.shape
    tq = Q_TILE
    return pl.pallas_call(
        _attn_kernel,
        out_shape=jax.ShapeDtypeStruct((nb, n_q, nh * V_HEAD), BF16),
        grid=(nb, nh, n_q // tq),
        in_specs=[pl.BlockSpec((1, 1, tq, dk), lambda b, h, i: (b, h, i, 0)),
                  pl.BlockSpec((1, 1, n_k, dk), lambda b, h, i: (b, h, 0, 0)),
                  pl.BlockSpec((1, 1, n_k, V_HEAD), lambda b, h, i: (b, h, 0, 0))],
        out_specs=pl.BlockSpec((1, tq, V_HEAD), lambda b, h, i: (b, i, h)),
        scratch_shapes=[pltpu.VMEM((tq, n_k), F32), pltpu.VMEM((tq, n_k), BF16), pltpu.VMEM((tq, V_HEAD), F32)],
        compiler_params=_params(3),
        name="mla_attn",
    )(q, k, v)


def _route(sel, s):
    n_e, n_t = sel.shape
    per = n_e // N_GROUPS
    sel3 = sel.reshape(N_GROUPS, per, n_t)
    s3 = s.reshape(N_GROUPS, per, n_t)
    iota_p = lax.broadcasted_iota(jnp.int32, (N_GROUPS, per, n_t), 1).astype(F32)
    iota_g = lax.broadcasted_iota(jnp.int32, (N_GROUPS, 1, n_t), 0).astype(F32)
    neg = -jnp.inf
    m1 = jnp.max(sel3, axis=1, keepdims=True)
    first = jnp.min(jnp.where(sel3 == m1, iota_p, float(per)), axis=1, keepdims=True)
    m2 = jnp.max(jnp.where(iota_p == first, neg, sel3), axis=1, keepdims=True)
    gs = m1 + m2
    gsel = jnp.zeros((N_GROUPS, 1, n_t), F32)
    for _ in range(TOPK_GROUPS):
        gm = jnp.max(gs, axis=0, keepdims=True)
        gfirst = jnp.min(jnp.where(gs == gm, iota_g, float(N_GROUPS)), axis=0, keepdims=True)
        pick = iota_g == gfirst
        gsel = jnp.where(pick, 1.0, gsel)
        gs = jnp.where(pick, neg, gs)
    val = jnp.where(gsel > 0.0, sel3, neg)
    iota_e = lax.broadcasted_iota(jnp.int32, (N_GROUPS, per, n_t), 0).astype(F32) * per + iota_p
    chosen = jnp.zeros((N_GROUPS, per, n_t), F32)
    idx, wts = [], []
    for _ in range(TOP_K):
        m = jnp.max(jnp.max(val, axis=1, keepdims=True), axis=0, keepdims=True)
        e = jnp.min(jnp.min(jnp.where(val == m, iota_e, float(n_e)), axis=1, keepdims=True), axis=0, keepdims=True)
        pick = iota_e == e
        wts.append(jnp.sum(jnp.sum(jnp.where(pick, s3, 0.0), axis=1, keepdims=True), axis=0, keepdims=True))
        idx.append(e)
        chosen = jnp.where(pick, 1.0, chosen)
        val = jnp.where(pick, neg, val)
    total = wts[0]
    for w in wts[1:]:
        total = total + w
    wts = [w / total * ROUTED_SCALE for w in wts]
    return idx, wts, chosen, iota_e


def _post_kernel(alpha, a_ref, w_ref, b_ref, x_ref, mod_ref, lng_ref, lnb_ref, rw_hi_ref, rw_lo_ref, rb_ref, tri_ref,
                 xn_ref, h2_ref, topi_ref, topw_ref, rank_ref, cnt_ref, run_ref):
    first_step = jnp.logical_and(pl.program_id(0) == 0, pl.program_id(1) == 0)

    @pl.when(first_step)
    def _():
        run_ref[...] = jnp.zeros_like(run_ref)

    y = _dot(a_ref[0], w_ref[...]) + b_ref[...]
    run = run_ref[...]
    for t0 in range(0, y.shape[0], ROUTE_SUB_TOKENS):
        rows = slice(t0, t0 + ROUTE_SUB_TOKENS)
        xn = _layer_norm(alpha * x_ref[0, rows, :] + mod_ref[0, 2:3, :] * y[rows], lng_ref[...], lnb_ref[...])
        xn_ref[0, rows, :] = xn
        h2 = xn * (1.0 + mod_ref[0, 4:5, :]) + mod_ref[0, 3:4, :]
        h_hi, h_lo = _split_bf16(h2)
        h2_ref[0, rows, :] = _pack_halves(h2)
        logits = _dot_nt(rw_hi_ref[...], h_hi) + _dot_nt(rw_hi_ref[...], h_lo) + _dot_nt(rw_lo_ref[...], h_hi)
        s = jax.nn.sigmoid(logits)
        idx, wts, chosen3, iota_e = _route(s + rb_ref[...], s)
        n_e, n_t = s.shape
        chosen = chosen3.reshape(n_e, n_t)
        rank = run[:, 0:1] + _dot(chosen.astype(BF16), tri_ref[...])
        rank3 = rank.reshape(chosen3.shape)
        for k in range(TOP_K):
            rk = jnp.sum(jnp.sum(jnp.where(iota_e == idx[k], rank3, 0.0), axis=1, keepdims=True), axis=0,
                         keepdims=True)
            topi_ref[k:k + 1, rows] = idx[k].reshape(1, n_t).astype(jnp.int32)
            topw_ref[k:k + 1, rows] = wts[k].reshape(1, n_t)
            rank_ref[k:k + 1, rows] = rk.reshape(1, n_t).astype(jnp.int32)
        run = run + jnp.sum(chosen, axis=1, keepdims=True)
    run_ref[...] = run
    cnt_ref[...] = run.astype(jnp.int32)


def _post(alpha, a, w, bias, x_all, mod, lng, lnb, rw_hi, rw_lo, rb, nt, ntx):
    nb, _, dk = a.shape
    d = x_all.shape[-1]
    n_e = rw_hi.shape[0]
    tl = SEQ_TILE
    n_tok = nb * nt * tl
    col_spec = lambda rows: pl.BlockSpec((rows, tl), lambda b, l: (0, b * nt + l))
    sub = ROUTE_SUB_TOKENS
    row_i = lax.broadcasted_iota(jnp.int32, (sub, sub), 0)
    col_i = lax.broadcasted_iota(jnp.int32, (sub, sub), 1)
    tri = jnp.where(row_i < col_i, 1.0, 0.0).astype(BF16)
    return pl.pallas_call(
        functools.partial(_post_kernel, alpha),
        out_shape=(jax.ShapeDtypeStruct((nb, nt * tl, d), F32),
                   jax.ShapeDtypeStruct((nb, nt * tl, d // 2), U32),
                   jax.ShapeDtypeStruct((TOP_K, n_tok), jnp.int32),
                   jax.ShapeDtypeStruct((TOP_K, n_tok), F32),
                   jax.ShapeDtypeStruct((TOP_K, n_tok), jnp.int32),
                   jax.ShapeDtypeStruct((n_e, LANE), jnp.int32)),
        grid=(nb, nt),
        in_specs=[_tok_spec(dk), _full_spec(w.shape), _full_spec(bias.shape), _tok_spec(d),
                  _mod_spec(d, nb, ntx), _full_spec(lng.shape), _full_spec(lnb.shape),
                  _full_spec(rw_hi.shape), _full_spec(rw_lo.shape), _full_spec(rb.shape), _full_spec(tri.shape)],
        out_specs=(_tok_spec(d), _tok_spec(d // 2), col_spec(TOP_K), col_spec(TOP_K), col_spec(TOP_K),
                   _full_spec((n_e, LANE))),
        scratch_shapes=[pltpu.VMEM((n_e, LANE), F32)],
        compiler_params=_params(2),
        name="mixer_post",
    )(a, w, bias, x_all, mod, lng, lnb, rw_hi, rw_lo, rb, tri)


def _gmm_kernel(be_ref, nu_ref, xs_ref, w1_ref, w3_ref, w2_ref, ys_ref, w13_s, w2_s):
    i = pl.program_id(0)
    de = w2_ref.shape[2]
    changed = jnp.logical_or(i == 0, be_ref[i] != be_ref[jnp.maximum(i - 1, 0)])

    @pl.when(changed)
    def _():
        w13_s[:, :de] = w1_ref[0, 0].astype(BF16)
        w13_s[:, de:] = w3_ref[0, 0].astype(BF16)
        w2_s[...] = w2_ref[0, 0].astype(BF16)

    @pl.when(i < nu_ref[0])
    def _():
        for r0 in range(0, xs_ref.shape[0], MOE_SUB_ROWS):
            rows = slice(r0, r0 + MOE_SUB_ROWS)
            z = _dot_packed(xs_ref[rows, :], w13_s)
            hmid = (_silu(z[:, :de]) * z[:, de:]).astype(BF16)
            ys_ref[rows, :] = _pack_halves(_dot(hmid, w2_s[...]))


def _grouped_ffn(block_expert, n_used, xs, w1, w3, w2, layer):
    n_rows, dp = xs.shape
    d = 2 * dp
    bm = MOE_BLOCK_ROWS
    de = w2.shape[2]
    grid_spec = pltpu.PrefetchScalarGridSpec(
        num_scalar_prefetch=2,
        grid=(n_rows // bm,),
        in_specs=[pl.BlockSpec((bm, dp), lambda i, be, nu: (i, 0)),
                  pl.BlockSpec((1, 1, d, de), lambda i, be, nu: (layer, be[i], 0, 0)),
                  pl.BlockSpec((1, 1, d, de), lambda i, be, nu: (layer, be[i], 0, 0)),
                  pl.BlockSpec((1, 1, de, d), lambda i, be, nu: (layer, be[i], 0, 0))],
        out_specs=pl.BlockSpec((bm, dp), lambda i, be, nu: (i, 0)),
        scratch_shapes=[pltpu.VMEM((d, 2 * de), BF16), pltpu.VMEM((de, d), BF16)],
    )
    return pl.pallas_call(
        _gmm_kernel,
        out_shape=jax.ShapeDtypeStruct((n_rows, dp), U32),
        grid_spec=grid_spec,
        compiler_params=_params(1),
        name="moe_grouped_ffn",
    )(block_expert, n_used, xs, w1, w3, w2)


def _sc_mesh():
    return plsc.VectorSubcoreMesh(core_axis_name="c", subcore_axis_name="s",
                                  num_cores=SC_CORES, num_subcores=SC_SUBCORES)


def _sc_worker():
    return lax.axis_index("s") * SC_CORES + lax.axis_index("c")


def _sc_window(rows_per_worker):
    for win in SC_WINDOWS:
        if rows_per_worker % (2 * win) == 0:
            return win
    raise ValueError(f"no SparseCore window divides {rows_per_worker} rows per worker")


def _sc_row_scatter(src, dest, n_out):
    n_k, n_tok = dest.shape
    dp = src.shape[1]
    assert n_tok % SC_WORKERS == 0
    per_w = n_tok // SC_WORKERS
    win = _sc_window(per_w)
    n_win = per_w // win
    idx = dest.reshape(n_k, SC_WORKERS, n_win, win).transpose(1, 2, 0, 3)

    def body(src_hbm, idx_hbm, out_hbm, idx_v, rows_v, load_sem, scat_sem):
        wid = _sc_worker()
        base = wid * per_w
        pltpu.sync_copy(idx_hbm.at[wid], idx_v)

        def load(g, slot):
            return pltpu.make_async_copy(src_hbm.at[pl.ds(base + g * win, win)], rows_v.at[slot], load_sem.at[slot])

        def scatter(g, slot, k):
            return pltpu.make_async_copy(rows_v.at[slot], out_hbm.at[idx_v.at[g, k]], scat_sem.at[slot])

        load(0, 0).start()

        @pl.loop(0, n_win, step=2)
        def _(g):
            for slot in range(2):
                cur = g + slot
                load(cur, slot).wait()

                @pl.when(cur + 1 < n_win)
                def _():
                    @pl.when(cur >= 1)
                    def _():
                        for k in range(n_k):
                            scatter(cur - 1, 1 - slot, k).wait()
                    load(cur + 1, 1 - slot).start()

                for k in range(n_k):
                    scatter(cur, slot, k).start()

        for slot in range(2):
            for k in range(n_k):
                scatter(n_win - 2 + slot, slot, k).wait()

    return pl.kernel(
        body, mesh=_sc_mesh(),
        out_type=jax.ShapeDtypeStruct((n_out, dp), src.dtype),
        scratch_types=[pltpu.VMEM((n_win, n_k, win), jnp.int32),
                       pltpu.VMEM((2, win, dp), src.dtype),
                       pltpu.SemaphoreType.DMA((2,)),
                       pltpu.SemaphoreType.DMA((2,))],
        compiler_params=pltpu.CompilerParams(use_tc_tiling_on_sc=True),
        name="sc_dispatch_scatter",
    )(src, idx)


def _sc_combine(table, dest, gates):
    n_k, n_tok = dest.shape
    dp = table.shape[1]
    d = 2 * dp
    lanes = SC_LANES
    assert n_tok % SC_WORKERS == 0
    per_w = n_tok // SC_WORKERS
    tok_win = SC_COMBINE_TOKENS
    rows = tok_win * n_k
    n_win = per_w // tok_win
    assert per_w % (2 * tok_win) == 0 and rows <= 128
    idx = dest.T.reshape(-1)
    gate_rows = gates.T.reshape(-1)

    def body(table_hbm, idx_hbm, gate_hbm, out_hbm, idx_v, rows_v, gate_v, out_v, gather_sem, gate_sem, put_sem):
        wid = _sc_worker()
        tok0 = wid * per_w
        pltpu.sync_copy(idx_hbm.at[pl.ds(tok0 * n_k, per_w * n_k)], idx_v)

        def gather(g, slot):
            return pltpu.make_async_copy(table_hbm.at[idx_v.at[pl.ds(g * rows, rows)]], rows_v.at[slot],
                                         gather_sem.at[slot])

        def load_gates(g, slot):
            return pltpu.make_async_copy(gate_hbm.at[pl.ds((tok0 + g * tok_win) * n_k, rows)], gate_v.at[slot],
                                         gate_sem.at[slot])

        def put(g, slot):
            return pltpu.make_async_copy(out_v.at[slot], out_hbm.at[pl.ds(tok0 + g * tok_win, tok_win)],
                                         put_sem.at[slot])

        def reduce_window(slot):
            @pl.loop(0, tok_win)
            def _(t):
                slot_idx = jnp.full((lanes,), slot, jnp.int32)
                zero_idx = jnp.zeros((lanes,), jnp.int32)
                g_k = [plsc.load_gather(gate_v, [slot_idx, zero_idx + (t * n_k + k)]) for k in range(n_k)]

                @pl.loop(0, dp // lanes, step=SC_COMBINE_UNROLL)
                def _(v0):
                    for u in range(SC_COMBINE_UNROLL):
                        v = v0 + u
                        words = pl.ds(v * lanes, lanes)
                        acc_lo = jnp.zeros((lanes,), F32)
                        acc_hi = jnp.zeros((lanes,), F32)
                        for k in range(n_k):
                            w = rows_v[slot, t * n_k + k, words]
                            acc_lo = acc_lo + g_k[k] * plsc.bitcast(w << 16, F32)
                            acc_hi = acc_hi + g_k[k] * plsc.bitcast(w & HIGH_HALF_MASK, F32)
                        out_v[slot, t, words] = acc_lo
                        out_v[slot, t, pl.ds(dp + v * lanes, lanes)] = acc_hi

        gather(0, 0).start()
        load_gates(0, 0).start()

        @pl.loop(0, n_win, step=2)
        def _(g):
            for slot in range(2):
                cur = g + slot
                gather(cur, slot).wait()
                load_gates(cur, slot).wait()

                @pl.when(cur + 1 < n_win)
                def _():
                    gather(cur + 1, 1 - slot).start()
                    load_gates(cur + 1, 1 - slot).start()

                @pl.when(cur >= 2)
                def _():
                    put(cur - 2, slot).wait()

                reduce_window(slot)
                put(cur, slot).start()

        for slot in range(2):
            put(n_win - 2 + slot, slot).wait()

    return pl.kernel(
        body, mesh=_sc_mesh(),
        out_type=jax.ShapeDtypeStruct((n_tok, d), F32),
        scratch_types=[pltpu.VMEM((per_w * n_k,), jnp.int32),
                       pltpu.VMEM((2, rows, dp), table.dtype),
                       pltpu.VMEM((2, rows), F32),
                       pltpu.VMEM((2, tok_win, d), F32),
                       pltpu.SemaphoreType.DMA((2,)),
                       pltpu.SemaphoreType.DMA((2,)),
                       pltpu.SemaphoreType.DMA((2,))],
        compiler_params=pltpu.CompilerParams(use_tc_tiling_on_sc=True, needs_layout_passes=False),
        name="sc_combine_reduce",
    )(table, idx, gate_rows)


def _moe_out_kernel(alpha, x_ref, h2_ref, r_ref, mod_ref, ws13_ref, ws2_ref, lng_ref, lnb_ref, o_ref):
    de = ws2_ref.shape[0]
    z = _dot_packed(h2_ref[0], ws13_ref)
    hmid = (_silu(z[:, :de]) * z[:, de:]).astype(BF16)
    y = _dot(hmid, ws2_ref[...]) + r_ref[0]
    o_ref[0] = _layer_norm(alpha * x_ref[0] + mod_ref[0, 5:6, :] * y, lng_ref[...], lnb_ref[...])


def _moe_out(alpha, xn, h2, routed, mod, ws13, ws2, lng, lnb, nt, ntx):
    nb, _, d = xn.shape
    return pl.pallas_call(
        functools.partial(_moe_out_kernel, alpha),
        out_shape=jax.ShapeDtypeStruct((nb, nt * SEQ_TILE, d), F32),
        grid=(nb, nt),
        in_specs=[_tok_spec(d), _tok_spec(d // 2), _tok_spec(d), _mod_spec(d, nb, ntx),
                  _full_spec(ws13.shape), _full_spec(ws2.shape), _full_spec(lng.shape), _full_spec(lnb.shape)],
        out_specs=_tok_spec(d),
        compiler_params=_params(2),
        name="moe_out",
    )(xn, h2, routed, mod, ws13, ws2, lng, lnb)


def _dispatch_plan(counts, topi, rank, n_tok):
    n_e = counts.shape[0]
    bm = MOE_BLOCK_ROWS
    padded = (counts + bm - 1) // bm * bm
    pad_end = jnp.cumsum(padded)
    pad_start = pad_end - padded
    onehot = topi[:, None, :] == jnp.arange(n_e, dtype=jnp.int32)[None, :, None]
    dest = rank + jnp.sum(jnp.where(onehot, pad_start[None, :, None], 0), axis=1)
    n_blocks = n_tok * TOP_K // bm + n_e
    block_start = jnp.arange(n_blocks, dtype=jnp.int32) * bm
    block_expert = jnp.sum((pad_end[None, :] <= block_start[:, None]).astype(jnp.int32), axis=1)
    block_expert = jnp.minimum(block_expert, n_e - 1)
    n_used = (pad_end[-1] // bm).astype(jnp.int32).reshape(1)
    return dest.astype(jnp.int32), block_expert, n_used, n_blocks * bm


def _rope_tables(seq):
    n_freq = QK_ROPE // 4
    inv_freq = ROPE_THETA ** (-jnp.arange(n_freq, dtype=F32) / n_freq)
    pos = jnp.arange(seq, dtype=jnp.int32)
    r = (pos // GRID_W).astype(F32)
    col = (pos % GRID_W).astype(F32)
    ang = jnp.concatenate([r[:, None] * inv_freq, col[:, None] * inv_freq], -1)
    cos, sin = jnp.cos(ang), jnp.sin(ang)
    zeros = jnp.zeros((seq, LANE - QK_ROPE), F32)
    cos_slot = jnp.concatenate([cos, cos, zeros], -1)
    sin_slot = jnp.concatenate([-sin, sin, zeros], -1)
    return cos_slot, sin_slot


def _rope_slot_weights(w_rope):
    k, n, _ = w_rope.shape
    half = QK_ROPE // 2
    swapped = jnp.concatenate([w_rope[..., half:], w_rope[..., :half]], -1)
    pad = jnp.zeros((k, n, LANE - QK_ROPE), w_rope.dtype)
    plain = jnp.concatenate([w_rope, pad], -1).reshape(k, n * LANE)
    swp = jnp.concatenate([swapped, pad], -1).reshape(k, n * LANE)
    return jnp.concatenate([plain, swp], -1)


def kernel(x, c, ctx, c_ctx, ada_w, ada_b, ln_g, ln_b, conf_w1, conf_b1, conf_dw, conf_dwb, conf_ng, conf_nb, conf_w2, conf_b2, sc_w_in, sc_dw, sc_w_out, mla_w_dqkv, mla_q_g, mla_kv_g, mla_w_uq, mla_w_uk, mla_w_uv, mla_w_o, moe_router, moe_bias, moe_w1, moe_w3, moe_w2, sh_w1, sh_w3, sh_w2):
    nb, seq, d = x.shape
    l_ctx = ctx.shape[1]
    depth = ada_w.shape[0]
    alpha = (2.0 * depth) ** 0.25
    tl = SEQ_TILE
    assert seq % tl == 0 and l_ctx % tl == 0 and seq % Q_TILE == 0 and seq % GRID_W == 0
    ntx = seq // tl
    nt_all = (seq + l_ctx) // tl
    attn_layers = [i for i in range(depth) if i % N_MIXERS == 2]
    last_ctx_reader = attn_layers[-1] if attn_layers else -1

    rows = -(-(nb + 1) // 8) * 8
    c_all = jnp.zeros((rows, d), F32).at[:nb].set(c).at[nb].set(c_ctx)
    mod_all = _modulation(c_all, ada_w, ada_b).reshape(depth, rows, N_MOD, d)

    assert nb % N_CHAINS == 0
    nbc = nb // N_CHAINS
    chains = [jnp.concatenate([x[c0:c0 + nbc], ctx[c0:c0 + nbc]], axis=1) for c0 in range(0, nb, nbc)]
    q_rank, kv_rank = mla_q_g.shape[1], mla_kv_g.shape[1]
    cos_t, sin_t = _rope_tables(seq)
    row = lambda v: v.reshape(1, -1)

    for i in range(depth):
        need_ctx = i < last_ctx_reader
        kind, j = i % N_MIXERS, i // N_MIXERS
        nt = nt_all if need_ctx else ntx
        n_tok = nbc * nt * tl

        if kind == 0:
            w_first = conf_w1[j].astype(BF16)
            dw_tiles = jnp.broadcast_to(conf_dw[j][:, None, :], (conf_dw.shape[1], SUBLANES, d))
            w_last, b_last = conf_w2[j].astype(BF16), row(conf_b2[j])
        elif kind == 1:
            w_first = sc_w_in[j].astype(BF16)
            w_last, b_last = sc_w_out[j].astype(BF16), jnp.zeros((1, d), F32)
        else:
            wdq = mla_w_dqkv[j]
            wd = wdq[:, :q_rank + kv_rank].astype(BF16)
            wkp = _rope_slot_weights(wdq[:, None, q_rank + kv_rank:]).astype(BF16)
            wuq = mla_w_uq[j].reshape(q_rank, MLA_HEADS, QK_NOPE + QK_ROPE)
            wqn = wuq[:, :, :QK_NOPE].reshape(q_rank, MLA_HEADS * QK_NOPE).astype(BF16)
            wqr = _rope_slot_weights(wuq[:, :, QK_NOPE:]).astype(BF16)
            wuk, wuv = mla_w_uk[j].astype(BF16), mla_w_uv[j].astype(BF16)
            w_last, b_last = mla_w_o[j].astype(BF16), jnp.zeros((1, d), F32)
        rw_hi, rw_lo = _split_bf16(moe_router[i].T)
        ws13 = jnp.concatenate([sh_w1[i], sh_w3[i]], axis=-1).astype(BF16)
        ws2 = sh_w2[i].astype(BF16)

        for ci in range(N_CHAINS):
            x_all = chains[ci]
            mod = jnp.concatenate([mod_all[i, ci * nbc:(ci + 1) * nbc], mod_all[i, nb:nb + 1]], axis=0)

            if kind == 0:
                u = _conf_in(x_all, mod, w_first, row(conf_b1[j]), nt, ntx)
                a = _conf_conv(u, dw_tiles, row(conf_dwb[j]), row(conf_ng[j]), row(conf_nb[j]), nt, ntx)
            elif kind == 1:
                a = _sc_conv(x_all, mod, w_first, sc_dw[j], nt, ntx)
            else:
                q, k, v = _mla_proj(x_all, mod, wd, wkp, row(mla_q_g[j]), row(mla_kv_g[j]), wqn, wqr,
                                    wuk, wuv, cos_t, sin_t, nt_all, ntx)
                a = _attention(q, k, v, nt * tl)

            xn, h2, topi, topw, rank, counts = _post(alpha, a, w_last, b_last, x_all, mod, row(ln_g[i, 0]),
                                                     row(ln_b[i, 0]), rw_hi, rw_lo, moe_bias[i].reshape(-1, 1),
                                                     nt, ntx)

            dest, block_expert, n_used, n_rows = _dispatch_plan(counts[:, 0], topi, rank, n_tok)
            xs = _sc_row_scatter(h2.reshape(n_tok, d // 2), dest, n_rows)
            ys = _grouped_ffn(block_expert, n_used, xs, moe_w1, moe_w3, moe_w2, i)
            routed = _sc_combine(ys, dest, topw).reshape(nbc, nt * tl, d)

            chains[ci] = _moe_out(alpha, xn, h2, routed, mod, ws13, ws2,
                                  row(ln_g[i, 1]), row(ln_b[i, 1]), nt, ntx)
    return jnp.concatenate([xc[:, :seq] for xc in chains], axis=0)
```

```python
import functools

import numpy as np
import jax
import jax.numpy as jnp
from jax import lax
from jax.experimental import pallas as pl
from jax.experimental.pallas import tpu as pltpu
from jax.experimental.pallas import tpu_sc as plsc

F32 = jnp.float32
BF16 = jnp.bfloat16
U32 = jnp.uint32
HIGH_HALF_MASK = np.uint32(0xFFFF0000)

GRID_W = 64
N_MIXERS = 3
LN_EPS = 1e-5
RMS_EPS = 1e-6
N_MOD = 6
MLA_HEADS = 8
QK_NOPE = 128
QK_ROPE = 64
V_HEAD = 128
ROPE_THETA = 10000.0
ATTN_SCALE = (QK_NOPE + QK_ROPE) ** -0.5
Q_SCALE = ATTN_SCALE * 1.4426950408889634
TOP_K = 8
N_GROUPS = 8
TOPK_GROUPS = 4
ROUTED_SCALE = 2.5

SEQ_TILE = 256
CONV_HALO = 16
SHORT_HALO = 8
CONV_ROW_CHUNK = 64
LANE = 128
SUBLANES = 8
Q_TILE = 1024
ATTN_SUB_ROWS = 256
ATTN_ROW_CHUNK = 16
MOE_BLOCK_ROWS = 1024
ROUTE_SUB_TOKENS = 128
MOD_COL_TILE = 1536
VMEM_LIMIT = 48 * 1024 * 1024
SC_CORES = 2
SC_SUBCORES = 16
SC_WORKERS = SC_CORES * SC_SUBCORES
SC_LANES = 16
SC_COMBINE_UNROLL = 4
SC_COMBINE_TOKENS = 8
SC_WINDOWS = (64, 32, 16)
N_CHAINS = 1


def _params(n_axes):
    return pltpu.CompilerParams(dimension_semantics=("arbitrary",) * n_axes,
                                vmem_limit_bytes=VMEM_LIMIT)


def _split_bf16(a):
    hi = a.astype(BF16)
    lo = (a - hi.astype(F32)).astype(BF16)
    return hi, lo


def _dot(a, b):
    return jnp.dot(a, b, preferred_element_type=F32)


def _dot_nt(a, b):
    return lax.dot_general(a, b, (((1,), (1,)), ((), ())), preferred_element_type=F32)


def _pack_halves(v):
    half = v.shape[-1] // 2
    lo = lax.bitcast_convert_type(v[:, :half].astype(BF16).astype(F32), U32) >> 16
    hi = lax.bitcast_convert_type(v[:, half:].astype(BF16).astype(F32), U32) & HIGH_HALF_MASK
    return hi | lo


def _unpack_halves(p):
    lo = lax.bitcast_convert_type(p << 16, F32)
    hi = lax.bitcast_convert_type(p & HIGH_HALF_MASK, F32)
    return lo, hi


def _dot_packed(p, w):
    lo, hi = _unpack_halves(p)
    return _dot(jnp.concatenate([lo.astype(BF16), hi.astype(BF16)], axis=1), w[...])


def _layer_norm(v, g, b):
    mu = jnp.mean(v, axis=-1, keepdims=True)
    c = v - mu
    var = jnp.mean(c * c, axis=-1, keepdims=True)
    return c * lax.rsqrt(var + LN_EPS) * g + b


def _silu(v):
    return v * jax.nn.sigmoid(v)


def _mod_kernel(c_ref, w_ref, b_ref, o_ref):
    a = _silu(c_ref[...])
    a_hi, a_lo = _split_bf16(a)
    w_hi, w_lo = _split_bf16(w_ref[0])
    o_ref[0] = _dot(a_hi, w_hi) + _dot(a_hi, w_lo) + _dot(a_lo, w_hi) + b_ref[0]


def _modulation(c_all, ada_w, ada_b):
    depth, d, n = ada_w.shape
    rows = c_all.shape[0]
    tn = MOD_COL_TILE
    return pl.pallas_call(
        _mod_kernel,
        out_shape=jax.ShapeDtypeStruct((depth, rows, n), F32),
        grid=(depth, n // tn),
        in_specs=[pl.BlockSpec((rows, d), lambda i, j: (0, 0)),
                  pl.BlockSpec((1, d, tn), lambda i, j: (i, 0, j)),
                  pl.BlockSpec((1, 1, tn), lambda i, j: (i, 0, j))],
        out_specs=pl.BlockSpec((1, rows, tn), lambda i, j: (i, 0, j)),
        compiler_params=_params(2),
        name="adaln_mod",
    )(c_all, ada_w, ada_b.reshape(depth, 1, n))


def _tok_spec(d, tl=SEQ_TILE):
    return pl.BlockSpec((1, tl, d), lambda b, l: (b, l, 0))


def _mod_spec(d, n_batch, ntx):
    return pl.BlockSpec((1, N_MOD, d), lambda b, l: (jnp.where(l < ntx, b, n_batch), 0, 0))


def _full_spec(shape):
    zeros = (0,) * len(shape)
    return pl.BlockSpec(shape, lambda b, l: zeros)


def _halo_specs(d, halo, seq_len, tl=SEQ_TILE):
    per_tile = tl // halo
    last = seq_len // halo - 1
    prev = pl.BlockSpec((1, halo, d), lambda b, l: (b, jnp.maximum(l * per_tile - 1, 0), 0))
    nxt = pl.BlockSpec((1, halo, d), lambda b, l: (b, jnp.minimum((l + 1) * per_tile, last), 0))
    return prev, nxt


def _segment_edges(l, ntx, nt):
    first = jnp.logical_or(l == 0, l == ntx)
    last = jnp.logical_or(l == ntx - 1, l == nt - 1)
    return first, last


def _window_rows(prev_ref, cur_ref, next_ref, win_ref):
    halo, tl = prev_ref.shape[1], cur_ref.shape[1]
    win_ref[0:halo, :] = prev_ref[0]
    win_ref[halo:halo + tl, :] = cur_ref[0]
    win_ref[halo + tl:, :] = next_ref[0]
    return win_ref[...]


def _edge_mask(n_rows, halo, first, last):
    r = lax.broadcasted_iota(jnp.int32, (n_rows, 1), 0)
    outside = jnp.logical_or(jnp.logical_and(first, r < halo), jnp.logical_and(last, r >= n_rows - halo))
    return jnp.where(outside, 0.0, 1.0)


def _conf_in_kernel(x_ref, mod_ref, w1_ref, b1_ref, u_ref):
    d = x_ref.shape[-1]
    h = x_ref[0] * (1.0 + mod_ref[0, 1:2, :]) + mod_ref[0, 0:1, :]
    z = _dot(h.astype(BF16), w1_ref[...]) + b1_ref[...]
    u_ref[0] = z[:, :d] * jax.nn.sigmoid(z[:, d:])


def _conf_in(x_all, mod, w1, b1, nt, ntx):
    nb, seq_len, d = x_all.shape
    return pl.pallas_call(
        _conf_in_kernel,
        out_shape=jax.ShapeDtypeStruct((nb, nt * SEQ_TILE, d), F32),
        grid=(nb, nt),
        in_specs=[_tok_spec(d), _mod_spec(d, nb, ntx), _full_spec(w1.shape), _full_spec(b1.shape)],
        out_specs=_tok_spec(d),
        compiler_params=_params(2),
        name="conf_in",
    )(x_all, mod, w1, b1)


def _conf_conv_kernel(ntx, nt, up_ref, uc_ref, un_ref, dw_ref, dwb_ref, ng_ref, nb_ref,
                      a_ref, sh_ref, conv_ref):
    tl, d = uc_ref.shape[1], uc_ref.shape[2]
    taps = dw_ref.shape[0]
    lead = CONV_HALO - (taps - 1) // 2
    first, last = _segment_edges(pl.program_id(1), ntx, nt)
    sh_ref[0, 0:CONV_HALO, :] = jnp.where(first, 0.0, up_ref[0])
    sh_ref[0, CONV_HALO:CONV_HALO + tl, :] = uc_ref[0]
    sh_ref[0, CONV_HALO + tl:, :] = jnp.where(last, 0.0, un_ref[0])
    span = tl + 2 * CONV_HALO - SUBLANES
    for s in range(1, SUBLANES):
        sh_ref[s, 0:span, :] = sh_ref[0, s:s + span, :]
    groups = CONV_ROW_CHUNK // SUBLANES

    def row_chunk(i, carry):
        r0 = pl.multiple_of(i * CONV_ROW_CHUNK, CONV_ROW_CHUNK)
        for c0 in range(0, d, LANE):
            accs = [jnp.zeros((SUBLANES, LANE), F32) for _ in range(groups)]
            for k in range(taps):
                res = (lead + k) % SUBLANES
                off = lead + k - res
                w = dw_ref[k, :, c0:c0 + LANE]
                for g in range(groups):
                    lo = r0 + (off + g * SUBLANES)
                    accs[g] = accs[g] + w * sh_ref[res, pl.ds(lo, SUBLANES), c0:c0 + LANE]
            for g in range(groups):
                conv_ref[pl.ds(r0 + g * SUBLANES, SUBLANES), c0:c0 + LANE] = accs[g]
        return carry

    lax.fori_loop(0, tl // CONV_ROW_CHUNK, row_chunk, 0)
    v = _layer_norm(conv_ref[...] + dwb_ref[...], ng_ref[...], nb_ref[...])
    a_ref[0] = _silu(v).astype(BF16)


def _conf_conv(u, dw, dwb, ng, nb_, nt, ntx):
    nb, seq_len, d = u.shape
    prev, nxt = _halo_specs(d, CONV_HALO, seq_len)
    return pl.pallas_call(
        functools.partial(_conf_conv_kernel, ntx, nt),
        out_shape=jax.ShapeDtypeStruct((nb, seq_len, d), BF16),
        grid=(nb, nt),
        in_specs=[prev, _tok_spec(d), nxt, _full_spec(dw.shape), _full_spec(dwb.shape),
                  _full_spec(ng.shape), _full_spec(nb_.shape)],
        out_specs=_tok_spec(d),
        scratch_shapes=[pltpu.VMEM((SUBLANES, SEQ_TILE + 2 * CONV_HALO, d), F32),
                        pltpu.VMEM((SEQ_TILE, d), F32)],
        compiler_params=_params(2),
        name="conf_conv",
    )(u, u, u, dw, dwb, ng, nb_)


def _sc_conv_kernel(ntx, nt, xp_ref, xc_ref, xn_ref, mod_ref, w_ref, dw_ref, a_ref, win_ref, ext_ref):
    tl, d = xc_ref.shape[1], xc_ref.shape[2]
    taps = dw_ref.shape[0]
    lead = SHORT_HALO - (taps - 1) // 2
    first, last = _segment_edges(pl.program_id(1), ntx, nt)
    xw = _window_rows(xp_ref, xc_ref, xn_ref, win_ref)
    h = xw * (1.0 + mod_ref[0, 1:2, :]) + mod_ref[0, 0:1, :]
    z = _dot(h.astype(BF16), w_ref[...])
    gb = z[SHORT_HALO:SHORT_HALO + tl, :d]
    ext_ref[...] = z[:, d:2 * d] * z[:, 2 * d:] * _edge_mask(z.shape[0], SHORT_HALO, first, last)
    acc = dw_ref[0:1, :] * ext_ref[lead:lead + tl, :]
    for k in range(1, taps):
        acc = acc + dw_ref[k:k + 1, :] * ext_ref[lead + k:lead + k + tl, :]
    a_ref[0] = (gb * acc).astype(BF16)


def _sc_conv(x_all, mod, w_in, dw, nt, ntx):
    nb, seq_len, d = x_all.shape
    prev, nxt = _halo_specs(d, SHORT_HALO, seq_len)
    win = pltpu.VMEM((SEQ_TILE + 2 * SHORT_HALO, d), F32)
    return pl.pallas_call(
        functools.partial(_sc_conv_kernel, ntx, nt),
        out_shape=jax.ShapeDtypeStruct((nb, nt * SEQ_TILE, d), BF16),
        grid=(nb, nt),
        in_specs=[prev, _tok_spec(d), nxt, _mod_spec(d, nb, ntx), _full_spec(w_in.shape), _full_spec(dw.shape)],
        out_specs=_tok_spec(d),
        scratch_shapes=[win, win],
        compiler_params=_params(2),
        name="sc_conv",
    )(x_all, x_all, x_all, mod, w_in, dw)


def _rms(v, g):
    return v * lax.rsqrt(jnp.mean(v * v, axis=-1, keepdims=True) + RMS_EPS) * g


def _mla_proj_kernel(ntx, q_rank,
                     x_ref, mod_ref, wd_ref, wkp_ref, qg_ref, kvg_ref, wqn_ref, wqr_ref, wuk_ref, wuv_ref,
                     cos_ref, sin_ref, q_ref, k_ref, v_ref):
    is_latent = pl.program_id(1) < ntx
    h = (x_ref[0] * (1.0 + mod_ref[0, 1:2, :]) + mod_ref[0, 0:1, :]).astype(BF16)
    dn = _dot(h, wd_ref[...])
    cq = _rms(dn[:, :q_rank], qg_ref[...]).astype(BF16)
    ckv = _rms(dn[:, q_rank:], kvg_ref[...]).astype(BF16)
    cos = jnp.where(is_latent, cos_ref[...], 1.0)
    sin = jnp.where(is_latent, sin_ref[...], 0.0)
    kp2 = _dot(h, wkp_ref[...])
    kp = kp2[:, :LANE] * cos + kp2[:, LANE:] * sin
    kn = _dot(ckv, wuk_ref[...])
    vv = _dot(ckv, wuv_ref[...])
    qn = _dot(cq, wqn_ref[...]) * Q_SCALE
    qr2 = _dot(cq, wqr_ref[...])
    hw = MLA_HEADS * LANE
    for hd in range(MLA_HEADS):
        sl = slice(hd * LANE, (hd + 1) * LANE)
        qr = (qr2[:, sl] * cos + qr2[:, hw + hd * LANE:hw + (hd + 1) * LANE] * sin) * Q_SCALE
        q_ref[0, hd, :, 0:LANE] = qn[:, sl].astype(BF16)
        q_ref[0, hd, :, LANE:] = qr.astype(BF16)
        k_ref[0, hd, :, 0:LANE] = kn[:, sl].astype(BF16)
        k_ref[0, hd, :, LANE:] = kp.astype(BF16)
        v_ref[0, hd] = vv[:, sl].astype(BF16)


def _mla_proj(x_all, mod, wd, wkp, qg, kvg, wqn, wqr, wuk, wuv, cos_t, sin_t, nt, ntx):
    nb, seq_len, d = x_all.shape
    q_rank = qg.shape[-1]
    tl = SEQ_TILE
    rope_spec = pl.BlockSpec((tl, LANE), lambda b, l: (jnp.minimum(l, ntx - 1), 0))
    qk_shape = jax.ShapeDtypeStruct((nb, MLA_HEADS, nt * tl, 2 * LANE), BF16)
    v_shape = jax.ShapeDtypeStruct((nb, MLA_HEADS, nt * tl, LANE), BF16)
    qk_spec = pl.BlockSpec((1, MLA_HEADS, tl, 2 * LANE), lambda b, l: (b, 0, l, 0))
    v_spec = pl.BlockSpec((1, MLA_HEADS, tl, LANE), lambda b, l: (b, 0, l, 0))
    return pl.pallas_call(
        functools.partial(_mla_proj_kernel, ntx, q_rank),
        out_shape=(qk_shape, qk_shape, v_shape),
        grid=(nb, nt),
        in_specs=[_tok_spec(d), _mod_spec(d, nb, ntx), _full_spec(wd.shape), _full_spec(wkp.shape),
                  _full_spec(qg.shape), _full_spec(kvg.shape), _full_spec(wqn.shape), _full_spec(wqr.shape),
                  _full_spec(wuk.shape), _full_spec(wuv.shape), rope_spec, rope_spec],
        out_specs=(qk_spec, qk_spec, v_spec),
        compiler_params=_params(2),
        name="mla_proj",
    )(x_all, mod, wd, wkp, qg, kvg, wqn, wqr, wuk, wuv, cos_t, sin_t)


def _attn_kernel(q_ref, k_ref, v_ref, o_ref, s_ref, p_ref, l_ref):
    tq = q_ref.shape[2]
    for r0 in range(0, tq, ATTN_SUB_ROWS):
        rows = slice(r0, r0 + ATTN_SUB_ROWS)
        s_ref[rows, :] = _dot_nt(q_ref[0, 0, rows, :], k_ref[0, 0])
        for c0 in range(r0, r0 + ATTN_SUB_ROWS, ATTN_ROW_CHUNK):
            chunk = slice(c0, c0 + ATTN_ROW_CHUNK)
            s = s_ref[chunk, :]
            p = jnp.exp2(s - jnp.max(s, axis=-1, keepdims=True))
            l_ref[chunk, :] = jnp.broadcast_to(jnp.sum(p, axis=-1, keepdims=True), (ATTN_ROW_CHUNK, V_HEAD))
            p_ref[chunk, :] = p.astype(BF16)
        o = _dot(p_ref[rows, :], v_ref[0, 0])
        o_ref[0, rows, :] = (o / l_ref[rows, :]).astype(BF16)


def _attention(q, k, v, n_q):
    nb, nh, n_k, dk = k.shape
    tq = Q_TILE
    return pl.pallas_call(
        _attn_kernel,
        out_shape=jax.ShapeDtypeStruct((nb, n_q, nh * V_HEAD), BF16),
        grid=(nb, nh, n_q // tq),
        in_specs=[pl.BlockSpec((1, 1, tq, dk), lambda b, h, i: (b, h, i, 0)),
                  pl.BlockSpec((1, 1, n_k, dk), lambda b, h, i: (b, h, 0, 0)),
                  pl.BlockSpec((1, 1, n_k, V_HEAD), lambda b, h, i: (b, h, 0, 0))],
        out_specs=pl.BlockSpec((1, tq, V_HEAD), lambda b, h, i: (b, i, h)),
        scratch_shapes=[pltpu.VMEM((tq, n_k), F32), pltpu.VMEM((tq, n_k), BF16), pltpu.VMEM((tq, V_HEAD), F32)],
        compiler_params=_params(3),
        name="mla_attn",
    )(q, k, v)


def _route(sel, s):
    n_e, n_t = sel.shape
    per = n_e // N_GROUPS
    sel3 = sel.reshape(N_GROUPS, per, n_t)
    s3 = s.reshape(N_GROUPS, per, n_t)
    iota_p = lax.broadcasted_iota(jnp.int32, (N_GROUPS, per, n_t), 1).astype(F32)
    iota_g = lax.broadcasted_iota(jnp.int32, (N_GROUPS, 1, n_t), 0).astype(F32)
    neg = -jnp.inf
    m1 = jnp.max(sel3, axis=1, keepdims=True)
    first = jnp.min(jnp.where(sel3 == m1, iota_p, float(per)), axis=1, keepdims=True)
    m2 = jnp.max(jnp.where(iota_p == first, neg, sel3), axis=1, keepdims=True)
    gs = m1 + m2
    gsel = jnp.zeros((N_GROUPS, 1, n_t), F32)
    for _ in range(TOPK_GROUPS):
        gm = jnp.max(gs, axis=0, keepdims=True)
        gfirst = jnp.min(jnp.where(gs == gm, iota_g, float(N_GROUPS)), axis=0, keepdims=True)
        pick = iota_g == gfirst
        gsel = jnp.where(pick, 1.0, gsel)
        gs = jnp.where(pick, neg, gs)
    val = jnp.where(gsel > 0.0, sel3, neg)
    iota_e = lax.broadcasted_iota(jnp.int32, (N_GROUPS, per, n_t), 0).astype(F32) * per + iota_p
    chosen = jnp.zeros((N_GROUPS, per, n_t), F32)
    idx, wts = [], []
    for _ in range(TOP_K):
        m = jnp.max(jnp.max(val, axis=1, keepdims=True), axis=0, keepdims=True)
        e = jnp.min(jnp.min(jnp.where(val == m, iota_e, float(n_e)), axis=1, keepdims=True), axis=0, keepdims=True)
        pick = iota_e == e
        wts.append(jnp.sum(jnp.sum(jnp.where(pick, s3, 0.0), axis=1, keepdims=True), axis=0, keepdims=True))
        idx.append(e)
        chosen = jnp.where(pick, 1.0, chosen)
        val = jnp.where(pick, neg, val)
    total = wts[0]
    for w in wts[1:]:
        total = total + w
    wts = [w / total * ROUTED_SCALE for w in wts]
    return idx, wts, chosen, iota_e


def _post_kernel(alpha, a_ref, w_ref, b_ref, x_ref, mod_ref, lng_ref, lnb_ref, rw_hi_ref, rw_lo_ref, rb_ref, tri_ref,
                 xn_ref, h2_ref, topi_ref, topw_ref, rank_ref, cnt_ref, run_ref):
    first_step = jnp.logical_and(pl.program_id(0) == 0, pl.program_id(1) == 0)

    @pl.when(first_step)
    def _():
        run_ref[...] = jnp.zeros_like(run_ref)

    y = _dot(a_ref[0], w_ref[...]) + b_ref[...]
    run = run_ref[...]
    for t0 in range(0, y.shape[0], ROUTE_SUB_TOKENS):
        rows = slice(t0, t0 + ROUTE_SUB_TOKENS)
        xn = _layer_norm(alpha * x_ref[0, rows, :] + mod_ref[0, 2:3, :] * y[rows], lng_ref[...], lnb_ref[...])
        xn_ref[0, rows, :] = xn
        h2 = xn * (1.0 + mod_ref[0, 4:5, :]) + mod_ref[0, 3:4, :]
        h_hi, h_lo = _split_bf16(h2)
        h2_ref[0, rows, :] = _pack_halves(h2)
        logits = _dot_nt(rw_hi_ref[...], h_hi) + _dot_nt(rw_hi_ref[...], h_lo) + _dot_nt(rw_lo_ref[...], h_hi)
        s = jax.nn.sigmoid(logits)
        idx, wts, chosen3, iota_e = _route(s + rb_ref[...], s)
        n_e, n_t = s.shape
        chosen = chosen3.reshape(n_e, n_t)
        rank = run[:, 0:1] + _dot(chosen.astype(BF16), tri_ref[...])
        rank3 = rank.reshape(chosen3.shape)
        for k in range(TOP_K):
            rk = jnp.sum(jnp.sum(jnp.where(iota_e == idx[k], rank3, 0.0), axis=1, keepdims=True), axis=0,
                         keepdims=True)
            topi_ref[k:k + 1, rows] = idx[k].reshape(1, n_t).astype(jnp.int32)
            topw_ref[k:k + 1, rows] = wts[k].reshape(1, n_t)
            rank_ref[k:k + 1, rows] = rk.reshape(1, n_t).astype(jnp.int32)
        run = run + jnp.sum(chosen, axis=1, keepdims=True)
    run_ref[...] = run
    cnt_ref[...] = run.astype(jnp.int32)


def _post(alpha, a, w, bias, x_all, mod, lng, lnb, rw_hi, rw_lo, rb, nt, ntx):
    nb, _, dk = a.shape
    d = x_all.shape[-1]
    n_e = rw_hi.shape[0]
    tl = SEQ_TILE
    n_tok = nb * nt * tl
    col_spec = lambda rows: pl.BlockSpec((rows, tl), lambda b, l: (0, b * nt + l))
    sub = ROUTE_SUB_TOKENS
    row_i = lax.broadcasted_iota(jnp.int32, (sub, sub), 0)
    col_i = lax.broadcasted_iota(jnp.int32, (sub, sub), 1)
    tri = jnp.where(row_i < col_i, 1.0, 0.0).astype(BF16)
    return pl.pallas_call(
        functools.partial(_post_kernel, alpha),
        out_shape=(jax.ShapeDtypeStruct((nb, nt * tl, d), F32),
                   jax.ShapeDtypeStruct((nb, nt * tl, d // 2), U32),
                   jax.ShapeDtypeStruct((TOP_K, n_tok), jnp.int32),
                   jax.ShapeDtypeStruct((TOP_K, n_tok), F32),
                   jax.ShapeDtypeStruct((TOP_K, n_tok), jnp.int32),
                   jax.ShapeDtypeStruct((n_e, LANE), jnp.int32)),
        grid=(nb, nt),
        in_specs=[_tok_spec(dk), _full_spec(w.shape), _full_spec(bias.shape), _tok_spec(d),
                  _mod_spec(d, nb, ntx), _full_spec(lng.shape), _full_spec(lnb.shape),
                  _full_spec(rw_hi.shape), _full_spec(rw_lo.shape), _full_spec(rb.shape), _full_spec(tri.shape)],
        out_specs=(_tok_spec(d), _tok_spec(d // 2), col_spec(TOP_K), col_spec(TOP_K), col_spec(TOP_K),
                   _full_spec((n_e, LANE))),
        scratch_shapes=[pltpu.VMEM((n_e, LANE), F32)],
        compiler_params=_params(2),
        name="mixer_post",
    )(a, w, bias, x_all, mod, lng, lnb, rw_hi, rw_lo, rb, tri)


def _gmm_kernel(be_ref, br_ref, xs_ref, w1_ref, w3_ref, w2_ref, ys_ref, w13_s, w2_s):
    i = pl.program_id(0)
    bm = xs_ref.shape[0]
    de = w2_ref.shape[2]
    changed = jnp.logical_or(i == 0, be_ref[i] != be_ref[jnp.maximum(i - 1, 0)])

    @pl.when(changed)
    def _():
        w13_s[:, :de] = w1_ref[0, 0].astype(BF16)
        w13_s[:, de:] = w3_ref[0, 0].astype(BF16)
        w2_s[...] = w2_ref[0, 0].astype(BF16)

    def ffn(n_rows):
        z = _dot_packed(xs_ref[0:n_rows, :], w13_s)
        hmid = (_silu(z[:, :de]) * z[:, de:]).astype(BF16)
        ys_ref[0:n_rows, :] = _pack_halves(_dot(hmid, w2_s[...]))

    valid = br_ref[i]

    @pl.when(valid > bm // 2)
    def _():
        ffn(bm)

    @pl.when(jnp.logical_and(valid > 0, valid <= bm // 2))
    def _():
        ffn(bm // 2)


def _grouped_ffn(block_expert, block_rows, xs, w1, w3, w2, layer):
    n_rows, dp = xs.shape
    d = 2 * dp
    bm = MOE_BLOCK_ROWS
    de = w2.shape[2]
    grid_spec = pltpu.PrefetchScalarGridSpec(
        num_scalar_prefetch=2,
        grid=(n_rows // bm,),
        in_specs=[pl.BlockSpec((bm, dp), lambda i, be, br: (i, 0)),
                  pl.BlockSpec((1, 1, d, de), lambda i, be, br: (layer, be[i], 0, 0)),
                  pl.BlockSpec((1, 1, d, de), lambda i, be, br: (layer, be[i], 0, 0)),
                  pl.BlockSpec((1, 1, de, d), lambda i, be, br: (layer, be[i], 0, 0))],
        out_specs=pl.BlockSpec((bm, dp), lambda i, be, br: (i, 0)),
        scratch_shapes=[pltpu.VMEM((d, 2 * de), BF16), pltpu.VMEM((de, d), BF16)],
    )
    return pl.pallas_call(
        _gmm_kernel,
        out_shape=jax.ShapeDtypeStruct((n_rows, dp), U32),
        grid_spec=grid_spec,
        compiler_params=_params(1),
        name="moe_grouped_ffn",
    )(block_expert, block_rows, xs, w1, w3, w2)


def _sc_mesh():
    return plsc.VectorSubcoreMesh(core_axis_name="c", subcore_axis_name="s",
                                  num_cores=SC_CORES, num_subcores=SC_SUBCORES)


def _sc_worker():
    return lax.axis_index("s") * SC_CORES + lax.axis_index("c")


def _sc_window(rows_per_worker):
    for win in SC_WINDOWS:
        if rows_per_worker % (2 * win) == 0:
            return win
    raise ValueError(f"no SparseCore window divides {rows_per_worker} rows per worker")


def _sc_row_scatter(src, dest, n_out):
    n_k, n_tok = dest.shape
    dp = src.shape[1]
    assert n_tok % SC_WORKERS == 0
    per_w = n_tok // SC_WORKERS
    win = _sc_window(per_w)
    n_win = per_w // win
    idx = dest.reshape(n_k, SC_WORKERS, n_win, win).transpose(1, 2, 0, 3)

    def body(src_hbm, idx_hbm, out_hbm, idx_v, rows_v, load_sem, scat_sem):
        wid = _sc_worker()
        base = wid * per_w
        pltpu.sync_copy(idx_hbm.at[wid], idx_v)

        def load(g, slot):
            return pltpu.make_async_copy(src_hbm.at[pl.ds(base + g * win, win)], rows_v.at[slot], load_sem.at[slot])

        def scatter(g, slot, k):
            return pltpu.make_async_copy(rows_v.at[slot], out_hbm.at[idx_v.at[g, k]], scat_sem.at[slot])

        load(0, 0).start()

        @pl.loop(0, n_win, step=2)
        def _(g):
            for slot in range(2):
                cur = g + slot
                load(cur, slot).wait()

                @pl.when(cur + 1 < n_win)
                def _():
                    @pl.when(cur >= 1)
                    def _():
                        for k in range(n_k):
                            scatter(cur - 1, 1 - slot, k).wait()
                    load(cur + 1, 1 - slot).start()

                for k in range(n_k):
                    scatter(cur, slot, k).start()

        for slot in range(2):
            for k in range(n_k):
                scatter(n_win - 2 + slot, slot, k).wait()

    return pl.kernel(
        body, mesh=_sc_mesh(),
        out_type=jax.ShapeDtypeStruct((n_out, dp), src.dtype),
        scratch_types=[pltpu.VMEM((n_win, n_k, win), jnp.int32),
                       pltpu.VMEM((2, win, dp), src.dtype),
                       pltpu.SemaphoreType.DMA((2,)),
                       pltpu.SemaphoreType.DMA((2,))],
        compiler_params=pltpu.CompilerParams(use_tc_tiling_on_sc=True),
        name="sc_dispatch_scatter",
    )(src, idx)


def _sc_combine(table, dest, gates):
    n_k, n_tok = dest.shape
    dp = table.shape[1]
    d = 2 * dp
    lanes = SC_LANES
    assert n_tok % SC_WORKERS == 0
    per_w = n_tok // SC_WORKERS
    tok_win = SC_COMBINE_TOKENS
    rows = tok_win * n_k
    n_win = per_w // tok_win
    assert per_w % (2 * tok_win) == 0 and rows <= 128
    idx = dest.T.reshape(-1)
    gate_rows = gates.T.reshape(-1)

    def body(table_hbm, idx_hbm, gate_hbm, out_hbm, idx_v, rows_v, gate_v, out_v, gather_sem, gate_sem, put_sem):
        wid = _sc_worker()
        tok0 = wid * per_w
        pltpu.sync_copy(idx_hbm.at[pl.ds(tok0 * n_k, per_w * n_k)], idx_v)

        def gather(g, slot):
            return pltpu.make_async_copy(table_hbm.at[idx_v.at[pl.ds(g * rows, rows)]], rows_v.at[slot],
                                         gather_sem.at[slot])

        def load_gates(g, slot):
            return pltpu.make_async_copy(gate_hbm.at[pl.ds((tok0 + g * tok_win) * n_k, rows)], gate_v.at[slot],
                                         gate_sem.at[slot])

        def put(g, slot):
            return pltpu.make_async_copy(out_v.at[slot], out_hbm.at[pl.ds(tok0 + g * tok_win, tok_win)],
                                         put_sem.at[slot])

        def reduce_window(slot):
            @pl.loop(0, tok_win)
            def _(t):
                slot_idx = jnp.full((lanes,), slot, jnp.int32)
                zero_idx = jnp.zeros((lanes,), jnp.int32)
                g_k = [plsc.load_gather(gate_v, [slot_idx, zero_idx + (t * n_k + k)]) for k in range(n_k)]

                @pl.loop(0, dp // lanes, step=SC_COMBINE_UNROLL)
                def _(v0):
                    for u in range(SC_COMBINE_UNROLL):
                        v = v0 + u
                        words = pl.ds(v * lanes, lanes)
                        acc_lo = jnp.zeros((lanes,), F32)
                        acc_hi = jnp.zeros((lanes,), F32)
                        for k in range(n_k):
                            w = rows_v[slot, t * n_k + k, words]
                            acc_lo = acc_lo + g_k[k] * plsc.bitcast(w << 16, F32)
                            acc_hi = acc_hi + g_k[k] * plsc.bitcast(w & HIGH_HALF_MASK, F32)
                        out_v[slot, t, words] = acc_lo
                        out_v[slot, t, pl.ds(dp + v * lanes, lanes)] = acc_hi

        gather(0, 0).start()
        load_gates(0, 0).start()

        @pl.loop(0, n_win, step=2)
        def _(g):
            for slot in range(2):
                cur = g + slot
                gather(cur, slot).wait()
                load_gates(cur, slot).wait()

                @pl.when(cur + 1 < n_win)
                def _():
                    gather(cur + 1, 1 - slot).start()
                    load_gates(cur + 1, 1 - slot).start()

                @pl.when(cur >= 2)
                def _():
                    put(cur - 2, slot).wait()

                reduce_window(slot)
                put(cur, slot).start()

        for slot in range(2):
            put(n_win - 2 + slot, slot).wait()

    return pl.kernel(
        body, mesh=_sc_mesh(),
        out_type=jax.ShapeDtypeStruct((n_tok, d), F32),
        scratch_types=[pltpu.VMEM((per_w * n_k,), jnp.int32),
                       pltpu.VMEM((2, rows, dp), table.dtype),
                       pltpu.VMEM((2, rows), F32),
                       pltpu.VMEM((2, tok_win, d), F32),
                       pltpu.SemaphoreType.DMA((2,)),
                       pltpu.SemaphoreType.DMA((2,)),
                       pltpu.SemaphoreType.DMA((2,))],
        compiler_params=pltpu.CompilerParams(use_tc_tiling_on_sc=True, needs_layout_passes=False),
        name="sc_combine_reduce",
    )(table, idx, gate_rows)


def _moe_out_kernel(alpha, x_ref, h2_ref, r_ref, mod_ref, ws13_ref, ws2_ref, lng_ref, lnb_ref, o_ref):
    de = ws2_ref.shape[0]
    z = _dot_packed(h2_ref[0], ws13_ref)
    hmid = (_silu(z[:, :de]) * z[:, de:]).astype(BF16)
    y = _dot(hmid, ws2_ref[...]) + r_ref[0]
    o_ref[0] = _layer_norm(alpha * x_ref[0] + mod_ref[0, 5:6, :] * y, lng_ref[...], lnb_ref[...])


def _moe_out(alpha, xn, h2, routed, mod, ws13, ws2, lng, lnb, nt, ntx):
    nb, _, d = xn.shape
    return pl.pallas_call(
        functools.partial(_moe_out_kernel, alpha),
        out_shape=jax.ShapeDtypeStruct((nb, nt * SEQ_TILE, d), F32),
        grid=(nb, nt),
        in_specs=[_tok_spec(d), _tok_spec(d // 2), _tok_spec(d), _mod_spec(d, nb, ntx),
                  _full_spec(ws13.shape), _full_spec(ws2.shape), _full_spec(lng.shape), _full_spec(lnb.shape)],
        out_specs=_tok_spec(d),
        compiler_params=_params(2),
        name="moe_out",
    )(xn, h2, routed, mod, ws13, ws2, lng, lnb)


def _dispatch_plan(counts, topi, rank, n_tok):
    n_e = counts.shape[0]
    bm = MOE_BLOCK_ROWS
    padded = (counts + bm - 1) // bm * bm
    pad_end = jnp.cumsum(padded)
    pad_start = pad_end - padded
    onehot = topi[:, None, :] == jnp.arange(n_e, dtype=jnp.int32)[None, :, None]
    dest = rank + jnp.sum(jnp.where(onehot, pad_start[None, :, None], 0), axis=1)
    n_blocks = n_tok * TOP_K // bm + n_e
    block_start = jnp.arange(n_blocks, dtype=jnp.int32) * bm
    block_expert = jnp.sum((pad_end[None, :] <= block_start[:, None]).astype(jnp.int32), axis=1)
    block_expert = jnp.minimum(block_expert, n_e - 1)
    block_rows = jnp.clip(jnp.take(pad_start + counts, block_expert) - block_start, 0, bm).astype(jnp.int32)
    return dest.astype(jnp.int32), block_expert, block_rows, n_blocks * bm


def _rope_tables(seq):
    n_freq = QK_ROPE // 4
    inv_freq = ROPE_THETA ** (-jnp.arange(n_freq, dtype=F32) / n_freq)
    pos = jnp.arange(seq, dtype=jnp.int32)
    r = (pos // GRID_W).astype(F32)
    col = (pos % GRID_W).astype(F32)
    ang = jnp.concatenate([r[:, None] * inv_freq, col[:, None] * inv_freq], -1)
    cos, sin = jnp.cos(ang), jnp.sin(ang)
    zeros = jnp.zeros((seq, LANE - QK_ROPE), F32)
    cos_slot = jnp.concatenate([cos, cos, zeros], -1)
    sin_slot = jnp.concatenate([-sin, sin, zeros], -1)
    return cos_slot, sin_slot


def _rope_slot_weights(w_rope):
    k, n, _ = w_rope.shape
    half = QK_ROPE // 2
    swapped = jnp.concatenate([w_rope[..., half:], w_rope[..., :half]], -1)
    pad = jnp.zeros((k, n, LANE - QK_ROPE), w_rope.dtype)
    plain = jnp.concatenate([w_rope, pad], -1).reshape(k, n * LANE)
    swp = jnp.concatenate([swapped, pad], -1).reshape(k, n * LANE)
    return jnp.concatenate([plain, swp], -1)


def kernel(x, c, ctx, c_ctx, ada_w, ada_b, ln_g, ln_b, conf_w1, conf_b1, conf_dw, conf_dwb, conf_ng, conf_nb, conf_w2, conf_b2, sc_w_in, sc_dw, sc_w_out, mla_w_dqkv, mla_q_g, mla_kv_g, mla_w_uq, mla_w_uk, mla_w_uv, mla_w_o, moe_router, moe_bias, moe_w1, moe_w3, moe_w2, sh_w1, sh_w3, sh_w2):
    nb, seq, d = x.shape
    l_ctx = ctx.shape[1]
    depth = ada_w.shape[0]
    alpha = (2.0 * depth) ** 0.25
    tl = SEQ_TILE
    assert seq % tl == 0 and l_ctx % tl == 0 and seq % Q_TILE == 0 and seq % GRID_W == 0
    ntx = seq // tl
    nt_all = (seq + l_ctx) // tl
    attn_layers = [i for i in range(depth) if i % N_MIXERS == 2]
    last_ctx_reader = attn_layers[-1] if attn_layers else -1

    rows = -(-(nb + 1) // 8) * 8
    c_all = jnp.zeros((rows, d), F32).at[:nb].set(c).at[nb].set(c_ctx)
    mod_all = _modulation(c_all, ada_w, ada_b).reshape(depth, rows, N_MOD, d)

    assert nb % N_CHAINS == 0
    nbc = nb // N_CHAINS
    chains = [jnp.concatenate([x[c0:c0 + nbc], ctx[c0:c0 + nbc]], axis=1) for c0 in range(0, nb, nbc)]
    q_rank, kv_rank = mla_q_g.shape[1], mla_kv_g.shape[1]
    cos_t, sin_t = _rope_tables(seq)
    row = lambda v: v.reshape(1, -1)

    for i in range(depth):
        need_ctx = i < last_ctx_reader
        kind, j = i % N_MIXERS, i // N_MIXERS
        nt = nt_all if need_ctx else ntx
        n_tok = nbc * nt * tl

        if kind == 0:
            w_first = conf_w1[j].astype(BF16)
            dw_tiles = jnp.broadcast_to(conf_dw[j][:, None, :], (conf_dw.shape[1], SUBLANES, d))
            w_last, b_last = conf_w2[j].astype(BF16), row(conf_b2[j])
        elif kind == 1:
            w_first = sc_w_in[j].astype(BF16)
            w_last, b_last = sc_w_out[j].astype(BF16), jnp.zeros((1, d), F32)
        else:
            wdq = mla_w_dqkv[j]
            wd = wdq[:, :q_rank + kv_rank].astype(BF16)
            wkp = _rope_slot_weights(wdq[:, None, q_rank + kv_rank:]).astype(BF16)
            wuq = mla_w_uq[j].reshape(q_rank, MLA_HEADS, QK_NOPE + QK_ROPE)
            wqn = wuq[:, :, :QK_NOPE].reshape(q_rank, MLA_HEADS * QK_NOPE).astype(BF16)
            wqr = _rope_slot_weights(wuq[:, :, QK_NOPE:]).astype(BF16)
            wuk, wuv = mla_w_uk[j].astype(BF16), mla_w_uv[j].astype(BF16)
            w_last, b_last = mla_w_o[j].astype(BF16), jnp.zeros((1, d), F32)
        rw_hi, rw_lo = _split_bf16(moe_router[i].T)
        ws13 = jnp.concatenate([sh_w1[i], sh_w3[i]], axis=-1).astype(BF16)
        ws2 = sh_w2[i].astype(BF16)

        for ci in range(N_CHAINS):
            x_all = chains[ci]
            mod = jnp.concatenate([mod_all[i, ci * nbc:(ci + 1) * nbc], mod_all[i, nb:nb + 1]], axis=0)

            if kind == 0:
                u = _conf_in(x_all, mod, w_first, row(conf_b1[j]), nt, ntx)
                a = _conf_conv(u, dw_tiles, row(conf_dwb[j]), row(conf_ng[j]), row(conf_nb[j]), nt, ntx)
            elif kind == 1:
                a = _sc_conv(x_all, mod, w_first, sc_dw[j], nt, ntx)
            else:
                q, k, v = _mla_proj(x_all, mod, wd, wkp, row(mla_q_g[j]), row(mla_kv_g[j]), wqn, wqr,
                                    wuk, wuv, cos_t, sin_t, nt_all, ntx)
                a = _attention(q, k, v, nt * tl)

            xn, h2, topi, topw, rank, counts = _post(alpha, a, w_last, b_last, x_all, mod, row(ln_g[i, 0]),
                                                     row(ln_b[i, 0]), rw_hi, rw_lo, moe_bias[i].reshape(-1, 1),
                                                     nt, ntx)

            dest, block_expert, block_rows, n_rows = _dispatch_plan(counts[:, 0], topi, rank, n_tok)
            xs = _sc_row_scatter(h2.reshape(n_tok, d // 2), dest, n_rows)
            ys = _grouped_ffn(block_expert, block_rows, xs, moe_w1, moe_w3, moe_w2, i)
            routed = _sc_combine(ys, dest, topw).reshape(nbc, nt * tl, d)

            chains[ci] = _moe_out(alpha, xn, h2, routed, mod, ws13, ws2,
                                  row(ln_g[i, 1]), row(ln_b[i, 1]), nt, ntx)
    return jnp.concatenate([xc[:, :seq] for xc in chains], axis=0)
```

```python
import functools

import numpy as np
import jax
import jax.numpy as jnp
from jax import lax
from jax.experimental import pallas as pl
from jax.experimental.pallas import tpu as pltpu
from jax.experimental.pallas import tpu_sc as plsc

F32 = jnp.float32
BF16 = jnp.bfloat16
U32 = jnp.uint32
HIGH_HALF_MASK = np.uint32(0xFFFF0000)

GRID_W = 64
N_MIXERS = 3
LN_EPS = 1e-5
RMS_EPS = 1e-6
N_MOD = 6
MLA_HEADS = 8
QK_NOPE = 128
QK_ROPE = 64
V_HEAD = 128
ROPE_THETA = 10000.0
ATTN_SCALE = (QK_NOPE + QK_ROPE) ** -0.5
Q_SCALE = ATTN_SCALE * 1.4426950408889634
TOP_K = 8
N_GROUPS = 8
TOPK_GROUPS = 4
ROUTED_SCALE = 2.5

SEQ_TILE = 256
CONV_HALO = 16
SHORT_HALO = 8
CONV_ROW_CHUNK = 64
LANE = 128
SUBLANES = 8
Q_TILE = 1024
ATTN_SUB_ROWS = 256
ATTN_ROW_CHUNK = 16
MOE_BLOCK_ROWS = 1024
ROUTE_SUB_TOKENS = 128
MOD_COL_TILE = 1536
VMEM_LIMIT = 48 * 1024 * 1024
SC_CORES = 2
SC_SUBCORES = 16
SC_WORKERS = SC_CORES * SC_SUBCORES
SC_LANES = 16
SC_COMBINE_UNROLL = 4
SC_COMBINE_TOKENS = 8
SC_WINDOWS = (64, 32, 16)
N_CHAINS = 1


def _params(n_axes):
    return pltpu.CompilerParams(dimension_semantics=("arbitrary",) * n_axes,
                                vmem_limit_bytes=VMEM_LIMIT)


def _split_bf16(a):
    hi = a.astype(BF16)
    lo = (a - hi.astype(F32)).astype(BF16)
    return hi, lo


def _dot(a, b):
    return jnp.dot(a, b, preferred_element_type=F32)


def _dot_nt(a, b):
    return lax.dot_general(a, b, (((1,), (1,)), ((), ())), preferred_element_type=F32)


def _pack_halves(v):
    half = v.shape[-1] // 2
    lo = lax.bitcast_convert_type(v[:, :half].astype(BF16).astype(F32), U32) >> 16
    hi = lax.bitcast_convert_type(v[:, half:].astype(BF16).astype(F32), U32) & HIGH_HALF_MASK
    return hi | lo


def _unpack_halves(p):
    lo = lax.bitcast_convert_type(p << 16, F32)
    hi = lax.bitcast_convert_type(p & HIGH_HALF_MASK, F32)
    return lo, hi


def _dot_packed(p, w):
    lo, hi = _unpack_halves(p)
    return _dot(jnp.concatenate([lo.astype(BF16), hi.astype(BF16)], axis=1), w[...])


def _layer_norm(v, g, b):
    mu = jnp.mean(v, axis=-1, keepdims=True)
    c = v - mu
    var = jnp.mean(c * c, axis=-1, keepdims=True)
    return c * lax.rsqrt(var + LN_EPS) * g + b


def _silu(v):
    return v * jax.nn.sigmoid(v)


def _mod_kernel(c_ref, w_ref, b_ref, o_ref):
    a = _silu(c_ref[...])
    a_hi, a_lo = _split_bf16(a)
    w_hi, w_lo = _split_bf16(w_ref[0])
    o_ref[0] = _dot(a_hi, w_hi) + _dot(a_hi, w_lo) + _dot(a_lo, w_hi) + b_ref[0]


def _modulation(c_all, ada_w, ada_b):
    depth, d, n = ada_w.shape
    rows = c_all.shape[0]
    tn = MOD_COL_TILE
    return pl.pallas_call(
        _mod_kernel,
        out_shape=jax.ShapeDtypeStruct((depth, rows, n), F32),
        grid=(depth, n // tn),
        in_specs=[pl.BlockSpec((rows, d), lambda i, j: (0, 0)),
                  pl.BlockSpec((1, d, tn), lambda i, j: (i, 0, j)),
                  pl.BlockSpec((1, 1, tn), lambda i, j: (i, 0, j))],
        out_specs=pl.BlockSpec((1, rows, tn), lambda i, j: (i, 0, j)),
        compiler_params=_params(2),
        name="adaln_mod",
    )(c_all, ada_w, ada_b.reshape(depth, 1, n))


def _tok_spec(d, tl=SEQ_TILE):
    return pl.BlockSpec((1, tl, d), lambda b, l: (b, l, 0))


def _mod_spec(d, n_batch, ntx):
    return pl.BlockSpec((1, N_MOD, d), lambda b, l: (jnp.where(l < ntx, b, n_batch), 0, 0))


def _full_spec(shape):
    zeros = (0,) * len(shape)
    return pl.BlockSpec(shape, lambda b, l: zeros)


def _halo_specs(d, halo, seq_len, tl=SEQ_TILE):
    per_tile = tl // halo
    last = seq_len // halo - 1
    prev = pl.BlockSpec((1, halo, d), lambda b, l: (b, jnp.maximum(l * per_tile - 1, 0), 0))
    nxt = pl.BlockSpec((1, halo, d), lambda b, l: (b, jnp.minimum((l + 1) * per_tile, last), 0))
    return prev, nxt


def _segment_edges(l, ntx, nt):
    first = jnp.logical_or(l == 0, l == ntx)
    last = jnp.logical_or(l == ntx - 1, l == nt - 1)
    return first, last


def _window_rows(prev_ref, cur_ref, next_ref, win_ref):
    halo, tl = prev_ref.shape[1], cur_ref.shape[1]
    win_ref[0:halo, :] = prev_ref[0]
    win_ref[halo:halo + tl, :] = cur_ref[0]
    win_ref[halo + tl:, :] = next_ref[0]
    return win_ref[...]


def _edge_mask(n_rows, halo, first, last):
    r = lax.broadcasted_iota(jnp.int32, (n_rows, 1), 0)
    outside = jnp.logical_or(jnp.logical_and(first, r < halo), jnp.logical_and(last, r >= n_rows - halo))
    return jnp.where(outside, 0.0, 1.0)


def _conf_in_kernel(x_ref, mod_ref, w1_ref, b1_ref, u_ref):
    d = x_ref.shape[-1]
    h = x_ref[0] * (1.0 + mod_ref[0, 1:2, :]) + mod_ref[0, 0:1, :]
    z = _dot(h.astype(BF16), w1_ref[...]) + b1_ref[...]
    u_ref[0] = z[:, :d] * jax.nn.sigmoid(z[:, d:])


def _conf_in(x_all, mod, w1, b1, nt, ntx):
    nb, seq_len, d = x_all.shape
    return pl.pallas_call(
        _conf_in_kernel,
        out_shape=jax.ShapeDtypeStruct((nb, nt * SEQ_TILE, d), F32),
        grid=(nb, nt),
        in_specs=[_tok_spec(d), _mod_spec(d, nb, ntx), _full_spec(w1.shape), _full_spec(b1.shape)],
        out_specs=_tok_spec(d),
        compiler_params=_params(2),
        name="conf_in",
    )(x_all, mod, w1, b1)


def _conf_conv_kernel(ntx, nt, up_ref, uc_ref, un_ref, dw_ref, dwb_ref, ng_ref, nb_ref,
                      a_ref, sh_ref, conv_ref):
    tl, d = uc_ref.shape[1], uc_ref.shape[2]
    taps = dw_ref.shape[0]
    lead = CONV_HALO - (taps - 1) // 2
    first, last = _segment_edges(pl.program_id(1), ntx, nt)
    sh_ref[0, 0:CONV_HALO, :] = jnp.where(first, 0.0, up_ref[0])
    sh_ref[0, CONV_HALO:CONV_HALO + tl, :] = uc_ref[0]
    sh_ref[0, CONV_HALO + tl:, :] = jnp.where(last, 0.0, un_ref[0])
    span = tl + 2 * CONV_HALO - SUBLANES
    for s in range(1, SUBLANES):
        sh_ref[s, 0:span, :] = sh_ref[0, s:s + span, :]
    groups = CONV_ROW_CHUNK // SUBLANES

    def row_chunk(i, carry):
        r0 = pl.multiple_of(i * CONV_ROW_CHUNK, CONV_ROW_CHUNK)
        for c0 in range(0, d, LANE):
            accs = [jnp.zeros((SUBLANES, LANE), F32) for _ in range(groups)]
            for k in range(taps):
                res = (lead + k) % SUBLANES
                off = lead + k - res
                w = dw_ref[k, :, c0:c0 + LANE]
                for g in range(groups):
                    lo = r0 + (off + g * SUBLANES)
                    accs[g] = accs[g] + w * sh_ref[res, pl.ds(lo, SUBLANES), c0:c0 + LANE]
            for g in range(groups):
                conv_ref[pl.ds(r0 + g * SUBLANES, SUBLANES), c0:c0 + LANE] = accs[g]
        return carry

    lax.fori_loop(0, tl // CONV_ROW_CHUNK, row_chunk, 0)
    v = _layer_norm(conv_ref[...] + dwb_ref[...], ng_ref[...], nb_ref[...])
    a_ref[0] = _silu(v).astype(BF16)


def _conf_conv(u, dw, dwb, ng, nb_, nt, ntx):
    nb, seq_len, d = u.shape
    prev, nxt = _halo_specs(d, CONV_HALO, seq_len)
    return pl.pallas_call(
        functools.partial(_conf_conv_kernel, ntx, nt),
        out_shape=jax.ShapeDtypeStruct((nb, seq_len, d), BF16),
        grid=(nb, nt),
        in_specs=[prev, _tok_spec(d), nxt, _full_spec(dw.shape), _full_spec(dwb.shape),
                  _full_spec(ng.shape), _full_spec(nb_.shape)],
        out_specs=_tok_spec(d),
        scratch_shapes=[pltpu.VMEM((SUBLANES, SEQ_TILE + 2 * CONV_HALO, d), F32),
                        pltpu.VMEM((SEQ_TILE, d), F32)],
        compiler_params=_params(2),
        name="conf_conv",
    )(u, u, u, dw, dwb, ng, nb_)


def _sc_conv_kernel(ntx, nt, xp_ref, xc_ref, xn_ref, mod_ref, w_ref, dw_ref, a_ref, win_ref, ext_ref):
    tl, d = xc_ref.shape[1], xc_ref.shape[2]
    taps = dw_ref.shape[0]
    lead = SHORT_HALO - (taps - 1) // 2
    first, last = _segment_edges(pl.program_id(1), ntx, nt)
    xw = _window_rows(xp_ref, xc_ref, xn_ref, win_ref)
    h = xw * (1.0 + mod_ref[0, 1:2, :]) + mod_ref[0, 0:1, :]
    z = _dot(h.astype(BF16), w_ref[...])
    gb = z[SHORT_HALO:SHORT_HALO + tl, :d]
    ext_ref[...] = z[:, d:2 * d] * z[:, 2 * d:] * _edge_mask(z.shape[0], SHORT_HALO, first, last)
    acc = dw_ref[0:1, :] * ext_ref[lead:lead + tl, :]
    for k in range(1, taps):
        acc = acc + dw_ref[k:k + 1, :] * ext_ref[lead + k:lead + k + tl, :]
    a_ref[0] = (gb * acc).astype(BF16)


def _sc_conv(x_all, mod, w_in, dw, nt, ntx):
    nb, seq_len, d = x_all.shape
    prev, nxt = _halo_specs(d, SHORT_HALO, seq_len)
    win = pltpu.VMEM((SEQ_TILE + 2 * SHORT_HALO, d), F32)
    return pl.pallas_call(
        functools.partial(_sc_conv_kernel, ntx, nt),
        out_shape=jax.ShapeDtypeStruct((nb, nt * SEQ_TILE, d), BF16),
        grid=(nb, nt),
        in_specs=[prev, _tok_spec(d), nxt, _mod_spec(d, nb, ntx), _full_spec(w_in.shape), _full_spec(dw.shape)],
        out_specs=_tok_spec(d),
        scratch_shapes=[win, win],
        compiler_params=_params(2),
        name="sc_conv",
    )(x_all, x_all, x_all, mod, w_in, dw)


def _rms(v, g):
    return v * lax.rsqrt(jnp.mean(v * v, axis=-1, keepdims=True) + RMS_EPS) * g


def _mla_proj_kernel(ntx, q_rank,
                     x_ref, mod_ref, wd_ref, wkp_ref, qg_ref, kvg_ref, wqn_ref, wqr_ref, wuk_ref, wuv_ref,
                     cos_ref, sin_ref, q_ref, k_ref, v_ref):
    is_latent = pl.program_id(1) < ntx
    h = (x_ref[0] * (1.0 + mod_ref[0, 1:2, :]) + mod_ref[0, 0:1, :]).astype(BF16)
    dn = _dot(h, wd_ref[...])
    cq = _rms(dn[:, :q_rank], qg_ref[...]).astype(BF16)
    ckv = _rms(dn[:, q_rank:], kvg_ref[...]).astype(BF16)
    cos = jnp.where(is_latent, cos_ref[...], 1.0)
    sin = jnp.where(is_latent, sin_ref[...], 0.0)
    kp2 = _dot(h, wkp_ref[...])
    kp = kp2[:, :LANE] * cos + kp2[:, LANE:] * sin
    kn = _dot(ckv, wuk_ref[...])
    vv = _dot(ckv, wuv_ref[...])
    qn = _dot(cq, wqn_ref[...]) * Q_SCALE
    qr2 = _dot(cq, wqr_ref[...])
    hw = MLA_HEADS * LANE
    for hd in range(MLA_HEADS):
        sl = slice(hd * LANE, (hd + 1) * LANE)
        qr = (qr2[:, sl] * cos + qr2[:, hw + hd * LANE:hw + (hd + 1) * LANE] * sin) * Q_SCALE
        q_ref[0, hd, :, 0:LANE] = qn[:, sl].astype(BF16)
        q_ref[0, hd, :, LANE:] = qr.astype(BF16)
        k_ref[0, hd, :, 0:LANE] = kn[:, sl].astype(BF16)
        k_ref[0, hd, :, LANE:] = kp.astype(BF16)
        v_ref[0, hd] = vv[:, sl].astype(BF16)


def _mla_proj(x_all, mod, wd, wkp, qg, kvg, wqn, wqr, wuk, wuv, cos_t, sin_t, nt, ntx):
    nb, seq_len, d = x_all.shape
    q_rank = qg.shape[-1]
    tl = SEQ_TILE
    rope_spec = pl.BlockSpec((tl, LANE), lambda b, l: (jnp.minimum(l, ntx - 1), 0))
    qk_shape = jax.ShapeDtypeStruct((nb, MLA_HEADS, nt * tl, 2 * LANE), BF16)
    v_shape = jax.ShapeDtypeStruct((nb, MLA_HEADS, nt * tl, LANE), BF16)
    qk_spec = pl.BlockSpec((1, MLA_HEADS, tl, 2 * LANE), lambda b, l: (b, 0, l, 0))
    v_spec = pl.BlockSpec((1, MLA_HEADS, tl, LANE), lambda b, l: (b, 0, l, 0))
    return pl.pallas_call(
        functools.partial(_mla_proj_kernel, ntx, q_rank),
        out_shape=(qk_shape, qk_shape, v_shape),
        grid=(nb, nt),
        in_specs=[_tok_spec(d), _mod_spec(d, nb, ntx), _full_spec(wd.shape), _full_spec(wkp.shape),
                  _full_spec(qg.shape), _full_spec(kvg.shape), _full_spec(wqn.shape), _full_spec(wqr.shape),
                  _full_spec(wuk.shape), _full_spec(wuv.shape), rope_spec, rope_spec],
        out_specs=(qk_spec, qk_spec, v_spec),
        compiler_params=_params(2),
        name="mla_proj",
    )(x_all, mod, wd, wkp, qg, kvg, wqn, wqr, wuk, wuv, cos_t, sin_t)


def _attn_kernel(q_ref, k_ref, v_ref, o_ref, s_ref, p_ref, l_ref):
    tq = q_ref.shape[2]
    for r0 in range(0, tq, ATTN_SUB_ROWS):
        rows = slice(r0, r0 + ATTN_SUB_ROWS)
        s_ref[rows, :] = _dot_nt(q_ref[0, 0, rows, :], k_ref[0, 0])
        for c0 in range(r0, r0 + ATTN_SUB_ROWS, ATTN_ROW_CHUNK):
            chunk = slice(c0, c0 + ATTN_ROW_CHUNK)
            s = s_ref[chunk, :]
            p = jnp.exp2(s - jnp.max(s, axis=-1, keepdims=True))
            l_ref[chunk, :] = jnp.broadcast_to(jnp.sum(p, axis=-1, keepdims=True), (ATTN_ROW_CHUNK, V_HEAD))
            p_ref[chunk, :] = p.astype(BF16)
        o = _dot(p_ref[rows, :], v_ref[0, 0])
        o_ref[0, rows, :] = (o / l_ref[rows, :]).astype(BF16)


def _attention(q, k, v, n_q):
    nb, nh, n_k, dk = k.shape
    tq = Q_TILE
    return pl.pallas_call(
        _attn_kernel,
        out_shape=jax.ShapeDtypeStruct((nb, n_q, nh * V_HEAD), BF16),
        grid=(nb, nh, n_q // tq),
        in_specs=[pl.BlockSpec((1, 1, tq, dk), lambda b, h, i: (b, h, i, 0)),
                  pl.BlockSpec((1, 1, n_k, dk), lambda b, h, i: (b, h, 0, 0)),
                  pl.BlockSpec((1, 1, n_k, V_HEAD), lambda b, h, i: (b, h, 0, 0))],
        out_specs=pl.BlockSpec((1, tq, V_HEAD), lambda b, h, i: (b, i, h)),
        scratch_shapes=[pltpu.VMEM((tq, n_k), F32), pltpu.VMEM((tq, n_k), BF16), pltpu.VMEM((tq, V_HEAD), F32)],
        compiler_params=_params(3),
        name="mla_attn",
    )(q, k, v)


def _route(sel, s):
    n_e, n_t = sel.shape
    per = n_e // N_GROUPS
    sel3 = sel.reshape(N_GROUPS, per, n_t)
    s3 = s.reshape(N_GROUPS, per, n_t)
    iota_p = lax.broadcasted_iota(jnp.int32, (N_GROUPS, per, n_t), 1).astype(F32)
    iota_g = lax.broadcasted_iota(jnp.int32, (N_GROUPS, 1, n_t), 0).astype(F32)
    neg = -jnp.inf
    m1 = jnp.max(sel3, axis=1, keepdims=True)
    first = jnp.min(jnp.where(sel3 == m1, iota_p, float(per)), axis=1, keepdims=True)
    m2 = jnp.max(jnp.where(iota_p == first, neg, sel3), axis=1, keepdims=True)
    gs = m1 + m2
    gsel = jnp.zeros((N_GROUPS, 1, n_t), F32)
    for _ in range(TOPK_GROUPS):
        gm = jnp.max(gs, axis=0, keepdims=True)
        gfirst = jnp.min(jnp.where(gs == gm, iota_g, float(N_GROUPS)), axis=0, keepdims=True)
        pick = iota_g == gfirst
        gsel = jnp.where(pick, 1.0, gsel)
        gs = jnp.where(pick, neg, gs)
    val = jnp.where(gsel > 0.0, sel3, neg)
    iota_e = lax.broadcasted_iota(jnp.int32, (N_GROUPS, per, n_t), 0).astype(F32) * per + iota_p
    chosen = jnp.zeros((N_GROUPS, per, n_t), F32)
    idx, wts = [], []
    for _ in range(TOP_K):
        m = jnp.max(jnp.max(val, axis=1, keepdims=True), axis=0, keepdims=True)
        e = jnp.min(jnp.min(jnp.where(val == m, iota_e, float(n_e)), axis=1, keepdims=True), axis=0, keepdims=True)
        pick = iota_e == e
        wts.append(jnp.sum(jnp.sum(jnp.where(pick, s3, 0.0), axis=1, keepdims=True), axis=0, keepdims=True))
        idx.append(e)
        chosen = jnp.where(pick, 1.0, chosen)
        val = jnp.where(pick, neg, val)
    total = wts[0]
    for w in wts[1:]:
        total = total + w
    wts = [w / total * ROUTED_SCALE for w in wts]
    return idx, wts, chosen, iota_e


def _post_kernel(alpha, a_ref, w_ref, b_ref, x_ref, mod_ref, lng_ref, lnb_ref, rw_hi_ref, rw_lo_ref, rb_ref, tri_ref,
                 xn_ref, h2_ref, topi_ref, topw_ref, rank_ref, cnt_ref, run_ref):
    first_step = jnp.logical_and(pl.program_id(0) == 0, pl.program_id(1) == 0)

    @pl.when(first_step)
    def _():
        run_ref[...] = jnp.zeros_like(run_ref)

    y = _dot(a_ref[0], w_ref[...]) + b_ref[...]
    run = run_ref[...]
    for t0 in range(0, y.shape[0], ROUTE_SUB_TOKENS):
        rows = slice(t0, t0 + ROUTE_SUB_TOKENS)
        xn = _layer_norm(alpha * x_ref[0, rows, :] + mod_ref[0, 2:3, :] * y[rows], lng_ref[...], lnb_ref[...])
        xn_ref[0, rows, :] = xn
        h2 = xn * (1.0 + mod_ref[0, 4:5, :]) + mod_ref[0, 3:4, :]
        h_hi, h_lo = _split_bf16(h2)
        h2_ref[0, rows, :] = _pack_halves(h2)
        logits = _dot_nt(rw_hi_ref[...], h_hi) + _dot_nt(rw_hi_ref[...], h_lo) + _dot_nt(rw_lo_ref[...], h_hi)
        s = jax.nn.sigmoid(logits)
        idx, wts, chosen3, iota_e = _route(s + rb_ref[...], s)
        n_e, n_t = s.shape
        chosen = chosen3.reshape(n_e, n_t)
        rank = run[:, 0:1] + _dot(chosen.astype(BF16), tri_ref[...])
        rank3 = rank.reshape(chosen3.shape)
        for k in range(TOP_K):
            rk = jnp.sum(jnp.sum(jnp.where(iota_e == idx[k], rank3, 0.0), axis=1, keepdims=True), axis=0,
                         keepdims=True)
            topi_ref[k:k + 1, rows] = idx[k].reshape(1, n_t).astype(jnp.int32)
            topw_ref[k:k + 1, rows] = wts[k].reshape(1, n_t)
            rank_ref[k:k + 1, rows] = rk.reshape(1, n_t).astype(jnp.int32)
        run = run + jnp.sum(chosen, axis=1, keepdims=True)
    run_ref[...] = run
    cnt_ref[...] = run.astype(jnp.int32)


def _post(alpha, a, w, bias, x_all, mod, lng, lnb, rw_hi, rw_lo, rb, nt, ntx):
    nb, _, dk = a.shape
    d = x_all.shape[-1]
    n_e = rw_hi.shape[0]
    tl = SEQ_TILE
    n_tok = nb * nt * tl
    col_spec = lambda rows: pl.BlockSpec((rows, tl), lambda b, l: (0, b * nt + l))
    sub = ROUTE_SUB_TOKENS
    row_i = lax.broadcasted_iota(jnp.int32, (sub, sub), 0)
    col_i = lax.broadcasted_iota(jnp.int32, (sub, sub), 1)
    tri = jnp.where(row_i < col_i, 1.0, 0.0).astype(BF16)
    return pl.pallas_call(
        functools.partial(_post_kernel, alpha),
        out_shape=(jax.ShapeDtypeStruct((nb, nt * tl, d), F32),
                   jax.ShapeDtypeStruct((nb, nt * tl, d // 2), U32),
                   jax.ShapeDtypeStruct((TOP_K, n_tok), jnp.int32),
                   jax.ShapeDtypeStruct((TOP_K, n_tok), F32),
                   jax.ShapeDtypeStruct((TOP_K, n_tok), jnp.int32),
                   jax.ShapeDtypeStruct((n_e, LANE), jnp.int32)),
        grid=(nb, nt),
        in_specs=[_tok_spec(dk), _full_spec(w.shape), _full_spec(bias.shape), _tok_spec(d),
                  _mod_spec(d, nb, ntx), _full_spec(lng.shape), _full_spec(lnb.shape),
                  _full_spec(rw_hi.shape), _full_spec(rw_lo.shape), _full_spec(rb.shape), _full_spec(tri.shape)],
        out_specs=(_tok_spec(d), _tok_spec(d // 2), col_spec(TOP_K), col_spec(TOP_K), col_spec(TOP_K),
                   _full_spec((n_e, LANE))),
        scratch_shapes=[pltpu.VMEM((n_e, LANE), F32)],
        compiler_params=_params(2),
        name="mixer_post",
    )(a, w, bias, x_all, mod, lng, lnb, rw_hi, rw_lo, rb, tri)


def _gmm_kernel(be_ref, br_ref, xs_ref, w1_ref, w3_ref, w2_ref, ys_ref, w13_s, w2_s):
    i = pl.program_id(0)
    de = w2_ref.shape[2]
    changed = jnp.logical_or(i == 0, be_ref[i] != be_ref[jnp.maximum(i - 1, 0)])

    @pl.when(changed)
    def _():
        w13_s[:, :de] = w1_ref[0, 0].astype(BF16)
        w13_s[:, de:] = w3_ref[0, 0].astype(BF16)
        w2_s[...] = w2_ref[0, 0].astype(BF16)

    @pl.when(br_ref[i] > 0)
    def _():
        z = _dot_packed(xs_ref[...], w13_s)
        hmid = (_silu(z[:, :de]) * z[:, de:]).astype(BF16)
        ys_ref[...] = _pack_halves(_dot(hmid, w2_s[...]))


def _grouped_ffn(block_expert, block_rows, xs, w1, w3, w2, layer):
    n_rows, dp = xs.shape
    d = 2 * dp
    bm = MOE_BLOCK_ROWS
    de = w2.shape[2]
    grid_spec = pltpu.PrefetchScalarGridSpec(
        num_scalar_prefetch=2,
        grid=(n_rows // bm,),
        in_specs=[pl.BlockSpec((bm, dp), lambda i, be, br: (i, 0)),
                  pl.BlockSpec((1, 1, d, de), lambda i, be, br: (layer, be[i], 0, 0)),
                  pl.BlockSpec((1, 1, d, de), lambda i, be, br: (layer, be[i], 0, 0)),
                  pl.BlockSpec((1, 1, de, d), lambda i, be, br: (layer, be[i], 0, 0))],
        out_specs=pl.BlockSpec((bm, dp), lambda i, be, br: (i, 0)),
        scratch_shapes=[pltpu.VMEM((d, 2 * de), BF16), pltpu.VMEM((de, d), BF16)],
    )
    return pl.pallas_call(
        _gmm_kernel,
        out_shape=jax.ShapeDtypeStruct((n_rows, dp), U32),
        grid_spec=grid_spec,
        compiler_params=_params(1),
        name="moe_grouped_ffn",
    )(block_expert, block_rows, xs, w1, w3, w2)


def _sc_mesh():
    return plsc.VectorSubcoreMesh(core_axis_name="c", subcore_axis_name="s",
                                  num_cores=SC_CORES, num_subcores=SC_SUBCORES)


def _sc_worker():
    return lax.axis_index("s") * SC_CORES + lax.axis_index("c")


def _sc_window(rows_per_worker):
    for win in SC_WINDOWS:
        if rows_per_worker % (2 * win) == 0:
            return win
    raise ValueError(f"no SparseCore window divides {rows_per_worker} rows per worker")


def _sc_row_scatter(src, dest, n_out):
    n_k, n_tok = dest.shape
    dp = src.shape[1]
    assert n_tok % SC_WORKERS == 0
    per_w = n_tok // SC_WORKERS
    win = _sc_window(per_w)
    n_win = per_w // win
    idx = dest.reshape(n_k, SC_WORKERS, n_win, win).transpose(1, 2, 0, 3)

    def body(src_hbm, idx_hbm, out_hbm, idx_v, rows_v, load_sem, scat_sem):
        wid = _sc_worker()
        base = wid * per_w
        pltpu.sync_copy(idx_hbm.at[wid], idx_v)

        def load(g, slot):
            return pltpu.make_async_copy(src_hbm.at[pl.ds(base + g * win, win)], rows_v.at[slot], load_sem.at[slot])

        def scatter(g, slot, k):
            return pltpu.make_async_copy(rows_v.at[slot], out_hbm.at[idx_v.at[g, k]], scat_sem.at[slot])

        load(0, 0).start()

        @pl.loop(0, n_win, step=2)
        def _(g):
            for slot in range(2):
                cur = g + slot
                load(cur, slot).wait()

                @pl.when(cur + 1 < n_win)
                def _():
                    @pl.when(cur >= 1)
                    def _():
                        for k in range(n_k):
                            scatter(cur - 1, 1 - slot, k).wait()
                    load(cur + 1, 1 - slot).start()

                for k in range(n_k):
                    scatter(cur, slot, k).start()

        for slot in range(2):
            for k in range(n_k):
                scatter(n_win - 2 + slot, slot, k).wait()

    return pl.kernel(
        body, mesh=_sc_mesh(),
        out_type=jax.ShapeDtypeStruct((n_out, dp), src.dtype),
        scratch_types=[pltpu.VMEM((n_win, n_k, win), jnp.int32),
                       pltpu.VMEM((2, win, dp), src.dtype),
                       pltpu.SemaphoreType.DMA((2,)),
                       pltpu.SemaphoreType.DMA((2,))],
        compiler_params=pltpu.CompilerParams(use_tc_tiling_on_sc=True),
        name="sc_dispatch_scatter",
    )(src, idx)


def _sc_combine(table, dest, gates):
    n_k, n_tok = dest.shape
    dp = table.shape[1]
    d = 2 * dp
    lanes = SC_LANES
    assert n_tok % SC_WORKERS == 0
    per_w = n_tok // SC_WORKERS
    tok_win = SC_COMBINE_TOKENS
    rows = tok_win * n_k
    n_win = per_w // tok_win
    assert per_w % (2 * tok_win) == 0 and rows <= 128
    idx = dest.T.reshape(-1)
    gate_rows = gates.T.reshape(-1)

    def body(table_hbm, idx_hbm, gate_hbm, out_hbm, idx_v, rows_v, gate_v, out_v, gather_sem, gate_sem, put_sem):
        wid = _sc_worker()
        tok0 = wid * per_w
        pltpu.sync_copy(idx_hbm.at[pl.ds(tok0 * n_k, per_w * n_k)], idx_v)

        def gather(g, slot):
            return pltpu.make_async_copy(table_hbm.at[idx_v.at[pl.ds(g * rows, rows)]], rows_v.at[slot],
                                         gather_sem.at[slot])

        def load_gates(g, slot):
            return pltpu.make_async_copy(gate_hbm.at[pl.ds((tok0 + g * tok_win) * n_k, rows)], gate_v.at[slot],
                                         gate_sem.at[slot])

        def put(g, slot):
            return pltpu.make_async_copy(out_v.at[slot], out_hbm.at[pl.ds(tok0 + g * tok_win, tok_win)],
                                         put_sem.at[slot])

        def reduce_window(slot):
            @pl.loop(0, tok_win)
            def _(t):
                slot_idx = jnp.full((lanes,), slot, jnp.int32)
                zero_idx = jnp.zeros((lanes,), jnp.int32)
                g_k = [plsc.load_gather(gate_v, [slot_idx, zero_idx + (t * n_k + k)]) for k in range(n_k)]

                @plsc.parallel_loop(0, dp // lanes, 1, unroll=SC_COMBINE_UNROLL)
                def _(v):
                    words = pl.ds(v * lanes, lanes)
                    acc_lo = jnp.zeros((lanes,), F32)
                    acc_hi = jnp.zeros((lanes,), F32)
                    for k in range(n_k):
                        w = rows_v[slot, t * n_k + k, words]
                        acc_lo = acc_lo + g_k[k] * plsc.bitcast(w << 16, F32)
                        acc_hi = acc_hi + g_k[k] * plsc.bitcast(w & HIGH_HALF_MASK, F32)
                    out_v[slot, t, words] = acc_lo
                    out_v[slot, t, pl.ds(dp + v * lanes, lanes)] = acc_hi

        gather(0, 0).start()
        load_gates(0, 0).start()

        @pl.loop(0, n_win, step=2)
        def _(g):
            for slot in range(2):
                cur = g + slot
                gather(cur, slot).wait()
                load_gates(cur, slot).wait()

                @pl.when(cur + 1 < n_win)
                def _():
                    gather(cur + 1, 1 - slot).start()
                    load_gates(cur + 1, 1 - slot).start()

                @pl.when(cur >= 2)
                def _():
                    put(cur - 2, slot).wait()

                reduce_window(slot)
                put(cur, slot).start()

        for slot in range(2):
            put(n_win - 2 + slot, slot).wait()

    return pl.kernel(
        body, mesh=_sc_mesh(),
        out_type=jax.ShapeDtypeStruct((n_tok, d), F32),
        scratch_types=[pltpu.VMEM((per_w * n_k,), jnp.int32),
                       pltpu.VMEM((2, rows, dp), table.dtype),
                       pltpu.VMEM((2, rows), F32),
                       pltpu.VMEM((2, tok_win, d), F32),
                       pltpu.SemaphoreType.DMA((2,)),
                       pltpu.SemaphoreType.DMA((2,)),
                       pltpu.SemaphoreType.DMA((2,))],
        compiler_params=pltpu.CompilerParams(use_tc_tiling_on_sc=True, needs_layout_passes=False),
        name="sc_combine_reduce",
    )(table, idx, gate_rows)


def _moe_out_kernel(alpha, x_ref, h2_ref, r_ref, mod_ref, ws13_ref, ws2_ref, lng_ref, lnb_ref, o_ref):
    de = ws2_ref.shape[0]
    z = _dot_packed(h2_ref[0], ws13_ref)
    hmid = (_silu(z[:, :de]) * z[:, de:]).astype(BF16)
    y = _dot(hmid, ws2_ref[...]) + r_ref[0]
    o_ref[0] = _layer_norm(alpha * x_ref[0] + mod_ref[0, 5:6, :] * y, lng_ref[...], lnb_ref[...])


def _moe_out(alpha, xn, h2, routed, mod, ws13, ws2, lng, lnb, nt, ntx):
    nb, _, d = xn.shape
    return pl.pallas_call(
        functools.partial(_moe_out_kernel, alpha),
        out_shape=jax.ShapeDtypeStruct((nb, nt * SEQ_TILE, d), F32),
        grid=(nb, nt),
        in_specs=[_tok_spec(d), _tok_spec(d // 2), _tok_spec(d), _mod_spec(d, nb, ntx),
                  _full_spec(ws13.shape), _full_spec(ws2.shape), _full_spec(lng.shape), _full_spec(lnb.shape)],
        out_specs=_tok_spec(d),
        compiler_params=_params(2),
        name="moe_out",
    )(xn, h2, routed, mod, ws13, ws2, lng, lnb)


def _dispatch_plan(counts, topi, rank, n_tok):
    n_e = counts.shape[0]
    bm = MOE_BLOCK_ROWS
    padded = (counts + bm - 1) // bm * bm
    pad_end = jnp.cumsum(padded)
    pad_start = pad_end - padded
    onehot = topi[:, None, :] == jnp.arange(n_e, dtype=jnp.int32)[None, :, None]
    dest = rank + jnp.sum(jnp.where(onehot, pad_start[None, :, None], 0), axis=1)
    n_blocks = n_tok * TOP_K // bm + n_e
    block_start = jnp.arange(n_blocks, dtype=jnp.int32) * bm
    block_expert = jnp.sum((pad_end[None, :] <= block_start[:, None]).astype(jnp.int32), axis=1)
    block_expert = jnp.minimum(block_expert, n_e - 1)
    block_rows = jnp.clip(jnp.take(pad_start + counts, block_expert) - block_start, 0, bm).astype(jnp.int32)
    return dest.astype(jnp.int32), block_expert, block_rows, n_blocks * bm


def _rope_tables(seq):
    n_freq = QK_ROPE // 4
    inv_freq = ROPE_THETA ** (-jnp.arange(n_freq, dtype=F32) / n_freq)
    pos = jnp.arange(seq, dtype=jnp.int32)
    r = (pos // GRID_W).astype(F32)
    col = (pos % GRID_W).astype(F32)
    ang = jnp.concatenate([r[:, None] * inv_freq, col[:, None] * inv_freq], -1)
    cos, sin = jnp.cos(ang), jnp.sin(ang)
    zeros = jnp.zeros((seq, LANE - QK_ROPE), F32)
    cos_slot = jnp.concatenate([cos, cos, zeros], -1)
    sin_slot = jnp.concatenate([-sin, sin, zeros], -1)
    return cos_slot, sin_slot


def _rope_slot_weights(w_rope):
    k, n, _ = w_rope.shape
    half = QK_ROPE // 2
    swapped = jnp.concatenate([w_rope[..., half:], w_rope[..., :half]], -1)
    pad = jnp.zeros((k, n, LANE - QK_ROPE), w_rope.dtype)
    plain = jnp.concatenate([w_rope, pad], -1).reshape(k, n * LANE)
    swp = jnp.concatenate([swapped, pad], -1).reshape(k, n * LANE)
    return jnp.concatenate([plain, swp], -1)


def kernel(x, c, ctx, c_ctx, ada_w, ada_b, ln_g, ln_b, conf_w1, conf_b1, conf_dw, conf_dwb, conf_ng, conf_nb, conf_w2, conf_b2, sc_w_in, sc_dw, sc_w_out, mla_w_dqkv, mla_q_g, mla_kv_g, mla_w_uq, mla_w_uk, mla_w_uv, mla_w_o, moe_router, moe_bias, moe_w1, moe_w3, moe_w2, sh_w1, sh_w3, sh_w2):
    nb, seq, d = x.shape
    l_ctx = ctx.shape[1]
    depth = ada_w.shape[0]
    alpha = (2.0 * depth) ** 0.25
    tl = SEQ_TILE
    assert seq % tl == 0 and l_ctx % tl == 0 and seq % Q_TILE == 0 and seq % GRID_W == 0
    ntx = seq // tl
    nt_all = (seq + l_ctx) // tl
    attn_layers = [i for i in range(depth) if i % N_MIXERS == 2]
    last_ctx_reader = attn_layers[-1] if attn_layers else -1

    rows = -(-(nb + 1) // 8) * 8
    c_all = jnp.zeros((rows, d), F32).at[:nb].set(c).at[nb].set(c_ctx)
    mod_all = _modulation(c_all, ada_w, ada_b).reshape(depth, rows, N_MOD, d)

    assert nb % N_CHAINS == 0
    nbc = nb // N_CHAINS
    chains = [jnp.concatenate([x[c0:c0 + nbc], ctx[c0:c0 + nbc]], axis=1) for c0 in range(0, nb, nbc)]
    q_rank, kv_rank = mla_q_g.shape[1], mla_kv_g.shape[1]
    cos_t, sin_t = _rope_tables(seq)
    row = lambda v: v.reshape(1, -1)

    for i in range(depth):
        need_ctx = i < last_ctx_reader
        kind, j = i % N_MIXERS, i // N_MIXERS
        nt = nt_all if need_ctx else ntx
        n_tok = nbc * nt * tl

        if kind == 0:
            w_first = conf_w1[j].astype(BF16)
            dw_tiles = jnp.broadcast_to(conf_dw[j][:, None, :], (conf_dw.shape[1], SUBLANES, d))
            w_last, b_last = conf_w2[j].astype(BF16), row(conf_b2[j])
        elif kind == 1:
            w_first = sc_w_in[j].astype(BF16)
            w_last, b_last = sc_w_out[j].astype(BF16), jnp.zeros((1, d), F32)
        else:
            wdq = mla_w_dqkv[j]
            wd = wdq[:, :q_rank + kv_rank].astype(BF16)
            wkp = _rope_slot_weights(wdq[:, None, q_rank + kv_rank:]).astype(BF16)
            wuq = mla_w_uq[j].reshape(q_rank, MLA_HEADS, QK_NOPE + QK_ROPE)
            wqn = wuq[:, :, :QK_NOPE].reshape(q_rank, MLA_HEADS * QK_NOPE).astype(BF16)
            wqr = _rope_slot_weights(wuq[:, :, QK_NOPE:]).astype(BF16)
            wuk, wuv = mla_w_uk[j].astype(BF16), mla_w_uv[j].astype(BF16)
            w_last, b_last = mla_w_o[j].astype(BF16), jnp.zeros((1, d), F32)
        rw_hi, rw_lo = _split_bf16(moe_router[i].T)
        ws13 = jnp.concatenate([sh_w1[i], sh_w3[i]], axis=-1).astype(BF16)
        ws2 = sh_w2[i].astype(BF16)

        for ci in range(N_CHAINS):
            x_all = chains[ci]
            mod = jnp.concatenate([mod_all[i, ci * nbc:(ci + 1) * nbc], mod_all[i, nb:nb + 1]], axis=0)

            if kind == 0:
                u = _conf_in(x_all, mod, w_first, row(conf_b1[j]), nt, ntx)
                a = _conf_conv(u, dw_tiles, row(conf_dwb[j]), row(conf_ng[j]), row(conf_nb[j]), nt, ntx)
            elif kind == 1:
                a = _sc_conv(x_all, mod, w_first, sc_dw[j], nt, ntx)
            else:
                q, k, v = _mla_proj(x_all, mod, wd, wkp, row(mla_q_g[j]), row(mla_kv_g[j]), wqn, wqr,
                                    wuk, wuv, cos_t, sin_t, nt_all, ntx)
                a = _attention(q, k, v, nt * tl)

            xn, h2, topi, topw, rank, counts = _post(alpha, a, w_last, b_last, x_all, mod, row(ln_g[i, 0]),
                                                     row(ln_b[i, 0]), rw_hi, rw_lo, moe_bias[i].reshape(-1, 1),
                                                     nt, ntx)

            dest, block_expert, block_rows, n_rows = _dispatch_plan(counts[:, 0], topi, rank, n_tok)
            xs = _sc_row_scatter(h2.reshape(n_tok, d // 2), dest, n_rows)
            ys = _grouped_ffn(block_expert, block_rows, xs, moe_w1, moe_w3, moe_w2, i)
            routed = _sc_combine(ys, dest, topw).reshape(nbc, nt * tl, d)

            chains[ci] = _moe_out(alpha, xn, h2, routed, mod, ws13, ws2,
                                  row(ln_g[i, 1]), row(ln_b[i, 1]), nt, ntx)
    return jnp.concatenate([xc[:, :seq] for xc in chains], axis=0)
```

```python
import functools

import numpy as np
import jax
import jax.numpy as jnp
from jax import lax
from jax.experimental import pallas as pl
from jax.experimental.pallas import tpu as pltpu
from jax.experimental.pallas import tpu_sc as plsc

F32 = jnp.float32
BF16 = jnp.bfloat16
U32 = jnp.uint32
HIGH_HALF_MASK = np.uint32(0xFFFF0000)

GRID_W = 64
N_MIXERS = 3
LN_EPS = 1e-5
RMS_EPS = 1e-6
N_MOD = 6
MLA_HEADS = 8
QK_NOPE = 128
QK_ROPE = 64
V_HEAD = 128
ROPE_THETA = 10000.0
ATTN_SCALE = (QK_NOPE + QK_ROPE) ** -0.5
Q_SCALE = ATTN_SCALE * 1.4426950408889634
TOP_K = 8
N_GROUPS = 8
TOPK_GROUPS = 4
ROUTED_SCALE = 2.5

SEQ_TILE = 256
CONV_HALO = 16
SHORT_HALO = 8
CONV_ROW_CHUNK = 64
LANE = 128
SUBLANES = 8
Q_TILE = 2048
ATTN_SUB_ROWS = 256
ATTN_ROW_CHUNK = 16
MOE_BLOCK_ROWS = 1024
ROUTE_SUB_TOKENS = 128
MOD_COL_TILE = 1536
VMEM_LIMIT = 48 * 1024 * 1024
SC_CORES = 2
SC_SUBCORES = 16
SC_WORKERS = SC_CORES * SC_SUBCORES
SC_LANES = 16
SC_COMBINE_UNROLL = 4
SC_COMBINE_TOKENS = 8
SC_WINDOWS = (64, 32, 16)
N_CHAINS = 1


def _params(n_axes):
    return pltpu.CompilerParams(dimension_semantics=("arbitrary",) * n_axes,
                                vmem_limit_bytes=VMEM_LIMIT)


def _split_bf16(a):
    hi = a.astype(BF16)
    lo = (a - hi.astype(F32)).astype(BF16)
    return hi, lo


def _dot(a, b):
    return jnp.dot(a, b, preferred_element_type=F32)


def _dot_nt(a, b):
    return lax.dot_general(a, b, (((1,), (1,)), ((), ())), preferred_element_type=F32)


def _pack_halves(v):
    half = v.shape[-1] // 2
    lo = lax.bitcast_convert_type(v[:, :half].astype(BF16).astype(F32), U32) >> 16
    hi = lax.bitcast_convert_type(v[:, half:].astype(BF16).astype(F32), U32) & HIGH_HALF_MASK
    return hi | lo


def _unpack_halves(p):
    lo = lax.bitcast_convert_type(p << 16, F32)
    hi = lax.bitcast_convert_type(p & HIGH_HALF_MASK, F32)
    return lo, hi


def _dot_packed(p, w):
    lo, hi = _unpack_halves(p)
    return _dot(jnp.concatenate([lo.astype(BF16), hi.astype(BF16)], axis=1), w[...])


def _layer_norm(v, g, b):
    mu = jnp.mean(v, axis=-1, keepdims=True)
    c = v - mu
    var = jnp.mean(c * c, axis=-1, keepdims=True)
    return c * lax.rsqrt(var + LN_EPS) * g + b


def _silu(v):
    return v * jax.nn.sigmoid(v)


def _mod_kernel(c_ref, w_ref, b_ref, o_ref):
    a = _silu(c_ref[...])
    a_hi, a_lo = _split_bf16(a)
    w_hi, w_lo = _split_bf16(w_ref[0])
    o_ref[0] = _dot(a_hi, w_hi) + _dot(a_hi, w_lo) + _dot(a_lo, w_hi) + b_ref[0]


def _modulation(c_all, ada_w, ada_b):
    depth, d, n = ada_w.shape
    rows = c_all.shape[0]
    tn = MOD_COL_TILE
    return pl.pallas_call(
        _mod_kernel,
        out_shape=jax.ShapeDtypeStruct((depth, rows, n), F32),
        grid=(depth, n // tn),
        in_specs=[pl.BlockSpec((rows, d), lambda i, j: (0, 0)),
                  pl.BlockSpec((1, d, tn), lambda i, j: (i, 0, j)),
                  pl.BlockSpec((1, 1, tn), lambda i, j: (i, 0, j))],
        out_specs=pl.BlockSpec((1, rows, tn), lambda i, j: (i, 0, j)),
        compiler_params=_params(2),
        name="adaln_mod",
    )(c_all, ada_w, ada_b.reshape(depth, 1, n))


def _tok_spec(d, tl=SEQ_TILE):
    return pl.BlockSpec((1, tl, d), lambda b, l: (b, l, 0))


def _mod_spec(d, n_batch, ntx):
    return pl.BlockSpec((1, N_MOD, d), lambda b, l: (jnp.where(l < ntx, b, n_batch), 0, 0))


def _full_spec(shape):
    zeros = (0,) * len(shape)
    return pl.BlockSpec(shape, lambda b, l: zeros)


def _halo_specs(d, halo, seq_len, tl=SEQ_TILE):
    per_tile = tl // halo
    last = seq_len // halo - 1
    prev = pl.BlockSpec((1, halo, d), lambda b, l: (b, jnp.maximum(l * per_tile - 1, 0), 0))
    nxt = pl.BlockSpec((1, halo, d), lambda b, l: (b, jnp.minimum((l + 1) * per_tile, last), 0))
    return prev, nxt


def _segment_edges(l, ntx, nt):
    first = jnp.logical_or(l == 0, l == ntx)
    last = jnp.logical_or(l == ntx - 1, l == nt - 1)
    return first, last


def _window_rows(prev_ref, cur_ref, next_ref, win_ref):
    halo, tl = prev_ref.shape[1], cur_ref.shape[1]
    win_ref[0:halo, :] = prev_ref[0]
    win_ref[halo:halo + tl, :] = cur_ref[0]
    win_ref[halo + tl:, :] = next_ref[0]
    return win_ref[...]


def _edge_mask(n_rows, halo, first, last):
    r = lax.broadcasted_iota(jnp.int32, (n_rows, 1), 0)
    outside = jnp.logical_or(jnp.logical_and(first, r < halo), jnp.logical_and(last, r >= n_rows - halo))
    return jnp.where(outside, 0.0, 1.0)


def _conf_in_kernel(x_ref, mod_ref, w1_ref, b1_ref, u_ref):
    d = x_ref.shape[-1]
    h = x_ref[0] * (1.0 + mod_ref[0, 1:2, :]) + mod_ref[0, 0:1, :]
    z = _dot(h.astype(BF16), w1_ref[...]) + b1_ref[...]
    u_ref[0] = z[:, :d] * jax.nn.sigmoid(z[:, d:])


def _conf_in(x_all, mod, w1, b1, nt, ntx):
    nb, seq_len, d = x_all.shape
    return pl.pallas_call(
        _conf_in_kernel,
        out_shape=jax.ShapeDtypeStruct((nb, nt * SEQ_TILE, d), F32),
        grid=(nb, nt),
        in_specs=[_tok_spec(d), _mod_spec(d, nb, ntx), _full_spec(w1.shape), _full_spec(b1.shape)],
        out_specs=_tok_spec(d),
        compiler_params=_params(2),
        name="conf_in",
    )(x_all, mod, w1, b1)


def _conf_conv_kernel(ntx, nt, up_ref, uc_ref, un_ref, dw_ref, dwb_ref, ng_ref, nb_ref,
                      a_ref, sh_ref, conv_ref):
    tl, d = uc_ref.shape[1], uc_ref.shape[2]
    taps = dw_ref.shape[0]
    lead = CONV_HALO - (taps - 1) // 2
    first, last = _segment_edges(pl.program_id(1), ntx, nt)
    sh_ref[0, 0:CONV_HALO, :] = jnp.where(first, 0.0, up_ref[0])
    sh_ref[0, CONV_HALO:CONV_HALO + tl, :] = uc_ref[0]
    sh_ref[0, CONV_HALO + tl:, :] = jnp.where(last, 0.0, un_ref[0])
    span = tl + 2 * CONV_HALO - SUBLANES
    for s in range(1, SUBLANES):
        sh_ref[s, 0:span, :] = sh_ref[0, s:s + span, :]
    groups = CONV_ROW_CHUNK // SUBLANES

    def row_chunk(i, carry):
        r0 = pl.multiple_of(i * CONV_ROW_CHUNK, CONV_ROW_CHUNK)
        for c0 in range(0, d, LANE):
            accs = [jnp.zeros((SUBLANES, LANE), F32) for _ in range(groups)]
            for k in range(taps):
                res = (lead + k) % SUBLANES
                off = lead + k - res
                w = dw_ref[k, :, c0:c0 + LANE]
                for g in range(groups):
                    lo = r0 + (off + g * SUBLANES)
                    accs[g] = accs[g] + w * sh_ref[res, pl.ds(lo, SUBLANES), c0:c0 + LANE]
            for g in range(groups):
                conv_ref[pl.ds(r0 + g * SUBLANES, SUBLANES), c0:c0 + LANE] = accs[g]
        return carry

    lax.fori_loop(0, tl // CONV_ROW_CHUNK, row_chunk, 0)
    v = _layer_norm(conv_ref[...] + dwb_ref[...], ng_ref[...], nb_ref[...])
    a_ref[0] = _silu(v).astype(BF16)


def _conf_conv(u, dw, dwb, ng, nb_, nt, ntx):
    nb, seq_len, d = u.shape
    prev, nxt = _halo_specs(d, CONV_HALO, seq_len)
    return pl.pallas_call(
        functools.partial(_conf_conv_kernel, ntx, nt),
        out_shape=jax.ShapeDtypeStruct((nb, seq_len, d), BF16),
        grid=(nb, nt),
        in_specs=[prev, _tok_spec(d), nxt, _full_spec(dw.shape), _full_spec(dwb.shape),
                  _full_spec(ng.shape), _full_spec(nb_.shape)],
        out_specs=_tok_spec(d),
        scratch_shapes=[pltpu.VMEM((SUBLANES, SEQ_TILE + 2 * CONV_HALO, d), F32),
                        pltpu.VMEM((SEQ_TILE, d), F32)],
        compiler_params=_params(2),
        name="conf_conv",
    )(u, u, u, dw, dwb, ng, nb_)


def _sc_conv_kernel(ntx, nt, xp_ref, xc_ref, xn_ref, mod_ref, w_ref, dw_ref, a_ref, win_ref, ext_ref):
    tl, d = xc_ref.shape[1], xc_ref.shape[2]
    taps = dw_ref.shape[0]
    lead = SHORT_HALO - (taps - 1) // 2
    first, last = _segment_edges(pl.program_id(1), ntx, nt)
    xw = _window_rows(xp_ref, xc_ref, xn_ref, win_ref)
    h = xw * (1.0 + mod_ref[0, 1:2, :]) + mod_ref[0, 0:1, :]
    z = _dot(h.astype(BF16), w_ref[...])
    gb = z[SHORT_HALO:SHORT_HALO + tl, :d]
    ext_ref[...] = z[:, d:2 * d] * z[:, 2 * d:] * _edge_mask(z.shape[0], SHORT_HALO, first, last)
    acc = dw_ref[0:1, :] * ext_ref[lead:lead + tl, :]
    for k in range(1, taps):
        acc = acc + dw_ref[k:k + 1, :] * ext_ref[lead + k:lead + k + tl, :]
    a_ref[0] = (gb * acc).astype(BF16)


def _sc_conv(x_all, mod, w_in, dw, nt, ntx):
    nb, seq_len, d = x_all.shape
    prev, nxt = _halo_specs(d, SHORT_HALO, seq_len)
    win = pltpu.VMEM((SEQ_TILE + 2 * SHORT_HALO, d), F32)
    return pl.pallas_call(
        functools.partial(_sc_conv_kernel, ntx, nt),
        out_shape=jax.ShapeDtypeStruct((nb, nt * SEQ_TILE, d), BF16),
        grid=(nb, nt),
        in_specs=[prev, _tok_spec(d), nxt, _mod_spec(d, nb, ntx), _full_spec(w_in.shape), _full_spec(dw.shape)],
        out_specs=_tok_spec(d),
        scratch_shapes=[win, win],
        compiler_params=_params(2),
        name="sc_conv",
    )(x_all, x_all, x_all, mod, w_in, dw)


def _rms(v, g):
    return v * lax.rsqrt(jnp.mean(v * v, axis=-1, keepdims=True) + RMS_EPS) * g


def _mla_proj_kernel(ntx, q_rank,
                     x_ref, mod_ref, wd_ref, wkp_ref, qg_ref, kvg_ref, wqn_ref, wqr_ref, wuk_ref, wuv_ref,
                     cos_ref, sin_ref, q_ref, k_ref, v_ref):
    is_latent = pl.program_id(1) < ntx
    h = (x_ref[0] * (1.0 + mod_ref[0, 1:2, :]) + mod_ref[0, 0:1, :]).astype(BF16)
    dn = _dot(h, wd_ref[...])
    cq = _rms(dn[:, :q_rank], qg_ref[...]).astype(BF16)
    ckv = _rms(dn[:, q_rank:], kvg_ref[...]).astype(BF16)
    cos = jnp.where(is_latent, cos_ref[...], 1.0)
    sin = jnp.where(is_latent, sin_ref[...], 0.0)
    kp2 = _dot(h, wkp_ref[...])
    kp = kp2[:, :LANE] * cos + kp2[:, LANE:] * sin
    kn = _dot(ckv, wuk_ref[...])
    vv = _dot(ckv, wuv_ref[...])
    qn = _dot(cq, wqn_ref[...]) * Q_SCALE
    qr2 = _dot(cq, wqr_ref[...])
    hw = MLA_HEADS * LANE
    for hd in range(MLA_HEADS):
        sl = slice(hd * LANE, (hd + 1) * LANE)
        qr = (qr2[:, sl] * cos + qr2[:, hw + hd * LANE:hw + (hd + 1) * LANE] * sin) * Q_SCALE
        q_ref[0, hd, :, 0:LANE] = qn[:, sl].astype(BF16)
        q_ref[0, hd, :, LANE:] = qr.astype(BF16)
        k_ref[0, hd, :, 0:LANE] = kn[:, sl].astype(BF16)
        k_ref[0, hd, :, LANE:] = kp.astype(BF16)
        v_ref[0, hd] = vv[:, sl].astype(BF16)


def _mla_proj(x_all, mod, wd, wkp, qg, kvg, wqn, wqr, wuk, wuv, cos_t, sin_t, nt, ntx):
    nb, seq_len, d = x_all.shape
    q_rank = qg.shape[-1]
    tl = SEQ_TILE
    rope_spec = pl.BlockSpec((tl, LANE), lambda b, l: (jnp.minimum(l, ntx - 1), 0))
    qk_shape = jax.ShapeDtypeStruct((nb, MLA_HEADS, nt * tl, 2 * LANE), BF16)
    v_shape = jax.ShapeDtypeStruct((nb, MLA_HEADS, nt * tl, LANE), BF16)
    qk_spec = pl.BlockSpec((1, MLA_HEADS, tl, 2 * LANE), lambda b, l: (b, 0, l, 0))
    v_spec = pl.BlockSpec((1, MLA_HEADS, tl, LANE), lambda b, l: (b, 0, l, 0))
    return pl.pallas_call(
        functools.partial(_mla_proj_kernel, ntx, q_rank),
        out_shape=(qk_shape, qk_shape, v_shape),
        grid=(nb, nt),
        in_specs=[_tok_spec(d), _mod_spec(d, nb, ntx), _full_spec(wd.shape), _full_spec(wkp.shape),
                  _full_spec(qg.shape), _full_spec(kvg.shape), _full_spec(wqn.shape), _full_spec(wqr.shape),
                  _full_spec(wuk.shape), _full_spec(wuv.shape), rope_spec, rope_spec],
        out_specs=(qk_spec, qk_spec, v_spec),
        compiler_params=_params(2),
        name="mla_proj",
    )(x_all, mod, wd, wkp, qg, kvg, wqn, wqr, wuk, wuv, cos_t, sin_t)


def _attn_kernel(q_ref, k_ref, v_ref, o_ref, s_ref, p_ref, l_ref):
    tq = q_ref.shape[2]
    for r0 in range(0, tq, ATTN_SUB_ROWS):
        rows = slice(r0, r0 + ATTN_SUB_ROWS)
        s_ref[rows, :] = _dot_nt(q_ref[0, 0, rows, :], k_ref[0, 0])
        for c0 in range(r0, r0 + ATTN_SUB_ROWS, ATTN_ROW_CHUNK):
            chunk = slice(c0, c0 + ATTN_ROW_CHUNK)
            s = s_ref[chunk, :]
            p = jnp.exp2(s - jnp.max(s, axis=-1, keepdims=True))
            l_ref[chunk, :] = jnp.broadcast_to(jnp.sum(p, axis=-1, keepdims=True), (ATTN_ROW_CHUNK, V_HEAD))
            p_ref[chunk, :] = p.astype(BF16)
        o = _dot(p_ref[rows, :], v_ref[0, 0])
        o_ref[0, rows, :] = (o / l_ref[rows, :]).astype(BF16)


def _attention(q, k, v, n_q):
    nb, nh, n_k, dk = k.shape
    tq = Q_TILE
    return pl.pallas_call(
        _attn_kernel,
        out_shape=jax.ShapeDtypeStruct((nb, n_q, nh * V_HEAD), BF16),
        grid=(nb, nh, n_q // tq),
        in_specs=[pl.BlockSpec((1, 1, tq, dk), lambda b, h, i: (b, h, i, 0)),
                  pl.BlockSpec((1, 1, n_k, dk), lambda b, h, i: (b, h, 0, 0)),
                  pl.BlockSpec((1, 1, n_k, V_HEAD), lambda b, h, i: (b, h, 0, 0))],
        out_specs=pl.BlockSpec((1, tq, V_HEAD), lambda b, h, i: (b, i, h)),
        scratch_shapes=[pltpu.VMEM((tq, n_k), F32), pltpu.VMEM((tq, n_k), BF16), pltpu.VMEM((tq, V_HEAD), F32)],
        compiler_params=_params(3),
        name="mla_attn",
    )(q, k, v)


def _route(sel, s):
    n_e, n_t = sel.shape
    per = n_e // N_GROUPS
    sel3 = sel.reshape(N_GROUPS, per, n_t)
    s3 = s.reshape(N_GROUPS, per, n_t)
    iota_p = lax.broadcasted_iota(jnp.int32, (N_GROUPS, per, n_t), 1).astype(F32)
    iota_g = lax.broadcasted_iota(jnp.int32, (N_GROUPS, 1, n_t), 0).astype(F32)
    neg = -jnp.inf
    m1 = jnp.max(sel3, axis=1, keepdims=True)
    first = jnp.min(jnp.where(sel3 == m1, iota_p, float(per)), axis=1, keepdims=True)
    m2 = jnp.max(jnp.where(iota_p == first, neg, sel3), axis=1, keepdims=True)
    gs = m1 + m2
    gsel = jnp.zeros((N_GROUPS, 1, n_t), F32)
    for _ in range(TOPK_GROUPS):
        gm = jnp.max(gs, axis=0, keepdims=True)
        gfirst = jnp.min(jnp.where(gs == gm, iota_g, float(N_GROUPS)), axis=0, keepdims=True)
        pick = iota_g == gfirst
        gsel = jnp.where(pick, 1.0, gsel)
        gs = jnp.where(pick, neg, gs)
    val = jnp.where(gsel > 0.0, sel3, neg)
    iota_e = lax.broadcasted_iota(jnp.int32, (N_GROUPS, per, n_t), 0).astype(F32) * per + iota_p
    chosen = jnp.zeros((N_GROUPS, per, n_t), F32)
    idx, wts = [], []
    for _ in range(TOP_K):
        m = jnp.max(jnp.max(val, axis=1, keepdims=True), axis=0, keepdims=True)
        e = jnp.min(jnp.min(jnp.where(val == m, iota_e, float(n_e)), axis=1, keepdims=True), axis=0, keepdims=True)
        pick = iota_e == e
        wts.append(jnp.sum(jnp.sum(jnp.where(pick, s3, 0.0), axis=1, keepdims=True), axis=0, keepdims=True))
        idx.append(e)
        chosen = jnp.where(pick, 1.0, chosen)
        val = jnp.where(pick, neg, val)
    total = wts[0]
    for w in wts[1:]:
        total = total + w
    wts = [w / total * ROUTED_SCALE for w in wts]
    return idx, wts, chosen, iota_e


def _post_kernel(alpha, a_ref, w_ref, b_ref, x_ref, mod_ref, lng_ref, lnb_ref, rw_hi_ref, rw_lo_ref, rb_ref, tri_ref,
                 xn_ref, h2_ref, topi_ref, topw_ref, rank_ref, cnt_ref, run_ref):
    first_step = jnp.logical_and(pl.program_id(0) == 0, pl.program_id(1) == 0)

    @pl.when(first_step)
    def _():
        run_ref[...] = jnp.zeros_like(run_ref)

    y = _dot(a_ref[0], w_ref[...]) + b_ref[...]
    run = run_ref[...]
    for t0 in range(0, y.shape[0], ROUTE_SUB_TOKENS):
        rows = slice(t0, t0 + ROUTE_SUB_TOKENS)
        xn = _layer_norm(alpha * x_ref[0, rows, :] + mod_ref[0, 2:3, :] * y[rows], lng_ref[...], lnb_ref[...])
        xn_ref[0, rows, :] = xn
        h2 = xn * (1.0 + mod_ref[0, 4:5, :]) + mod_ref[0, 3:4, :]
        h_hi, h_lo = _split_bf16(h2)
        h2_ref[0, rows, :] = _pack_halves(h2)
        logits = _dot_nt(rw_hi_ref[...], h_hi) + _dot_nt(rw_hi_ref[...], h_lo) + _dot_nt(rw_lo_ref[...], h_hi)
        s = jax.nn.sigmoid(logits)
        idx, wts, chosen3, iota_e = _route(s + rb_ref[...], s)
        n_e, n_t = s.shape
        chosen = chosen3.reshape(n_e, n_t)
        rank = run[:, 0:1] + _dot(chosen.astype(BF16), tri_ref[...])
        rank3 = rank.reshape(chosen3.shape)
        for k in range(TOP_K):
            rk = jnp.sum(jnp.sum(jnp.where(iota_e == idx[k], rank3, 0.0), axis=1, keepdims=True), axis=0,
                         keepdims=True)
            topi_ref[k:k + 1, rows] = idx[k].reshape(1, n_t).astype(jnp.int32)
            topw_ref[k:k + 1, rows] = wts[k].reshape(1, n_t)
            rank_ref[k:k + 1, rows] = rk.reshape(1, n_t).astype(jnp.int32)
        run = run + jnp.sum(chosen, axis=1, keepdims=True)
    run_ref[...] = run
    cnt_ref[...] = run.astype(jnp.int32)


def _post(alpha, a, w, bias, x_all, mod, lng, lnb, rw_hi, rw_lo, rb, nt, ntx):
    nb, _, dk = a.shape
    d = x_all.shape[-1]
    n_e = rw_hi.shape[0]
    tl = SEQ_TILE
    n_tok = nb * nt * tl
    col_spec = lambda rows: pl.BlockSpec((rows, tl), lambda b, l: (0, b * nt + l))
    sub = ROUTE_SUB_TOKENS
    row_i = lax.broadcasted_iota(jnp.int32, (sub, sub), 0)
    col_i = lax.broadcasted_iota(jnp.int32, (sub, sub), 1)
    tri = jnp.where(row_i < col_i, 1.0, 0.0).astype(BF16)
    return pl.pallas_call(
        functools.partial(_post_kernel, alpha),
        out_shape=(jax.ShapeDtypeStruct((nb, nt * tl, d), F32),
                   jax.ShapeDtypeStruct((nb, nt * tl, d // 2), U32),
                   jax.ShapeDtypeStruct((TOP_K, n_tok), jnp.int32),
                   jax.ShapeDtypeStruct((TOP_K, n_tok), F32),
                   jax.ShapeDtypeStruct((TOP_K, n_tok), jnp.int32),
                   jax.ShapeDtypeStruct((n_e, LANE), jnp.int32)),
        grid=(nb, nt),
        in_specs=[_tok_spec(dk), _full_spec(w.shape), _full_spec(bias.shape), _tok_spec(d),
                  _mod_spec(d, nb, ntx), _full_spec(lng.shape), _full_spec(lnb.shape),
                  _full_spec(rw_hi.shape), _full_spec(rw_lo.shape), _full_spec(rb.shape), _full_spec(tri.shape)],
        out_specs=(_tok_spec(d), _tok_spec(d // 2), col_spec(TOP_K), col_spec(TOP_K), col_spec(TOP_K),
                   _full_spec((n_e, LANE))),
        scratch_shapes=[pltpu.VMEM((n_e, LANE), F32)],
        compiler_params=_params(2),
        name="mixer_post",
    )(a, w, bias, x_all, mod, lng, lnb, rw_hi, rw_lo, rb, tri)


def _gmm_kernel(be_ref, br_ref, xs_ref, w1_ref, w3_ref, w2_ref, ys_ref, w13_s, w2_s):
    i = pl.program_id(0)
    de = w2_ref.shape[2]
    changed = jnp.logical_or(i == 0, be_ref[i] != be_ref[jnp.maximum(i - 1, 0)])

    @pl.when(changed)
    def _():
        w13_s[:, :de] = w1_ref[0, 0].astype(BF16)
        w13_s[:, de:] = w3_ref[0, 0].astype(BF16)
        w2_s[...] = w2_ref[0, 0].astype(BF16)

    @pl.when(br_ref[i] > 0)
    def _():
        z = _dot_packed(xs_ref[...], w13_s)
        hmid = (_silu(z[:, :de]) * z[:, de:]).astype(BF16)
        ys_ref[...] = _pack_halves(_dot(hmid, w2_s[...]))


def _grouped_ffn(block_expert, block_rows, xs, w1, w3, w2, layer):
    n_rows, dp = xs.shape
    d = 2 * dp
    bm = MOE_BLOCK_ROWS
    de = w2.shape[2]
    grid_spec = pltpu.PrefetchScalarGridSpec(
        num_scalar_prefetch=2,
        grid=(n_rows // bm,),
        in_specs=[pl.BlockSpec((bm, dp), lambda i, be, br: (i, 0)),
                  pl.BlockSpec((1, 1, d, de), lambda i, be, br: (layer, be[i], 0, 0)),
                  pl.BlockSpec((1, 1, d, de), lambda i, be, br: (layer, be[i], 0, 0)),
                  pl.BlockSpec((1, 1, de, d), lambda i, be, br: (layer, be[i], 0, 0))],
        out_specs=pl.BlockSpec((bm, dp), lambda i, be, br: (i, 0)),
        scratch_shapes=[pltpu.VMEM((d, 2 * de), BF16), pltpu.VMEM((de, d), BF16)],
    )
    return pl.pallas_call(
        _gmm_kernel,
        out_shape=jax.ShapeDtypeStruct((n_rows, dp), U32),
        grid_spec=grid_spec,
        compiler_params=_params(1),
        name="moe_grouped_ffn",
    )(block_expert, block_rows, xs, w1, w3, w2)


def _sc_mesh():
    return plsc.VectorSubcoreMesh(core_axis_name="c", subcore_axis_name="s",
                                  num_cores=SC_CORES, num_subcores=SC_SUBCORES)


def _sc_worker():
    return lax.axis_index("s") * SC_CORES + lax.axis_index("c")


def _sc_window(rows_per_worker):
    for win in SC_WINDOWS:
        if rows_per_worker % (2 * win) == 0:
            return win
    raise ValueError(f"no SparseCore window divides {rows_per_worker} rows per worker")


def _sc_row_scatter(src, dest, n_out):
    n_k, n_tok = dest.shape
    dp = src.shape[1]
    assert n_tok % SC_WORKERS == 0
    per_w = n_tok // SC_WORKERS
    win = _sc_window(per_w)
    n_win = per_w // win
    idx = dest.reshape(n_k, SC_WORKERS, n_win, win).transpose(1, 2, 0, 3)

    def body(src_hbm, idx_hbm, out_hbm, idx_v, rows_v, load_sem, scat_sem):
        wid = _sc_worker()
        base = wid * per_w
        pltpu.sync_copy(idx_hbm.at[wid], idx_v)

        def load(g, slot):
            return pltpu.make_async_copy(src_hbm.at[pl.ds(base + g * win, win)], rows_v.at[slot], load_sem.at[slot])

        def scatter(g, slot, k):
            return pltpu.make_async_copy(rows_v.at[slot], out_hbm.at[idx_v.at[g, k]], scat_sem.at[slot])

        load(0, 0).start()

        @pl.loop(0, n_win, step=2)
        def _(g):
            for slot in range(2):
                cur = g + slot
                load(cur, slot).wait()

                @pl.when(cur + 1 < n_win)
                def _():
                    @pl.when(cur >= 1)
                    def _():
                        for k in range(n_k):
                            scatter(cur - 1, 1 - slot, k).wait()
                    load(cur + 1, 1 - slot).start()

                for k in range(n_k):
                    scatter(cur, slot, k).start()

        for slot in range(2):
            for k in range(n_k):
                scatter(n_win - 2 + slot, slot, k).wait()

    return pl.kernel(
        body, mesh=_sc_mesh(),
        out_type=jax.ShapeDtypeStruct((n_out, dp), src.dtype),
        scratch_types=[pltpu.VMEM((n_win, n_k, win), jnp.int32),
                       pltpu.VMEM((2, win, dp), src.dtype),
                       pltpu.SemaphoreType.DMA((2,)),
                       pltpu.SemaphoreType.DMA((2,))],
        compiler_params=pltpu.CompilerParams(use_tc_tiling_on_sc=True),
        name="sc_dispatch_scatter",
    )(src, idx)


def _sc_combine(table, dest, gates):
    n_k, n_tok = dest.shape
    dp = table.shape[1]
    d = 2 * dp
    lanes = SC_LANES
    assert n_tok % SC_WORKERS == 0
    per_w = n_tok // SC_WORKERS
    tok_win = SC_COMBINE_TOKENS
    rows = tok_win * n_k
    n_win = per_w // tok_win
    assert per_w % (2 * tok_win) == 0 and rows <= 128
    idx = dest.T.reshape(-1)
    gate_rows = gates.T.reshape(-1)

    def body(table_hbm, idx_hbm, gate_hbm, out_hbm, idx_v, rows_v, gate_v, out_v, gather_sem, gate_sem, put_sem):
        wid = _sc_worker()
        tok0 = wid * per_w
        pltpu.sync_copy(idx_hbm.at[pl.ds(tok0 * n_k, per_w * n_k)], idx_v)

        def gather(g, slot):
            return pltpu.make_async_copy(table_hbm.at[idx_v.at[pl.ds(g * rows, rows)]], rows_v.at[slot],
                                         gather_sem.at[slot])

        def load_gates(g, slot):
            return pltpu.make_async_copy(gate_hbm.at[pl.ds((tok0 + g * tok_win) * n_k, rows)], gate_v.at[slot],
                                         gate_sem.at[slot])

        def put(g, slot):
            return pltpu.make_async_copy(out_v.at[slot], out_hbm.at[pl.ds(tok0 + g * tok_win, tok_win)],
                                         put_sem.at[slot])

        def reduce_window(slot):
            @pl.loop(0, tok_win)
            def _(t):
                slot_idx = jnp.full((lanes,), slot, jnp.int32)
                zero_idx = jnp.zeros((lanes,), jnp.int32)
                g_k = [plsc.load_gather(gate_v, [slot_idx, zero_idx + (t * n_k + k)]) for k in range(n_k)]

                @plsc.parallel_loop(0, dp // lanes, 1, unroll=SC_COMBINE_UNROLL)
                def _(v):
                    words = pl.ds(v * lanes, lanes)
                    acc_lo = jnp.zeros((lanes,), F32)
                    acc_hi = jnp.zeros((lanes,), F32)
                    for k in range(n_k):
                        w = rows_v[slot, t * n_k + k, words]
                        acc_lo = acc_lo + g_k[k] * plsc.bitcast(w << 16, F32)
                        acc_hi = acc_hi + g_k[k] * plsc.bitcast(w & HIGH_HALF_MASK, F32)
                    out_v[slot, t, words] = acc_lo
                    out_v[slot, t, pl.ds(dp + v * lanes, lanes)] = acc_hi

        gather(0, 0).start()
        load_gates(0, 0).start()

        @pl.loop(0, n_win, step=2)
        def _(g):
            for slot in range(2):
                cur = g + slot
                gather(cur, slot).wait()
                load_gates(cur, slot).wait()

                @pl.when(cur + 1 < n_win)
                def _():
                    gather(cur + 1, 1 - slot).start()
                    load_gates(cur + 1, 1 - slot).start()

                @pl.when(cur >= 2)
                def _():
                    put(cur - 2, slot).wait()

                reduce_window(slot)
                put(cur, slot).start()

        for slot in range(2):
            put(n_win - 2 + slot, slot).wait()

    return pl.kernel(
        body, mesh=_sc_mesh(),
        out_type=jax.ShapeDtypeStruct((n_tok, d), F32),
        scratch_types=[pltpu.VMEM((per_w * n_k,), jnp.int32),
                       pltpu.VMEM((2, rows, dp), table.dtype),
                       pltpu.VMEM((2, rows), F32),
                       pltpu.VMEM((2, tok_win, d), F32),
                       pltpu.SemaphoreType.DMA((2,)),
                       pltpu.SemaphoreType.DMA((2,)),
                       pltpu.SemaphoreType.DMA((2,))],
        compiler_params=pltpu.CompilerParams(use_tc_tiling_on_sc=True, needs_layout_passes=False),
        name="sc_combine_reduce",
    )(table, idx, gate_rows)


def _moe_out_kernel(alpha, x_ref, h2_ref, r_ref, mod_ref, ws13_ref, ws2_ref, lng_ref, lnb_ref, o_ref):
    de = ws2_ref.shape[0]
    z = _dot_packed(h2_ref[0], ws13_ref)
    hmid = (_silu(z[:, :de]) * z[:, de:]).astype(BF16)
    y = _dot(hmid, ws2_ref[...]) + r_ref[0]
    o_ref[0] = _layer_norm(alpha * x_ref[0] + mod_ref[0, 5:6, :] * y, lng_ref[...], lnb_ref[...])


def _moe_out(alpha, xn, h2, routed, mod, ws13, ws2, lng, lnb, nt, ntx):
    nb, _, d = xn.shape
    return pl.pallas_call(
        functools.partial(_moe_out_kernel, alpha),
        out_shape=jax.ShapeDtypeStruct((nb, nt * SEQ_TILE, d), F32),
        grid=(nb, nt),
        in_specs=[_tok_spec(d), _tok_spec(d // 2), _tok_spec(d), _mod_spec(d, nb, ntx),
                  _full_spec(ws13.shape), _full_spec(ws2.shape), _full_spec(lng.shape), _full_spec(lnb.shape)],
        out_specs=_tok_spec(d),
        compiler_params=_params(2),
        name="moe_out",
    )(xn, h2, routed, mod, ws13, ws2, lng, lnb)


def _dispatch_plan(counts, topi, rank, n_tok):
    n_e = counts.shape[0]
    bm = MOE_BLOCK_ROWS
    padded = (counts + bm - 1) // bm * bm
    pad_end = jnp.cumsum(padded)
    pad_start = pad_end - padded
    onehot = topi[:, None, :] == jnp.arange(n_e, dtype=jnp.int32)[None, :, None]
    dest = rank + jnp.sum(jnp.where(onehot, pad_start[None, :, None], 0), axis=1)
    n_blocks = n_tok * TOP_K // bm + n_e
    block_start = jnp.arange(n_blocks, dtype=jnp.int32) * bm
    block_expert = jnp.sum((pad_end[None, :] <= block_start[:, None]).astype(jnp.int32), axis=1)
    block_expert = jnp.minimum(block_expert, n_e - 1)
    block_rows = jnp.clip(jnp.take(pad_start + counts, block_expert) - block_start, 0, bm).astype(jnp.int32)
    return dest.astype(jnp.int32), block_expert, block_rows, n_blocks * bm


def _rope_tables(seq):
    n_freq = QK_ROPE // 4
    inv_freq = ROPE_THETA ** (-jnp.arange(n_freq, dtype=F32) / n_freq)
    pos = jnp.arange(seq, dtype=jnp.int32)
    r = (pos // GRID_W).astype(F32)
    col = (pos % GRID_W).astype(F32)
    ang = jnp.concatenate([r[:, None] * inv_freq, col[:, None] * inv_freq], -1)
    cos, sin = jnp.cos(ang), jnp.sin(ang)
    zeros = jnp.zeros((seq, LANE - QK_ROPE), F32)
    cos_slot = jnp.concatenate([cos, cos, zeros], -1)
    sin_slot = jnp.concatenate([-sin, sin, zeros], -1)
    return cos_slot, sin_slot


def _rope_slot_weights(w_rope):
    k, n, _ = w_rope.shape
    half = QK_ROPE // 2
    swapped = jnp.concatenate([w_rope[..., half:], w_rope[..., :half]], -1)
    pad = jnp.zeros((k, n, LANE - QK_ROPE), w_rope.dtype)
    plain = jnp.concatenate([w_rope, pad], -1).reshape(k, n * LANE)
    swp = jnp.concatenate([swapped, pad], -1).reshape(k, n * LANE)
    return jnp.concatenate([plain, swp], -1)


def kernel(x, c, ctx, c_ctx, ada_w, ada_b, ln_g, ln_b, conf_w1, conf_b1, conf_dw, conf_dwb, conf_ng, conf_nb, conf_w2, conf_b2, sc_w_in, sc_dw, sc_w_out, mla_w_dqkv, mla_q_g, mla_kv_g, mla_w_uq, mla_w_uk, mla_w_uv, mla_w_o, moe_router, moe_bias, moe_w1, moe_w3, moe_w2, sh_w1, sh_w3, sh_w2):
    nb, seq, d = x.shape
    l_ctx = ctx.shape[1]
    depth = ada_w.shape[0]
    alpha = (2.0 * depth) ** 0.25
    tl = SEQ_TILE
    assert seq % tl == 0 and l_ctx % tl == 0 and seq % Q_TILE == 0 and seq % GRID_W == 0
    ntx = seq // tl
    nt_all = (seq + l_ctx) // tl
    attn_layers = [i for i in range(depth) if i % N_MIXERS == 2]
    last_ctx_reader = attn_layers[-1] if attn_layers else -1

    rows = -(-(nb + 1) // 8) * 8
    c_all = jnp.zeros((rows, d), F32).at[:nb].set(c).at[nb].set(c_ctx)
    mod_all = _modulation(c_all, ada_w, ada_b).reshape(depth, rows, N_MOD, d)

    assert nb % N_CHAINS == 0
    nbc = nb // N_CHAINS
    chains = [jnp.concatenate([x[c0:c0 + nbc], ctx[c0:c0 + nbc]], axis=1) for c0 in range(0, nb, nbc)]
    q_rank, kv_rank = mla_q_g.shape[1], mla_kv_g.shape[1]
    cos_t, sin_t = _rope_tables(seq)
    row = lambda v: v.reshape(1, -1)

    for i in range(depth):
        need_ctx = i < last_ctx_reader
        kind, j = i % N_MIXERS, i // N_MIXERS
        nt = nt_all if need_ctx else ntx
        n_tok = nbc * nt * tl

        if kind == 0:
            w_first = conf_w1[j].astype(BF16)
            dw_tiles = jnp.broadcast_to(conf_dw[j][:, None, :], (conf_dw.shape[1], SUBLANES, d))
            w_last, b_last = conf_w2[j].astype(BF16), row(conf_b2[j])
        elif kind == 1:
            w_first = sc_w_in[j].astype(BF16)
            w_last, b_last = sc_w_out[j].astype(BF16), jnp.zeros((1, d), F32)
        else:
            wdq = mla_w_dqkv[j]
            wd = wdq[:, :q_rank + kv_rank].astype(BF16)
            wkp = _rope_slot_weights(wdq[:, None, q_rank + kv_rank:]).astype(BF16)
            wuq = mla_w_uq[j].reshape(q_rank, MLA_HEADS, QK_NOPE + QK_ROPE)
            wqn = wuq[:, :, :QK_NOPE].reshape(q_rank, MLA_HEADS * QK_NOPE).astype(BF16)
            wqr = _rope_slot_weights(wuq[:, :, QK_NOPE:]).astype(BF16)
            wuk, wuv = mla_w_uk[j].astype(BF16), mla_w_uv[j].astype(BF16)
            w_last, b_last = mla_w_o[j].astype(BF16), jnp.zeros((1, d), F32)
        rw_hi, rw_lo = _split_bf16(moe_router[i].T)
        ws13 = jnp.concatenate([sh_w1[i], sh_w3[i]], axis=-1).astype(BF16)
        ws2 = sh_w2[i].astype(BF16)

        for ci in range(N_CHAINS):
            x_all = chains[ci]
            mod = jnp.concatenate([mod_all[i, ci * nbc:(ci + 1) * nbc], mod_all[i, nb:nb + 1]], axis=0)

            if kind == 0:
                u = _conf_in(x_all, mod, w_first, row(conf_b1[j]), nt, ntx)
                a = _conf_conv(u, dw_tiles, row(conf_dwb[j]), row(conf_ng[j]), row(conf_nb[j]), nt, ntx)
            elif kind == 1:
                a = _sc_conv(x_all, mod, w_first, sc_dw[j], nt, ntx)
            else:
                q, k, v = _mla_proj(x_all, mod, wd, wkp, row(mla_q_g[j]), row(mla_kv_g[j]), wqn, wqr,
                                    wuk, wuv, cos_t, sin_t, nt_all, ntx)
                a = _attention(q, k, v, nt * tl)

            xn, h2, topi, topw, rank, counts = _post(alpha, a, w_last, b_last, x_all, mod, row(ln_g[i, 0]),
                                                     row(ln_b[i, 0]), rw_hi, rw_lo, moe_bias[i].reshape(-1, 1),
                                                     nt, ntx)

            dest, block_expert, block_rows, n_rows = _dispatch_plan(counts[:, 0], topi, rank, n_tok)
            xs = _sc_row_scatter(h2.reshape(n_tok, d // 2), dest, n_rows)
            ys = _grouped_ffn(block_expert, block_rows, xs, moe_w1, moe_w3, moe_w2, i)
            routed = _sc_combine(ys, dest, topw).reshape(nbc, nt * tl, d)

            chains[ci] = _moe_out(alpha, xn, h2, routed, mod, ws13, ws2,
                                  row(ln_g[i, 1]), row(ln_b[i, 1]), nt, ntx)
    return jnp.concatenate([xc[:, :seq] for xc in chains], axis=0)
```

```python
import functools

import numpy as np
import jax
import jax.numpy as jnp
from jax import lax
from jax.experimental import pallas as pl
from jax.experimental.pallas import tpu as pltpu
from jax.experimental.pallas import tpu_sc as plsc

F32 = jnp.float32
BF16 = jnp.bfloat16
U32 = jnp.uint32
HIGH_HALF_MASK = np.uint32(0xFFFF0000)

GRID_W = 64
N_MIXERS = 3
LN_EPS = 1e-5
RMS_EPS = 1e-6
N_MOD = 6
MLA_HEADS = 8
QK_NOPE = 128
QK_ROPE = 64
V_HEAD = 128
ROPE_THETA = 10000.0
ATTN_SCALE = (QK_NOPE + QK_ROPE) ** -0.5
Q_SCALE = ATTN_SCALE * 1.4426950408889634
TOP_K = 8
N_GROUPS = 8
TOPK_GROUPS = 4
ROUTED_SCALE = 2.5

SEQ_TILE = 256
CONV_HALO = 16
SHORT_HALO = 8
CONV_ROW_CHUNK = 64
LANE = 128
SUBLANES = 8
Q_TILE = 2048
ATTN_SUB_ROWS = 256
ATTN_ROW_CHUNK = 16
MOE_BLOCK_ROWS = 1024
ROUTE_SUB_TOKENS = 128
MOD_COL_TILE = 1536
VMEM_LIMIT = 48 * 1024 * 1024
SC_CORES = 2
SC_SUBCORES = 16
SC_WORKERS = SC_CORES * SC_SUBCORES
SC_LANES = 16
SC_COMBINE_UNROLL = 4
SC_COMBINE_TOKENS = 8
SC_WINDOWS = (64, 32, 16)
N_CHAINS = 1


def _params(n_axes):
    return pltpu.CompilerParams(dimension_semantics=("arbitrary",) * n_axes,
                                vmem_limit_bytes=VMEM_LIMIT)


def _split_bf16(a):
    hi = a.astype(BF16)
    lo = (a - hi.astype(F32)).astype(BF16)
    return hi, lo


def _dot(a, b):
    return jnp.dot(a, b, preferred_element_type=F32)


def _dot_nt(a, b):
    return lax.dot_general(a, b, (((1,), (1,)), ((), ())), preferred_element_type=F32)


def _pack_halves(v):
    half = v.shape[-1] // 2
    lo = lax.bitcast_convert_type(v[:, :half].astype(BF16).astype(F32), U32) >> 16
    hi = lax.bitcast_convert_type(v[:, half:].astype(BF16).astype(F32), U32) & HIGH_HALF_MASK
    return hi | lo


def _unpack_halves(p):
    lo = lax.bitcast_convert_type(p << 16, F32)
    hi = lax.bitcast_convert_type(p & HIGH_HALF_MASK, F32)
    return lo, hi


def _dot_packed(p, w):
    lo, hi = _unpack_halves(p)
    return _dot(jnp.concatenate([lo.astype(BF16), hi.astype(BF16)], axis=1), w[...])


def _layer_norm(v, g, b):
    mu = jnp.mean(v, axis=-1, keepdims=True)
    c = v - mu
    var = jnp.mean(c * c, axis=-1, keepdims=True)
    return c * lax.rsqrt(var + LN_EPS) * g + b


def _silu(v):
    return v * jax.nn.sigmoid(v)


def _mod_kernel(c_ref, w_ref, b_ref, o_ref):
    a = _silu(c_ref[...])
    a_hi, a_lo = _split_bf16(a)
    w_hi, w_lo = _split_bf16(w_ref[0])
    o_ref[0] = _dot(a_hi, w_hi) + _dot(a_hi, w_lo) + _dot(a_lo, w_hi) + b_ref[0]


def _modulation(c_all, ada_w, ada_b):
    depth, d, n = ada_w.shape
    rows = c_all.shape[0]
    tn = MOD_COL_TILE
    return pl.pallas_call(
        _mod_kernel,
        out_shape=jax.ShapeDtypeStruct((depth, rows, n), F32),
        grid=(depth, n // tn),
        in_specs=[pl.BlockSpec((rows, d), lambda i, j: (0, 0)),
                  pl.BlockSpec((1, d, tn), lambda i, j: (i, 0, j)),
                  pl.BlockSpec((1, 1, tn), lambda i, j: (i, 0, j))],
        out_specs=pl.BlockSpec((1, rows, tn), lambda i, j: (i, 0, j)),
        compiler_params=_params(2),
        name="adaln_mod",
    )(c_all, ada_w, ada_b.reshape(depth, 1, n))


def _tok_spec(d, tl=SEQ_TILE):
    return pl.BlockSpec((1, tl, d), lambda b, l: (b, l, 0))


def _stream_specs(d, ntx, split, tl=SEQ_TILE):
    if split:
        return (pl.BlockSpec((1, tl, d), lambda b, l: (b, jnp.minimum(l, ntx - 1), 0)),
                pl.BlockSpec((1, tl, d), lambda b, l: (b, jnp.maximum(l - ntx, 0), 0)))
    return _tok_spec(d, tl), pl.BlockSpec((1, tl, d), lambda b, l: (b, 0, 0))


def _stream_tile(split, ntx, x_ref, c_ref):
    if not split:
        return x_ref[0]
    return jnp.where(pl.program_id(1) < ntx, x_ref[0], c_ref[0])


def _mod_spec(d, n_batch, ntx):
    return pl.BlockSpec((1, N_MOD, d), lambda b, l: (jnp.where(l < ntx, b, n_batch), 0, 0))


def _full_spec(shape):
    zeros = (0,) * len(shape)
    return pl.BlockSpec(shape, lambda b, l: zeros)


def _halo_specs(d, halo, seq_len, tl=SEQ_TILE):
    per_tile = tl // halo
    last = seq_len // halo - 1
    prev = pl.BlockSpec((1, halo, d), lambda b, l: (b, jnp.maximum(l * per_tile - 1, 0), 0))
    nxt = pl.BlockSpec((1, halo, d), lambda b, l: (b, jnp.minimum((l + 1) * per_tile, last), 0))
    return prev, nxt


def _segment_edges(l, ntx, nt):
    first = jnp.logical_or(l == 0, l == ntx)
    last = jnp.logical_or(l == ntx - 1, l == nt - 1)
    return first, last


def _window_rows(prev_ref, cur_ref, next_ref, win_ref):
    halo, tl = prev_ref.shape[1], cur_ref.shape[1]
    win_ref[0:halo, :] = prev_ref[0]
    win_ref[halo:halo + tl, :] = cur_ref[0]
    win_ref[halo + tl:, :] = next_ref[0]
    return win_ref[...]


def _edge_mask(n_rows, halo, first, last):
    r = lax.broadcasted_iota(jnp.int32, (n_rows, 1), 0)
    outside = jnp.logical_or(jnp.logical_and(first, r < halo), jnp.logical_and(last, r >= n_rows - halo))
    return jnp.where(outside, 0.0, 1.0)


def _conf_in_kernel(split, ntx, x_ref, c_ref, mod_ref, w1_ref, b1_ref, u_ref):
    d = x_ref.shape[-1]
    h = _stream_tile(split, ntx, x_ref, c_ref) * (1.0 + mod_ref[0, 1:2, :]) + mod_ref[0, 0:1, :]
    z = _dot(h.astype(BF16), w1_ref[...]) + b1_ref[...]
    u_ref[0] = z[:, :d] * jax.nn.sigmoid(z[:, d:])


def _conf_in(x_part, c_part, split, mod, w1, b1, nt, ntx):
    nb, _, d = x_part.shape
    return pl.pallas_call(
        functools.partial(_conf_in_kernel, split, ntx),
        out_shape=jax.ShapeDtypeStruct((nb, nt * SEQ_TILE, d), F32),
        grid=(nb, nt),
        in_specs=[*_stream_specs(d, ntx, split), _mod_spec(d, nb, ntx), _full_spec(w1.shape),
                  _full_spec(b1.shape)],
        out_specs=_tok_spec(d),
        compiler_params=_params(2),
        name="conf_in",
    )(x_part, c_part, mod, w1, b1)


def _conf_conv_kernel(ntx, nt, up_ref, uc_ref, un_ref, dw_ref, dwb_ref, ng_ref, nb_ref,
                      a_ref, sh_ref, conv_ref):
    tl, d = uc_ref.shape[1], uc_ref.shape[2]
    taps = dw_ref.shape[0]
    lead = CONV_HALO - (taps - 1) // 2
    first, last = _segment_edges(pl.program_id(1), ntx, nt)
    sh_ref[0, 0:CONV_HALO, :] = jnp.where(first, 0.0, up_ref[0])
    sh_ref[0, CONV_HALO:CONV_HALO + tl, :] = uc_ref[0]
    sh_ref[0, CONV_HALO + tl:, :] = jnp.where(last, 0.0, un_ref[0])
    span = tl + 2 * CONV_HALO - SUBLANES
    for s in range(1, SUBLANES):
        sh_ref[s, 0:span, :] = sh_ref[0, s:s + span, :]
    groups = CONV_ROW_CHUNK // SUBLANES

    def row_chunk(i, carry):
        r0 = pl.multiple_of(i * CONV_ROW_CHUNK, CONV_ROW_CHUNK)
        for c0 in range(0, d, LANE):
            accs = [jnp.zeros((SUBLANES, LANE), F32) for _ in range(groups)]
            for k in range(taps):
                res = (lead + k) % SUBLANES
                off = lead + k - res
                w = dw_ref[k, :, c0:c0 + LANE]
                for g in range(groups):
                    lo = r0 + (off + g * SUBLANES)
                    accs[g] = accs[g] + w * sh_ref[res, pl.ds(lo, SUBLANES), c0:c0 + LANE]
            for g in range(groups):
                conv_ref[pl.ds(r0 + g * SUBLANES, SUBLANES), c0:c0 + LANE] = accs[g]
        return carry

    lax.fori_loop(0, tl // CONV_ROW_CHUNK, row_chunk, 0)
    v = _layer_norm(conv_ref[...] + dwb_ref[...], ng_ref[...], nb_ref[...])
    a_ref[0] = _silu(v).astype(BF16)


def _conf_conv(u, dw, dwb, ng, nb_, nt, ntx):
    nb, seq_len, d = u.shape
    prev, nxt = _halo_specs(d, CONV_HALO, seq_len)
    return pl.pallas_call(
        functools.partial(_conf_conv_kernel, ntx, nt),
        out_shape=jax.ShapeDtypeStruct((nb, seq_len, d), BF16),
        grid=(nb, nt),
        in_specs=[prev, _tok_spec(d), nxt, _full_spec(dw.shape), _full_spec(dwb.shape),
                  _full_spec(ng.shape), _full_spec(nb_.shape)],
        out_specs=_tok_spec(d),
        scratch_shapes=[pltpu.VMEM((SUBLANES, SEQ_TILE + 2 * CONV_HALO, d), F32),
                        pltpu.VMEM((SEQ_TILE, d), F32)],
        compiler_params=_params(2),
        name="conf_conv",
    )(u, u, u, dw, dwb, ng, nb_)


def _sc_conv_kernel(ntx, nt, xp_ref, xc_ref, xn_ref, mod_ref, w_ref, dw_ref, a_ref, win_ref, ext_ref):
    tl, d = xc_ref.shape[1], xc_ref.shape[2]
    taps = dw_ref.shape[0]
    lead = SHORT_HALO - (taps - 1) // 2
    first, last = _segment_edges(pl.program_id(1), ntx, nt)
    xw = _window_rows(xp_ref, xc_ref, xn_ref, win_ref)
    h = xw * (1.0 + mod_ref[0, 1:2, :]) + mod_ref[0, 0:1, :]
    z = _dot(h.astype(BF16), w_ref[...])
    gb = z[SHORT_HALO:SHORT_HALO + tl, :d]
    ext_ref[...] = z[:, d:2 * d] * z[:, 2 * d:] * _edge_mask(z.shape[0], SHORT_HALO, first, last)
    acc = dw_ref[0:1, :] * ext_ref[lead:lead + tl, :]
    for k in range(1, taps):
        acc = acc + dw_ref[k:k + 1, :] * ext_ref[lead + k:lead + k + tl, :]
    a_ref[0] = (gb * acc).astype(BF16)


def _sc_conv(x_all, mod, w_in, dw, nt, ntx):
    nb, seq_len, d = x_all.shape
    prev, nxt = _halo_specs(d, SHORT_HALO, seq_len)
    win = pltpu.VMEM((SEQ_TILE + 2 * SHORT_HALO, d), F32)
    return pl.pallas_call(
        functools.partial(_sc_conv_kernel, ntx, nt),
        out_shape=jax.ShapeDtypeStruct((nb, nt * SEQ_TILE, d), BF16),
        grid=(nb, nt),
        in_specs=[prev, _tok_spec(d), nxt, _mod_spec(d, nb, ntx), _full_spec(w_in.shape), _full_spec(dw.shape)],
        out_specs=_tok_spec(d),
        scratch_shapes=[win, win],
        compiler_params=_params(2),
        name="sc_conv",
    )(x_all, x_all, x_all, mod, w_in, dw)


def _rms(v, g):
    return v * lax.rsqrt(jnp.mean(v * v, axis=-1, keepdims=True) + RMS_EPS) * g


def _mla_proj_kernel(ntx, q_rank,
                     x_ref, mod_ref, wd_ref, wkp_ref, qg_ref, kvg_ref, wqn_ref, wqr_ref, wuk_ref, wuv_ref,
                     cos_ref, sin_ref, q_ref, k_ref, v_ref):
    is_latent = pl.program_id(1) < ntx
    h = (x_ref[0] * (1.0 + mod_ref[0, 1:2, :]) + mod_ref[0, 0:1, :]).astype(BF16)
    dn = _dot(h, wd_ref[...])
    cq = _rms(dn[:, :q_rank], qg_ref[...]).astype(BF16)
    ckv = _rms(dn[:, q_rank:], kvg_ref[...]).astype(BF16)
    cos = jnp.where(is_latent, cos_ref[...], 1.0)
    sin = jnp.where(is_latent, sin_ref[...], 0.0)
    kp2 = _dot(h, wkp_ref[...])
    kp = kp2[:, :LANE] * cos + kp2[:, LANE:] * sin
    kn = _dot(ckv, wuk_ref[...])
    vv = _dot(ckv, wuv_ref[...])
    qn = _dot(cq, wqn_ref[...]) * Q_SCALE
    qr2 = _dot(cq, wqr_ref[...])
    hw = MLA_HEADS * LANE
    for hd in range(MLA_HEADS):
        sl = slice(hd * LANE, (hd + 1) * LANE)
        qr = (qr2[:, sl] * cos + qr2[:, hw + hd * LANE:hw + (hd + 1) * LANE] * sin) * Q_SCALE
        q_ref[0, hd, :, 0:LANE] = qn[:, sl].astype(BF16)
        q_ref[0, hd, :, LANE:] = qr.astype(BF16)
        k_ref[0, hd, :, 0:LANE] = kn[:, sl].astype(BF16)
        k_ref[0, hd, :, LANE:] = kp.astype(BF16)
        v_ref[0, hd] = vv[:, sl].astype(BF16)


def _mla_proj(x_all, mod, wd, wkp, qg, kvg, wqn, wqr, wuk, wuv, cos_t, sin_t, nt, ntx):
    nb, seq_len, d = x_all.shape
    q_rank = qg.shape[-1]
    tl = SEQ_TILE
    rope_spec = pl.BlockSpec((tl, LANE), lambda b, l: (jnp.minimum(l, ntx - 1), 0))
    qk_shape = jax.ShapeDtypeStruct((nb, MLA_HEADS, nt * tl, 2 * LANE), BF16)
    v_shape = jax.ShapeDtypeStruct((nb, MLA_HEADS, nt * tl, LANE), BF16)
    qk_spec = pl.BlockSpec((1, MLA_HEADS, tl, 2 * LANE), lambda b, l: (b, 0, l, 0))
    v_spec = pl.BlockSpec((1, MLA_HEADS, tl, LANE), lambda b, l: (b, 0, l, 0))
    return pl.pallas_call(
        functools.partial(_mla_proj_kernel, ntx, q_rank),
        out_shape=(qk_shape, qk_shape, v_shape),
        grid=(nb, nt),
        in_specs=[_tok_spec(d), _mod_spec(d, nb, ntx), _full_spec(wd.shape), _full_spec(wkp.shape),
                  _full_spec(qg.shape), _full_spec(kvg.shape), _full_spec(wqn.shape), _full_spec(wqr.shape),
                  _full_spec(wuk.shape), _full_spec(wuv.shape), rope_spec, rope_spec],
        out_specs=(qk_spec, qk_spec, v_spec),
        compiler_params=_params(2),
        name="mla_proj",
    )(x_all, mod, wd, wkp, qg, kvg, wqn, wqr, wuk, wuv, cos_t, sin_t)


def _attn_kernel(q_ref, k_ref, v_ref, o_ref, s_ref, p_ref, l_ref):
    tq = q_ref.shape[2]
    for r0 in range(0, tq, ATTN_SUB_ROWS):
        rows = slice(r0, r0 + ATTN_SUB_ROWS)
        s_ref[rows, :] = _dot_nt(q_ref[0, 0, rows, :], k_ref[0, 0])
        for c0 in range(r0, r0 + ATTN_SUB_ROWS, ATTN_ROW_CHUNK):
            chunk = slice(c0, c0 + ATTN_ROW_CHUNK)
            s = s_ref[chunk, :]
            p = jnp.exp2(s - jnp.max(s, axis=-1, keepdims=True))
            l_ref[chunk, :] = jnp.broadcast_to(jnp.sum(p, axis=-1, keepdims=True), (ATTN_ROW_CHUNK, V_HEAD))
            p_ref[chunk, :] = p.astype(BF16)
        o = _dot(p_ref[rows, :], v_ref[0, 0])
        o_ref[0, rows, :] = (o / l_ref[rows, :]).astype(BF16)


def _attention(q, k, v, n_q):
    nb, nh, n_k, dk = k.shape
    tq = Q_TILE
    return pl.pallas_call(
        _attn_kernel,
        out_shape=jax.ShapeDtypeStruct((nb, n_q, nh * V_HEAD), BF16),
        grid=(nb, nh, n_q // tq),
        in_specs=[pl.BlockSpec((1, 1, tq, dk), lambda b, h, i: (b, h, i, 0)),
                  pl.BlockSpec((1, 1, n_k, dk), lambda b, h, i: (b, h, 0, 0)),
                  pl.BlockSpec((1, 1, n_k, V_HEAD), lambda b, h, i: (b, h, 0, 0))],
        out_specs=pl.BlockSpec((1, tq, V_HEAD), lambda b, h, i: (b, i, h)),
        scratch_shapes=[pltpu.VMEM((tq, n_k), F32), pltpu.VMEM((tq, n_k), BF16), pltpu.VMEM((tq, V_HEAD), F32)],
        compiler_params=_params(3),
        name="mla_attn",
    )(q, k, v)


def _route(sel, s):
    n_e, n_t = sel.shape
    per = n_e // N_GROUPS
    sel3 = sel.reshape(N_GROUPS, per, n_t)
    s3 = s.reshape(N_GROUPS, per, n_t)
    iota_p = lax.broadcasted_iota(jnp.int32, (N_GROUPS, per, n_t), 1).astype(F32)
    iota_g = lax.broadcasted_iota(jnp.int32, (N_GROUPS, 1, n_t), 0).astype(F32)
    neg = -jnp.inf
    m1 = jnp.max(sel3, axis=1, keepdims=True)
    first = jnp.min(jnp.where(sel3 == m1, iota_p, float(per)), axis=1, keepdims=True)
    m2 = jnp.max(jnp.where(iota_p == first, neg, sel3), axis=1, keepdims=True)
    gs = m1 + m2
    gsel = jnp.zeros((N_GROUPS, 1, n_t), F32)
    for _ in range(TOPK_GROUPS):
        gm = jnp.max(gs, axis=0, keepdims=True)
        gfirst = jnp.min(jnp.where(gs == gm, iota_g, float(N_GROUPS)), axis=0, keepdims=True)
        pick = iota_g == gfirst
        gsel = jnp.where(pick, 1.0, gsel)
        gs = jnp.where(pick, neg, gs)
    val = jnp.where(gsel > 0.0, sel3, neg)
    iota_e = lax.broadcasted_iota(jnp.int32, (N_GROUPS, per, n_t), 0).astype(F32) * per + iota_p
    chosen = jnp.zeros((N_GROUPS, per, n_t), F32)
    idx, wts = [], []
    for _ in range(TOP_K):
        m = jnp.max(jnp.max(val, axis=1, keepdims=True), axis=0, keepdims=True)
        e = jnp.min(jnp.min(jnp.where(val == m, iota_e, float(n_e)), axis=1, keepdims=True), axis=0, keepdims=True)
        pick = iota_e == e
        wts.append(jnp.sum(jnp.sum(jnp.where(pick, s3, 0.0), axis=1, keepdims=True), axis=0, keepdims=True))
        idx.append(e)
        chosen = jnp.where(pick, 1.0, chosen)
        val = jnp.where(pick, neg, val)
    total = wts[0]
    for w in wts[1:]:
        total = total + w
    wts = [w / total * ROUTED_SCALE for w in wts]
    return idx, wts, chosen, iota_e


def _post_kernel(alpha, split, ntx, a_ref, w_ref, b_ref, x_ref, c_ref, mod_ref, lng_ref, lnb_ref, rw_hi_ref, rw_lo_ref, rb_ref, tri_ref,
                 xn_ref, h2_ref, topi_ref, topw_ref, rank_ref, cnt_ref, run_ref):
    first_step = jnp.logical_and(pl.program_id(0) == 0, pl.program_id(1) == 0)

    @pl.when(first_step)
    def _():
        run_ref[...] = jnp.zeros_like(run_ref)

    y = _dot(a_ref[0], w_ref[...]) + b_ref[...]
    x_tile = _stream_tile(split, ntx, x_ref, c_ref)
    run = run_ref[...]
    for t0 in range(0, y.shape[0], ROUTE_SUB_TOKENS):
        rows = slice(t0, t0 + ROUTE_SUB_TOKENS)
        xn = _layer_norm(alpha * x_tile[rows] + mod_ref[0, 2:3, :] * y[rows], lng_ref[...], lnb_ref[...])
        xn_ref[0, rows, :] = xn
        h2 = xn * (1.0 + mod_ref[0, 4:5, :]) + mod_ref[0, 3:4, :]
        h_hi, h_lo = _split_bf16(h2)
        h2_ref[0, rows, :] = _pack_halves(h2)
        logits = _dot_nt(rw_hi_ref[...], h_hi) + _dot_nt(rw_hi_ref[...], h_lo) + _dot_nt(rw_lo_ref[...], h_hi)
        s = jax.nn.sigmoid(logits)
        idx, wts, chosen3, iota_e = _route(s + rb_ref[...], s)
        n_e, n_t = s.shape
        chosen = chosen3.reshape(n_e, n_t)
        rank = run[:, 0:1] + _dot(chosen.astype(BF16), tri_ref[...])
        rank3 = rank.reshape(chosen3.shape)
        for k in range(TOP_K):
            rk = jnp.sum(jnp.sum(jnp.where(iota_e == idx[k], rank3, 0.0), axis=1, keepdims=True), axis=0,
                         keepdims=True)
            topi_ref[k:k + 1, rows] = idx[k].reshape(1, n_t).astype(jnp.int32)
            topw_ref[k:k + 1, rows] = wts[k].reshape(1, n_t)
            rank_ref[k:k + 1, rows] = rk.reshape(1, n_t).astype(jnp.int32)
        run = run + jnp.sum(chosen, axis=1, keepdims=True)
    run_ref[...] = run
    cnt_ref[...] = run.astype(jnp.int32)


def _post(alpha, a, w, bias, x_part, c_part, split, mod, lng, lnb, rw_hi, rw_lo, rb, nt, ntx):
    nb, _, dk = a.shape
    d = x_part.shape[-1]
    n_e = rw_hi.shape[0]
    tl = SEQ_TILE
    n_tok = nb * nt * tl
    col_spec = lambda rows: pl.BlockSpec((rows, tl), lambda b, l: (0, b * nt + l))
    sub = ROUTE_SUB_TOKENS
    row_i = lax.broadcasted_iota(jnp.int32, (sub, sub), 0)
    col_i = lax.broadcasted_iota(jnp.int32, (sub, sub), 1)
    tri = jnp.where(row_i < col_i, 1.0, 0.0).astype(BF16)
    return pl.pallas_call(
        functools.partial(_post_kernel, alpha, split, ntx),
        out_shape=(jax.ShapeDtypeStruct((nb, nt * tl, d), F32),
                   jax.ShapeDtypeStruct((nb, nt * tl, d // 2), U32),
                   jax.ShapeDtypeStruct((TOP_K, n_tok), jnp.int32),
                   jax.ShapeDtypeStruct((TOP_K, n_tok), F32),
                   jax.ShapeDtypeStruct((TOP_K, n_tok), jnp.int32),
                   jax.ShapeDtypeStruct((n_e, LANE), jnp.int32)),
        grid=(nb, nt),
        in_specs=[_tok_spec(dk), _full_spec(w.shape), _full_spec(bias.shape), *_stream_specs(d, ntx, split),
                  _mod_spec(d, nb, ntx), _full_spec(lng.shape), _full_spec(lnb.shape),
                  _full_spec(rw_hi.shape), _full_spec(rw_lo.shape), _full_spec(rb.shape), _full_spec(tri.shape)],
        out_specs=(_tok_spec(d), _tok_spec(d // 2), col_spec(TOP_K), col_spec(TOP_K), col_spec(TOP_K),
                   _full_spec((n_e, LANE))),
        scratch_shapes=[pltpu.VMEM((n_e, LANE), F32)],
        compiler_params=_params(2),
        name="mixer_post",
    )(a, w, bias, x_part, c_part, mod, lng, lnb, rw_hi, rw_lo, rb, tri)


def _gmm_kernel(be_ref, br_ref, xs_ref, w1_ref, w3_ref, w2_ref, ys_ref, w13_s, w2_s):
    i = pl.program_id(0)
    de = w2_ref.shape[2]
    changed = jnp.logical_or(i == 0, be_ref[i] != be_ref[jnp.maximum(i - 1, 0)])

    @pl.when(changed)
    def _():
        w13_s[:, :de] = w1_ref[0, 0].astype(BF16)
        w13_s[:, de:] = w3_ref[0, 0].astype(BF16)
        w2_s[...] = w2_ref[0, 0].astype(BF16)

    @pl.when(br_ref[i] > 0)
    def _():
        z = _dot_packed(xs_ref[...], w13_s)
        hmid = (_silu(z[:, :de]) * z[:, de:]).astype(BF16)
        ys_ref[...] = _pack_halves(_dot(hmid, w2_s[...]))


def _grouped_ffn(block_expert, block_rows, xs, w1, w3, w2, layer):
    n_rows, dp = xs.shape
    d = 2 * dp
    bm = MOE_BLOCK_ROWS
    de = w2.shape[2]
    grid_spec = pltpu.PrefetchScalarGridSpec(
        num_scalar_prefetch=2,
        grid=(n_rows // bm,),
        in_specs=[pl.BlockSpec((bm, dp), lambda i, be, br: (i, 0)),
                  pl.BlockSpec((1, 1, d, de), lambda i, be, br: (layer, be[i], 0, 0)),
                  pl.BlockSpec((1, 1, d, de), lambda i, be, br: (layer, be[i], 0, 0)),
                  pl.BlockSpec((1, 1, de, d), lambda i, be, br: (layer, be[i], 0, 0))],
        out_specs=pl.BlockSpec((bm, dp), lambda i, be, br: (i, 0)),
        scratch_shapes=[pltpu.VMEM((d, 2 * de), BF16), pltpu.VMEM((de, d), BF16)],
    )
    return pl.pallas_call(
        _gmm_kernel,
        out_shape=jax.ShapeDtypeStruct((n_rows, dp), U32),
        grid_spec=grid_spec,
        compiler_params=_params(1),
        name="moe_grouped_ffn",
    )(block_expert, block_rows, xs, w1, w3, w2)


def _sc_mesh():
    return plsc.VectorSubcoreMesh(core_axis_name="c", subcore_axis_name="s",
                                  num_cores=SC_CORES, num_subcores=SC_SUBCORES)


def _sc_worker():
    return lax.axis_index("s") * SC_CORES + lax.axis_index("c")


def _sc_window(rows_per_worker):
    for win in SC_WINDOWS:
        if rows_per_worker % (2 * win) == 0:
            return win
    raise ValueError(f"no SparseCore window divides {rows_per_worker} rows per worker")


def _sc_row_scatter(src, dest, n_out):
    n_k, n_tok = dest.shape
    dp = src.shape[1]
    assert n_tok % SC_WORKERS == 0
    per_w = n_tok // SC_WORKERS
    win = _sc_window(per_w)
    n_win = per_w // win
    idx = dest.reshape(n_k, SC_WORKERS, n_win, win).transpose(1, 2, 0, 3)

    def body(src_hbm, idx_hbm, out_hbm, idx_v, rows_v, load_sem, scat_sem):
        wid = _sc_worker()
        base = wid * per_w
        pltpu.sync_copy(idx_hbm.at[wid], idx_v)

        def load(g, slot):
            return pltpu.make_async_copy(src_hbm.at[pl.ds(base + g * win, win)], rows_v.at[slot], load_sem.at[slot])

        def scatter(g, slot, k):
            return pltpu.make_async_copy(rows_v.at[slot], out_hbm.at[idx_v.at[g, k]], scat_sem.at[slot])

        load(0, 0).start()

        @pl.loop(0, n_win, step=2)
        def _(g):
            for slot in range(2):
                cur = g + slot
                load(cur, slot).wait()

                @pl.when(cur + 1 < n_win)
                def _():
                    @pl.when(cur >= 1)
                    def _():
                        for k in range(n_k):
                            scatter(cur - 1, 1 - slot, k).wait()
                    load(cur + 1, 1 - slot).start()

                for k in range(n_k):
                    scatter(cur, slot, k).start()

        for slot in range(2):
            for k in range(n_k):
                scatter(n_win - 2 + slot, slot, k).wait()

    return pl.kernel(
        body, mesh=_sc_mesh(),
        out_type=jax.ShapeDtypeStruct((n_out, dp), src.dtype),
        scratch_types=[pltpu.VMEM((n_win, n_k, win), jnp.int32),
                       pltpu.VMEM((2, win, dp), src.dtype),
                       pltpu.SemaphoreType.DMA((2,)),
                       pltpu.SemaphoreType.DMA((2,))],
        compiler_params=pltpu.CompilerParams(use_tc_tiling_on_sc=True),
        name="sc_dispatch_scatter",
    )(src, idx)


def _sc_combine(table, dest, gates):
    n_k, n_tok = dest.shape
    dp = table.shape[1]
    d = 2 * dp
    lanes = SC_LANES
    assert n_tok % SC_WORKERS == 0
    per_w = n_tok // SC_WORKERS
    tok_win = SC_COMBINE_TOKENS
    rows = tok_win * n_k
    n_win = per_w // tok_win
    assert per_w % (2 * tok_win) == 0 and rows <= 128
    idx = dest.T.reshape(-1)
    gate_rows = gates.T.reshape(-1)

    def body(table_hbm, idx_hbm, gate_hbm, out_hbm, idx_v, rows_v, gate_v, out_v, gather_sem, gate_sem, put_sem):
        wid = _sc_worker()
        tok0 = wid * per_w
        pltpu.sync_copy(idx_hbm.at[pl.ds(tok0 * n_k, per_w * n_k)], idx_v)

        def gather(g, slot):
            return pltpu.make_async_copy(table_hbm.at[idx_v.at[pl.ds(g * rows, rows)]], rows_v.at[slot],
                                         gather_sem.at[slot])

        def load_gates(g, slot):
            return pltpu.make_async_copy(gate_hbm.at[pl.ds((tok0 + g * tok_win) * n_k, rows)], gate_v.at[slot],
                                         gate_sem.at[slot])

        def put(g, slot):
            return pltpu.make_async_copy(out_v.at[slot], out_hbm.at[pl.ds(tok0 + g * tok_win, tok_win)],
                                         put_sem.at[slot])

        def reduce_window(slot):
            @pl.loop(0, tok_win)
            def _(t):
                slot_idx = jnp.full((lanes,), slot, jnp.int32)
                zero_idx = jnp.zeros((lanes,), jnp.int32)
                g_k = [plsc.load_gather(gate_v, [slot_idx, zero_idx + (t * n_k + k)]) for k in range(n_k)]

                @plsc.parallel_loop(0, dp // lanes, 1, unroll=SC_COMBINE_UNROLL)
                def _(v):
                    words = pl.ds(v * lanes, lanes)
                    acc_lo = jnp.zeros((lanes,), F32)
                    acc_hi = jnp.zeros((lanes,), F32)
                    for k in range(n_k):
                        w = rows_v[slot, t * n_k + k, words]
                        acc_lo = acc_lo + g_k[k] * plsc.bitcast(w << 16, F32)
                        acc_hi = acc_hi + g_k[k] * plsc.bitcast(w & HIGH_HALF_MASK, F32)
                    out_v[slot, t, words] = acc_lo
                    out_v[slot, t, pl.ds(dp + v * lanes, lanes)] = acc_hi

        gather(0, 0).start()
        load_gates(0, 0).start()

        @pl.loop(0, n_win, step=2)
        def _(g):
            for slot in range(2):
                cur = g + slot
                gather(cur, slot).wait()
                load_gates(cur, slot).wait()

                @pl.when(cur + 1 < n_win)
                def _():
                    gather(cur + 1, 1 - slot).start()
                    load_gates(cur + 1, 1 - slot).start()

                @pl.when(cur >= 2)
                def _():
                    put(cur - 2, slot).wait()

                reduce_window(slot)
                put(cur, slot).start()

        for slot in range(2):
            put(n_win - 2 + slot, slot).wait()

    return pl.kernel(
        body, mesh=_sc_mesh(),
        out_type=jax.ShapeDtypeStruct((n_tok, d), F32),
        scratch_types=[pltpu.VMEM((per_w * n_k,), jnp.int32),
                       pltpu.VMEM((2, rows, dp), table.dtype),
                       pltpu.VMEM((2, rows), F32),
                       pltpu.VMEM((2, tok_win, d), F32),
                       pltpu.SemaphoreType.DMA((2,)),
                       pltpu.SemaphoreType.DMA((2,)),
                       pltpu.SemaphoreType.DMA((2,))],
        compiler_params=pltpu.CompilerParams(use_tc_tiling_on_sc=True, needs_layout_passes=False),
        name="sc_combine_reduce",
    )(table, idx, gate_rows)


def _moe_out_kernel(alpha, x_ref, h2_ref, r_ref, mod_ref, ws13_ref, ws2_ref, lng_ref, lnb_ref, o_ref):
    de = ws2_ref.shape[0]
    z = _dot_packed(h2_ref[0], ws13_ref)
    hmid = (_silu(z[:, :de]) * z[:, de:]).astype(BF16)
    y = _dot(hmid, ws2_ref[...]) + r_ref[0]
    o_ref[0] = _layer_norm(alpha * x_ref[0] + mod_ref[0, 5:6, :] * y, lng_ref[...], lnb_ref[...])


def _moe_out(alpha, xn, h2, routed, mod, ws13, ws2, lng, lnb, nt, ntx):
    nb, _, d = xn.shape
    return pl.pallas_call(
        functools.partial(_moe_out_kernel, alpha),
        out_shape=jax.ShapeDtypeStruct((nb, nt * SEQ_TILE, d), F32),
        grid=(nb, nt),
        in_specs=[_tok_spec(d), _tok_spec(d // 2), _tok_spec(d), _mod_spec(d, nb, ntx),
                  _full_spec(ws13.shape), _full_spec(ws2.shape), _full_spec(lng.shape), _full_spec(lnb.shape)],
        out_specs=_tok_spec(d),
        compiler_params=_params(2),
        name="moe_out",
    )(xn, h2, routed, mod, ws13, ws2, lng, lnb)


def _dispatch_plan(counts, topi, rank, n_tok):
    n_e = counts.shape[0]
    bm = MOE_BLOCK_ROWS
    padded = (counts + bm - 1) // bm * bm
    pad_end = jnp.cumsum(padded)
    pad_start = pad_end - padded
    onehot = topi[:, None, :] == jnp.arange(n_e, dtype=jnp.int32)[None, :, None]
    dest = rank + jnp.sum(jnp.where(onehot, pad_start[None, :, None], 0), axis=1)
    n_blocks = n_tok * TOP_K // bm + n_e
    block_start = jnp.arange(n_blocks, dtype=jnp.int32) * bm
    block_expert = jnp.sum((pad_end[None, :] <= block_start[:, None]).astype(jnp.int32), axis=1)
    block_expert = jnp.minimum(block_expert, n_e - 1)
    block_rows = jnp.clip(jnp.take(pad_start + counts, block_expert) - block_start, 0, bm).astype(jnp.int32)
    return dest.astype(jnp.int32), block_expert, block_rows, n_blocks * bm


def _rope_tables(seq):
    n_freq = QK_ROPE // 4
    inv_freq = ROPE_THETA ** (-jnp.arange(n_freq, dtype=F32) / n_freq)
    pos = jnp.arange(seq, dtype=jnp.int32)
    r = (pos // GRID_W).astype(F32)
    col = (pos % GRID_W).astype(F32)
    ang = jnp.concatenate([r[:, None] * inv_freq, col[:, None] * inv_freq], -1)
    cos, sin = jnp.cos(ang), jnp.sin(ang)
    zeros = jnp.zeros((seq, LANE - QK_ROPE), F32)
    cos_slot = jnp.concatenate([cos, cos, zeros], -1)
    sin_slot = jnp.concatenate([-sin, sin, zeros], -1)
    return cos_slot, sin_slot


def _rope_slot_weights(w_rope):
    k, n, _ = w_rope.shape
    half = QK_ROPE // 2
    swapped = jnp.concatenate([w_rope[..., half:], w_rope[..., :half]], -1)
    pad = jnp.zeros((k, n, LANE - QK_ROPE), w_rope.dtype)
    plain = jnp.concatenate([w_rope, pad], -1).reshape(k, n * LANE)
    swp = jnp.concatenate([swapped, pad], -1).reshape(k, n * LANE)
    return jnp.concatenate([plain, swp], -1)


def kernel(x, c, ctx, c_ctx, ada_w, ada_b, ln_g, ln_b, conf_w1, conf_b1, conf_dw, conf_dwb, conf_ng, conf_nb, conf_w2, conf_b2, sc_w_in, sc_dw, sc_w_out, mla_w_dqkv, mla_q_g, mla_kv_g, mla_w_uq, mla_w_uk, mla_w_uv, mla_w_o, moe_router, moe_bias, moe_w1, moe_w3, moe_w2, sh_w1, sh_w3, sh_w2):
    nb, seq, d = x.shape
    l_ctx = ctx.shape[1]
    depth = ada_w.shape[0]
    alpha = (2.0 * depth) ** 0.25
    tl = SEQ_TILE
    assert seq % tl == 0 and l_ctx % tl == 0 and seq % Q_TILE == 0 and seq % GRID_W == 0
    ntx = seq // tl
    nt_all = (seq + l_ctx) // tl
    attn_layers = [i for i in range(depth) if i % N_MIXERS == 2]
    last_ctx_reader = attn_layers[-1] if attn_layers else -1

    rows = -(-(nb + 1) // 8) * 8
    c_all = jnp.zeros((rows, d), F32).at[:nb].set(c).at[nb].set(c_ctx)
    mod_all = _modulation(c_all, ada_w, ada_b).reshape(depth, rows, N_MOD, d)

    assert nb % N_CHAINS == 0
    nbc = nb // N_CHAINS
    split_first = N_CHAINS == 1
    chains = [x if split_first else jnp.concatenate([x[c0:c0 + nbc], ctx[c0:c0 + nbc]], axis=1)
              for c0 in range(0, nb, nbc)]
    q_rank, kv_rank = mla_q_g.shape[1], mla_kv_g.shape[1]
    cos_t, sin_t = _rope_tables(seq)
    row = lambda v: v.reshape(1, -1)

    for i in range(depth):
        need_ctx = i < last_ctx_reader
        kind, j = i % N_MIXERS, i // N_MIXERS
        nt = nt_all if need_ctx else ntx
        n_tok = nbc * nt * tl

        if kind == 0:
            w_first = conf_w1[j].astype(BF16)
            dw_tiles = jnp.broadcast_to(conf_dw[j][:, None, :], (conf_dw.shape[1], SUBLANES, d))
            w_last, b_last = conf_w2[j].astype(BF16), row(conf_b2[j])
        elif kind == 1:
            w_first = sc_w_in[j].astype(BF16)
            w_last, b_last = sc_w_out[j].astype(BF16), jnp.zeros((1, d), F32)
        else:
            wdq = mla_w_dqkv[j]
            wd = wdq[:, :q_rank + kv_rank].astype(BF16)
            wkp = _rope_slot_weights(wdq[:, None, q_rank + kv_rank:]).astype(BF16)
            wuq = mla_w_uq[j].reshape(q_rank, MLA_HEADS, QK_NOPE + QK_ROPE)
            wqn = wuq[:, :, :QK_NOPE].reshape(q_rank, MLA_HEADS * QK_NOPE).astype(BF16)
            wqr = _rope_slot_weights(wuq[:, :, QK_NOPE:]).astype(BF16)
            wuk, wuv = mla_w_uk[j].astype(BF16), mla_w_uv[j].astype(BF16)
            w_last, b_last = mla_w_o[j].astype(BF16), jnp.zeros((1, d), F32)
        rw_hi, rw_lo = _split_bf16(moe_router[i].T)
        ws13 = jnp.concatenate([sh_w1[i], sh_w3[i]], axis=-1).astype(BF16)
        ws2 = sh_w2[i].astype(BF16)

        for ci in range(N_CHAINS):
            x_all = chains[ci]
            split = split_first and i == 0
            c_part = ctx if split else x_all
            mod = jnp.concatenate([mod_all[i, ci * nbc:(ci + 1) * nbc], mod_all[i, nb:nb + 1]], axis=0)

            if kind == 0:
                u = _conf_in(x_all, c_part, split, mod, w_first, row(conf_b1[j]), nt, ntx)
                a = _conf_conv(u, dw_tiles, row(conf_dwb[j]), row(conf_ng[j]), row(conf_nb[j]), nt, ntx)
            elif kind == 1:
                a = _sc_conv(x_all, mod, w_first, sc_dw[j], nt, ntx)
            else:
                q, k, v = _mla_proj(x_all, mod, wd, wkp, row(mla_q_g[j]), row(mla_kv_g[j]), wqn, wqr,
                                    wuk, wuv, cos_t, sin_t, nt_all, ntx)
                a = _attention(q, k, v, nt * tl)

            xn, h2, topi, topw, rank, counts = _post(alpha, a, w_last, b_last, x_all, c_part, split, mod, row(ln_g[i, 0]),
                                                     row(ln_b[i, 0]), rw_hi, rw_lo, moe_bias[i].reshape(-1, 1),
                                                     nt, ntx)

            dest, block_expert, block_rows, n_rows = _dispatch_plan(counts[:, 0], topi, rank, n_tok)
            xs = _sc_row_scatter(h2.reshape(n_tok, d // 2), dest, n_rows)
            ys = _grouped_ffn(block_expert, block_rows, xs, moe_w1, moe_w3, moe_w2, i)
            routed = _sc_combine(ys, dest, topw).reshape(nbc, nt * tl, d)

            chains[ci] = _moe_out(alpha, xn, h2, routed, mod, ws13, ws2,
                                  row(ln_g[i, 1]), row(ln_b[i, 1]), nt, ntx)
    return jnp.concatenate([xc[:, :seq] for xc in chains], axis=0)
```
